```python
import jax, jax.numpy as jnp
from jax import lax
import numpy as np

D_MODEL = 1024
BATCH = 4
SEQ = 4096
DEPTH = 1

CHUNK = 64
Q_BLOCK = 128
CONV_DIM = D_MODEL
CONV_WIDTH = 3
N_HEADS = 8
QK_NOPE_DIM = 128
QK_ROPE_DIM = 64
V_HEAD_DIM = 128
QK_HEAD_DIM = QK_NOPE_DIM + QK_ROPE_DIM
Q_LORA_RANK = 384
KV_LORA_RANK = 256
ROPE_BASE = 10000.0
N_EXPERTS = 32
TOP_K = 4
D_FF_EXPERT = D_MODEL
SWIGLU_LIMIT = 7.0
SWIGLU_ALPHA = 1.702
MOE_BLOCK = 128
LN_EPS = 1e-5
RMS_EPS = 1e-6
DEEPNORM_ALPHA = (2 * DEPTH) ** 0.25
DEEPNORM_BETA = (8 * DEPTH) ** -0.25
IN_SPLITS = (CONV_DIM, CONV_DIM, CONV_DIM, Q_LORA_RANK, KV_LORA_RANK, QK_ROPE_DIM, D_MODEL, D_MODEL)
IN_DIM = sum(IN_SPLITS)

kernel_name = "hybrid_conv_mla_moe_deepnorm_block"


def layer_norm(u, g, b):
    uf = u.astype(jnp.float32)
    mu = jnp.mean(uf, axis=-1, keepdims=True)
    var = jnp.mean(jnp.square(uf - mu), axis=-1, keepdims=True)
    return ((uf - mu) * lax.rsqrt(var + LN_EPS) * g.astype(jnp.float32) + b.astype(jnp.float32)).astype(u.dtype)


def rms_norm(u, g):
    uf = u.astype(jnp.float32)
    ms = jnp.mean(jnp.square(uf), axis=-1, keepdims=True)
    return (uf * lax.rsqrt(ms + RMS_EPS) * g.astype(jnp.float32)).astype(u.dtype)


def rope_tables(positions, dtype):
    inv_freq = ROPE_BASE ** (-jnp.arange(0, QK_ROPE_DIM, 2, dtype=jnp.float32) / QK_ROPE_DIM)
    ang = positions.astype(jnp.float32)[..., None] * inv_freq
    return jnp.cos(ang).astype(dtype), jnp.sin(ang).astype(dtype)


def apply_rope(u, cos, sin):
    half = u.shape[-1] // 2
    u1, u2 = u[..., :half], u[..., half:]
    return jnp.concatenate([u1 * cos - u2 * sin, u1 * sin + u2 * cos], axis=-1)


def causal_depthwise_conv(u, w):
    c = u.shape[-1]
    return lax.conv_general_dilated(u, w[:, None, :].astype(u.dtype), window_strides=(1,),
                                    padding=[(CONV_WIDTH - 1, 0)],
                                    dimension_numbers=('NWC', 'WIO', 'NWC'),
                                    feature_group_count=c)


def chunk_causal_attention(q, k, v):
    b, s, h, dqk = q.shape
    nqb = s // Q_BLOCK
    qb = q.reshape(b, nqb, Q_BLOCK, h, dqk).transpose(1, 0, 2, 3, 4)
    key_chunk = jnp.arange(s) // CHUNK
    scale = dqk ** -0.5

    def one_block(args):
        q_blk, blk = args
        q_chunk = (blk * Q_BLOCK + jnp.arange(Q_BLOCK)) // CHUNK
        allowed = key_chunk[None, :] <= q_chunk[:, None]
        scores = jnp.einsum('bqhd,bkhd->bhqk', q_blk, k).astype(jnp.float32) * scale
        scores = jnp.where(allowed[None, None], scores, -jnp.inf)
        probs = jax.nn.softmax(scores, axis=-1).astype(v.dtype)
        return jnp.einsum('bhqk,bkhd->bqhd', probs, v)

    out = lax.map(one_block, (qb, jnp.arange(nqb)))
    return out.transpose(1, 0, 2, 3, 4).reshape(b, s, h, v.shape[-1])


def hybrid_mixer(x, cos, sin, w_in, conv_w, q_norm_g, w_uq, kv_norm_g, w_uk, w_uv,
                 w_conv_branch, w_attn_branch, w_out):
    b, s, _ = x.shape
    proj = x @ w_in
    offs = [int(o) for o in np.cumsum(IN_SPLITS)[:-1]]
    gb, gc, hc, q_lat, kv_lat, k_pe, g_conv, g_attn = jnp.split(proj, offs, axis=-1)
    y_conv = (gb * causal_depthwise_conv(gc * hc, conv_w)) @ w_conv_branch
    q = (rms_norm(q_lat, q_norm_g) @ w_uq).reshape(b, s, N_HEADS, QK_HEAD_DIM)
    q_nope, q_pe = q[..., :QK_NOPE_DIM], q[..., QK_NOPE_DIM:]
    q_pe = apply_rope(q_pe, cos[:, :, None, :], sin[:, :, None, :])
    ckv = rms_norm(kv_lat, kv_norm_g)
    k_nope = (ckv @ w_uk).reshape(b, s, N_HEADS, QK_NOPE_DIM)
    v = (ckv @ w_uv).reshape(b, s, N_HEADS, V_HEAD_DIM)
    k_pe = jnp.broadcast_to(apply_rope(k_pe, cos, sin)[:, :, None, :], (b, s, N_HEADS, QK_ROPE_DIM))
    q_full = jnp.concatenate([q_nope, q_pe], axis=-1)
    k_full = jnp.concatenate([k_nope, k_pe], axis=-1)
    attn = chunk_causal_attention(q_full, k_full, v).reshape(b, s, N_HEADS * V_HEAD_DIM)
    y_attn = attn @ w_attn_branch
    merged = jax.nn.sigmoid(g_conv) * y_conv + jax.nn.sigmoid(g_attn) * y_attn
    return merged @ w_out


def moe_ffn(x2d, w_router, b_router, w_gate_up, b_gate_up, w_down, b_down):
    t, d = x2d.shape
    n_assign = t * TOP_K
    logits = (x2d @ w_router + b_router).astype(jnp.float32)
    top_logits, top_idx = lax.top_k(logits, TOP_K)
    top_w = jax.nn.softmax(top_logits, axis=-1)
    flat_e = top_idx.reshape(-1)
    flat_tok = jnp.arange(n_assign, dtype=jnp.int32) // TOP_K
    flat_w = top_w.reshape(-1)
    order = jnp.argsort(flat_e)
    sorted_e = flat_e[order]
    counts = jnp.bincount(flat_e, length=N_EXPERTS)
    start = jnp.cumsum(counts) - counts
    padded = (counts + MOE_BLOCK - 1) // MOE_BLOCK * MOE_BLOCK
    padded_end = jnp.cumsum(padded)
    padded_start = padded_end - padded
    dest = padded_start[sorted_e] + (jnp.arange(n_assign, dtype=jnp.int32) - start[sorted_e])
    n_blocks = n_assign // MOE_BLOCK + N_EXPERTS
    n_slots = n_blocks * MOE_BLOCK
    slot_tok = jnp.zeros((n_slots,), jnp.int32).at[dest].set(flat_tok[order])
    slot_w = jnp.zeros((n_slots,), jnp.float32).at[dest].set(flat_w[order])
    block_e = jnp.minimum(jnp.searchsorted(padded_end, jnp.arange(n_blocks) * MOE_BLOCK, side='right'),
                          N_EXPERTS - 1)

    def expert_block(args):
        tok, e = args
        xb = x2d[tok]
        gu = xb @ w_gate_up[e] + b_gate_up[e]
        gate = jnp.minimum(gu[:, :D_FF_EXPERT], SWIGLU_LIMIT)
        up = jnp.clip(gu[:, D_FF_EXPERT:], -SWIGLU_LIMIT, SWIGLU_LIMIT)
        hid = (up + 1) * (gate * jax.nn.sigmoid(SWIGLU_ALPHA * gate))
        return hid @ w_down[e] + b_down[e]

    out = lax.map(expert_block, (slot_tok.reshape(n_blocks, MOE_BLOCK), block_e))
    out = out.reshape(n_slots, d) * slot_w[:, None].astype(out.dtype)
    return jnp.zeros_like(x2d).at[slot_tok].add(out)


def setup_inputs(seed: int = 0) -> dict:
    key = jax.random.key(seed)
    ks = jax.random.split(key, 24)
    L = DEPTH
    beta = DEEPNORM_BETA

    def nrm(k, shape, scale):
        return jax.random.normal(k, shape, jnp.float32) * scale

    def gain(k, dim):
        return jnp.ones((L, dim), jnp.float32) + 0.01 * jax.random.normal(k, (L, dim), jnp.float32)

    x = jax.random.normal(ks[0], (BATCH, SEQ, D_MODEL), jnp.float32)
    offset = jax.random.randint(ks[1], (BATCH, 1), 0, 8192, dtype=jnp.int32)
    positions = offset + jnp.arange(SEQ, dtype=jnp.int32)[None, :]
    return {
        "x": x,
        "positions": positions,
        "w_in": nrm(ks[2], (L, D_MODEL, IN_DIM), D_MODEL ** -0.5),
        "conv_w": nrm(ks[3], (L, CONV_WIDTH, CONV_DIM), CONV_WIDTH ** -0.5),
        "q_norm_g": gain(ks[4], Q_LORA_RANK),
        "w_uq": nrm(ks[5], (L, Q_LORA_RANK, N_HEADS * QK_HEAD_DIM), Q_LORA_RANK ** -0.5),
        "kv_norm_g": gain(ks[6], KV_LORA_RANK),
        "w_uk": nrm(ks[7], (L, KV_LORA_RANK, N_HEADS * QK_NOPE_DIM), KV_LORA_RANK ** -0.5),
        "w_uv": nrm(ks[8], (L, KV_LORA_RANK, N_HEADS * V_HEAD_DIM), beta * KV_LORA_RANK ** -0.5),
        "w_conv_branch": nrm(ks[9], (L, CONV_DIM, D_MODEL), beta * CONV_DIM ** -0.5),
        "w_attn_branch": nrm(ks[10], (L, N_HEADS * V_HEAD_DIM, D_MODEL), beta * (N_HEADS * V_HEAD_DIM) ** -0.5),
        "w_out": nrm(ks[11], (L, D_MODEL, D_MODEL), beta * D_MODEL ** -0.5),
        "ln1_g": gain(ks[12], D_MODEL),
        "ln1_b": nrm(ks[13], (L, D_MODEL), 0.01),
        "w_router": nrm(ks[14], (L, D_MODEL, N_EXPERTS), D_MODEL ** -0.5),
        "b_router": nrm(ks[15], (L, N_EXPERTS), 0.01),
        "w_gate_up": nrm(ks[16], (L, N_EXPERTS, D_MODEL, 2 * D_FF_EXPERT), beta * D_MODEL ** -0.5),
        "b_gate_up": nrm(ks[17], (L, N_EXPERTS, 2 * D_FF_EXPERT), 0.01),
        "w_down": nrm(ks[18], (L, N_EXPERTS, D_FF_EXPERT, D_MODEL), beta * D_FF_EXPERT ** -0.5),
        "b_down": nrm(ks[19], (L, N_EXPERTS, D_MODEL), 0.01),
        "ln2_g": gain(ks[20], D_MODEL),
        "ln2_b": nrm(ks[21], (L, D_MODEL), 0.01),
    }


def reference(x, positions, w_in, conv_w, q_norm_g, w_uq, kv_norm_g, w_uk, w_uv,
              w_conv_branch, w_attn_branch, w_out, ln1_g, ln1_b, w_router, b_router,
              w_gate_up, b_gate_up, w_down, b_down, ln2_g, ln2_b):
    b, s, d = x.shape
    cos, sin = rope_tables(positions, x.dtype)
    for l in range(DEPTH):
        mix = hybrid_mixer(x, cos, sin, w_in[l], conv_w[l], q_norm_g[l], w_uq[l], kv_norm_g[l],
                           w_uk[l], w_uv[l], w_conv_branch[l], w_attn_branch[l], w_out[l])
        x = layer_norm(DEEPNORM_ALPHA * x + mix, ln1_g[l], ln1_b[l])
        ffn = moe_ffn(x.reshape(b * s, d), w_router[l], b_router[l], w_gate_up[l], b_gate_up[l],
                      w_down[l], b_down[l]).reshape(b, s, d)
        x = layer_norm(DEEPNORM_ALPHA * x + ffn, ln2_g[l], ln2_b[l])
    return x
```

```python
import functools

import numpy as np
import jax
import jax.numpy as jnp
from jax import lax
from jax.experimental import pallas as pl
from jax.experimental.pallas import tpu as pltpu

D_MODEL = 1024
CHUNK = 64
CONV_DIM = D_MODEL
CONV_WIDTH = 3
N_HEADS = 8
QK_NOPE_DIM = 128
QK_ROPE_DIM = 64
V_HEAD_DIM = 128
QK_HEAD_DIM = QK_NOPE_DIM + QK_ROPE_DIM
Q_LORA_RANK = 384
KV_LORA_RANK = 256
ROPE_BASE = 10000.0
N_EXPERTS = 32
TOP_K = 4
D_FF_EXPERT = D_MODEL
SWIGLU_LIMIT = 7.0
SWIGLU_ALPHA = 1.702
LN_EPS = 1e-5
RMS_EPS = 1e-6
DEPTH = 1
DEEPNORM_ALPHA = (2 * DEPTH) ** 0.25

QK_PAD = 256
ROPE_PAD = 128
LAT_DIM = Q_LORA_RANK + KV_LORA_RANK + 2 * ROPE_PAD

PROJ_TM = 256
ATTN_TQ = 512
ATTN_TK = 512
POST_TM = 512
MOE_BM = 256
COMB_TM = 512
HALO = 8
NEG_BIG = -1e30
VMEM_LIMIT = 56 * 1024 * 1024

_BF = jnp.bfloat16
_F32 = jnp.float32


def _dot(a, b):
    return jnp.dot(a, b, preferred_element_type=_F32)


def _const_spec(shape):
    nd = len(shape)
    return pl.BlockSpec(shape, lambda *_: (0,) * nd, pipeline_mode=pl.Buffered(1))


def _proj_kernel(tiles_per_seq,
                 x_ref, cos_ref, sin_ref, wg3_ref, wlat_ref, wgates_ref, convw_ref,
                 qg_ref, kvg_ref, wuqa_ref, wuqb_ref, wuk_ref, wuv_ref, wcb_ref,
                 q_ref, kn_ref, kpe_ref, v_ref, ycg_ref, sga_ref, ubuf):
    i = pl.program_id(0)
    tm = x_ref.shape[0]
    xb = x_ref[...].astype(_BF)

    @pl.when(i % tiles_per_seq == 0)
    def _():
        ubuf[0:HALO, :] = jnp.zeros((HALO, CONV_DIM), _F32)

    g3 = _dot(xb, wg3_ref[...])
    u = g3[:, CONV_DIM:2 * CONV_DIM] * g3[:, 2 * CONV_DIM:]
    ubuf[HALO:HALO + tm, :] = u
    cw = convw_ref[...]
    conv = (cw[2:3, :] * u + cw[1:2, :] * ubuf[HALO - 1:HALO - 1 + tm, :]
            + cw[0:1, :] * ubuf[HALO - 2:HALO - 2 + tm, :])
    ubuf[0:HALO, :] = ubuf[tm:tm + HALO, :]
    yc = _dot((g3[:, :CONV_DIM] * conv).astype(_BF), wcb_ref[...])
    gates = _dot(xb, wgates_ref[...])
    ycg_ref[...] = (jax.nn.sigmoid(gates[:, :D_MODEL]) * yc).astype(_BF)
    sga_ref[...] = jax.nn.sigmoid(gates[:, D_MODEL:]).astype(_BF)

    lat = _dot(xb, wlat_ref[...])
    cosb = cos_ref[...]
    sinb = sin_ref[...]
    q_lat = lat[:, :Q_LORA_RANK]
    rq = q_lat * lax.rsqrt(jnp.mean(q_lat * q_lat, axis=-1, keepdims=True) + RMS_EPS) * qg_ref[...]
    kv_lat = lat[:, Q_LORA_RANK:Q_LORA_RANK + KV_LORA_RANK]
    ckv = (kv_lat * lax.rsqrt(jnp.mean(kv_lat * kv_lat, axis=-1, keepdims=True) + RMS_EPS)
           * kvg_ref[...]).astype(_BF)
    o = Q_LORA_RANK + KV_LORA_RANK
    kpe_ref[...] = (lat[:, o:o + ROPE_PAD] * cosb + lat[:, o + ROPE_PAD:] * sinb).astype(_BF)
    kn_ref[...] = _dot(ckv, wuk_ref[...]).astype(_BF)
    v_ref[...] = _dot(ckv, wuv_ref[...]).astype(_BF)

    rqb = rq.astype(_BF)
    qa = _dot(rqb, wuqa_ref[...])
    qb = _dot(rqb, wuqb_ref[...])
    scale = QK_HEAD_DIM ** -0.5
    for h in range(N_HEADS):
        lo = h * QK_PAD
        q_ref[:, lo:lo + QK_NOPE_DIM] = (qa[:, lo:lo + QK_NOPE_DIM] * scale).astype(_BF)
        hi = (qa[:, lo + QK_NOPE_DIM:lo + QK_PAD] * cosb
              + qb[:, h * ROPE_PAD:(h + 1) * ROPE_PAD] * sinb)
        q_ref[:, lo + QK_NOPE_DIM:lo + QK_PAD] = (hi * scale).astype(_BF)


def _proj_call(x2d, cosb, sinb, wg3, wlat, wgates, convw, qg, kvg, wuqa, wuqb, wuk, wuv, wcb, seq):
    t = x2d.shape[0]
    tm = PROJ_TM
    row = lambda w: pl.BlockSpec((tm, w), lambda i: (i, 0))
    outs = [
        jax.ShapeDtypeStruct((t, N_HEADS * QK_PAD), _BF),
        jax.ShapeDtypeStruct((t, N_HEADS * QK_NOPE_DIM), _BF),
        jax.ShapeDtypeStruct((t, ROPE_PAD), _BF),
        jax.ShapeDtypeStruct((t, N_HEADS * V_HEAD_DIM), _BF),
        jax.ShapeDtypeStruct((t, D_MODEL), _BF),
        jax.ShapeDtypeStruct((t, D_MODEL), _BF),
    ]
    return pl.pallas_call(
        functools.partial(_proj_kernel, seq // tm),
        grid=(t // tm,),
        in_specs=[row(D_MODEL), row(ROPE_PAD), row(ROPE_PAD)]
        + [_const_spec(a.shape) for a in (wg3, wlat, wgates, convw, qg, kvg, wuqa, wuqb, wuk, wuv, wcb)],
        out_specs=[row(s.shape[1]) for s in outs],
        out_shape=outs,
        scratch_shapes=[pltpu.VMEM((tm + HALO, CONV_DIM), _F32)],
        compiler_params=pltpu.CompilerParams(dimension_semantics=("arbitrary",),
                                             vmem_limit_bytes=VMEM_LIMIT),
        name="proj_conv_qkv",
    )(x2d, cosb, sinb, wg3, wlat, wgates, convw, qg, kvg, wuqa, wuqb, wuk, wuv, wcb)


def _attn_kernel(q_ref, kn_ref, kpe_ref, v_ref, o_ref, kfull):
    i = pl.program_id(2)
    tq = q_ref.shape[0]
    tk = ATTN_TK

    @pl.when(i == 0)
    def _():
        kfull[:, :QK_NOPE_DIM] = kn_ref[...]
        kfull[:, QK_NOPE_DIM:] = kpe_ref[...]

    q = q_ref[...]

    def step(start, carry, mask):
        m, l, acc = carry
        k = kfull[pl.ds(start, tk), :]
        s = lax.dot_general(q, k, (((1,), (1,)), ((), ())), preferred_element_type=_F32)
        if mask is not None:
            s = jnp.where(mask, s, NEG_BIG)
        m_new = jnp.maximum(m, jnp.max(s, axis=-1, keepdims=True))
        a = jnp.exp(m - m_new)
        p = jnp.exp(s - m_new)
        l = a * l + jnp.sum(p, axis=-1, keepdims=True)
        acc = a * acc + _dot(p.astype(_BF), v_ref[pl.ds(start, tk), :])
        return m_new, l, acc

    init = (jnp.full((tq, 1), NEG_BIG, _F32), jnp.zeros((tq, 1), _F32),
            jnp.zeros((tq, V_HEAD_DIM), _F32))
    nfull = i * (tq // tk)
    carry = lax.fori_loop(
        0, nfull, lambda j, c: step(pl.multiple_of(j * tk, tk), c, None), init)
    for d in range(tq // tk):
        qc = lax.broadcasted_iota(jnp.int32, (tq, tk), 0) // CHUNK
        kc = (lax.broadcasted_iota(jnp.int32, (tq, tk), 1) + d * tk) // CHUNK
        carry = step(pl.multiple_of(i * tq + d * tk, tk), carry, kc <= qc)
    _, l, acc = carry
    o_ref[...] = (acc / l).astype(_BF)


def _attn_call(q, kn, kpe, v):
    b, s, _ = q.shape
    tq = ATTN_TQ
    return pl.pallas_call(
        _attn_kernel,
        grid=(b, N_HEADS, s // tq),
        in_specs=[
            pl.BlockSpec((None, tq, QK_PAD), lambda bi, h, i: (bi, i, h)),
            pl.BlockSpec((None, s, QK_NOPE_DIM), lambda bi, h, i: (bi, 0, h)),
            pl.BlockSpec((None, s, ROPE_PAD), lambda bi, h, i: (bi, 0, 0)),
            pl.BlockSpec((None, s, V_HEAD_DIM), lambda bi, h, i: (bi, 0, h)),
        ],
        out_specs=pl.BlockSpec((None, tq, V_HEAD_DIM), lambda bi, h, i: (bi, i, h)),
        out_shape=jax.ShapeDtypeStruct((b, s, N_HEADS * V_HEAD_DIM), _BF),
        scratch_shapes=[pltpu.VMEM((s, QK_PAD), _BF)],
        compiler_params=pltpu.CompilerParams(
            dimension_semantics=("arbitrary", "arbitrary", "arbitrary"),
            vmem_limit_bytes=VMEM_LIMIT),
        name="chunk_causal_attention",
    )(q, kn, kpe, v)


def _layer_norm(z, g, b):
    mu = jnp.mean(z, axis=-1, keepdims=True)
    zc = z - mu
    var = jnp.mean(zc * zc, axis=-1, keepdims=True)
    return zc * lax.rsqrt(var + LN_EPS) * g + b


def _post_kernel(attn_ref, ycg_ref, sga_ref, x_ref, wab_ref, wout_ref, g1_ref, b1_ref,
                 wr_ref, br_ref, h1_ref, h1b_ref, idx_ref, tw_ref):
    tm = x_ref.shape[0]
    y_attn = _dot(attn_ref[...], wab_ref[...])
    merged = ycg_ref[...].astype(_F32) + sga_ref[...].astype(_F32) * y_attn
    mix = _dot(merged.astype(_BF), wout_ref[...])
    h1 = _layer_norm(DEEPNORM_ALPHA * x_ref[...] + mix, g1_ref[...], b1_ref[...])
    h1_ref[...] = h1
    h1b_ref[...] = h1.astype(_BF)

    logits = jnp.dot(h1, wr_ref[...], preferred_element_type=_F32,
                     precision=lax.Precision.HIGHEST) + br_ref[...]
    lane = lax.broadcasted_iota(jnp.int32, (tm, N_EXPERTS), 1)
    out_lane = lax.broadcasted_iota(jnp.int32, (tm, 128), 1)
    idx_out = jnp.zeros((tm, 128), jnp.int32)
    val_out = jnp.zeros((tm, 128), _F32)
    work = logits
    top0 = None
    denom = jnp.zeros((tm, 1), _F32)
    for k in range(TOP_K):
        mx = jnp.max(work, axis=-1, keepdims=True)
        sel = jnp.min(jnp.where(work == mx, lane, N_EXPERTS), axis=-1, keepdims=True)
        if k == 0:
            top0 = mx
        e = jnp.exp(mx - top0)
        denom = denom + e
        idx_out = jnp.where(out_lane == k, sel, idx_out)
        val_out = jnp.where(out_lane == k, e, val_out)
        work = jnp.where(lane == sel, -jnp.inf, work)
    idx_ref[...] = idx_out
    tw_ref[...] = val_out / denom


def _post_call(attn, ycg, sga, x2d, wab, wout, g1, b1, wr, br):
    t = x2d.shape[0]
    tm = POST_TM
    row = lambda w: pl.BlockSpec((tm, w), lambda i: (i, 0))
    outs = [
        jax.ShapeDtypeStruct((t, D_MODEL), _F32),
        jax.ShapeDtypeStruct((t, D_MODEL), _BF),
        jax.ShapeDtypeStruct((t, 128), jnp.int32),
        jax.ShapeDtypeStruct((t, 128), _F32),
    ]
    return pl.pallas_call(
        _post_kernel,
        grid=(t // tm,),
        in_specs=[row(D_MODEL)] * 4 + [_const_spec(a.shape) for a in (wab, wout, g1, b1, wr, br)],
        out_specs=[row(s.shape[1]) for s in outs],
        out_shape=outs,
        compiler_params=pltpu.CompilerParams(dimension_semantics=("arbitrary",),
                                             vmem_limit_bytes=VMEM_LIMIT),
        name="merge_out_ln1_router",
    )(attn, ycg, sga, x2d, wab, wout, g1, b1, wr, br)


def _moe_kernel(be_ref, nused_ref, xs_ref, wgu_ref, bgu_ref, wd_ref, bd_ref, y_ref, wgu_bf, wd_bf):
    i = pl.program_id(0)
    prev = be_ref[jnp.maximum(i - 1, 0)]
    fresh = jnp.logical_or(i == 0, be_ref[i] != prev)
    used = i < nused_ref[0]

    @pl.when(jnp.logical_and(fresh, used))
    def _():
        wgu_bf[...] = wgu_ref[...].astype(_BF)
        wd_bf[...] = wd_ref[...].astype(_BF)

    @pl.when(used)
    def _():
        gu = _dot(xs_ref[...], wgu_bf[...]) + bgu_ref[...]
        gate = jnp.minimum(gu[:, :D_FF_EXPERT], SWIGLU_LIMIT)
        up = jnp.clip(gu[:, D_FF_EXPERT:], -SWIGLU_LIMIT, SWIGLU_LIMIT)
        hid = (up + 1.0) * (gate * jax.nn.sigmoid(SWIGLU_ALPHA * gate))
        y_ref[...] = (_dot(hid.astype(_BF), wd_bf[...]) + bd_ref[...]).astype(_BF)

    @pl.when(jnp.logical_not(used))
    def _():
        y_ref[...] = jnp.zeros(y_ref.shape, _BF)


def _moe_call(block_e, n_used, xs, wgu, bgu, wd, bd):
    n_slots = xs.shape[0]
    bm = MOE_BM
    n_blocks = n_slots // bm
    grid_spec = pltpu.PrefetchScalarGridSpec(
        num_scalar_prefetch=2,
        grid=(n_blocks,),
        in_specs=[
            pl.BlockSpec((bm, D_MODEL), lambda i, be, nu: (jnp.minimum(i, nu[0] - 1), 0)),
            pl.BlockSpec((None, D_MODEL, 2 * D_FF_EXPERT), lambda i, be, nu: (be[i], 0, 0)),
            pl.BlockSpec((None, 1, 2 * D_FF_EXPERT), lambda i, be, nu: (be[i], 0, 0)),
            pl.BlockSpec((None, D_FF_EXPERT, D_MODEL), lambda i, be, nu: (be[i], 0, 0)),
            pl.BlockSpec((None, 1, D_MODEL), lambda i, be, nu: (be[i], 0, 0)),
        ],
        out_specs=pl.BlockSpec((bm, D_MODEL), lambda i, be, nu: (i, 0)),
        scratch_shapes=[pltpu.VMEM((D_MODEL, 2 * D_FF_EXPERT), _BF),
                        pltpu.VMEM((D_FF_EXPERT, D_MODEL), _BF)],
    )
    return pl.pallas_call(
        _moe_kernel,
        grid_spec=grid_spec,
        out_shape=jax.ShapeDtypeStruct((n_slots, D_MODEL), _BF),
        compiler_params=pltpu.CompilerParams(dimension_semantics=("arbitrary",),
                                             vmem_limit_bytes=VMEM_LIMIT),
        name="expert_ffn",
    )(block_e, n_used, xs, wgu, bgu, wd, bd)


def _combine_kernel(yg_ref, tw_ref, h1_ref, g2_ref, b2_ref, o_ref):
    tw = tw_ref[...]
    ffn = tw[:, 0:1] * yg_ref[:, 0:D_MODEL].astype(_F32)
    for k in range(1, TOP_K):
        ffn = ffn + tw[:, k:k + 1] * yg_ref[:, k * D_MODEL:(k + 1) * D_MODEL].astype(_F32)
    o_ref[...] = _layer_norm(DEEPNORM_ALPHA * h1_ref[...] + ffn, g2_ref[...], b2_ref[...])


def _combine_call(yg, tw, h1, g2, b2):
    t = h1.shape[0]
    tm = COMB_TM
    row = lambda w: pl.BlockSpec((tm, w), lambda i: (i, 0))
    return pl.pallas_call(
        _combine_kernel,
        grid=(t // tm,),
        in_specs=[row(TOP_K * D_MODEL), row(128), row(D_MODEL),
                  _const_spec(g2.shape), _const_spec(b2.shape)],
        out_specs=row(D_MODEL),
        out_shape=jax.ShapeDtypeStruct((t, D_MODEL), _F32),
        compiler_params=pltpu.CompilerParams(dimension_semantics=("arbitrary",),
                                             vmem_limit_bytes=VMEM_LIMIT),
        name="combine_ln2",
    )(yg, tw, h1, g2, b2)


def _prepare_weights(w_in, w_uq):
    o_q = 3 * CONV_DIM
    o_kv = o_q + Q_LORA_RANK
    o_pe = o_kv + KV_LORA_RANK
    o_gc = o_pe + QK_ROPE_DIM
    half = QK_ROPE_DIM // 2
    wg3 = w_in[:, :o_q]
    kpe = w_in[:, o_pe:o_gc]
    kpe_sw = jnp.concatenate([kpe[:, half:], kpe[:, :half]], axis=1)
    zpad = jnp.zeros((D_MODEL, ROPE_PAD - QK_ROPE_DIM), w_in.dtype)
    wlat = jnp.concatenate([w_in[:, o_q:o_pe], kpe, zpad, kpe_sw, zpad], axis=1)
    wgates = w_in[:, o_gc:]
    wq = w_uq.reshape(Q_LORA_RANK, N_HEADS, QK_HEAD_DIM)
    wuqa = jnp.pad(wq, ((0, 0), (0, 0), (0, QK_PAD - QK_HEAD_DIM)))
    pe = wq[:, :, QK_NOPE_DIM:]
    pe_sw = jnp.concatenate([pe[:, :, half:], pe[:, :, :half]], axis=2)
    wuqb = jnp.pad(pe_sw, ((0, 0), (0, 0), (0, ROPE_PAD - QK_ROPE_DIM)))
    return (wg3.astype(_BF), wlat.astype(_BF), wgates.astype(_BF),
            wuqa.reshape(Q_LORA_RANK, N_HEADS * QK_PAD).astype(_BF),
            wuqb.reshape(Q_LORA_RANK, N_HEADS * ROPE_PAD).astype(_BF))


def _rope_tables(positions):
    inv_freq = ROPE_BASE ** (-jnp.arange(0, QK_ROPE_DIM, 2, dtype=_F32) / QK_ROPE_DIM)
    ang = positions.astype(_F32).reshape(-1)[:, None] * inv_freq
    cos, sin = jnp.cos(ang), jnp.sin(ang)
    z = jnp.zeros((ang.shape[0], ROPE_PAD - QK_ROPE_DIM), _F32)
    return (jnp.concatenate([cos, cos, z], axis=1), jnp.concatenate([-sin, sin, z], axis=1))


def _route(top_idx, n_tok):
    bm = MOE_BM
    onehot = (top_idx[:, :, None] == jnp.arange(N_EXPERTS, dtype=jnp.int32)).sum(axis=1).astype(jnp.int32)
    incl = jnp.cumsum(onehot, axis=0)
    counts = incl[-1]
    pos = incl - onehot
    padded = (counts + bm - 1) // bm * bm
    padded_end = jnp.cumsum(padded)
    padded_start = padded_end - padded
    slot = jnp.take_along_axis(pos + padded_start[None, :], top_idx, axis=1)
    n_blocks = n_tok * TOP_K // bm + N_EXPERTS
    block_e = jnp.minimum(
        jnp.searchsorted(padded_end, jnp.arange(n_blocks, dtype=jnp.int32) * bm, side='right'),
        N_EXPERTS - 1).astype(jnp.int32)
    n_used = (padded_end[-1] // bm).astype(jnp.int32).reshape(1)
    return slot.astype(jnp.int32), block_e, n_used, n_blocks * bm


def kernel(x, positions, w_in, conv_w, q_norm_g, w_uq, kv_norm_g, w_uk, w_uv, w_conv_branch,
           w_attn_branch, w_out, ln1_g, ln1_b, w_router, b_router, w_gate_up, b_gate_up, w_down,
           b_down, ln2_g, ln2_b):
    b, s, d = x.shape
    t = b * s
    x2d = x.reshape(t, d)
    cosb, sinb = _rope_tables(positions)
    for l in range(DEPTH):
        wg3, wlat, wgates, wuqa, wuqb = _prepare_weights(w_in[l], w_uq[l])
        q, kn, kpe, v, ycg, sga = _proj_call(
            x2d, cosb, sinb, wg3, wlat, wgates, conv_w[l], q_norm_g[l][None, :], kv_norm_g[l][None, :],
            wuqa, wuqb, w_uk[l].astype(_BF), w_uv[l].astype(_BF), w_conv_branch[l].astype(_BF), s)
        attn = _attn_call(q.reshape(b, s, -1), kn.reshape(b, s, -1), kpe.reshape(b, s, -1),
                          v.reshape(b, s, -1)).reshape(t, -1)
        h1, h1b, idx_pad, tw_pad = _post_call(
            attn, ycg, sga, x2d, w_attn_branch[l].astype(_BF), w_out[l].astype(_BF),
            ln1_g[l][None, :], ln1_b[l][None, :], w_router[l], b_router[l][None, :])
        top_idx = idx_pad[:, :TOP_K]
        slot, block_e, n_used, n_slots = _route(top_idx, t)
        slot_tok = jnp.zeros((n_slots,), jnp.int32).at[slot.reshape(-1)].set(
            jnp.arange(t * TOP_K, dtype=jnp.int32) // TOP_K)
        xs = jnp.take(h1b, slot_tok, axis=0)
        y = _moe_call(block_e, n_used, xs, w_gate_up[l], b_gate_up[l][:, None, :],
                      w_down[l], b_down[l][:, None, :])
        yg = jnp.take(y, slot.reshape(-1), axis=0).reshape(t, TOP_K * D_MODEL)
        x2d = _combine_call(yg, tw_pad, h1, ln2_g[l][None, :], ln2_b[l][None, :])
    return x2d.reshape(b, s, d)
```

```python
import functools

import numpy as np
import jax
import jax.numpy as jnp
from jax import lax
from jax.experimental import pallas as pl
from jax.experimental.pallas import tpu as pltpu
from jax.experimental.pallas import tpu_sc as plsc

D_MODEL = 1024
CHUNK = 64
CONV_DIM = D_MODEL
CONV_WIDTH = 3
N_HEADS = 8
QK_NOPE_DIM = 128
QK_ROPE_DIM = 64
V_HEAD_DIM = 128
QK_HEAD_DIM = QK_NOPE_DIM + QK_ROPE_DIM
Q_LORA_RANK = 384
KV_LORA_RANK = 256
ROPE_BASE = 10000.0
N_EXPERTS = 32
TOP_K = 4
D_FF_EXPERT = D_MODEL
SWIGLU_LIMIT = 7.0
SWIGLU_ALPHA = 1.702
LN_EPS = 1e-5
RMS_EPS = 1e-6
DEPTH = 1
DEEPNORM_ALPHA = (2 * DEPTH) ** 0.25

QK_PAD = 256
ROPE_PAD = 128
LAT_DIM = Q_LORA_RANK + KV_LORA_RANK + 2 * ROPE_PAD

PROJ_TM = 256
ATTN_TQ = 512
ATTN_TK = 512
POST_TM = 512
MOE_BM = 256
COMB_TM = 512
SC_GATHER_WINDOW = 128
SC_ROW_SPLIT = 2
PACKED = D_MODEL // 2
HALO = 8
NEG_BIG = -1e30
VMEM_LIMIT = 56 * 1024 * 1024

_BF = jnp.bfloat16
_F32 = jnp.float32


def _dot(a, b):
    return jnp.dot(a, b, preferred_element_type=_F32)


def _const_spec(shape):
    nd = len(shape)
    return pl.BlockSpec(shape, lambda *_: (0,) * nd, pipeline_mode=pl.Buffered(1))


def _pack_rows(a):
    half = a.shape[1] // 2
    lo = lax.bitcast_convert_type(a[:, :half].astype(_BF).astype(_F32), jnp.int32)
    hi = lax.bitcast_convert_type(a[:, half:].astype(_BF).astype(_F32), jnp.int32)
    return lax.shift_right_logical(lo, 16) | (hi & jnp.int32(-65536))


def _unpack_rows(p):
    lo = lax.bitcast_convert_type(lax.shift_left(p, 16), _F32)
    hi = lax.bitcast_convert_type(p & jnp.int32(-65536), _F32)
    return lo, hi


def _sc_gather(table, idx):
    full_rows, full_width = idx.shape[0], table.shape[1]
    table = table.reshape(table.shape[0] * SC_ROW_SPLIT, full_width // SC_ROW_SPLIT)
    idx = (idx[:, None] * SC_ROW_SPLIT + jnp.arange(SC_ROW_SPLIT, dtype=jnp.int32)).reshape(-1)
    return _sc_gather_rows(table, idx).reshape(full_rows, full_width)


def _sc_gather_rows(table, idx):
    n_rows = idx.shape[0]
    width = table.shape[1]
    window = SC_GATHER_WINDOW
    mesh = plsc.VectorSubcoreMesh(core_axis_name="core", subcore_axis_name="subcore")

    @functools.partial(pl.kernel, mesh=mesh,
                       out_type=jax.ShapeDtypeStruct((n_rows, width), table.dtype))
    def gather_kernel(table_hbm, idx_hbm, out_hbm):
        def body(idx_vmem, out_vmem):
            pltpu.sync_copy(table_hbm.at[idx_vmem.at[0]], out_vmem)

        pltpu.emit_pipeline(
            body,
            grid=(n_rows // window,),
            in_specs=[pl.BlockSpec((1, window), lambda i: (0, i))],
            out_specs=[pl.BlockSpec((window, width), lambda i: (i, 0))],
            core_axis_name=("core", "subcore"),
            dimension_semantics=(pltpu.PARALLEL,),
        )(idx_hbm, out_hbm)

    return gather_kernel(table, idx.reshape(1, n_rows))


def _proj_kernel(tiles_per_seq,
                 x_ref, cos_ref, sin_ref, wg3_ref, wlat_ref, wgates_ref, convw_ref,
                 qg_ref, kvg_ref, wuqa_ref, wuqb_ref, wuk_ref, wuv_ref, wcb_ref,
                 q_ref, kn_ref, kpe_ref, v_ref, ycg_ref, sga_ref, ubuf):
    i = pl.program_id(0)
    tm = x_ref.shape[0]
    xb = x_ref[...].astype(_BF)

    @pl.when(i % tiles_per_seq == 0)
    def _():
        ubuf[0:HALO, :] = jnp.zeros((HALO, CONV_DIM), _F32)

    g3 = _dot(xb, wg3_ref[...])
    u = g3[:, CONV_DIM:2 * CONV_DIM] * g3[:, 2 * CONV_DIM:]
    ubuf[HALO:HALO + tm, :] = u
    cw = convw_ref[...]
    conv = (cw[2:3, :] * u + cw[1:2, :] * ubuf[HALO - 1:HALO - 1 + tm, :]
            + cw[0:1, :] * ubuf[HALO - 2:HALO - 2 + tm, :])
    ubuf[0:HALO, :] = ubuf[tm:tm + HALO, :]
    yc = _dot((g3[:, :CONV_DIM] * conv).astype(_BF), wcb_ref[...])
    gates = _dot(xb, wgates_ref[...])
    ycg_ref[...] = (jax.nn.sigmoid(gates[:, :D_MODEL]) * yc).astype(_BF)
    sga_ref[...] = jax.nn.sigmoid(gates[:, D_MODEL:]).astype(_BF)

    lat = _dot(xb, wlat_ref[...])
    cosb = cos_ref[...]
    sinb = sin_ref[...]
    q_lat = lat[:, :Q_LORA_RANK]
    rq = q_lat * lax.rsqrt(jnp.mean(q_lat * q_lat, axis=-1, keepdims=True) + RMS_EPS) * qg_ref[...]
    kv_lat = lat[:, Q_LORA_RANK:Q_LORA_RANK + KV_LORA_RANK]
    ckv = (kv_lat * lax.rsqrt(jnp.mean(kv_lat * kv_lat, axis=-1, keepdims=True) + RMS_EPS)
           * kvg_ref[...]).astype(_BF)
    o = Q_LORA_RANK + KV_LORA_RANK
    kpe_ref[...] = (lat[:, o:o + ROPE_PAD] * cosb + lat[:, o + ROPE_PAD:] * sinb).astype(_BF)
    kn_ref[...] = _dot(ckv, wuk_ref[...]).astype(_BF)
    v_ref[...] = _dot(ckv, wuv_ref[...]).astype(_BF)

    rqb = rq.astype(_BF)
    qa = _dot(rqb, wuqa_ref[...])
    qb = _dot(rqb, wuqb_ref[...])
    scale = QK_HEAD_DIM ** -0.5
    for h in range(N_HEADS):
        lo = h * QK_PAD
        q_ref[:, lo:lo + QK_NOPE_DIM] = (qa[:, lo:lo + QK_NOPE_DIM] * scale).astype(_BF)
        hi = (qa[:, lo + QK_NOPE_DIM:lo + QK_PAD] * cosb
              + qb[:, h * ROPE_PAD:(h + 1) * ROPE_PAD] * sinb)
        q_ref[:, lo + QK_NOPE_DIM:lo + QK_PAD] = (hi * scale).astype(_BF)


def _proj_call(x2d, cosb, sinb, wg3, wlat, wgates, convw, qg, kvg, wuqa, wuqb, wuk, wuv, wcb, seq):
    t = x2d.shape[0]
    tm = PROJ_TM
    row = lambda w: pl.BlockSpec((tm, w), lambda i: (i, 0))
    outs = [
        jax.ShapeDtypeStruct((t, N_HEADS * QK_PAD), _BF),
        jax.ShapeDtypeStruct((t, N_HEADS * QK_NOPE_DIM), _BF),
        jax.ShapeDtypeStruct((t, ROPE_PAD), _BF),
        jax.ShapeDtypeStruct((t, N_HEADS * V_HEAD_DIM), _BF),
        jax.ShapeDtypeStruct((t, D_MODEL), _BF),
        jax.ShapeDtypeStruct((t, D_MODEL), _BF),
    ]
    return pl.pallas_call(
        functools.partial(_proj_kernel, seq // tm),
        grid=(t // tm,),
        in_specs=[row(D_MODEL), row(ROPE_PAD), row(ROPE_PAD)]
        + [_const_spec(a.shape) for a in (wg3, wlat, wgates, convw, qg, kvg, wuqa, wuqb, wuk, wuv, wcb)],
        out_specs=[row(s.shape[1]) for s in outs],
        out_shape=outs,
        scratch_shapes=[pltpu.VMEM((tm + HALO, CONV_DIM), _F32)],
        compiler_params=pltpu.CompilerParams(dimension_semantics=("arbitrary",),
                                             vmem_limit_bytes=VMEM_LIMIT),
        name="proj_conv_qkv",
    )(x2d, cosb, sinb, wg3, wlat, wgates, convw, qg, kvg, wuqa, wuqb, wuk, wuv, wcb)


def _attn_kernel(q_ref, kn_ref, kpe_ref, v_ref, o_ref, kfull):
    i = pl.program_id(2)
    tq = q_ref.shape[0]
    tk = ATTN_TK

    @pl.when(i == 0)
    def _():
        kfull[:, :QK_NOPE_DIM] = kn_ref[...]
        kfull[:, QK_NOPE_DIM:] = kpe_ref[...]

    q = q_ref[...]

    def step(start, carry, mask):
        m, l, acc = carry
        k = kfull[pl.ds(start, tk), :]
        s = lax.dot_general(q, k, (((1,), (1,)), ((), ())), preferred_element_type=_F32)
        if mask is not None:
            s = jnp.where(mask, s, NEG_BIG)
        m_new = jnp.maximum(m, jnp.max(s, axis=-1, keepdims=True))
        a = jnp.exp(m - m_new)
        p = jnp.exp(s - m_new)
        l = a * l + jnp.sum(p, axis=-1, keepdims=True)
        acc = a * acc + _dot(p.astype(_BF), v_ref[pl.ds(start, tk), :])
        return m_new, l, acc

    init = (jnp.full((tq, 1), NEG_BIG, _F32), jnp.zeros((tq, 1), _F32),
            jnp.zeros((tq, V_HEAD_DIM), _F32))
    nfull = i * (tq // tk)
    carry = lax.fori_loop(
        0, nfull, lambda j, c: step(pl.multiple_of(j * tk, tk), c, None), init)
    for d in range(tq // tk):
        qc = lax.broadcasted_iota(jnp.int32, (tq, tk), 0) // CHUNK
        kc = (lax.broadcasted_iota(jnp.int32, (tq, tk), 1) + d * tk) // CHUNK
        carry = step(pl.multiple_of(i * tq + d * tk, tk), carry, kc <= qc)
    _, l, acc = carry
    o_ref[...] = (acc / l).astype(_BF)


def _attn_call(q, kn, kpe, v):
    b, s, _ = q.shape
    tq = ATTN_TQ
    return pl.pallas_call(
        _attn_kernel,
        grid=(b, N_HEADS, s // tq),
        in_specs=[
            pl.BlockSpec((None, tq, QK_PAD), lambda bi, h, i: (bi, i, h)),
            pl.BlockSpec((None, s, QK_NOPE_DIM), lambda bi, h, i: (bi, 0, h)),
            pl.BlockSpec((None, s, ROPE_PAD), lambda bi, h, i: (bi, 0, 0)),
            pl.BlockSpec((None, s, V_HEAD_DIM), lambda bi, h, i: (bi, 0, h)),
        ],
        out_specs=pl.BlockSpec((None, tq, V_HEAD_DIM), lambda bi, h, i: (bi, i, h)),
        out_shape=jax.ShapeDtypeStruct((b, s, N_HEADS * V_HEAD_DIM), _BF),
        scratch_shapes=[pltpu.VMEM((s, QK_PAD), _BF)],
        compiler_params=pltpu.CompilerParams(
            dimension_semantics=("arbitrary", "arbitrary", "arbitrary"),
            vmem_limit_bytes=VMEM_LIMIT),
        name="chunk_causal_attention",
    )(q, kn, kpe, v)


def _layer_norm(z, g, b):
    mu = jnp.mean(z, axis=-1, keepdims=True)
    zc = z - mu
    var = jnp.mean(zc * zc, axis=-1, keepdims=True)
    return zc * lax.rsqrt(var + LN_EPS) * g + b


def _post_kernel(attn_ref, ycg_ref, sga_ref, x_ref, wab_ref, wout_ref, g1_ref, b1_ref,
                 wr_ref, br_ref, h1_ref, h1p_ref, idx_ref, tw_ref):
    tm = x_ref.shape[0]
    y_attn = _dot(attn_ref[...], wab_ref[...])
    merged = ycg_ref[...].astype(_F32) + sga_ref[...].astype(_F32) * y_attn
    mix = _dot(merged.astype(_BF), wout_ref[...])
    h1 = _layer_norm(DEEPNORM_ALPHA * x_ref[...] + mix, g1_ref[...], b1_ref[...])
    h1_ref[...] = h1
    h1p_ref[...] = _pack_rows(h1)

    logits = jnp.dot(h1, wr_ref[...], preferred_element_type=_F32,
                     precision=lax.Precision.HIGHEST) + br_ref[...]
    lane = lax.broadcasted_iota(jnp.int32, (tm, N_EXPERTS), 1)
    out_lane = lax.broadcasted_iota(jnp.int32, (tm, 128), 1)
    idx_out = jnp.zeros((tm, 128), jnp.int32)
    val_out = jnp.zeros((tm, 128), _F32)
    work = logits
    top0 = None
    denom = jnp.zeros((tm, 1), _F32)
    for k in range(TOP_K):
        mx = jnp.max(work, axis=-1, keepdims=True)
        sel = jnp.min(jnp.where(work == mx, lane, N_EXPERTS), axis=-1, keepdims=True)
        if k == 0:
            top0 = mx
        e = jnp.exp(mx - top0)
        denom = denom + e
        idx_out = jnp.where(out_lane == k, sel, idx_out)
        val_out = jnp.where(out_lane == k, e, val_out)
        work = jnp.where(lane == sel, -jnp.inf, work)
    idx_ref[...] = idx_out
    tw_ref[...] = val_out / denom


def _post_call(attn, ycg, sga, x2d, wab, wout, g1, b1, wr, br):
    t = x2d.shape[0]
    tm = POST_TM
    row = lambda w: pl.BlockSpec((tm, w), lambda i: (i, 0))
    outs = [
        jax.ShapeDtypeStruct((t, D_MODEL), _F32),
        jax.ShapeDtypeStruct((t, PACKED), jnp.int32),
        jax.ShapeDtypeStruct((t, 128), jnp.int32),
        jax.ShapeDtypeStruct((t, 128), _F32),
    ]
    return pl.pallas_call(
        _post_kernel,
        grid=(t // tm,),
        in_specs=[row(D_MODEL)] * 4 + [_const_spec(a.shape) for a in (wab, wout, g1, b1, wr, br)],
        out_specs=[row(s.shape[1]) for s in outs],
        out_shape=outs,
        compiler_params=pltpu.CompilerParams(dimension_semantics=("arbitrary",),
                                             vmem_limit_bytes=VMEM_LIMIT),
        name="merge_out_ln1_router",
    )(attn, ycg, sga, x2d, wab, wout, g1, b1, wr, br)


def _moe_kernel(be_ref, nused_ref, xs_ref, wgu_ref, bgu_ref, wd_ref, bd_ref, y_ref, wgu_bf, wd_bf):
    i = pl.program_id(0)
    prev = be_ref[jnp.maximum(i - 1, 0)]
    fresh = jnp.logical_or(i == 0, be_ref[i] != prev)
    used = i < nused_ref[0]

    @pl.when(jnp.logical_and(fresh, used))
    def _():
        wgu_bf[...] = wgu_ref[...].astype(_BF)
        wd_bf[...] = wd_ref[...].astype(_BF)

    @pl.when(used)
    def _():
        xs = jnp.concatenate(_unpack_rows(xs_ref[...]), axis=1).astype(_BF)
        gu = _dot(xs, wgu_bf[...]) + bgu_ref[...]
        gate = jnp.minimum(gu[:, :D_FF_EXPERT], SWIGLU_LIMIT)
        up = jnp.clip(gu[:, D_FF_EXPERT:], -SWIGLU_LIMIT, SWIGLU_LIMIT)
        hid = (up + 1.0) * (gate * jax.nn.sigmoid(SWIGLU_ALPHA * gate))
        y_ref[...] = _pack_rows(_dot(hid.astype(_BF), wd_bf[...]) + bd_ref[...])

    @pl.when(jnp.logical_not(used))
    def _():
        y_ref[...] = jnp.zeros(y_ref.shape, jnp.int32)


def _moe_call(block_e, n_used, xs, wgu, bgu, wd, bd):
    n_slots = xs.shape[0]
    bm = MOE_BM
    n_blocks = n_slots // bm
    grid_spec = pltpu.PrefetchScalarGridSpec(
        num_scalar_prefetch=2,
        grid=(n_blocks,),
        in_specs=[
            pl.BlockSpec((bm, PACKED), lambda i, be, nu: (jnp.minimum(i, nu[0] - 1), 0)),
            pl.BlockSpec((None, D_MODEL, 2 * D_FF_EXPERT), lambda i, be, nu: (be[i], 0, 0)),
            pl.BlockSpec((None, 1, 2 * D_FF_EXPERT), lambda i, be, nu: (be[i], 0, 0)),
            pl.BlockSpec((None, D_FF_EXPERT, D_MODEL), lambda i, be, nu: (be[i], 0, 0)),
            pl.BlockSpec((None, 1, D_MODEL), lambda i, be, nu: (be[i], 0, 0)),
        ],
        out_specs=pl.BlockSpec((bm, PACKED), lambda i, be, nu: (i, 0)),
        scratch_shapes=[pltpu.VMEM((D_MODEL, 2 * D_FF_EXPERT), _BF),
                        pltpu.VMEM((D_FF_EXPERT, D_MODEL), _BF)],
    )
    return pl.pallas_call(
        _moe_kernel,
        grid_spec=grid_spec,
        out_shape=jax.ShapeDtypeStruct((n_slots, PACKED), jnp.int32),
        compiler_params=pltpu.CompilerParams(dimension_semantics=("arbitrary",),
                                             vmem_limit_bytes=VMEM_LIMIT),
        name="expert_ffn",
    )(block_e, n_used, xs, wgu, bgu, wd, bd)


def _combine_kernel(yg_ref, tw_ref, h1_ref, g2_ref, b2_ref, o_ref):
    tw = tw_ref[...]
    lo = jnp.zeros((tw.shape[0], PACKED), _F32)
    hi = jnp.zeros((tw.shape[0], PACKED), _F32)
    for k in range(TOP_K):
        y_lo, y_hi = _unpack_rows(yg_ref[:, k * PACKED:(k + 1) * PACKED])
        lo = lo + tw[:, k:k + 1] * y_lo
        hi = hi + tw[:, k:k + 1] * y_hi
    ffn = jnp.concatenate([lo, hi], axis=1)
    o_ref[...] = _layer_norm(DEEPNORM_ALPHA * h1_ref[...] + ffn, g2_ref[...], b2_ref[...])


def _combine_call(yg, tw, h1, g2, b2):
    t = h1.shape[0]
    tm = COMB_TM
    row = lambda w: pl.BlockSpec((tm, w), lambda i: (i, 0))
    return pl.pallas_call(
        _combine_kernel,
        grid=(t // tm,),
        in_specs=[row(TOP_K * PACKED), row(128), row(D_MODEL),
                  _const_spec(g2.shape), _const_spec(b2.shape)],
        out_specs=row(D_MODEL),
        out_shape=jax.ShapeDtypeStruct((t, D_MODEL), _F32),
        compiler_params=pltpu.CompilerParams(dimension_semantics=("arbitrary",),
                                             vmem_limit_bytes=VMEM_LIMIT),
        name="combine_ln2",
    )(yg, tw, h1, g2, b2)


def _prepare_weights(w_in, w_uq):
    o_q = 3 * CONV_DIM
    o_kv = o_q + Q_LORA_RANK
    o_pe = o_kv + KV_LORA_RANK
    o_gc = o_pe + QK_ROPE_DIM
    half = QK_ROPE_DIM // 2
    wg3 = w_in[:, :o_q]
    kpe = w_in[:, o_pe:o_gc]
    kpe_sw = jnp.concatenate([kpe[:, half:], kpe[:, :half]], axis=1)
    zpad = jnp.zeros((D_MODEL, ROPE_PAD - QK_ROPE_DIM), w_in.dtype)
    wlat = jnp.concatenate([w_in[:, o_q:o_pe], kpe, zpad, kpe_sw, zpad], axis=1)
    wgates = w_in[:, o_gc:]
    wq = w_uq.reshape(Q_LORA_RANK, N_HEADS, QK_HEAD_DIM)
    wuqa = jnp.pad(wq, ((0, 0), (0, 0), (0, QK_PAD - QK_HEAD_DIM)))
    pe = wq[:, :, QK_NOPE_DIM:]
    pe_sw = jnp.concatenate([pe[:, :, half:], pe[:, :, :half]], axis=2)
    wuqb = jnp.pad(pe_sw, ((0, 0), (0, 0), (0, ROPE_PAD - QK_ROPE_DIM)))
    return (wg3.astype(_BF), wlat.astype(_BF), wgates.astype(_BF),
            wuqa.reshape(Q_LORA_RANK, N_HEADS * QK_PAD).astype(_BF),
            wuqb.reshape(Q_LORA_RANK, N_HEADS * ROPE_PAD).astype(_BF))


def _rope_tables(positions):
    inv_freq = ROPE_BASE ** (-jnp.arange(0, QK_ROPE_DIM, 2, dtype=_F32) / QK_ROPE_DIM)
    ang = positions.astype(_F32).reshape(-1)[:, None] * inv_freq
    cos, sin = jnp.cos(ang), jnp.sin(ang)
    z = jnp.zeros((ang.shape[0], ROPE_PAD - QK_ROPE_DIM), _F32)
    return (jnp.concatenate([cos, cos, z], axis=1), jnp.concatenate([-sin, sin, z], axis=1))


def _route(top_idx, n_tok):
    bm = MOE_BM
    onehot = (top_idx[:, :, None] == jnp.arange(N_EXPERTS, dtype=jnp.int32)).sum(axis=1).astype(jnp.int32)
    incl = jnp.cumsum(onehot, axis=0)
    counts = incl[-1]
    pos = incl - onehot
    padded = (counts + bm - 1) // bm * bm
    padded_end = jnp.cumsum(padded)
    padded_start = padded_end - padded
    slot = jnp.take_along_axis(pos + padded_start[None, :], top_idx, axis=1)
    n_blocks = n_tok * TOP_K // bm + N_EXPERTS
    block_start = jnp.arange(n_blocks, dtype=jnp.int32) * bm
    block_e = jnp.minimum((padded_end[None, :] <= block_start[:, None]).sum(axis=1),
                          N_EXPERTS - 1).astype(jnp.int32)
    n_used = (padded_end[-1] // bm).astype(jnp.int32).reshape(1)
    return slot.astype(jnp.int32), block_e, n_used, n_blocks * bm


def kernel(x, positions, w_in, conv_w, q_norm_g, w_uq, kv_norm_g, w_uk, w_uv, w_conv_branch,
           w_attn_branch, w_out, ln1_g, ln1_b, w_router, b_router, w_gate_up, b_gate_up, w_down,
           b_down, ln2_g, ln2_b):
    b, s, d = x.shape
    t = b * s
    x2d = x.reshape(t, d)
    cosb, sinb = _rope_tables(positions)
    for l in range(DEPTH):
        wg3, wlat, wgates, wuqa, wuqb = _prepare_weights(w_in[l], w_uq[l])
        q, kn, kpe, v, ycg, sga = _proj_call(
            x2d, cosb, sinb, wg3, wlat, wgates, conv_w[l], q_norm_g[l][None, :], kv_norm_g[l][None, :],
            wuqa, wuqb, w_uk[l].astype(_BF), w_uv[l].astype(_BF), w_conv_branch[l].astype(_BF), s)
        attn = _attn_call(q.reshape(b, s, -1), kn.reshape(b, s, -1), kpe.reshape(b, s, -1),
                          v.reshape(b, s, -1)).reshape(t, -1)
        h1, h1p, idx_pad, tw_pad = _post_call(
            attn, ycg, sga, x2d, w_attn_branch[l].astype(_BF), w_out[l].astype(_BF),
            ln1_g[l][None, :], ln1_b[l][None, :], w_router[l], b_router[l][None, :])
        top_idx = idx_pad[:, :TOP_K]
        slot, block_e, n_used, n_slots = _route(top_idx, t)
        slot_tok = jnp.zeros((n_slots,), jnp.int32).at[slot.reshape(-1)].set(
            jnp.arange(t * TOP_K, dtype=jnp.int32) // TOP_K)
        xs = _sc_gather(h1p, slot_tok)
        y = _moe_call(block_e, n_used, xs, w_gate_up[l], b_gate_up[l][:, None, :],
                      w_down[l], b_down[l][:, None, :])
        yg = _sc_gather(y, slot.reshape(-1)).reshape(t, TOP_K * PACKED)
        x2d = _combine_call(yg, tw_pad, h1, ln2_g[l][None, :], ln2_b[l][None, :])
    return x2d.reshape(b, s, d)
```

```python
import functools

import numpy as np
import jax
import jax.numpy as jnp
from jax import lax
from jax.experimental import pallas as pl
from jax.experimental.pallas import tpu as pltpu
from jax.experimental.pallas import tpu_sc as plsc

D_MODEL = 1024
CHUNK = 64
CONV_DIM = D_MODEL
CONV_WIDTH = 3
N_HEADS = 8
QK_NOPE_DIM = 128
QK_ROPE_DIM = 64
V_HEAD_DIM = 128
QK_HEAD_DIM = QK_NOPE_DIM + QK_ROPE_DIM
Q_LORA_RANK = 384
KV_LORA_RANK = 256
ROPE_BASE = 10000.0
N_EXPERTS = 32
TOP_K = 4
D_FF_EXPERT = D_MODEL
SWIGLU_LIMIT = 7.0
SWIGLU_ALPHA = 1.702
LN_EPS = 1e-5
RMS_EPS = 1e-6
DEPTH = 1
DEEPNORM_ALPHA = (2 * DEPTH) ** 0.25

QK_PAD = 256
ROPE_PAD = 128
LAT_DIM = Q_LORA_RANK + KV_LORA_RANK + 2 * ROPE_PAD

PROJ_TM = 256
ATTN_TQ = 512
ATTN_TK = 512
POST_TM = 512
MOE_BM = 256
COMB_TM = 512
SC_WINDOW = 128
PACKED = D_MODEL // 2
HALF = PACKED // 2
HALO = 8
NEG_BIG = -1e30
VMEM_LIMIT = 56 * 1024 * 1024

_BF = jnp.bfloat16
_F32 = jnp.float32


def _dot(a, b):
    return jnp.dot(a, b, preferred_element_type=_F32)


def _const_spec(shape):
    nd = len(shape)
    return pl.BlockSpec(shape, lambda *_: (0,) * nd, pipeline_mode=pl.Buffered(1))


def _pack_rows(a):
    half = a.shape[1] // 2
    lo = lax.bitcast_convert_type(a[:, :half].astype(_BF).astype(_F32), jnp.int32)
    hi = lax.bitcast_convert_type(a[:, half:].astype(_BF).astype(_F32), jnp.int32)
    return lax.shift_right_logical(lo, 16) | (hi & jnp.int32(-65536))


def _unpack_rows(p):
    lo = lax.bitcast_convert_type(lax.shift_left(p, 16), _F32)
    hi = lax.bitcast_convert_type(p & jnp.int32(-65536), _F32)
    return lo, hi


def _sc_mesh():
    return plsc.VectorSubcoreMesh(core_axis_name="core", subcore_axis_name="subcore")


def _sc_scatter_rows(tables, idx, n_out):
    n_rows, width = tables[0].shape
    n_tab = len(tables)
    n_copies = idx.shape[0]
    window = SC_WINDOW

    @functools.partial(pl.kernel, mesh=_sc_mesh(),
                       out_type=[jax.ShapeDtypeStruct((n_out, width), tables[0].dtype)] * n_tab)
    def scatter_kernel(*refs):
        idx_hbm = refs[n_tab]
        for src_hbm, out_hbm in zip(refs[:n_tab], refs[n_tab + 1:]):
            def body(src_vmem, idx_vmem, out_hbm=out_hbm):
                for k in range(n_copies):
                    pltpu.sync_copy(src_vmem, out_hbm.at[idx_vmem.at[k]])

            pltpu.emit_pipeline(
                body,
                grid=(n_rows // window,),
                in_specs=[pl.BlockSpec((window, width), lambda i: (i, 0)),
                          pl.BlockSpec((n_copies, window), lambda i: (0, i))],
                out_specs=[],
                core_axis_name=("core", "subcore"),
                dimension_semantics=(pltpu.PARALLEL,),
            )(src_hbm, idx_hbm)

    return scatter_kernel(*tables, idx)


def _sc_gather_rows(tables, idx):
    n_rows = idx.shape[0]
    width = tables[0].shape[1]
    n_tab = len(tables)
    window = SC_WINDOW

    @functools.partial(pl.kernel, mesh=_sc_mesh(),
                       out_type=[jax.ShapeDtypeStruct((n_rows, width), tables[0].dtype)] * n_tab)
    def gather_kernel(*refs):
        idx_hbm = refs[n_tab]
        for table_hbm, out_hbm in zip(refs[:n_tab], refs[n_tab + 1:]):
            def body(idx_vmem, out_vmem, table_hbm=table_hbm):
                pltpu.sync_copy(table_hbm.at[idx_vmem.at[0]], out_vmem)

            pltpu.emit_pipeline(
                body,
                grid=(n_rows // window,),
                in_specs=[pl.BlockSpec((1, window), lambda i: (0, i))],
                out_specs=[pl.BlockSpec((window, width), lambda i: (i, 0))],
                core_axis_name=("core", "subcore"),
                dimension_semantics=(pltpu.PARALLEL,),
            )(idx_hbm, out_hbm)

    return gather_kernel(*tables, idx.reshape(1, n_rows))


def _proj_kernel(tiles_per_seq,
                 x_ref, cos_ref, sin_ref, wg3_ref, wlat_ref, wgates_ref, convw_ref,
                 qg_ref, kvg_ref, wuqa_ref, wuqb_ref, wuk_ref, wuv_ref, wcb_ref,
                 q_ref, kn_ref, kpe_ref, v_ref, ycg_ref, sga_ref, ubuf):
    i = pl.program_id(0)
    tm = x_ref.shape[0]
    xb = x_ref[...].astype(_BF)

    @pl.when(i % tiles_per_seq == 0)
    def _():
        ubuf[0:HALO, :] = jnp.zeros((HALO, CONV_DIM), _F32)

    g3 = _dot(xb, wg3_ref[...])
    u = g3[:, CONV_DIM:2 * CONV_DIM] * g3[:, 2 * CONV_DIM:]
    ubuf[HALO:HALO + tm, :] = u
    cw = convw_ref[...]
    conv = (cw[2:3, :] * u + cw[1:2, :] * ubuf[HALO - 1:HALO - 1 + tm, :]
            + cw[0:1, :] * ubuf[HALO - 2:HALO - 2 + tm, :])
    ubuf[0:HALO, :] = ubuf[tm:tm + HALO, :]
    yc = _dot((g3[:, :CONV_DIM] * conv).astype(_BF), wcb_ref[...])
    gates = _dot(xb, wgates_ref[...])
    ycg_ref[...] = (jax.nn.sigmoid(gates[:, :D_MODEL]) * yc).astype(_BF)
    sga_ref[...] = jax.nn.sigmoid(gates[:, D_MODEL:]).astype(_BF)

    lat = _dot(xb, wlat_ref[...])
    cosb = cos_ref[...]
    sinb = sin_ref[...]
    q_lat = lat[:, :Q_LORA_RANK]
    rq = q_lat * lax.rsqrt(jnp.mean(q_lat * q_lat, axis=-1, keepdims=True) + RMS_EPS) * qg_ref[...]
    kv_lat = lat[:, Q_LORA_RANK:Q_LORA_RANK + KV_LORA_RANK]
    ckv = (kv_lat * lax.rsqrt(jnp.mean(kv_lat * kv_lat, axis=-1, keepdims=True) + RMS_EPS)
           * kvg_ref[...]).astype(_BF)
    o = Q_LORA_RANK + KV_LORA_RANK
    kpe_ref[...] = (lat[:, o:o + ROPE_PAD] * cosb + lat[:, o + ROPE_PAD:] * sinb).astype(_BF)
    kn_ref[...] = _dot(ckv, wuk_ref[...]).astype(_BF)
    v_ref[...] = _dot(ckv, wuv_ref[...]).astype(_BF)

    rqb = rq.astype(_BF)
    qa = _dot(rqb, wuqa_ref[...])
    qb = _dot(rqb, wuqb_ref[...])
    scale = QK_HEAD_DIM ** -0.5
    for h in range(N_HEADS):
        lo = h * QK_PAD
        q_ref[:, lo:lo + QK_NOPE_DIM] = (qa[:, lo:lo + QK_NOPE_DIM] * scale).astype(_BF)
        hi = (qa[:, lo + QK_NOPE_DIM:lo + QK_PAD] * cosb
              + qb[:, h * ROPE_PAD:(h + 1) * ROPE_PAD] * sinb)
        q_ref[:, lo + QK_NOPE_DIM:lo + QK_PAD] = (hi * scale).astype(_BF)


def _proj_call(x2d, cosb, sinb, wg3, wlat, wgates, convw, qg, kvg, wuqa, wuqb, wuk, wuv, wcb, seq):
    t = x2d.shape[0]
    tm = PROJ_TM
    row = lambda w: pl.BlockSpec((tm, w), lambda i: (i, 0))
    outs = [
        jax.ShapeDtypeStruct((t, N_HEADS * QK_PAD), _BF),
        jax.ShapeDtypeStruct((t, N_HEADS * QK_NOPE_DIM), _BF),
        jax.ShapeDtypeStruct((t, ROPE_PAD), _BF),
        jax.ShapeDtypeStruct((t, N_HEADS * V_HEAD_DIM), _BF),
        jax.ShapeDtypeStruct((t, D_MODEL), _BF),
        jax.ShapeDtypeStruct((t, D_MODEL), _BF),
    ]
    return pl.pallas_call(
        functools.partial(_proj_kernel, seq // tm),
        grid=(t // tm,),
        in_specs=[row(D_MODEL), row(ROPE_PAD), row(ROPE_PAD)]
        + [_const_spec(a.shape) for a in (wg3, wlat, wgates, convw, qg, kvg, wuqa, wuqb, wuk, wuv, wcb)],
        out_specs=[row(s.shape[1]) for s in outs],
        out_shape=outs,
        scratch_shapes=[pltpu.VMEM((tm + HALO, CONV_DIM), _F32)],
        compiler_params=pltpu.CompilerParams(dimension_semantics=("arbitrary",),
                                             vmem_limit_bytes=VMEM_LIMIT),
        name="proj_conv_qkv",
    )(x2d, cosb, sinb, wg3, wlat, wgates, convw, qg, kvg, wuqa, wuqb, wuk, wuv, wcb)


def _attn_kernel(q_ref, kn_ref, kpe_ref, v_ref, o_ref, kfull):
    i = pl.program_id(2)
    tq = q_ref.shape[0]
    tk = ATTN_TK

    @pl.when(i == 0)
    def _():
        kfull[:, :QK_NOPE_DIM] = kn_ref[...]
        kfull[:, QK_NOPE_DIM:] = kpe_ref[...]

    q = q_ref[...]

    def step(start, carry, mask):
        m, l, acc = carry
        k = kfull[pl.ds(start, tk), :]
        s = lax.dot_general(q, k, (((1,), (1,)), ((), ())), preferred_element_type=_F32)
        if mask is not None:
            s = jnp.where(mask, s, NEG_BIG)
        m_new = jnp.maximum(m, jnp.max(s, axis=-1, keepdims=True))
        a = jnp.exp(m - m_new)
        p = jnp.exp(s - m_new)
        l = a * l + jnp.sum(p, axis=-1, keepdims=True)
        acc = a * acc + _dot(p.astype(_BF), v_ref[pl.ds(start, tk), :])
        return m_new, l, acc

    init = (jnp.full((tq, 1), NEG_BIG, _F32), jnp.zeros((tq, 1), _F32),
            jnp.zeros((tq, V_HEAD_DIM), _F32))
    nfull = i * (tq // tk)
    carry = lax.fori_loop(
        0, nfull, lambda j, c: step(pl.multiple_of(j * tk, tk), c, None), init)
    for d in range(tq // tk):
        qc = lax.broadcasted_iota(jnp.int32, (tq, tk), 0) // CHUNK
        kc = (lax.broadcasted_iota(jnp.int32, (tq, tk), 1) + d * tk) // CHUNK
        carry = step(pl.multiple_of(i * tq + d * tk, tk), carry, kc <= qc)
    _, l, acc = carry
    o_ref[...] = (acc / l).astype(_BF)


def _attn_call(q, kn, kpe, v):
    b, s, _ = q.shape
    tq = ATTN_TQ
    return pl.pallas_call(
        _attn_kernel,
        grid=(b, N_HEADS, s // tq),
        in_specs=[
            pl.BlockSpec((None, tq, QK_PAD), lambda bi, h, i: (bi, i, h)),
            pl.BlockSpec((None, s, QK_NOPE_DIM), lambda bi, h, i: (bi, 0, h)),
            pl.BlockSpec((None, s, ROPE_PAD), lambda bi, h, i: (bi, 0, 0)),
            pl.BlockSpec((None, s, V_HEAD_DIM), lambda bi, h, i: (bi, 0, h)),
        ],
        out_specs=pl.BlockSpec((None, tq, V_HEAD_DIM), lambda bi, h, i: (bi, i, h)),
        out_shape=jax.ShapeDtypeStruct((b, s, N_HEADS * V_HEAD_DIM), _BF),
        scratch_shapes=[pltpu.VMEM((s, QK_PAD), _BF)],
        compiler_params=pltpu.CompilerParams(
            dimension_semantics=("arbitrary", "arbitrary", "arbitrary"),
            vmem_limit_bytes=VMEM_LIMIT),
        name="chunk_causal_attention",
    )(q, kn, kpe, v)


def _layer_norm(z, g, b):
    mu = jnp.mean(z, axis=-1, keepdims=True)
    zc = z - mu
    var = jnp.mean(zc * zc, axis=-1, keepdims=True)
    return zc * lax.rsqrt(var + LN_EPS) * g + b


def _post_kernel(attn_ref, ycg_ref, sga_ref, x_ref, wab_ref, wout_ref, g1_ref, b1_ref,
                 wr_ref, br_ref, h1_ref, h1pa_ref, h1pb_ref, idx_ref, tw_ref):
    tm = x_ref.shape[0]
    y_attn = _dot(attn_ref[...], wab_ref[...])
    merged = ycg_ref[...].astype(_F32) + sga_ref[...].astype(_F32) * y_attn
    mix = _dot(merged.astype(_BF), wout_ref[...])
    h1 = _layer_norm(DEEPNORM_ALPHA * x_ref[...] + mix, g1_ref[...], b1_ref[...])
    h1_ref[...] = h1
    h1p = _pack_rows(h1)
    h1pa_ref[...] = h1p[:, :HALF]
    h1pb_ref[...] = h1p[:, HALF:]

    logits = jnp.dot(h1, wr_ref[...], preferred_element_type=_F32,
                     precision=lax.Precision.HIGHEST) + br_ref[...]
    lane = lax.broadcasted_iota(jnp.int32, (tm, N_EXPERTS), 1)
    out_lane = lax.broadcasted_iota(jnp.int32, (tm, 128), 1)
    idx_out = jnp.zeros((tm, 128), jnp.int32)
    val_out = jnp.zeros((tm, 128), _F32)
    work = logits
    top0 = None
    denom = jnp.zeros((tm, 1), _F32)
    for k in range(TOP_K):
        mx = jnp.max(work, axis=-1, keepdims=True)
        sel = jnp.min(jnp.where(work == mx, lane, N_EXPERTS), axis=-1, keepdims=True)
        if k == 0:
            top0 = mx
        e = jnp.exp(mx - top0)
        denom = denom + e
        idx_out = jnp.where(out_lane == k, sel, idx_out)
        val_out = jnp.where(out_lane == k, e, val_out)
        work = jnp.where(lane == sel, -jnp.inf, work)
    idx_ref[...] = idx_out
    tw_ref[...] = val_out / denom


def _post_call(attn, ycg, sga, x2d, wab, wout, g1, b1, wr, br):
    t = x2d.shape[0]
    tm = POST_TM
    row = lambda w: pl.BlockSpec((tm, w), lambda i: (i, 0))
    outs = [
        jax.ShapeDtypeStruct((t, D_MODEL), _F32),
        jax.ShapeDtypeStruct((t, HALF), jnp.int32),
        jax.ShapeDtypeStruct((t, HALF), jnp.int32),
        jax.ShapeDtypeStruct((t, 128), jnp.int32),
        jax.ShapeDtypeStruct((t, 128), _F32),
    ]
    return pl.pallas_call(
        _post_kernel,
        grid=(t // tm,),
        in_specs=[row(D_MODEL)] * 4 + [_const_spec(a.shape) for a in (wab, wout, g1, b1, wr, br)],
        out_specs=[row(s.shape[1]) for s in outs],
        out_shape=outs,
        compiler_params=pltpu.CompilerParams(dimension_semantics=("arbitrary",),
                                             vmem_limit_bytes=VMEM_LIMIT),
        name="merge_out_ln1_router",
    )(attn, ycg, sga, x2d, wab, wout, g1, b1, wr, br)


def _moe_kernel(be_ref, nused_ref, xa_ref, xb_ref, wgu_ref, bgu_ref, wd_ref, bd_ref, ya_ref, yb_ref,
                wgu_bf, wd_bf):
    i = pl.program_id(0)
    prev = be_ref[jnp.maximum(i - 1, 0)]
    fresh = jnp.logical_or(i == 0, be_ref[i] != prev)
    used = i < nused_ref[0]

    @pl.when(jnp.logical_and(fresh, used))
    def _():
        wgu_bf[...] = wgu_ref[...].astype(_BF)
        wd_bf[...] = wd_ref[...].astype(_BF)

    @pl.when(used)
    def _():
        a_lo, a_hi = _unpack_rows(xa_ref[...])
        b_lo, b_hi = _unpack_rows(xb_ref[...])
        xs = jnp.concatenate([a_lo, b_lo, a_hi, b_hi], axis=1).astype(_BF)
        gu = _dot(xs, wgu_bf[...]) + bgu_ref[...]
        gate = jnp.minimum(gu[:, :D_FF_EXPERT], SWIGLU_LIMIT)
        up = jnp.clip(gu[:, D_FF_EXPERT:], -SWIGLU_LIMIT, SWIGLU_LIMIT)
        hid = (up + 1.0) * (gate * jax.nn.sigmoid(SWIGLU_ALPHA * gate))
        yp = _pack_rows(_dot(hid.astype(_BF), wd_bf[...]) + bd_ref[...])
        ya_ref[...] = yp[:, :HALF]
        yb_ref[...] = yp[:, HALF:]

    @pl.when(jnp.logical_not(used))
    def _():
        ya_ref[...] = jnp.zeros(ya_ref.shape, jnp.int32)
        yb_ref[...] = jnp.zeros(yb_ref.shape, jnp.int32)


def _moe_call(block_e, n_used, xa, xb, wgu, bgu, wd, bd):
    n_slots = xa.shape[0]
    bm = MOE_BM
    n_blocks = n_slots // bm
    grid_spec = pltpu.PrefetchScalarGridSpec(
        num_scalar_prefetch=2,
        grid=(n_blocks,),
        in_specs=[
            pl.BlockSpec((bm, HALF), lambda i, be, nu: (jnp.minimum(i, nu[0] - 1), 0)),
            pl.BlockSpec((bm, HALF), lambda i, be, nu: (jnp.minimum(i, nu[0] - 1), 0)),
            pl.BlockSpec((None, D_MODEL, 2 * D_FF_EXPERT), lambda i, be, nu: (be[i], 0, 0)),
            pl.BlockSpec((None, 1, 2 * D_FF_EXPERT), lambda i, be, nu: (be[i], 0, 0)),
            pl.BlockSpec((None, D_FF_EXPERT, D_MODEL), lambda i, be, nu: (be[i], 0, 0)),
            pl.BlockSpec((None, 1, D_MODEL), lambda i, be, nu: (be[i], 0, 0)),
        ],
        out_specs=[pl.BlockSpec((bm, HALF), lambda i, be, nu: (i, 0))] * 2,
        scratch_shapes=[pltpu.VMEM((D_MODEL, 2 * D_FF_EXPERT), _BF),
                        pltpu.VMEM((D_FF_EXPERT, D_MODEL), _BF)],
    )
    return pl.pallas_call(
        _moe_kernel,
        grid_spec=grid_spec,
        out_shape=[jax.ShapeDtypeStruct((n_slots, HALF), jnp.int32)] * 2,
        compiler_params=pltpu.CompilerParams(dimension_semantics=("arbitrary",),
                                             vmem_limit_bytes=VMEM_LIMIT),
        name="expert_ffn",
    )(block_e, n_used, xa, xb, wgu, bgu, wd, bd)


def _combine_kernel(*refs):
    ya_refs, yb_refs = refs[:TOP_K], refs[TOP_K:2 * TOP_K]
    tw_ref, h1_ref, g2_ref, b2_ref, o_ref = refs[2 * TOP_K:]
    tw = tw_ref[...]
    parts = [jnp.zeros((tw.shape[0], HALF), _F32) for _ in range(4)]
    for k in range(TOP_K):
        a_lo, a_hi = _unpack_rows(ya_refs[k][...])
        b_lo, b_hi = _unpack_rows(yb_refs[k][...])
        w = tw[:, k:k + 1]
        parts = [p + w * y for p, y in zip(parts, (a_lo, b_lo, a_hi, b_hi))]
    ffn = jnp.concatenate(parts, axis=1)
    o_ref[...] = _layer_norm(DEEPNORM_ALPHA * h1_ref[...] + ffn, g2_ref[...], b2_ref[...])


def _combine_call(yga, ygb, tw, h1, g2, b2):
    t = h1.shape[0]
    tm = COMB_TM
    row = lambda w: pl.BlockSpec((tm, w), lambda i: (i, 0))
    krow = lambda k: pl.BlockSpec((tm, HALF), lambda i: (k * (t // tm) + i, 0))
    return pl.pallas_call(
        _combine_kernel,
        grid=(t // tm,),
        in_specs=[krow(k) for k in range(TOP_K)] * 2 + [row(128), row(D_MODEL),
                  _const_spec(g2.shape), _const_spec(b2.shape)],
        out_specs=row(D_MODEL),
        out_shape=jax.ShapeDtypeStruct((t, D_MODEL), _F32),
        compiler_params=pltpu.CompilerParams(dimension_semantics=("arbitrary",),
                                             vmem_limit_bytes=VMEM_LIMIT),
        name="combine_ln2",
    )(*([yga] * TOP_K + [ygb] * TOP_K), tw, h1, g2, b2)


def _prepare_weights(w_in, w_uq):
    o_q = 3 * CONV_DIM
    o_kv = o_q + Q_LORA_RANK
    o_pe = o_kv + KV_LORA_RANK
    o_gc = o_pe + QK_ROPE_DIM
    half = QK_ROPE_DIM // 2
    wg3 = w_in[:, :o_q]
    kpe = w_in[:, o_pe:o_gc]
    kpe_sw = jnp.concatenate([kpe[:, half:], kpe[:, :half]], axis=1)
    zpad = jnp.zeros((D_MODEL, ROPE_PAD - QK_ROPE_DIM), w_in.dtype)
    wlat = jnp.concatenate([w_in[:, o_q:o_pe], kpe, zpad, kpe_sw, zpad], axis=1)
    wgates = w_in[:, o_gc:]
    wq = w_uq.reshape(Q_LORA_RANK, N_HEADS, QK_HEAD_DIM)
    wuqa = jnp.pad(wq, ((0, 0), (0, 0), (0, QK_PAD - QK_HEAD_DIM)))
    pe = wq[:, :, QK_NOPE_DIM:]
    pe_sw = jnp.concatenate([pe[:, :, half:], pe[:, :, :half]], axis=2)
    wuqb = jnp.pad(pe_sw, ((0, 0), (0, 0), (0, ROPE_PAD - QK_ROPE_DIM)))
    return (wg3.astype(_BF), wlat.astype(_BF), wgates.astype(_BF),
            wuqa.reshape(Q_LORA_RANK, N_HEADS * QK_PAD).astype(_BF),
            wuqb.reshape(Q_LORA_RANK, N_HEADS * ROPE_PAD).astype(_BF))


def _rope_tables(positions):
    inv_freq = ROPE_BASE ** (-jnp.arange(0, QK_ROPE_DIM, 2, dtype=_F32) / QK_ROPE_DIM)
    ang = positions.astype(_F32).reshape(-1)[:, None] * inv_freq
    cos, sin = jnp.cos(ang), jnp.sin(ang)
    z = jnp.zeros((ang.shape[0], ROPE_PAD - QK_ROPE_DIM), _F32)
    return (jnp.concatenate([cos, cos, z], axis=1), jnp.concatenate([-sin, sin, z], axis=1))


def _route(top_idx, n_tok):
    bm = MOE_BM
    onehot = (top_idx[:, :, None] == jnp.arange(N_EXPERTS, dtype=jnp.int32)).sum(axis=1).astype(jnp.int32)
    incl = jnp.cumsum(onehot, axis=0)
    counts = incl[-1]
    pos = incl - onehot
    padded = (counts + bm - 1) // bm * bm
    padded_end = jnp.cumsum(padded)
    padded_start = padded_end - padded
    slot = jnp.take_along_axis(pos + padded_start[None, :], top_idx, axis=1)
    n_blocks = n_tok * TOP_K // bm + N_EXPERTS
    block_start = jnp.arange(n_blocks, dtype=jnp.int32) * bm
    block_e = jnp.minimum((padded_end[None, :] <= block_start[:, None]).sum(axis=1),
                          N_EXPERTS - 1).astype(jnp.int32)
    n_used = (padded_end[-1] // bm).astype(jnp.int32).reshape(1)
    return slot.astype(jnp.int32), block_e, n_used, n_blocks * bm


def kernel(x, positions, w_in, conv_w, q_norm_g, w_uq, kv_norm_g, w_uk, w_uv, w_conv_branch,
           w_attn_branch, w_out, ln1_g, ln1_b, w_router, b_router, w_gate_up, b_gate_up, w_down,
           b_down, ln2_g, ln2_b):
    b, s, d = x.shape
    t = b * s
    x2d = x.reshape(t, d)
    cosb, sinb = _rope_tables(positions)
    for l in range(DEPTH):
        wg3, wlat, wgates, wuqa, wuqb = _prepare_weights(w_in[l], w_uq[l])
        q, kn, kpe, v, ycg, sga = _proj_call(
            x2d, cosb, sinb, wg3, wlat, wgates, conv_w[l], q_norm_g[l][None, :], kv_norm_g[l][None, :],
            wuqa, wuqb, w_uk[l].astype(_BF), w_uv[l].astype(_BF), w_conv_branch[l].astype(_BF), s)
        attn = _attn_call(q.reshape(b, s, -1), kn.reshape(b, s, -1), kpe.reshape(b, s, -1),
                          v.reshape(b, s, -1)).reshape(t, -1)
        h1, h1pa, h1pb, idx_pad, tw_pad = _post_call(
            attn, ycg, sga, x2d, w_attn_branch[l].astype(_BF), w_out[l].astype(_BF),
            ln1_g[l][None, :], ln1_b[l][None, :], w_router[l], b_router[l][None, :])
        top_idx = idx_pad[:, :TOP_K]
        slot, block_e, n_used, n_slots = _route(top_idx, t)
        slot_km = slot.T
        xa, xb = _sc_scatter_rows([h1pa, h1pb], slot_km, n_slots)
        ya, yb = _moe_call(block_e, n_used, xa, xb, w_gate_up[l], b_gate_up[l][:, None, :],
                           w_down[l], b_down[l][:, None, :])
        yga, ygb = _sc_gather_rows([ya, yb], slot_km.reshape(-1))
        x2d = _combine_call(yga, ygb, tw_pad, h1, ln2_g[l][None, :], ln2_b[l][None, :])
    return x2d.reshape(b, s, d)
```

```python
import functools

import numpy as np
import jax
import jax.numpy as jnp
from jax import lax
from jax.experimental import pallas as pl
from jax.experimental.pallas import tpu as pltpu
from jax.experimental.pallas import tpu_sc as plsc

D_MODEL = 1024
CHUNK = 64
CONV_DIM = D_MODEL
CONV_WIDTH = 3
N_HEADS = 8
QK_NOPE_DIM = 128
QK_ROPE_DIM = 64
V_HEAD_DIM = 128
QK_HEAD_DIM = QK_NOPE_DIM + QK_ROPE_DIM
Q_LORA_RANK = 384
KV_LORA_RANK = 256
ROPE_BASE = 10000.0
N_EXPERTS = 32
TOP_K = 4
D_FF_EXPERT = D_MODEL
SWIGLU_LIMIT = 7.0
SWIGLU_ALPHA = 1.702
LN_EPS = 1e-5
RMS_EPS = 1e-6
DEPTH = 1
DEEPNORM_ALPHA = (2 * DEPTH) ** 0.25

QK_PAD = 256
ROPE_PAD = 128
LAT_DIM = Q_LORA_RANK + KV_LORA_RANK + 2 * ROPE_PAD

PROJ_TM = 512
ATTN_TQ = 512
ATTN_TK = 512
ONES_ROWS = 16
POST_TM = 512
MOE_BM = 512
COMB_TM = 512
SC_WINDOW = 128
PACKED = D_MODEL // 2
HALF = PACKED // 2
HALO = 8
NEG_BIG = -1e30
LOG2E = 1.4426950408889634
VMEM_LIMIT = 56 * 1024 * 1024

_BF = jnp.bfloat16
_F32 = jnp.float32


def _dot(a, b):
    return jnp.dot(a, b, preferred_element_type=_F32)


def _const_spec(shape):
    nd = len(shape)
    return pl.BlockSpec(shape, lambda *_: (0,) * nd, pipeline_mode=pl.Buffered(1))


def _pack_rows(a):
    half = a.shape[1] // 2
    lo = lax.bitcast_convert_type(a[:, :half].astype(_BF).astype(_F32), jnp.int32)
    hi = lax.bitcast_convert_type(a[:, half:].astype(_BF).astype(_F32), jnp.int32)
    return lax.shift_right_logical(lo, 16) | (hi & jnp.int32(-65536))


def _unpack_rows(p):
    lo = lax.bitcast_convert_type(lax.shift_left(p, 16), _F32)
    hi = lax.bitcast_convert_type(p & jnp.int32(-65536), _F32)
    return lo, hi


def _sc_mesh():
    return plsc.VectorSubcoreMesh(core_axis_name="core", subcore_axis_name="subcore")


def _sc_scatter_rows(tables, idx, n_out):
    n_rows, width = tables[0].shape
    n_tab = len(tables)
    n_copies = idx.shape[0]
    window = SC_WINDOW

    @functools.partial(pl.kernel, mesh=_sc_mesh(),
                       out_type=[jax.ShapeDtypeStruct((n_out, width), tables[0].dtype)] * n_tab)
    def scatter_kernel(*refs):
        idx_hbm = refs[n_tab]
        for src_hbm, out_hbm in zip(refs[:n_tab], refs[n_tab + 1:]):
            def body(src_vmem, idx_vmem, out_hbm=out_hbm):
                for k in range(n_copies):
                    pltpu.sync_copy(src_vmem, out_hbm.at[idx_vmem.at[k]])

            pltpu.emit_pipeline(
                body,
                grid=(n_rows // window,),
                in_specs=[pl.BlockSpec((window, width), lambda i: (i, 0)),
                          pl.BlockSpec((n_copies, window), lambda i: (0, i))],
                out_specs=[],
                core_axis_name=("core", "subcore"),
                dimension_semantics=(pltpu.PARALLEL,),
            )(src_hbm, idx_hbm)

    return scatter_kernel(*tables, idx)


def _sc_gather_rows(tables, idx):
    n_rows = idx.shape[0]
    width = tables[0].shape[1]
    n_tab = len(tables)
    window = SC_WINDOW

    @functools.partial(pl.kernel, mesh=_sc_mesh(),
                       out_type=[jax.ShapeDtypeStruct((n_rows, width), tables[0].dtype)] * n_tab)
    def gather_kernel(*refs):
        idx_hbm = refs[n_tab]
        for table_hbm, out_hbm in zip(refs[:n_tab], refs[n_tab + 1:]):
            def body(idx_vmem, out_vmem, table_hbm=table_hbm):
                pltpu.sync_copy(table_hbm.at[idx_vmem.at[0]], out_vmem)

            pltpu.emit_pipeline(
                body,
                grid=(n_rows // window,),
                in_specs=[pl.BlockSpec((1, window), lambda i: (0, i))],
                out_specs=[pl.BlockSpec((window, width), lambda i: (i, 0))],
                core_axis_name=("core", "subcore"),
                dimension_semantics=(pltpu.PARALLEL,),
            )(idx_hbm, out_hbm)

    return gather_kernel(*tables, idx.reshape(1, n_rows))


def _proj_kernel(tiles_per_seq,
                 x_ref, cos_ref, sin_ref, wg3_ref, wlat_ref, wgates_ref, convw_ref,
                 qg_ref, kvg_ref, wuqa_ref, wuqb_ref, wuk_ref, wuv_ref, wcb_ref,
                 q_ref, kn_ref, kpe_ref, v_ref, ycg_ref, sga_ref, ubuf):
    i = pl.program_id(0)
    tm = x_ref.shape[0]
    xb = x_ref[...].astype(_BF)

    @pl.when(i % tiles_per_seq == 0)
    def _():
        ubuf[0:HALO, :] = jnp.zeros((HALO, CONV_DIM), _F32)

    g3 = _dot(xb, wg3_ref[...])
    u = g3[:, CONV_DIM:2 * CONV_DIM] * g3[:, 2 * CONV_DIM:]
    ubuf[HALO:HALO + tm, :] = u
    cw = convw_ref[...]
    conv = (cw[2:3, :] * u + cw[1:2, :] * ubuf[HALO - 1:HALO - 1 + tm, :]
            + cw[0:1, :] * ubuf[HALO - 2:HALO - 2 + tm, :])
    ubuf[0:HALO, :] = ubuf[tm:tm + HALO, :]
    yc = _dot((g3[:, :CONV_DIM] * conv).astype(_BF), wcb_ref[...])
    gates = _dot(xb, wgates_ref[...])
    ycg_ref[...] = (jax.nn.sigmoid(gates[:, :D_MODEL]) * yc).astype(_BF)
    sga_ref[...] = jax.nn.sigmoid(gates[:, D_MODEL:]).astype(_BF)

    lat = _dot(xb, wlat_ref[...])
    cosb = cos_ref[...]
    sinb = sin_ref[...]
    q_lat = lat[:, :Q_LORA_RANK]
    rq = q_lat * lax.rsqrt(jnp.mean(q_lat * q_lat, axis=-1, keepdims=True) + RMS_EPS) * qg_ref[...]
    kv_lat = lat[:, Q_LORA_RANK:Q_LORA_RANK + KV_LORA_RANK]
    ckv = (kv_lat * lax.rsqrt(jnp.mean(kv_lat * kv_lat, axis=-1, keepdims=True) + RMS_EPS)
           * kvg_ref[...]).astype(_BF)
    o = Q_LORA_RANK + KV_LORA_RANK
    kpe_ref[...] = (lat[:, o:o + ROPE_PAD] * cosb + lat[:, o + ROPE_PAD:] * sinb).astype(_BF)
    kn_ref[...] = _dot(ckv, wuk_ref[...]).astype(_BF)
    v_ref[...] = _dot(ckv, wuv_ref[...]).astype(_BF)

    rqb = rq.astype(_BF)
    qa = _dot(rqb, wuqa_ref[...])
    qb = _dot(rqb, wuqb_ref[...])
    scale = QK_HEAD_DIM ** -0.5 * LOG2E
    for h in range(N_HEADS):
        lo = h * QK_PAD
        q_ref[:, lo:lo + QK_NOPE_DIM] = (qa[:, lo:lo + QK_NOPE_DIM] * scale).astype(_BF)
        hi = (qa[:, lo + QK_NOPE_DIM:lo + QK_PAD] * cosb
              + qb[:, h * ROPE_PAD:(h + 1) * ROPE_PAD] * sinb)
        q_ref[:, lo + QK_NOPE_DIM:lo + QK_PAD] = (hi * scale).astype(_BF)


def _proj_call(x2d, cosb, sinb, wg3, wlat, wgates, convw, qg, kvg, wuqa, wuqb, wuk, wuv, wcb, seq):
    t = x2d.shape[0]
    tm = PROJ_TM
    row = lambda w: pl.BlockSpec((tm, w), lambda i: (i, 0))
    outs = [
        jax.ShapeDtypeStruct((t, N_HEADS * QK_PAD), _BF),
        jax.ShapeDtypeStruct((t, N_HEADS * QK_NOPE_DIM), _BF),
        jax.ShapeDtypeStruct((t, ROPE_PAD), _BF),
        jax.ShapeDtypeStruct((t, N_HEADS * V_HEAD_DIM), _BF),
        jax.ShapeDtypeStruct((t, D_MODEL), _BF),
        jax.ShapeDtypeStruct((t, D_MODEL), _BF),
    ]
    return pl.pallas_call(
        functools.partial(_proj_kernel, seq // tm),
        grid=(t // tm,),
        in_specs=[row(D_MODEL), row(ROPE_PAD), row(ROPE_PAD)]
        + [_const_spec(a.shape) for a in (wg3, wlat, wgates, convw, qg, kvg, wuqa, wuqb, wuk, wuv, wcb)],
        out_specs=[row(s.shape[1]) for s in outs],
        out_shape=outs,
        scratch_shapes=[pltpu.VMEM((tm + HALO, CONV_DIM), _F32)],
        compiler_params=pltpu.CompilerParams(dimension_semantics=("arbitrary",),
                                             vmem_limit_bytes=VMEM_LIMIT),
        name="proj_conv_qkv",
    )(x2d, cosb, sinb, wg3, wlat, wgates, convw, qg, kvg, wuqa, wuqb, wuk, wuv, wcb)


def _attn_kernel(q_ref, kn_ref, kpe_ref, v_ref, o_ref, kfull, vt, s0, s1, p0, p1, a0, a1, m_ref, acc):
    i = pl.program_id(2)
    tq = q_ref.shape[0]
    tk = ATTN_TK
    s_bufs, p_bufs, a_bufs = (s0, s1), (p0, p1), (a0, a1)

    @pl.when(i == 0)
    def _():
        kfull[:, :QK_NOPE_DIM] = kn_ref[...]
        kfull[:, QK_NOPE_DIM:] = kpe_ref[...]
        vt[:V_HEAD_DIM, :] = v_ref[...].T
        vt[V_HEAD_DIM:, :] = jnp.ones((vt.shape[0] - V_HEAD_DIM, vt.shape[1]), _BF)

    def scores_t(j, dst):
        k = kfull[pl.ds(pl.multiple_of(j * tk, tk), tk), :]
        dst[...] = lax.dot_general(k, q_ref[...], (((1,), (1,)), ((), ())),
                                   preferred_element_type=_F32)

    def softmax_t(par, mask):
        s_t = s_bufs[par][...]
        if mask is not None:
            s_t = jnp.where(mask, s_t, NEG_BIG)
        m_old = m_ref[...]
        m_new = jnp.maximum(m_old, jnp.max(s_t, axis=0, keepdims=True))
        m_ref[...] = m_new
        a_bufs[par][...] = jnp.exp2(m_old - m_new)
        p_bufs[par][...] = jnp.exp2(s_t - m_new).astype(_BF)

    def accumulate(j, par):
        v_t = vt[:, pl.ds(pl.multiple_of(j * tk, tk), tk)]
        acc[...] = a_bufs[par][...] * acc[...] + _dot(v_t, p_bufs[par][...])

    def stage(j, par):
        softmax_t(par, None)
        scores_t(j + 1, s_bufs[1 - par])
        accumulate(jnp.maximum(j - 1, 0), 1 - par)

    def finish(par):
        accumulate(jnp.maximum(i - 1, 0), 1 - par)
        key_chunk = lax.broadcasted_iota(jnp.int32, (tk, tq), 0) // CHUNK
        qry_chunk = lax.broadcasted_iota(jnp.int32, (tk, tq), 1) // CHUNK
        softmax_t(par, key_chunk <= qry_chunk)
        accumulate(i, par)
        out_t = acc[:V_HEAD_DIM, :] / acc[V_HEAD_DIM:V_HEAD_DIM + 1, :]
        o_ref[...] = out_t.T.astype(_BF)

    scores_t(0, s0)
    p1[...] = jnp.zeros(p1.shape, _BF)
    a1[...] = jnp.ones(a1.shape, _F32)
    m_ref[...] = jnp.full(m_ref.shape, NEG_BIG, _F32)
    acc[...] = jnp.zeros(acc.shape, _F32)

    @pl.loop(0, i // 2)
    def _(jj):
        stage(2 * jj, 0)
        stage(2 * jj + 1, 1)

    @pl.when(i % 2 == 0)
    def _():
        finish(0)

    @pl.when(i % 2 == 1)
    def _():
        stage(i - 1, 0)
        finish(1)


def _attn_call(q, kn, kpe, v):
    b, s, _ = q.shape
    tq = ATTN_TQ
    return pl.pallas_call(
        _attn_kernel,
        grid=(b, N_HEADS, s // tq),
        in_specs=[
            pl.BlockSpec((None, tq, QK_PAD), lambda bi, h, i: (bi, i, h)),
            pl.BlockSpec((None, s, QK_NOPE_DIM), lambda bi, h, i: (bi, 0, h)),
            pl.BlockSpec((None, s, ROPE_PAD), lambda bi, h, i: (bi, 0, 0)),
            pl.BlockSpec((None, s, V_HEAD_DIM), lambda bi, h, i: (bi, 0, h)),
        ],
        out_specs=pl.BlockSpec((None, tq, V_HEAD_DIM), lambda bi, h, i: (bi, i, h)),
        out_shape=jax.ShapeDtypeStruct((b, s, N_HEADS * V_HEAD_DIM), _BF),
        scratch_shapes=[pltpu.VMEM((s, QK_PAD), _BF), pltpu.VMEM((V_HEAD_DIM + ONES_ROWS, s), _BF),
                        pltpu.VMEM((ATTN_TK, tq), _F32), pltpu.VMEM((ATTN_TK, tq), _F32),
                        pltpu.VMEM((ATTN_TK, tq), _BF), pltpu.VMEM((ATTN_TK, tq), _BF),
                        pltpu.VMEM((1, tq), _F32), pltpu.VMEM((1, tq), _F32), pltpu.VMEM((1, tq), _F32),
                        pltpu.VMEM((V_HEAD_DIM + ONES_ROWS, tq), _F32)],
        compiler_params=pltpu.CompilerParams(
            dimension_semantics=("arbitrary", "arbitrary", "arbitrary"),
            vmem_limit_bytes=VMEM_LIMIT),
        name="chunk_causal_attention",
    )(q, kn, kpe, v)


def _layer_norm(z, g, b):
    mu = jnp.mean(z, axis=-1, keepdims=True)
    zc = z - mu
    var = jnp.mean(zc * zc, axis=-1, keepdims=True)
    return zc * lax.rsqrt(var + LN_EPS) * g + b


def _post_kernel(attn_ref, ycg_ref, sga_ref, x_ref, wab_ref, wout_ref, g1_ref, b1_ref,
                 wr_ref, br_ref, h1_ref, h1pa_ref, h1pb_ref, idx_ref, tw_ref):
    tm = x_ref.shape[0]
    y_attn = _dot(attn_ref[...], wab_ref[...])
    merged = ycg_ref[...].astype(_F32) + sga_ref[...].astype(_F32) * y_attn
    mix = _dot(merged.astype(_BF), wout_ref[...])
    h1 = _layer_norm(DEEPNORM_ALPHA * x_ref[...] + mix, g1_ref[...], b1_ref[...])
    h1_ref[...] = h1
    h1p = _pack_rows(h1)
    h1pa_ref[...] = h1p[:, :HALF]
    h1pb_ref[...] = h1p[:, HALF:]

    logits = jnp.dot(h1, wr_ref[...], preferred_element_type=_F32,
                     precision=lax.Precision.HIGHEST) + br_ref[...]
    lane = lax.broadcasted_iota(jnp.int32, (tm, N_EXPERTS), 1)
    out_lane = lax.broadcasted_iota(jnp.int32, (tm, 128), 1)
    idx_out = jnp.zeros((tm, 128), jnp.int32)
    val_out = jnp.zeros((tm, 128), _F32)
    work = logits
    top0 = None
    denom = jnp.zeros((tm, 1), _F32)
    for k in range(TOP_K):
        mx = jnp.max(work, axis=-1, keepdims=True)
        sel = jnp.min(jnp.where(work == mx, lane, N_EXPERTS), axis=-1, keepdims=True)
        if k == 0:
            top0 = mx
        e = jnp.exp(mx - top0)
        denom = denom + e
        idx_out = jnp.where(out_lane == k, sel, idx_out)
        val_out = jnp.where(out_lane == k, e, val_out)
        work = jnp.where(lane == sel, -jnp.inf, work)
    idx_ref[...] = idx_out
    tw_ref[...] = val_out / denom


def _post_call(attn, ycg, sga, x2d, wab, wout, g1, b1, wr, br):
    t = x2d.shape[0]
    tm = POST_TM
    row = lambda w: pl.BlockSpec((tm, w), lambda i: (i, 0))
    outs = [
        jax.ShapeDtypeStruct((t, D_MODEL), _F32),
        jax.ShapeDtypeStruct((t, HALF), jnp.int32),
        jax.ShapeDtypeStruct((t, HALF), jnp.int32),
        jax.ShapeDtypeStruct((t, 128), jnp.int32),
        jax.ShapeDtypeStruct((t, 128), _F32),
    ]
    return pl.pallas_call(
        _post_kernel,
        grid=(t // tm,),
        in_specs=[row(D_MODEL)] * 4 + [_const_spec(a.shape) for a in (wab, wout, g1, b1, wr, br)],
        out_specs=[row(s.shape[1]) for s in outs],
        out_shape=outs,
        compiler_params=pltpu.CompilerParams(dimension_semantics=("arbitrary",),
                                             vmem_limit_bytes=VMEM_LIMIT),
        name="merge_out_ln1_router",
    )(attn, ycg, sga, x2d, wab, wout, g1, b1, wr, br)


def _moe_kernel(be_ref, nused_ref, xa_ref, xb_ref, wgu_ref, bgu_ref, wd_ref, bd_ref, ya_ref, yb_ref,
                wgu_bf, wd_bf):
    i = pl.program_id(0)
    prev = be_ref[jnp.maximum(i - 1, 0)]
    fresh = jnp.logical_or(i == 0, be_ref[i] != prev)
    used = i < nused_ref[0]

    @pl.when(jnp.logical_and(fresh, used))
    def _():
        wgu_bf[...] = wgu_ref[...].astype(_BF)
        wd_bf[...] = wd_ref[...].astype(_BF)

    @pl.when(used)
    def _():
        a_lo, a_hi = _unpack_rows(xa_ref[...])
        b_lo, b_hi = _unpack_rows(xb_ref[...])
        xs = jnp.concatenate([a_lo, b_lo, a_hi, b_hi], axis=1).astype(_BF)
        gu = _dot(xs, wgu_bf[...]) + bgu_ref[...]
        gate = jnp.minimum(gu[:, :D_FF_EXPERT], SWIGLU_LIMIT)
        up = jnp.clip(gu[:, D_FF_EXPERT:], -SWIGLU_LIMIT, SWIGLU_LIMIT)
        hid = (up + 1.0) * (gate * jax.nn.sigmoid(SWIGLU_ALPHA * gate))
        yp = _pack_rows(_dot(hid.astype(_BF), wd_bf[...]) + bd_ref[...])
        ya_ref[...] = yp[:, :HALF]
        yb_ref[...] = yp[:, HALF:]

    @pl.when(jnp.logical_not(used))
    def _():
        ya_ref[...] = jnp.zeros(ya_ref.shape, jnp.int32)
        yb_ref[...] = jnp.zeros(yb_ref.shape, jnp.int32)


def _moe_call(block_e, n_used, xa, xb, wgu, bgu, wd, bd):
    n_slots = xa.shape[0]
    bm = MOE_BM
    n_blocks = n_slots // bm
    grid_spec = pltpu.PrefetchScalarGridSpec(
        num_scalar_prefetch=2,
        grid=(n_blocks,),
        in_specs=[
            pl.BlockSpec((bm, HALF), lambda i, be, nu: (jnp.minimum(i, nu[0] - 1), 0)),
            pl.BlockSpec((bm, HALF), lambda i, be, nu: (jnp.minimum(i, nu[0] - 1), 0)),
            pl.BlockSpec((None, D_MODEL, 2 * D_FF_EXPERT), lambda i, be, nu: (be[i], 0, 0)),
            pl.BlockSpec((None, 1, 2 * D_FF_EXPERT), lambda i, be, nu: (be[i], 0, 0)),
            pl.BlockSpec((None, D_FF_EXPERT, D_MODEL), lambda i, be, nu: (be[i], 0, 0)),
            pl.BlockSpec((None, 1, D_MODEL), lambda i, be, nu: (be[i], 0, 0)),
        ],
        out_specs=[pl.BlockSpec((bm, HALF), lambda i, be, nu: (i, 0))] * 2,
        scratch_shapes=[pltpu.VMEM((D_MODEL, 2 * D_FF_EXPERT), _BF),
                        pltpu.VMEM((D_FF_EXPERT, D_MODEL), _BF)],
    )
    return pl.pallas_call(
        _moe_kernel,
        grid_spec=grid_spec,
        out_shape=[jax.ShapeDtypeStruct((n_slots, HALF), jnp.int32)] * 2,
        compiler_params=pltpu.CompilerParams(dimension_semantics=("arbitrary",),
                                             vmem_limit_bytes=VMEM_LIMIT),
        name="expert_ffn",
    )(block_e, n_used, xa, xb, wgu, bgu, wd, bd)


def _combine_kernel(*refs):
    ya_refs, yb_refs = refs[:TOP_K], refs[TOP_K:2 * TOP_K]
    tw_ref, h1_ref, g2_ref, b2_ref, o_ref = refs[2 * TOP_K:]
    tw = tw_ref[...]
    parts = [jnp.zeros((tw.shape[0], HALF), _F32) for _ in range(4)]
    for k in range(TOP_K):
        a_lo, a_hi = _unpack_rows(ya_refs[k][...])
        b_lo, b_hi = _unpack_rows(yb_refs[k][...])
        w = tw[:, k:k + 1]
        parts = [p + w * y for p, y in zip(parts, (a_lo, b_lo, a_hi, b_hi))]
    ffn = jnp.concatenate(parts, axis=1)
    o_ref[...] = _layer_norm(DEEPNORM_ALPHA * h1_ref[...] + ffn, g2_ref[...], b2_ref[...])


def _combine_call(yga, ygb, tw, h1, g2, b2):
    t = h1.shape[0]
    tm = COMB_TM
    row = lambda w: pl.BlockSpec((tm, w), lambda i: (i, 0))
    krow = lambda k: pl.BlockSpec((tm, HALF), lambda i: (k * (t // tm) + i, 0))
    return pl.pallas_call(
        _combine_kernel,
        grid=(t // tm,),
        in_specs=[krow(k) for k in range(TOP_K)] * 2 + [row(128), row(D_MODEL),
                  _const_spec(g2.shape), _const_spec(b2.shape)],
        out_specs=row(D_MODEL),
        out_shape=jax.ShapeDtypeStruct((t, D_MODEL), _F32),
        compiler_params=pltpu.CompilerParams(dimension_semantics=("arbitrary",),
                                             vmem_limit_bytes=VMEM_LIMIT),
        name="combine_ln2",
    )(*([yga] * TOP_K + [ygb] * TOP_K), tw, h1, g2, b2)


def _prepare_weights(w_in, w_uq):
    o_q = 3 * CONV_DIM
    o_kv = o_q + Q_LORA_RANK
    o_pe = o_kv + KV_LORA_RANK
    o_gc = o_pe + QK_ROPE_DIM
    half = QK_ROPE_DIM // 2
    wg3 = w_in[:, :o_q]
    kpe = w_in[:, o_pe:o_gc]
    kpe_sw = jnp.concatenate([kpe[:, half:], kpe[:, :half]], axis=1)
    zpad = jnp.zeros((D_MODEL, ROPE_PAD - QK_ROPE_DIM), w_in.dtype)
    wlat = jnp.concatenate([w_in[:, o_q:o_pe], kpe, zpad, kpe_sw, zpad], axis=1)
    wgates = w_in[:, o_gc:]
    wq = w_uq.reshape(Q_LORA_RANK, N_HEADS, QK_HEAD_DIM)
    wuqa = jnp.pad(wq, ((0, 0), (0, 0), (0, QK_PAD - QK_HEAD_DIM)))
    pe = wq[:, :, QK_NOPE_DIM:]
    pe_sw = jnp.concatenate([pe[:, :, half:], pe[:, :, :half]], axis=2)
    wuqb = jnp.pad(pe_sw, ((0, 0), (0, 0), (0, ROPE_PAD - QK_ROPE_DIM)))
    return (wg3.astype(_BF), wlat.astype(_BF), wgates.astype(_BF),
            wuqa.reshape(Q_LORA_RANK, N_HEADS * QK_PAD).astype(_BF),
            wuqb.reshape(Q_LORA_RANK, N_HEADS * ROPE_PAD).astype(_BF))


def _rope_tables(positions):
    inv_freq = ROPE_BASE ** (-jnp.arange(0, QK_ROPE_DIM, 2, dtype=_F32) / QK_ROPE_DIM)
    ang = positions.astype(_F32).reshape(-1)[:, None] * inv_freq
    cos, sin = jnp.cos(ang), jnp.sin(ang)
    z = jnp.zeros((ang.shape[0], ROPE_PAD - QK_ROPE_DIM), _F32)
    return (jnp.concatenate([cos, cos, z], axis=1), jnp.concatenate([-sin, sin, z], axis=1))


def _route(top_idx, n_tok):
    bm = MOE_BM
    onehot = (top_idx[:, :, None] == jnp.arange(N_EXPERTS, dtype=jnp.int32)).sum(axis=1).astype(jnp.int32)
    incl = jnp.cumsum(onehot, axis=0)
    counts = incl[-1]
    pos = incl - onehot
    padded = (counts + bm - 1) // bm * bm
    padded_end = jnp.cumsum(padded)
    padded_start = padded_end - padded
    slot = jnp.take_along_axis(pos + padded_start[None, :], top_idx, axis=1)
    n_blocks = n_tok * TOP_K // bm + N_EXPERTS
    block_start = jnp.arange(n_blocks, dtype=jnp.int32) * bm
    block_e = jnp.minimum((padded_end[None, :] <= block_start[:, None]).sum(axis=1),
                          N_EXPERTS - 1).astype(jnp.int32)
    n_used = (padded_end[-1] // bm).astype(jnp.int32).reshape(1)
    return slot.astype(jnp.int32), block_e, n_used, n_blocks * bm


def kernel(x, positions, w_in, conv_w, q_norm_g, w_uq, kv_norm_g, w_uk, w_uv, w_conv_branch,
           w_attn_branch, w_out, ln1_g, ln1_b, w_router, b_router, w_gate_up, b_gate_up, w_down,
           b_down, ln2_g, ln2_b):
    b, s, d = x.shape
    t = b * s
    x2d = x.reshape(t, d)
    cosb, sinb = _rope_tables(positions)
    for l in range(DEPTH):
        wg3, wlat, wgates, wuqa, wuqb = _prepare_weights(w_in[l], w_uq[l])
        q, kn, kpe, v, ycg, sga = _proj_call(
            x2d, cosb, sinb, wg3, wlat, wgates, conv_w[l], q_norm_g[l][None, :], kv_norm_g[l][None, :],
            wuqa, wuqb, w_uk[l].astype(_BF), w_uv[l].astype(_BF), w_conv_branch[l].astype(_BF), s)
        attn = _attn_call(q.reshape(b, s, -1), kn.reshape(b, s, -1), kpe.reshape(b, s, -1),
                          v.reshape(b, s, -1)).reshape(t, -1)
        h1, h1pa, h1pb, idx_pad, tw_pad = _post_call(
            attn, ycg, sga, x2d, w_attn_branch[l].astype(_BF), w_out[l].astype(_BF),
            ln1_g[l][None, :], ln1_b[l][None, :], w_router[l], b_router[l][None, :])
        top_idx = idx_pad[:, :TOP_K]
        slot, block_e, n_used, n_slots = _route(top_idx, t)
        slot_km = slot.T
        xa, xb = _sc_scatter_rows([h1pa, h1pb], slot_km, n_slots)
        ya, yb = _moe_call(block_e, n_used, xa, xb, w_gate_up[l], b_gate_up[l][:, None, :],
                           w_down[l], b_down[l][:, None, :])
        yga, ygb = _sc_gather_rows([ya, yb], slot_km.reshape(-1))
        x2d = _combine_call(yga, ygb, tw_pad, h1, ln2_g[l][None, :], ln2_b[l][None, :])
    return x2d.reshape(b, s, d)
```

```python
import functools

import numpy as np
import jax
import jax.numpy as jnp
from jax import lax
from jax.experimental import pallas as pl
from jax.experimental.pallas import tpu as pltpu
from jax.experimental.pallas import tpu_sc as plsc

D_MODEL = 1024
CHUNK = 64
CONV_DIM = D_MODEL
CONV_WIDTH = 3
N_HEADS = 8
QK_NOPE_DIM = 128
QK_ROPE_DIM = 64
V_HEAD_DIM = 128
QK_HEAD_DIM = QK_NOPE_DIM + QK_ROPE_DIM
Q_LORA_RANK = 384
KV_LORA_RANK = 256
ROPE_BASE = 10000.0
N_EXPERTS = 32
TOP_K = 4
D_FF_EXPERT = D_MODEL
SWIGLU_LIMIT = 7.0
SWIGLU_ALPHA = 1.702
LN_EPS = 1e-5
RMS_EPS = 1e-6
DEPTH = 1
DEEPNORM_ALPHA = (2 * DEPTH) ** 0.25

QK_PAD = 256
ROPE_PAD = 128
LAT_DIM = Q_LORA_RANK + KV_LORA_RANK + 2 * ROPE_PAD

PROJ_TM = 512
ATTN_TQ = 512
ATTN_TK = 512
ONES_ROWS = 16
POST_TM = 512
MOE_BM = 512
COMB_TM = 512
SC_WINDOW = 128
PACKED = D_MODEL // 2
HALF = PACKED // 2
HALO = 8
NEG_BIG = -1e30
LOG2E = 1.4426950408889634
VMEM_LIMIT = 56 * 1024 * 1024

_BF = jnp.bfloat16
_F32 = jnp.float32


def _dot(a, b):
    return jnp.dot(a, b, preferred_element_type=_F32)


def _const_spec(shape):
    nd = len(shape)
    return pl.BlockSpec(shape, lambda *_: (0,) * nd, pipeline_mode=pl.Buffered(1))


def _pack_rows(a):
    half = a.shape[1] // 2
    lo = lax.bitcast_convert_type(a[:, :half].astype(_BF).astype(_F32), jnp.int32)
    hi = lax.bitcast_convert_type(a[:, half:].astype(_BF).astype(_F32), jnp.int32)
    return lax.shift_right_logical(lo, 16) | (hi & jnp.int32(-65536))


def _unpack_rows(p):
    lo = lax.bitcast_convert_type(lax.shift_left(p, 16), _F32)
    hi = lax.bitcast_convert_type(p & jnp.int32(-65536), _F32)
    return lo, hi


def _sc_mesh():
    return plsc.VectorSubcoreMesh(core_axis_name="core", subcore_axis_name="subcore")


def _sc_scatter_rows(tables, idx, n_copies, n_out):
    n_rows, width = tables[0].shape
    n_tab = len(tables)
    window = SC_WINDOW

    @functools.partial(pl.kernel, mesh=_sc_mesh(),
                       out_type=[jax.ShapeDtypeStruct((n_out, width), tables[0].dtype)] * n_tab)
    def scatter_kernel(*refs):
        idx_hbm = refs[n_tab]
        for src_hbm, out_hbm in zip(refs[:n_tab], refs[n_tab + 1:]):
            def body(src_vmem, idx_vmem, out_hbm=out_hbm):
                for k in range(n_copies):
                    pltpu.sync_copy(src_vmem, out_hbm.at[idx_vmem.at[k]])

            pltpu.emit_pipeline(
                body,
                grid=(n_rows // window,),
                in_specs=[pl.BlockSpec((window, width), lambda i: (i, 0)),
                          pl.BlockSpec((idx.shape[0], window), lambda i: (0, i))],
                out_specs=[],
                core_axis_name=("core", "subcore"),
                dimension_semantics=(pltpu.PARALLEL,),
            )(src_hbm, idx_hbm)

    return scatter_kernel(*tables, idx)


def _sc_gather_rows(tables, idx, n_copies):
    n_rows = idx.shape[1]
    width = tables[0].shape[1]
    n_tab = len(tables)
    window = SC_WINDOW
    steps = n_rows // window

    @functools.partial(pl.kernel, mesh=_sc_mesh(),
                       out_type=[jax.ShapeDtypeStruct((n_copies * n_rows, width), tables[0].dtype)] * n_tab)
    def gather_kernel(*refs):
        idx_hbm = refs[n_tab]
        for table_hbm, out_hbm in zip(refs[:n_tab], refs[n_tab + 1:]):
            def body(idx_vmem, out_vmem, table_hbm=table_hbm):
                pltpu.sync_copy(table_hbm.at[idx_vmem.at[0]], out_vmem)

            pltpu.emit_pipeline(
                body,
                grid=(n_copies, steps),
                in_specs=[pl.BlockSpec((1, window), lambda k, i: (k, i))],
                out_specs=[pl.BlockSpec((window, width), lambda k, i: (k * steps + i, 0))],
                core_axis_name=("core", "subcore"),
                dimension_semantics=(pltpu.PARALLEL, pltpu.PARALLEL),
            )(idx_hbm, out_hbm)

    return gather_kernel(*tables, idx)


def _proj_kernel(tiles_per_seq,
                 x_ref, pos_ref, rope_ref, wg3_ref, wlat_ref, wgates_ref, convw_ref,
                 qg_ref, kvg_ref, wuqa_ref, wuqb_ref, wuk_ref, wuv_ref, wcb_ref,
                 q_ref, kn_ref, kpe_ref, v_ref, ycg_ref, sga_ref, ubuf):
    i = pl.program_id(0)
    tm = x_ref.shape[0]
    xb = x_ref[...].astype(_BF)

    @pl.when(i % tiles_per_seq == 0)
    def _():
        ubuf[0:HALO, :] = jnp.zeros((HALO, CONV_DIM), _F32)

    g3 = _dot(xb, wg3_ref[...])
    u = g3[:, CONV_DIM:2 * CONV_DIM] * g3[:, 2 * CONV_DIM:]
    ubuf[HALO:HALO + tm, :] = u
    cw = convw_ref[...]
    conv = (cw[2:3, :] * u + cw[1:2, :] * ubuf[HALO - 1:HALO - 1 + tm, :]
            + cw[0:1, :] * ubuf[HALO - 2:HALO - 2 + tm, :])
    ubuf[0:HALO, :] = ubuf[tm:tm + HALO, :]
    yc = _dot((g3[:, :CONV_DIM] * conv).astype(_BF), wcb_ref[...])
    gates = _dot(xb, wgates_ref[...])
    ycg_ref[...] = (jax.nn.sigmoid(gates[:, :D_MODEL]) * yc).astype(_BF)
    sga_ref[...] = jax.nn.sigmoid(gates[:, D_MODEL:]).astype(_BF)

    lat = _dot(xb, wlat_ref[...])
    ang = pos_ref[...].astype(_F32) * rope_ref[0:1, :]
    cosb = jnp.cos(ang) * rope_ref[1:2, :]
    sinb = jnp.sin(ang) * rope_ref[2:3, :]
    q_lat = lat[:, :Q_LORA_RANK]
    rq = q_lat * lax.rsqrt(jnp.mean(q_lat * q_lat, axis=-1, keepdims=True) + RMS_EPS) * qg_ref[...]
    kv_lat = lat[:, Q_LORA_RANK:Q_LORA_RANK + KV_LORA_RANK]
    ckv = (kv_lat * lax.rsqrt(jnp.mean(kv_lat * kv_lat, axis=-1, keepdims=True) + RMS_EPS)
           * kvg_ref[...]).astype(_BF)
    o = Q_LORA_RANK + KV_LORA_RANK
    kpe_ref[...] = (lat[:, o:o + ROPE_PAD] * cosb + lat[:, o + ROPE_PAD:] * sinb).astype(_BF)
    kn_ref[...] = _dot(ckv, wuk_ref[...]).astype(_BF)
    v_ref[...] = _dot(ckv, wuv_ref[...]).astype(_BF)

    rqb = rq.astype(_BF)
    qa = _dot(rqb, wuqa_ref[...])
    qb = _dot(rqb, wuqb_ref[...])
    scale = QK_HEAD_DIM ** -0.5 * LOG2E
    for h in range(N_HEADS):
        lo = h * QK_PAD
        q_ref[:, lo:lo + QK_NOPE_DIM] = (qa[:, lo:lo + QK_NOPE_DIM] * scale).astype(_BF)
        hi = (qa[:, lo + QK_NOPE_DIM:lo + QK_PAD] * cosb
              + qb[:, h * ROPE_PAD:(h + 1) * ROPE_PAD] * sinb)
        q_ref[:, lo + QK_NOPE_DIM:lo + QK_PAD] = (hi * scale).astype(_BF)


def _proj_call(x2d, pos, rope, wg3, wlat, wgates, convw, qg, kvg, wuqa, wuqb, wuk, wuv, wcb, seq):
    t = x2d.shape[0]
    tm = PROJ_TM
    row = lambda w: pl.BlockSpec((tm, w), lambda i: (i, 0))
    outs = [
        jax.ShapeDtypeStruct((t, N_HEADS * QK_PAD), _BF),
        jax.ShapeDtypeStruct((t, N_HEADS * QK_NOPE_DIM), _BF),
        jax.ShapeDtypeStruct((t, ROPE_PAD), _BF),
        jax.ShapeDtypeStruct((t, N_HEADS * V_HEAD_DIM), _BF),
        jax.ShapeDtypeStruct((t, D_MODEL), _BF),
        jax.ShapeDtypeStruct((t, D_MODEL), _BF),
    ]
    return pl.pallas_call(
        functools.partial(_proj_kernel, seq // tm),
        grid=(t // tm,),
        in_specs=[row(D_MODEL), row(1)]
        + [_const_spec(a.shape)
           for a in (rope, wg3, wlat, wgates, convw, qg, kvg, wuqa, wuqb, wuk, wuv, wcb)],
        out_specs=[row(s.shape[1]) for s in outs],
        out_shape=outs,
        scratch_shapes=[pltpu.VMEM((tm + HALO, CONV_DIM), _F32)],
        compiler_params=pltpu.CompilerParams(dimension_semantics=("arbitrary",),
                                             vmem_limit_bytes=VMEM_LIMIT),
        name="proj_conv_qkv",
    )(x2d, pos, rope, wg3, wlat, wgates, convw, qg, kvg, wuqa, wuqb, wuk, wuv, wcb)


def _attn_kernel(q_ref, kn_ref, kpe_ref, v_ref, o_ref, kfull, vt, s0, s1, p0, p1, a0, a1, m_ref, acc):
    i = pl.program_id(2)
    tq = q_ref.shape[0]
    tk = ATTN_TK
    s_bufs, p_bufs, a_bufs = (s0, s1), (p0, p1), (a0, a1)

    @pl.when(i == 0)
    def _():
        kfull[:, :QK_NOPE_DIM] = kn_ref[...]
        kfull[:, QK_NOPE_DIM:] = kpe_ref[...]
        vt[:V_HEAD_DIM, :] = v_ref[...].T
        vt[V_HEAD_DIM:, :] = jnp.ones((vt.shape[0] - V_HEAD_DIM, vt.shape[1]), _BF)

    def scores_t(j, dst):
        k = kfull[pl.ds(pl.multiple_of(j * tk, tk), tk), :]
        dst[...] = lax.dot_general(k, q_ref[...], (((1,), (1,)), ((), ())),
                                   preferred_element_type=_F32)

    def softmax_t(par, mask):
        s_t = s_bufs[par][...]
        if mask is not None:
            s_t = jnp.where(mask, s_t, NEG_BIG)
        m_old = m_ref[...]
        m_new = jnp.maximum(m_old, jnp.max(s_t, axis=0, keepdims=True))
        m_ref[...] = m_new
        a_bufs[par][...] = jnp.exp2(m_old - m_new)
        p_bufs[par][...] = jnp.exp2(s_t - m_new).astype(_BF)

    def accumulate(j, par):
        v_t = vt[:, pl.ds(pl.multiple_of(j * tk, tk), tk)]
        acc[...] = a_bufs[par][...] * acc[...] + _dot(v_t, p_bufs[par][...])

    def stage(j, par):
        softmax_t(par, None)
        scores_t(j + 1, s_bufs[1 - par])
        accumulate(jnp.maximum(j - 1, 0), 1 - par)

    def finish(par):
        accumulate(jnp.maximum(i - 1, 0), 1 - par)
        key_chunk = lax.broadcasted_iota(jnp.int32, (tk, tq), 0) // CHUNK
        qry_chunk = lax.broadcasted_iota(jnp.int32, (tk, tq), 1) // CHUNK
        softmax_t(par, key_chunk <= qry_chunk)
        accumulate(i, par)
        out_t = acc[:V_HEAD_DIM, :] / acc[V_HEAD_DIM:V_HEAD_DIM + 1, :]
        o_ref[...] = out_t.T.astype(_BF)

    scores_t(0, s0)
    p1[...] = jnp.zeros(p1.shape, _BF)
    a1[...] = jnp.ones(a1.shape, _F32)
    m_ref[...] = jnp.full(m_ref.shape, NEG_BIG, _F32)
    acc[...] = jnp.zeros(acc.shape, _F32)

    @pl.loop(0, i // 2)
    def _(jj):
        stage(2 * jj, 0)
        stage(2 * jj + 1, 1)

    @pl.when(i % 2 == 0)
    def _():
        finish(0)

    @pl.when(i % 2 == 1)
    def _():
        stage(i - 1, 0)
        finish(1)


def _attn_call(q, kn, kpe, v):
    b, s, _ = q.shape
    tq = ATTN_TQ
    return pl.pallas_call(
        _attn_kernel,
        grid=(b, N_HEADS, s // tq),
        in_specs=[
            pl.BlockSpec((None, tq, QK_PAD), lambda bi, h, i: (bi, i, h)),
            pl.BlockSpec((None, s, QK_NOPE_DIM), lambda bi, h, i: (bi, 0, h)),
            pl.BlockSpec((None, s, ROPE_PAD), lambda bi, h, i: (bi, 0, 0)),
            pl.BlockSpec((None, s, V_HEAD_DIM), lambda bi, h, i: (bi, 0, h)),
        ],
        out_specs=pl.BlockSpec((None, tq, V_HEAD_DIM), lambda bi, h, i: (bi, i, h)),
        out_shape=jax.ShapeDtypeStruct((b, s, N_HEADS * V_HEAD_DIM), _BF),
        scratch_shapes=[pltpu.VMEM((s, QK_PAD), _BF), pltpu.VMEM((V_HEAD_DIM + ONES_ROWS, s), _BF),
                        pltpu.VMEM((ATTN_TK, tq), _F32), pltpu.VMEM((ATTN_TK, tq), _F32),
                        pltpu.VMEM((ATTN_TK, tq), _BF), pltpu.VMEM((ATTN_TK, tq), _BF),
                        pltpu.VMEM((1, tq), _F32), pltpu.VMEM((1, tq), _F32), pltpu.VMEM((1, tq), _F32),
                        pltpu.VMEM((V_HEAD_DIM + ONES_ROWS, tq), _F32)],
        compiler_params=pltpu.CompilerParams(
            dimension_semantics=("arbitrary", "arbitrary", "arbitrary"),
            vmem_limit_bytes=VMEM_LIMIT),
        name="chunk_causal_attention",
    )(q, kn, kpe, v)


def _layer_norm(z, g, b):
    mu = jnp.mean(z, axis=-1, keepdims=True)
    zc = z - mu
    var = jnp.mean(zc * zc, axis=-1, keepdims=True)
    return zc * lax.rsqrt(var + LN_EPS) * g + b


def _post_kernel(expert_cap, attn_ref, ycg_ref, sga_ref, x_ref, wab_ref, wout_ref, g1_ref, b1_ref,
                 wr_ref, br_ref, tri_ref, h1_ref, h1pa_ref, h1pb_ref, slot_ref, tw_ref, cnt_ref, run):
    tm = x_ref.shape[0]

    @pl.when(pl.program_id(0) == 0)
    def _():
        run[...] = jnp.zeros(run.shape, _F32)

    y_attn = _dot(attn_ref[...], wab_ref[...])
    merged = ycg_ref[...].astype(_F32) + sga_ref[...].astype(_F32) * y_attn
    mix = _dot(merged.astype(_BF), wout_ref[...])
    h1 = _layer_norm(DEEPNORM_ALPHA * x_ref[...] + mix, g1_ref[...], b1_ref[...])
    h1_ref[...] = h1
    h1p = _pack_rows(h1)
    h1pa_ref[...] = h1p[:, :HALF]
    h1pb_ref[...] = h1p[:, HALF:]

    nt = (((1,), (1,)), ((), ()))
    h_hi = h1.astype(_BF)
    h_lo = (h1 - h_hi.astype(_F32)).astype(_BF)
    both = lax.dot_general(wr_ref[...], h_hi, nt, preferred_element_type=_F32)
    logits = (both[:N_EXPERTS] + both[N_EXPERTS:] + br_ref[:, 0:1]
              + lax.dot_general(wr_ref[:N_EXPERTS, :], h_lo, nt, preferred_element_type=_F32))
    expert = lax.broadcasted_iota(jnp.int32, (N_EXPERTS, tm), 0)
    work = logits
    chosen = jnp.zeros((N_EXPERTS, tm), jnp.bool_)
    sels, exps = [], []
    for k in range(TOP_K):
        mx = jnp.max(work, axis=0, keepdims=True)
        sel = jnp.min(jnp.where(work == mx, expert, N_EXPERTS), axis=0, keepdims=True)
        hit = expert == sel
        sels.append(sel)
        exps.append(jnp.exp(mx - (mx if k == 0 else top0)))
        if k == 0:
            top0 = mx
        chosen = jnp.logical_or(chosen, hit)
        work = jnp.where(hit, -jnp.inf, work)
    denom = exps[0] + exps[1] + exps[2] + exps[3]

    onehot = chosen.astype(_F32)
    before = _dot(onehot.astype(_BF), tri_ref[...]) + run[:, 0:1]
    rows = []
    for k in range(TOP_K):
        rank = jnp.sum(jnp.where(expert == sels[k], before, 0.0), axis=0, keepdims=True)
        rows.append(sels[k] * expert_cap + rank.astype(jnp.int32))
    slot_ref[...] = jnp.concatenate(rows + [jnp.zeros((8 - TOP_K, tm), jnp.int32)], axis=0)
    run[...] = run[...] + jnp.sum(onehot, axis=1, keepdims=True)
    cnt_ref[...] = run[...]
    tw_t = jnp.concatenate([e / denom for e in exps] + [jnp.zeros((128 - TOP_K, tm), _F32)], axis=0)
    tw_ref[...] = tw_t.T


def _post_call(attn, ycg, sga, x2d, wab, wout, g1, b1, wr, br):
    t = x2d.shape[0]
    tm = POST_TM
    row = lambda w: pl.BlockSpec((tm, w), lambda i: (i, 0))
    tri = (jnp.arange(tm)[:, None] < jnp.arange(tm)[None, :]).astype(_BF)
    outs = [
        jax.ShapeDtypeStruct((t, D_MODEL), _F32),
        jax.ShapeDtypeStruct((t, HALF), jnp.int32),
        jax.ShapeDtypeStruct((t, HALF), jnp.int32),
        jax.ShapeDtypeStruct((8, t), jnp.int32),
        jax.ShapeDtypeStruct((t, 128), _F32),
        jax.ShapeDtypeStruct((N_EXPERTS, 128), _F32),
    ]
    out_specs = [row(D_MODEL), row(HALF), row(HALF), pl.BlockSpec((8, tm), lambda i: (0, i)),
                 row(128), pl.BlockSpec((N_EXPERTS, 128), lambda i: (0, 0))]
    return pl.pallas_call(
        functools.partial(_post_kernel, t),
        grid=(t // tm,),
        in_specs=[row(D_MODEL)] * 4
        + [_const_spec(a.shape) for a in (wab, wout, g1, b1, wr, br, tri)],
        out_specs=out_specs,
        out_shape=outs,
        scratch_shapes=[pltpu.VMEM((N_EXPERTS, 128), _F32)],
        compiler_params=pltpu.CompilerParams(dimension_semantics=("arbitrary",),
                                             vmem_limit_bytes=VMEM_LIMIT),
        name="merge_out_ln1_router",
    )(attn, ycg, sga, x2d, wab, wout, g1, b1, wr, br, tri)


def _moe_kernel(be_ref, rb_ref, nused_ref, xa_ref, xb_ref, wgu_ref, bgu_ref, wd_ref, bd_ref,
                ya_ref, yb_ref, wgu_bf, wd_bf):
    i = pl.program_id(0)
    prev = be_ref[jnp.maximum(i - 1, 0)]
    fresh = jnp.logical_or(i == 0, be_ref[i] != prev)
    used = i < nused_ref[0]

    @pl.when(jnp.logical_and(fresh, used))
    def _():
        wgu_bf[...] = wgu_ref[...].astype(_BF)
        wd_bf[...] = wd_ref[...].astype(_BF)

    @pl.when(used)
    def _():
        a_lo, a_hi = _unpack_rows(xa_ref[...])
        b_lo, b_hi = _unpack_rows(xb_ref[...])
        xs = jnp.concatenate([a_lo, b_lo, a_hi, b_hi], axis=1).astype(_BF)
        gu = _dot(xs, wgu_bf[...]) + bgu_ref[...]
        gate = jnp.minimum(gu[:, :D_FF_EXPERT], SWIGLU_LIMIT)
        up = jnp.clip(gu[:, D_FF_EXPERT:], -SWIGLU_LIMIT, SWIGLU_LIMIT)
        hid = (up + 1.0) * (gate * jax.nn.sigmoid(SWIGLU_ALPHA * gate))
        yp = _pack_rows(_dot(hid.astype(_BF), wd_bf[...]) + bd_ref[...])
        ya_ref[...] = yp[:, :HALF]
        yb_ref[...] = yp[:, HALF:]


def _moe_call(block_e, block_row, n_used, xa, xb, wgu, bgu, wd, bd):
    n_slots = xa.shape[0]
    bm = MOE_BM
    n_blocks = block_e.shape[0]
    grid_spec = pltpu.PrefetchScalarGridSpec(
        num_scalar_prefetch=3,
        grid=(n_blocks,),
        in_specs=[
            pl.BlockSpec((bm, HALF), lambda i, be, rb, nu: (rb[i], 0)),
            pl.BlockSpec((bm, HALF), lambda i, be, rb, nu: (rb[i], 0)),
            pl.BlockSpec((None, D_MODEL, 2 * D_FF_EXPERT), lambda i, be, rb, nu: (be[i], 0, 0)),
            pl.BlockSpec((None, 1, 2 * D_FF_EXPERT), lambda i, be, rb, nu: (be[i], 0, 0)),
            pl.BlockSpec((None, D_FF_EXPERT, D_MODEL), lambda i, be, rb, nu: (be[i], 0, 0)),
            pl.BlockSpec((None, 1, D_MODEL), lambda i, be, rb, nu: (be[i], 0, 0)),
        ],
        out_specs=[pl.BlockSpec((bm, HALF), lambda i, be, rb, nu: (rb[i], 0))] * 2,
        scratch_shapes=[pltpu.VMEM((D_MODEL, 2 * D_FF_EXPERT), _BF),
                        pltpu.VMEM((D_FF_EXPERT, D_MODEL), _BF)],
    )
    return pl.pallas_call(
        _moe_kernel,
        grid_spec=grid_spec,
        out_shape=[jax.ShapeDtypeStruct((n_slots, HALF), jnp.int32)] * 2,
        compiler_params=pltpu.CompilerParams(dimension_semantics=("arbitrary",),
                                             vmem_limit_bytes=VMEM_LIMIT),
        name="expert_ffn",
    )(block_e, block_row, n_used, xa, xb, wgu, bgu, wd, bd)


def _combine_kernel(*refs):
    ya_refs, yb_refs = refs[:TOP_K], refs[TOP_K:2 * TOP_K]
    tw_ref, h1_ref, g2_ref, b2_ref, o_ref = refs[2 * TOP_K:]
    tw = tw_ref[...]
    parts = [jnp.zeros((tw.shape[0], HALF), _F32) for _ in range(4)]
    for k in range(TOP_K):
        a_lo, a_hi = _unpack_rows(ya_refs[k][...])
        b_lo, b_hi = _unpack_rows(yb_refs[k][...])
        w = tw[:, k:k + 1]
        parts = [p + w * y for p, y in zip(parts, (a_lo, b_lo, a_hi, b_hi))]
    ffn = jnp.concatenate(parts, axis=1)
    o_ref[...] = _layer_norm(DEEPNORM_ALPHA * h1_ref[...] + ffn, g2_ref[...], b2_ref[...])


def _combine_call(yga, ygb, tw, h1, g2, b2):
    t = h1.shape[0]
    tm = COMB_TM
    row = lambda w: pl.BlockSpec((tm, w), lambda i: (i, 0))
    krow = lambda k: pl.BlockSpec((tm, HALF), lambda i: (k * (t // tm) + i, 0))
    return pl.pallas_call(
        _combine_kernel,
        grid=(t // tm,),
        in_specs=[krow(k) for k in range(TOP_K)] * 2 + [row(128), row(D_MODEL),
                  _const_spec(g2.shape), _const_spec(b2.shape)],
        out_specs=row(D_MODEL),
        out_shape=jax.ShapeDtypeStruct((t, D_MODEL), _F32),
        compiler_params=pltpu.CompilerParams(dimension_semantics=("arbitrary",),
                                             vmem_limit_bytes=VMEM_LIMIT),
        name="combine_ln2",
    )(*([yga] * TOP_K + [ygb] * TOP_K), tw, h1, g2, b2)


def _prepare_weights(w_in, w_uq):
    o_q = 3 * CONV_DIM
    o_kv = o_q + Q_LORA_RANK
    o_pe = o_kv + KV_LORA_RANK
    o_gc = o_pe + QK_ROPE_DIM
    half = QK_ROPE_DIM // 2
    wg3 = w_in[:, :o_q]
    kpe = w_in[:, o_pe:o_gc]
    kpe_sw = jnp.concatenate([kpe[:, half:], kpe[:, :half]], axis=1)
    zpad = jnp.zeros((D_MODEL, ROPE_PAD - QK_ROPE_DIM), w_in.dtype)
    wlat = jnp.concatenate([w_in[:, o_q:o_pe], kpe, zpad, kpe_sw, zpad], axis=1)
    wgates = w_in[:, o_gc:]
    wq = w_uq.reshape(Q_LORA_RANK, N_HEADS, QK_HEAD_DIM)
    wuqa = jnp.pad(wq, ((0, 0), (0, 0), (0, QK_PAD - QK_HEAD_DIM)))
    pe = wq[:, :, QK_NOPE_DIM:]
    pe_sw = jnp.concatenate([pe[:, :, half:], pe[:, :, :half]], axis=2)
    wuqb = jnp.pad(pe_sw, ((0, 0), (0, 0), (0, ROPE_PAD - QK_ROPE_DIM)))
    return (wg3.astype(_BF), wlat.astype(_BF), wgates.astype(_BF),
            wuqa.reshape(Q_LORA_RANK, N_HEADS * QK_PAD).astype(_BF),
            wuqb.reshape(Q_LORA_RANK, N_HEADS * ROPE_PAD).astype(_BF))


def _rope_rows():
    half = QK_ROPE_DIM // 2
    inv_freq = ROPE_BASE ** (-jnp.arange(0, QK_ROPE_DIM, 2, dtype=_F32) / QK_ROPE_DIM)
    z = jnp.zeros((ROPE_PAD - QK_ROPE_DIM,), _F32)
    ones = jnp.ones((half,), _F32)
    rows = jnp.stack([jnp.concatenate([inv_freq, inv_freq, z]),
                      jnp.concatenate([ones, ones, z]),
                      jnp.concatenate([-ones, ones, z])])
    return jnp.pad(rows, ((0, 8 - rows.shape[0]), (0, 0)))


def _block_tables(counts, n_tok):
    bm = MOE_BM
    n_blocks = n_tok * TOP_K // bm + N_EXPERTS
    nblk = (counts + bm - 1) // bm
    end = jnp.cumsum(nblk)
    start = end - nblk
    n_used = end[-1]
    i = jnp.minimum(jnp.arange(n_blocks, dtype=jnp.int32), n_used - 1)
    block_e = (end[None, :] <= i[:, None]).sum(axis=1).astype(jnp.int32)
    first = jnp.sum(jnp.where(jnp.arange(N_EXPERTS)[None, :] == block_e[:, None], start[None, :], 0), axis=1)
    block_row = block_e * (n_tok // bm) + (i - first)
    return block_e, block_row.astype(jnp.int32), n_used.astype(jnp.int32).reshape(1)


def kernel(x, positions, w_in, conv_w, q_norm_g, w_uq, kv_norm_g, w_uk, w_uv, w_conv_branch,
           w_attn_branch, w_out, ln1_g, ln1_b, w_router, b_router, w_gate_up, b_gate_up, w_down,
           b_down, ln2_g, ln2_b):
    b, s, d = x.shape
    t = b * s
    x2d = x.reshape(t, d)
    pos = positions.reshape(t, 1).astype(jnp.int32)
    rope = _rope_rows()
    for l in range(DEPTH):
        wg3, wlat, wgates, wuqa, wuqb = _prepare_weights(w_in[l], w_uq[l])
        q, kn, kpe, v, ycg, sga = _proj_call(
            x2d, pos, rope, wg3, wlat, wgates, conv_w[l], q_norm_g[l][None, :], kv_norm_g[l][None, :],
            wuqa, wuqb, w_uk[l].astype(_BF), w_uv[l].astype(_BF), w_conv_branch[l].astype(_BF), s)
        attn = _attn_call(q.reshape(b, s, -1), kn.reshape(b, s, -1), kpe.reshape(b, s, -1),
                          v.reshape(b, s, -1)).reshape(t, -1)
        wr_t = w_router[l].T
        wr_hi = wr_t.astype(_BF)
        wr_lo = (wr_t - wr_hi.astype(_F32)).astype(_BF)
        h1, h1pa, h1pb, slots, tw_pad, counts = _post_call(
            attn, ycg, sga, x2d, w_attn_branch[l].astype(_BF), w_out[l].astype(_BF),
            ln1_g[l][None, :], ln1_b[l][None, :], jnp.concatenate([wr_hi, wr_lo], axis=0),
            jnp.broadcast_to(b_router[l][:, None], (N_EXPERTS, 128)))
        block_e, block_row, n_used = _block_tables(counts[:, 0].astype(jnp.int32), t)
        xa, xb = _sc_scatter_rows([h1pa, h1pb], slots, TOP_K, N_EXPERTS * t)
        ya, yb = _moe_call(block_e, block_row, n_used, xa, xb, w_gate_up[l], b_gate_up[l][:, None, :],
                           w_down[l], b_down[l][:, None, :])
        yga, ygb = _sc_gather_rows([ya, yb], slots, TOP_K)
        x2d = _combine_call(yga, ygb, tw_pad, h1, ln2_g[l][None, :], ln2_b[l][None, :])
    return x2d.reshape(b, s, d)
```

```python
import functools

import numpy as np
import jax
import jax.numpy as jnp
from jax import lax
from jax.experimental import pallas as pl
from jax.experimental.pallas import tpu as pltpu
from jax.experimental.pallas import tpu_sc as plsc

D_MODEL = 1024
CHUNK = 64
CONV_DIM = D_MODEL
CONV_WIDTH = 3
N_HEADS = 8
QK_NOPE_DIM = 128
QK_ROPE_DIM = 64
V_HEAD_DIM = 128
QK_HEAD_DIM = QK_NOPE_DIM + QK_ROPE_DIM
Q_LORA_RANK = 384
KV_LORA_RANK = 256
ROPE_BASE = 10000.0
N_EXPERTS = 32
TOP_K = 4
D_FF_EXPERT = D_MODEL
SWIGLU_LIMIT = 7.0
SWIGLU_ALPHA = 1.702
LN_EPS = 1e-5
RMS_EPS = 1e-6
DEPTH = 1
DEEPNORM_ALPHA = (2 * DEPTH) ** 0.25

QK_PAD = 256
ROPE_PAD = 128
LAT_DIM = Q_LORA_RANK + KV_LORA_RANK + 2 * ROPE_PAD

PROJ_TM = 512
ATTN_TQ = 512
ATTN_TK = 512
ONES_ROWS = 16
POST_TM = 512
MOE_BM = 512
MOE_FF_CHUNK = 256
COMB_TM = 512
SC_WINDOW = 128
PACKED = D_MODEL // 2
HALF = PACKED // 2
HALO = 8
NEG_BIG = -1e30
LOG2E = 1.4426950408889634
VMEM_LIMIT = 56 * 1024 * 1024

_BF = jnp.bfloat16
_F32 = jnp.float32


def _dot(a, b):
    return jnp.dot(a, b, preferred_element_type=_F32)


def _const_spec(shape):
    nd = len(shape)
    return pl.BlockSpec(shape, lambda *_: (0,) * nd, pipeline_mode=pl.Buffered(1))


def _pack_rows(a):
    half = a.shape[1] // 2
    lo = lax.bitcast_convert_type(a[:, :half].astype(_BF).astype(_F32), jnp.int32)
    hi = lax.bitcast_convert_type(a[:, half:].astype(_BF).astype(_F32), jnp.int32)
    return lax.shift_right_logical(lo, 16) | (hi & jnp.int32(-65536))


def _unpack_rows(p):
    lo = lax.bitcast_convert_type(lax.shift_left(p, 16), _F32)
    hi = lax.bitcast_convert_type(p & jnp.int32(-65536), _F32)
    return lo, hi


def _sc_mesh():
    return plsc.VectorSubcoreMesh(core_axis_name="core", subcore_axis_name="subcore")


def _sc_scatter_rows(tables, idx, n_copies, n_out):
    n_rows, width = tables[0].shape
    n_tab = len(tables)
    window = SC_WINDOW

    @functools.partial(pl.kernel, mesh=_sc_mesh(),
                       out_type=[jax.ShapeDtypeStruct((n_out, width), tables[0].dtype)] * n_tab)
    def scatter_kernel(*refs):
        idx_hbm = refs[n_tab]
        for src_hbm, out_hbm in zip(refs[:n_tab], refs[n_tab + 1:]):
            def body(src_vmem, idx_vmem, out_hbm=out_hbm):
                for k in range(n_copies):
                    pltpu.sync_copy(src_vmem, out_hbm.at[idx_vmem.at[k]])

            pltpu.emit_pipeline(
                body,
                grid=(n_rows // window,),
                in_specs=[pl.BlockSpec((window, width), lambda i: (i, 0)),
                          pl.BlockSpec((idx.shape[0], window), lambda i: (0, i))],
                out_specs=[],
                core_axis_name=("core", "subcore"),
                dimension_semantics=(pltpu.PARALLEL,),
            )(src_hbm, idx_hbm)

    return scatter_kernel(*tables, idx)


def _sc_gather_rows(tables, idx, n_copies):
    n_rows = idx.shape[1]
    width = tables[0].shape[1]
    n_tab = len(tables)
    window = SC_WINDOW
    steps = n_rows // window

    @functools.partial(pl.kernel, mesh=_sc_mesh(),
                       out_type=[jax.ShapeDtypeStruct((n_copies * n_rows, width), tables[0].dtype)] * n_tab)
    def gather_kernel(*refs):
        idx_hbm = refs[n_tab]
        for table_hbm, out_hbm in zip(refs[:n_tab], refs[n_tab + 1:]):
            def body(idx_vmem, out_vmem, table_hbm=table_hbm):
                pltpu.sync_copy(table_hbm.at[idx_vmem.at[0]], out_vmem)

            pltpu.emit_pipeline(
                body,
                grid=(n_copies, steps),
                in_specs=[pl.BlockSpec((1, window), lambda k, i: (k, i))],
                out_specs=[pl.BlockSpec((window, width), lambda k, i: (k * steps + i, 0))],
                core_axis_name=("core", "subcore"),
                dimension_semantics=(pltpu.PARALLEL, pltpu.PARALLEL),
            )(idx_hbm, out_hbm)

    return gather_kernel(*tables, idx)


def _proj_kernel(tiles_per_seq,
                 x_ref, pos_ref, rope_ref, wg3_ref, wlat_ref, wgates_ref, convw_ref,
                 qg_ref, kvg_ref, wuqa_ref, wuqb_ref, wuk_ref, wuv_ref, wcb_ref,
                 q_ref, kn_ref, kpe_ref, v_ref, ycg_ref, sga_ref, ubuf):
    i = pl.program_id(0)
    tm = x_ref.shape[0]
    xb = x_ref[...].astype(_BF)

    @pl.when(i % tiles_per_seq == 0)
    def _():
        ubuf[0:HALO, :] = jnp.zeros((HALO, CONV_DIM), _F32)

    g3 = _dot(xb, wg3_ref[...])
    u = g3[:, CONV_DIM:2 * CONV_DIM] * g3[:, 2 * CONV_DIM:]
    ubuf[HALO:HALO + tm, :] = u
    cw = convw_ref[...]
    conv = (cw[2:3, :] * u + cw[1:2, :] * ubuf[HALO - 1:HALO - 1 + tm, :]
            + cw[0:1, :] * ubuf[HALO - 2:HALO - 2 + tm, :])
    ubuf[0:HALO, :] = ubuf[tm:tm + HALO, :]
    yc = _dot((g3[:, :CONV_DIM] * conv).astype(_BF), wcb_ref[...])
    gates = _dot(xb, wgates_ref[...])
    ycg_ref[...] = (jax.nn.sigmoid(gates[:, :D_MODEL]) * yc).astype(_BF)
    sga_ref[...] = jax.nn.sigmoid(gates[:, D_MODEL:]).astype(_BF)

    lat = _dot(xb, wlat_ref[...])
    ang = pos_ref[...].astype(_F32) * rope_ref[0:1, :]
    cosb = jnp.cos(ang) * rope_ref[1:2, :]
    sinb = jnp.sin(ang) * rope_ref[2:3, :]
    q_lat = lat[:, :Q_LORA_RANK]
    rq = q_lat * lax.rsqrt(jnp.mean(q_lat * q_lat, axis=-1, keepdims=True) + RMS_EPS) * qg_ref[...]
    kv_lat = lat[:, Q_LORA_RANK:Q_LORA_RANK + KV_LORA_RANK]
    ckv = (kv_lat * lax.rsqrt(jnp.mean(kv_lat * kv_lat, axis=-1, keepdims=True) + RMS_EPS)
           * kvg_ref[...]).astype(_BF)
    o = Q_LORA_RANK + KV_LORA_RANK
    kpe_ref[...] = (lat[:, o:o + ROPE_PAD] * cosb + lat[:, o + ROPE_PAD:] * sinb).astype(_BF)
    kn = _dot(ckv, wuk_ref[...]).astype(_BF)
    vv = _dot(ckv, wuv_ref[...]).astype(_BF)
    for h in range(N_HEADS):
        kn_ref[h] = kn[:, h * QK_NOPE_DIM:(h + 1) * QK_NOPE_DIM]
        v_ref[h] = vv[:, h * V_HEAD_DIM:(h + 1) * V_HEAD_DIM]

    rqb = rq.astype(_BF)
    qa = _dot(rqb, wuqa_ref[...])
    qb = _dot(rqb, wuqb_ref[...])
    scale = QK_HEAD_DIM ** -0.5 * LOG2E
    for h in range(N_HEADS):
        lo = h * QK_PAD
        q_ref[h, :, :QK_NOPE_DIM] = (qa[:, lo:lo + QK_NOPE_DIM] * scale).astype(_BF)
        hi = (qa[:, lo + QK_NOPE_DIM:lo + QK_PAD] * cosb
              + qb[:, h * ROPE_PAD:(h + 1) * ROPE_PAD] * sinb)
        q_ref[h, :, QK_NOPE_DIM:] = (hi * scale).astype(_BF)


def _proj_call(x2d, pos, rope, wg3, wlat, wgates, convw, qg, kvg, wuqa, wuqb, wuk, wuv, wcb, seq):
    t = x2d.shape[0]
    tm = PROJ_TM
    tps = seq // tm
    row = lambda w: pl.BlockSpec((tm, w), lambda i: (i, 0))
    heads = lambda w: pl.BlockSpec((None, N_HEADS, tm, w), lambda i: (i // tps, 0, i % tps, 0))
    head_major = lambda w: jax.ShapeDtypeStruct((t // seq, N_HEADS, seq, w), _BF)
    outs = [
        head_major(QK_PAD),
        head_major(QK_NOPE_DIM),
        jax.ShapeDtypeStruct((t, ROPE_PAD), _BF),
        head_major(V_HEAD_DIM),
        jax.ShapeDtypeStruct((t, D_MODEL), _BF),
        jax.ShapeDtypeStruct((t, D_MODEL), _BF),
    ]
    return pl.pallas_call(
        functools.partial(_proj_kernel, tps),
        grid=(t // tm,),
        in_specs=[row(D_MODEL), row(1)]
        + [_const_spec(a.shape)
           for a in (rope, wg3, wlat, wgates, convw, qg, kvg, wuqa, wuqb, wuk, wuv, wcb)],
        out_specs=[heads(QK_PAD), heads(QK_NOPE_DIM), row(ROPE_PAD), heads(V_HEAD_DIM),
                   row(D_MODEL), row(D_MODEL)],
        out_shape=outs,
        scratch_shapes=[pltpu.VMEM((tm + HALO, CONV_DIM), _F32)],
        compiler_params=pltpu.CompilerParams(dimension_semantics=("arbitrary",),
                                             vmem_limit_bytes=VMEM_LIMIT),
        name="proj_conv_qkv",
    )(x2d, pos, rope, wg3, wlat, wgates, convw, qg, kvg, wuqa, wuqb, wuk, wuv, wcb)


def _attn_kernel(q_ref, kn_ref, kpe_ref, v_ref, o_ref, kfull, vt, s0, s1, p0, p1, a0, a1, m_ref, acc):
    i = pl.program_id(2)
    tq = q_ref.shape[0]
    tk = ATTN_TK
    s_bufs, p_bufs, a_bufs = (s0, s1), (p0, p1), (a0, a1)

    @pl.when(i == 0)
    def _():
        kfull[:, :QK_NOPE_DIM] = kn_ref[...]
        kfull[:, QK_NOPE_DIM:] = kpe_ref[...]
        vt[:V_HEAD_DIM, :] = v_ref[...].T
        vt[V_HEAD_DIM:, :] = jnp.ones((vt.shape[0] - V_HEAD_DIM, vt.shape[1]), _BF)

    def scores_t(j, dst):
        k = kfull[pl.ds(pl.multiple_of(j * tk, tk), tk), :]
        dst[...] = lax.dot_general(k, q_ref[...], (((1,), (1,)), ((), ())),
                                   preferred_element_type=_F32)

    def softmax_t(par, mask):
        s_t = s_bufs[par][...]
        if mask is not None:
            s_t = jnp.where(mask, s_t, NEG_BIG)
        m_old = m_ref[...]
        m_new = jnp.maximum(m_old, jnp.max(s_t, axis=0, keepdims=True))
        m_ref[...] = m_new
        a_bufs[par][...] = jnp.exp2(m_old - m_new)
        p_bufs[par][...] = jnp.exp2(s_t - m_new).astype(_BF)

    def accumulate(j, par):
        v_t = vt[:, pl.ds(pl.multiple_of(j * tk, tk), tk)]
        acc[...] = a_bufs[par][...] * acc[...] + _dot(v_t, p_bufs[par][...])

    def stage(j, par):
        softmax_t(par, None)
        scores_t(j + 1, s_bufs[1 - par])
        accumulate(jnp.maximum(j - 1, 0), 1 - par)

    def finish(par):
        accumulate(jnp.maximum(i - 1, 0), 1 - par)
        key_chunk = lax.broadcasted_iota(jnp.int32, (tk, tq), 0) // CHUNK
        qry_chunk = lax.broadcasted_iota(jnp.int32, (tk, tq), 1) // CHUNK
        softmax_t(par, key_chunk <= qry_chunk)
        accumulate(i, par)
        out_t = acc[:V_HEAD_DIM, :] / acc[V_HEAD_DIM:V_HEAD_DIM + 1, :]
        o_ref[...] = out_t.T.astype(_BF)

    scores_t(0, s0)
    p1[...] = jnp.zeros(p1.shape, _BF)
    a1[...] = jnp.ones(a1.shape, _F32)
    m_ref[...] = jnp.full(m_ref.shape, NEG_BIG, _F32)
    acc[...] = jnp.zeros(acc.shape, _F32)

    @pl.loop(0, i // 2)
    def _(jj):
        stage(2 * jj, 0)
        stage(2 * jj + 1, 1)

    @pl.when(i % 2 == 0)
    def _():
        finish(0)

    @pl.when(i % 2 == 1)
    def _():
        stage(i - 1, 0)
        finish(1)


def _attn_call(q, kn, kpe, v):
    b, _, s, _ = q.shape
    tq = ATTN_TQ
    return pl.pallas_call(
        _attn_kernel,
        grid=(b, N_HEADS, s // tq),
        in_specs=[
            pl.BlockSpec((None, None, tq, QK_PAD), lambda bi, h, i: (bi, h, i, 0)),
            pl.BlockSpec((None, None, s, QK_NOPE_DIM), lambda bi, h, i: (bi, h, 0, 0)),
            pl.BlockSpec((None, s, ROPE_PAD), lambda bi, h, i: (bi, 0, 0)),
            pl.BlockSpec((None, None, s, V_HEAD_DIM), lambda bi, h, i: (bi, h, 0, 0)),
        ],
        out_specs=pl.BlockSpec((None, None, tq, V_HEAD_DIM), lambda bi, h, i: (bi, h, i, 0)),
        out_shape=jax.ShapeDtypeStruct((b, N_HEADS, s, V_HEAD_DIM), _BF),
        scratch_shapes=[pltpu.VMEM((s, QK_PAD), _BF), pltpu.VMEM((V_HEAD_DIM + ONES_ROWS, s), _BF),
                        pltpu.VMEM((ATTN_TK, tq), _F32), pltpu.VMEM((ATTN_TK, tq), _F32),
                        pltpu.VMEM((ATTN_TK, tq), _BF), pltpu.VMEM((ATTN_TK, tq), _BF),
                        pltpu.VMEM((1, tq), _F32), pltpu.VMEM((1, tq), _F32), pltpu.VMEM((1, tq), _F32),
                        pltpu.VMEM((V_HEAD_DIM + ONES_ROWS, tq), _F32)],
        compiler_params=pltpu.CompilerParams(
            dimension_semantics=("arbitrary", "arbitrary", "arbitrary"),
            vmem_limit_bytes=VMEM_LIMIT),
        name="chunk_causal_attention",
    )(q, kn, kpe, v)


def _layer_norm(z, g, b):
    mu = jnp.mean(z, axis=-1, keepdims=True)
    zc = z - mu
    var = jnp.mean(zc * zc, axis=-1, keepdims=True)
    return zc * lax.rsqrt(var + LN_EPS) * g + b


def _post_kernel(expert_cap, attn_ref, ycg_ref, sga_ref, x_ref, wab_ref, wout_ref, g1_ref, b1_ref,
                 wr_ref, br_ref, tri_ref, h1_ref, h1pa_ref, h1pb_ref, slot_ref, tw_ref, cnt_ref, run):
    tm = x_ref.shape[0]

    @pl.when(pl.program_id(0) == 0)
    def _():
        run[...] = jnp.zeros(run.shape, _F32)

    attn = jnp.concatenate([attn_ref[h] for h in range(N_HEADS)], axis=1)
    y_attn = _dot(attn, wab_ref[...])
    merged = ycg_ref[...].astype(_F32) + sga_ref[...].astype(_F32) * y_attn
    mix = _dot(merged.astype(_BF), wout_ref[...])
    h1 = _layer_norm(DEEPNORM_ALPHA * x_ref[...] + mix, g1_ref[...], b1_ref[...])
    h1_ref[...] = h1
    h1p = _pack_rows(h1)
    h1pa_ref[...] = h1p[:, :HALF]
    h1pb_ref[...] = h1p[:, HALF:]

    nt = (((1,), (1,)), ((), ()))
    h_hi = h1.astype(_BF)
    h_lo = (h1 - h_hi.astype(_F32)).astype(_BF)
    both = lax.dot_general(wr_ref[...], h_hi, nt, preferred_element_type=_F32)
    logits = (both[:N_EXPERTS] + both[N_EXPERTS:] + br_ref[:, 0:1]
              + lax.dot_general(wr_ref[:N_EXPERTS, :], h_lo, nt, preferred_element_type=_F32))
    expert = lax.broadcasted_iota(jnp.int32, (N_EXPERTS, tm), 0)
    work = logits
    chosen = jnp.zeros((N_EXPERTS, tm), jnp.bool_)
    sels, exps = [], []
    for k in range(TOP_K):
        mx = jnp.max(work, axis=0, keepdims=True)
        sel = jnp.min(jnp.where(work == mx, expert, N_EXPERTS), axis=0, keepdims=True)
        hit = expert == sel
        sels.append(sel)
        exps.append(jnp.exp(mx - (mx if k == 0 else top0)))
        if k == 0:
            top0 = mx
        chosen = jnp.logical_or(chosen, hit)
        work = jnp.where(hit, -jnp.inf, work)
    denom = exps[0] + exps[1] + exps[2] + exps[3]

    onehot = chosen.astype(_F32)
    before = _dot(onehot.astype(_BF), tri_ref[...]) + run[:, 0:1]
    rows = []
    for k in range(TOP_K):
        rank = jnp.sum(jnp.where(expert == sels[k], before, 0.0), axis=0, keepdims=True)
        rows.append(sels[k] * expert_cap + rank.astype(jnp.int32))
    slot_ref[...] = jnp.concatenate(rows + [jnp.zeros((8 - TOP_K, tm), jnp.int32)], axis=0)
    run[...] = run[...] + jnp.sum(onehot, axis=1, keepdims=True)
    cnt_ref[...] = run[...]
    tw_t = jnp.concatenate([e / denom for e in exps] + [jnp.zeros((128 - TOP_K, tm), _F32)], axis=0)
    tw_ref[...] = tw_t.T


def _post_call(attn, ycg, sga, x2d, wab, wout, g1, b1, wr, br):
    t = x2d.shape[0]
    tm = POST_TM
    row = lambda w: pl.BlockSpec((tm, w), lambda i: (i, 0))
    tps = attn.shape[2] // tm
    tri = (jnp.arange(tm)[:, None] < jnp.arange(tm)[None, :]).astype(_BF)
    outs = [
        jax.ShapeDtypeStruct((t, D_MODEL), _F32),
        jax.ShapeDtypeStruct((t, HALF), jnp.int32),
        jax.ShapeDtypeStruct((t, HALF), jnp.int32),
        jax.ShapeDtypeStruct((8, t), jnp.int32),
        jax.ShapeDtypeStruct((t, 128), _F32),
        jax.ShapeDtypeStruct((N_EXPERTS, 128), _F32),
    ]
    out_specs = [row(D_MODEL), row(HALF), row(HALF), pl.BlockSpec((8, tm), lambda i: (0, i)),
                 row(128), pl.BlockSpec((N_EXPERTS, 128), lambda i: (0, 0))]
    return pl.pallas_call(
        functools.partial(_post_kernel, t),
        grid=(t // tm,),
        in_specs=[pl.BlockSpec((None, N_HEADS, tm, V_HEAD_DIM), lambda i: (i // tps, 0, i % tps, 0))]
        + [row(D_MODEL)] * 3
        + [_const_spec(a.shape) for a in (wab, wout, g1, b1, wr, br, tri)],
        out_specs=out_specs,
        out_shape=outs,
        scratch_shapes=[pltpu.VMEM((N_EXPERTS, 128), _F32)],
        compiler_params=pltpu.CompilerParams(dimension_semantics=("arbitrary",),
                                             vmem_limit_bytes=VMEM_LIMIT),
        name="merge_out_ln1_router",
    )(attn, ycg, sga, x2d, wab, wout, g1, b1, wr, br, tri)


def _moe_kernel(be_ref, rb_ref, nused_ref, xa_ref, xb_ref, wgu_ref, bgu_ref, wd_ref, bd_ref,
                ya_ref, yb_ref, wgu_bf, wd_bf):
    i = pl.program_id(0)
    prev = be_ref[jnp.maximum(i - 1, 0)]
    fresh = jnp.logical_or(i == 0, be_ref[i] != prev)
    used = i < nused_ref[0]

    @pl.when(jnp.logical_and(fresh, used))
    def _():
        wgu_bf[...] = wgu_ref[...].astype(_BF)
        wd_bf[...] = wd_ref[...].astype(_BF)

    @pl.when(used)
    def _():
        a_lo, a_hi = _unpack_rows(xa_ref[...])
        b_lo, b_hi = _unpack_rows(xb_ref[...])
        xs = jnp.concatenate([a_lo, b_lo, a_hi, b_hi], axis=1).astype(_BF)
        hid = []
        for c in range(D_FF_EXPERT // MOE_FF_CHUNK):
            lo, hi = c * MOE_FF_CHUNK, (c + 1) * MOE_FF_CHUNK
            gate = _dot(xs, wgu_bf[:, lo:hi]) + bgu_ref[:, lo:hi]
            up = (_dot(xs, wgu_bf[:, D_FF_EXPERT + lo:D_FF_EXPERT + hi])
                  + bgu_ref[:, D_FF_EXPERT + lo:D_FF_EXPERT + hi])
            gate = jnp.minimum(gate, SWIGLU_LIMIT)
            up = jnp.clip(up, -SWIGLU_LIMIT, SWIGLU_LIMIT)
            hid.append(((up + 1.0) * (gate * jax.nn.sigmoid(SWIGLU_ALPHA * gate))).astype(_BF))
        yp = _pack_rows(_dot(jnp.concatenate(hid, axis=1), wd_bf[...]) + bd_ref[...])
        ya_ref[...] = yp[:, :HALF]
        yb_ref[...] = yp[:, HALF:]


def _moe_call(block_e, block_row, n_used, xa, xb, wgu, bgu, wd, bd):
    n_slots = xa.shape[0]
    bm = MOE_BM
    n_blocks = block_e.shape[0]
    grid_spec = pltpu.PrefetchScalarGridSpec(
        num_scalar_prefetch=3,
        grid=(n_blocks,),
        in_specs=[
            pl.BlockSpec((bm, HALF), lambda i, be, rb, nu: (rb[i], 0)),
            pl.BlockSpec((bm, HALF), lambda i, be, rb, nu: (rb[i], 0)),
            pl.BlockSpec((None, D_MODEL, 2 * D_FF_EXPERT), lambda i, be, rb, nu: (be[i], 0, 0)),
            pl.BlockSpec((None, 1, 2 * D_FF_EXPERT), lambda i, be, rb, nu: (be[i], 0, 0)),
            pl.BlockSpec((None, D_FF_EXPERT, D_MODEL), lambda i, be, rb, nu: (be[i], 0, 0)),
            pl.BlockSpec((None, 1, D_MODEL), lambda i, be, rb, nu: (be[i], 0, 0)),
        ],
        out_specs=[pl.BlockSpec((bm, HALF), lambda i, be, rb, nu: (rb[i], 0))] * 2,
        scratch_shapes=[pltpu.VMEM((D_MODEL, 2 * D_FF_EXPERT), _BF),
                        pltpu.VMEM((D_FF_EXPERT, D_MODEL), _BF)],
    )
    return pl.pallas_call(
        _moe_kernel,
        grid_spec=grid_spec,
        out_shape=[jax.ShapeDtypeStruct((n_slots, HALF), jnp.int32)] * 2,
        compiler_params=pltpu.CompilerParams(dimension_semantics=("arbitrary",),
                                             vmem_limit_bytes=VMEM_LIMIT),
        name="expert_ffn",
    )(block_e, block_row, n_used, xa, xb, wgu, bgu, wd, bd)


def _combine_kernel(*refs):
    ya_refs, yb_refs = refs[:TOP_K], refs[TOP_K:2 * TOP_K]
    tw_ref, h1_ref, g2_ref, b2_ref, o_ref = refs[2 * TOP_K:]
    tw = tw_ref[...]
    parts = [jnp.zeros((tw.shape[0], HALF), _F32) for _ in range(4)]
    for k in range(TOP_K):
        a_lo, a_hi = _unpack_rows(ya_refs[k][...])
        b_lo, b_hi = _unpack_rows(yb_refs[k][...])
        w = tw[:, k:k + 1]
        parts = [p + w * y for p, y in zip(parts, (a_lo, b_lo, a_hi, b_hi))]
    ffn = jnp.concatenate(parts, axis=1)
    o_ref[...] = _layer_norm(DEEPNORM_ALPHA * h1_ref[...] + ffn, g2_ref[...], b2_ref[...])


def _combine_call(yga, ygb, tw, h1, g2, b2):
    t = h1.shape[0]
    tm = COMB_TM
    row = lambda w: pl.BlockSpec((tm, w), lambda i: (i, 0))
    krow = lambda k: pl.BlockSpec((tm, HALF), lambda i: (k * (t // tm) + i, 0))
    return pl.pallas_call(
        _combine_kernel,
        grid=(t // tm,),
        in_specs=[krow(k) for k in range(TOP_K)] * 2 + [row(128), row(D_MODEL),
                  _const_spec(g2.shape), _const_spec(b2.shape)],
        out_specs=row(D_MODEL),
        out_shape=jax.ShapeDtypeStruct((t, D_MODEL), _F32),
        compiler_params=pltpu.CompilerParams(dimension_semantics=("arbitrary",),
                                             vmem_limit_bytes=VMEM_LIMIT),
        name="combine_ln2",
    )(*([yga] * TOP_K + [ygb] * TOP_K), tw, h1, g2, b2)


def _prepare_weights(w_in, w_uq):
    o_q = 3 * CONV_DIM
    o_kv = o_q + Q_LORA_RANK
    o_pe = o_kv + KV_LORA_RANK
    o_gc = o_pe + QK_ROPE_DIM
    half = QK_ROPE_DIM // 2
    wg3 = w_in[:, :o_q]
    kpe = w_in[:, o_pe:o_gc]
    kpe_sw = jnp.concatenate([kpe[:, half:], kpe[:, :half]], axis=1)
    zpad = jnp.zeros((D_MODEL, ROPE_PAD - QK_ROPE_DIM), w_in.dtype)
    wlat = jnp.concatenate([w_in[:, o_q:o_pe], kpe, zpad, kpe_sw, zpad], axis=1)
    wgates = w_in[:, o_gc:]
    wq = w_uq.reshape(Q_LORA_RANK, N_HEADS, QK_HEAD_DIM)
    wuqa = jnp.pad(wq, ((0, 0), (0, 0), (0, QK_PAD - QK_HEAD_DIM)))
    pe = wq[:, :, QK_NOPE_DIM:]
    pe_sw = jnp.concatenate([pe[:, :, half:], pe[:, :, :half]], axis=2)
    wuqb = jnp.pad(pe_sw, ((0, 0), (0, 0), (0, ROPE_PAD - QK_ROPE_DIM)))
    return (wg3.astype(_BF), wlat.astype(_BF), wgates.astype(_BF),
            wuqa.reshape(Q_LORA_RANK, N_HEADS * QK_PAD).astype(_BF),
            wuqb.reshape(Q_LORA_RANK, N_HEADS * ROPE_PAD).astype(_BF))


def _rope_rows():
    half = QK_ROPE_DIM // 2
    inv_freq = ROPE_BASE ** (-jnp.arange(0, QK_ROPE_DIM, 2, dtype=_F32) / QK_ROPE_DIM)
    z = jnp.zeros((ROPE_PAD - QK_ROPE_DIM,), _F32)
    ones = jnp.ones((half,), _F32)
    rows = jnp.stack([jnp.concatenate([inv_freq, inv_freq, z]),
                      jnp.concatenate([ones, ones, z]),
                      jnp.concatenate([-ones, ones, z])])
    return jnp.pad(rows, ((0, 8 - rows.shape[0]), (0, 0)))


def _block_tables(counts, n_tok):
    bm = MOE_BM
    n_blocks = n_tok * TOP_K // bm + N_EXPERTS
    nblk = (counts + bm - 1) // bm
    end = jnp.cumsum(nblk)
    start = end - nblk
    n_used = end[-1]
    i = jnp.minimum(jnp.arange(n_blocks, dtype=jnp.int32), n_used - 1)
    block_e = (end[None, :] <= i[:, None]).sum(axis=1).astype(jnp.int32)
    first = jnp.sum(jnp.where(jnp.arange(N_EXPERTS)[None, :] == block_e[:, None], start[None, :], 0), axis=1)
    block_row = block_e * (n_tok // bm) + (i - first)
    return block_e, block_row.astype(jnp.int32), n_used.astype(jnp.int32).reshape(1)


def kernel(x, positions, w_in, conv_w, q_norm_g, w_uq, kv_norm_g, w_uk, w_uv, w_conv_branch,
           w_attn_branch, w_out, ln1_g, ln1_b, w_router, b_router, w_gate_up, b_gate_up, w_down,
           b_down, ln2_g, ln2_b):
    b, s, d = x.shape
    t = b * s
    x2d = x.reshape(t, d)
    pos = positions.reshape(t, 1).astype(jnp.int32)
    rope = _rope_rows()
    for l in range(DEPTH):
        wg3, wlat, wgates, wuqa, wuqb = _prepare_weights(w_in[l], w_uq[l])
        q, kn, kpe, v, ycg, sga = _proj_call(
            x2d, pos, rope, wg3, wlat, wgates, conv_w[l], q_norm_g[l][None, :], kv_norm_g[l][None, :],
            wuqa, wuqb, w_uk[l].astype(_BF), w_uv[l].astype(_BF), w_conv_branch[l].astype(_BF), s)
        attn = _attn_call(q, kn, kpe.reshape(b, s, -1), v)
        wr_t = w_router[l].T
        wr_hi = wr_t.astype(_BF)
        wr_lo = (wr_t - wr_hi.astype(_F32)).astype(_BF)
        h1, h1pa, h1pb, slots, tw_pad, counts = _post_call(
            attn, ycg, sga, x2d, w_attn_branch[l].astype(_BF), w_out[l].astype(_BF),
            ln1_g[l][None, :], ln1_b[l][None, :], jnp.concatenate([wr_hi, wr_lo], axis=0),
            jnp.broadcast_to(b_router[l][:, None], (N_EXPERTS, 128)))
        block_e, block_row, n_used = _block_tables(counts[:, 0].astype(jnp.int32), t)
        xa, xb = _sc_scatter_rows([h1pa, h1pb], slots, TOP_K, N_EXPERTS * t)
        ya, yb = _moe_call(block_e, block_row, n_used, xa, xb, w_gate_up[l], b_gate_up[l][:, None, :],
                           w_down[l], b_down[l][:, None, :])
        yga, ygb = _sc_gather_rows([ya, yb], slots, TOP_K)
        x2d = _combine_call(yga, ygb, tw_pad, h1, ln2_g[l][None, :], ln2_b[l][None, :])
    return x2d.reshape(b, s, d)
```

```python
import functools

import numpy as np
import jax
import jax.numpy as jnp
from jax import lax
from jax.experimental import pallas as pl
from jax.experimental.pallas import tpu as pltpu
from jax.experimental.pallas import tpu_sc as plsc

D_MODEL = 1024
CHUNK = 64
CONV_DIM = D_MODEL
CONV_WIDTH = 3
N_HEADS = 8
QK_NOPE_DIM = 128
QK_ROPE_DIM = 64
V_HEAD_DIM = 128
QK_HEAD_DIM = QK_NOPE_DIM + QK_ROPE_DIM
Q_LORA_RANK = 384
KV_LORA_RANK = 256
ROPE_BASE = 10000.0
N_EXPERTS = 32
TOP_K = 4
D_FF_EXPERT = D_MODEL
SWIGLU_LIMIT = 7.0
SWIGLU_ALPHA = 1.702
LN_EPS = 1e-5
RMS_EPS = 1e-6
DEPTH = 1
DEEPNORM_ALPHA = (2 * DEPTH) ** 0.25

QK_PAD = 256
ROPE_PAD = 128
LAT_DIM = Q_LORA_RANK + KV_LORA_RANK + 2 * ROPE_PAD

PROJ_TM = 512
ATTN_TQ = 512
ATTN_TK = 512
POST_TM = 512
MOE_BM = 512
COMB_TM = 512
SC_WINDOW = 128
PACKED = D_MODEL // 2
HALF = PACKED // 2
HALO = 8
NEG_BIG = -1e30
LOG2E = 1.4426950408889634
VMEM_LIMIT = 56 * 1024 * 1024

_BF = jnp.bfloat16
_F32 = jnp.float32


def _dot(a, b):
    return jnp.dot(a, b, preferred_element_type=_F32)


def _const_spec(shape):
    nd = len(shape)
    return pl.BlockSpec(shape, lambda *_: (0,) * nd, pipeline_mode=pl.Buffered(1))


def _pack_rows(a):
    half = a.shape[1] // 2
    lo = lax.bitcast_convert_type(a[:, :half].astype(_BF).astype(_F32), jnp.int32)
    hi = lax.bitcast_convert_type(a[:, half:].astype(_BF).astype(_F32), jnp.int32)
    return lax.shift_right_logical(lo, 16) | (hi & jnp.int32(-65536))


def _unpack_rows(p):
    lo = lax.bitcast_convert_type(lax.shift_left(p, 16), _F32)
    hi = lax.bitcast_convert_type(p & jnp.int32(-65536), _F32)
    return lo, hi


def _sc_mesh():
    return plsc.VectorSubcoreMesh(core_axis_name="core", subcore_axis_name="subcore")


def _sc_scatter_rows(tables, idx, n_copies, n_out):
    n_rows, width = tables[0].shape
    n_tab = len(tables)
    window = SC_WINDOW

    @functools.partial(pl.kernel, mesh=_sc_mesh(),
                       out_type=[jax.ShapeDtypeStruct((n_out, width), tables[0].dtype)] * n_tab)
    def scatter_kernel(*refs):
        idx_hbm = refs[n_tab]
        for src_hbm, out_hbm in zip(refs[:n_tab], refs[n_tab + 1:]):
            def body(src_vmem, idx_vmem, out_hbm=out_hbm):
                for k in range(n_copies):
                    pltpu.sync_copy(src_vmem, out_hbm.at[idx_vmem.at[k]])

            pltpu.emit_pipeline(
                body,
                grid=(n_rows // window,),
                in_specs=[pl.BlockSpec((window, width), lambda i: (i, 0)),
                          pl.BlockSpec((idx.shape[0], window), lambda i: (0, i))],
                out_specs=[],
                core_axis_name=("core", "subcore"),
                dimension_semantics=(pltpu.PARALLEL,),
            )(src_hbm, idx_hbm)

    return scatter_kernel(*tables, idx)


def _sc_gather_rows(tables, idx, n_copies):
    n_rows = idx.shape[1]
    width = tables[0].shape[1]
    n_tab = len(tables)
    window = SC_WINDOW
    steps = n_rows // window

    @functools.partial(pl.kernel, mesh=_sc_mesh(),
                       out_type=[jax.ShapeDtypeStruct((n_copies * n_rows, width), tables[0].dtype)] * n_tab)
    def gather_kernel(*refs):
        idx_hbm = refs[n_tab]
        for table_hbm, out_hbm in zip(refs[:n_tab], refs[n_tab + 1:]):
            def body(idx_vmem, out_vmem, table_hbm=table_hbm):
                pltpu.sync_copy(table_hbm.at[idx_vmem.at[0]], out_vmem)

            pltpu.emit_pipeline(
                body,
                grid=(n_copies, steps),
                in_specs=[pl.BlockSpec((1, window), lambda k, i: (k, i))],
                out_specs=[pl.BlockSpec((window, width), lambda k, i: (k * steps + i, 0))],
                core_axis_name=("core", "subcore"),
                dimension_semantics=(pltpu.PARALLEL, pltpu.PARALLEL),
            )(idx_hbm, out_hbm)

    return gather_kernel(*tables, idx)


def _proj_kernel(tiles_per_seq,
                 x_ref, pos_ref, rope_ref, wg3_ref, wlat_ref, wgates_ref, convw_ref,
                 qg_ref, kvg_ref, wuqa_ref, wuqb_ref, wuk_ref, wuv_ref, wcb_ref,
                 q_ref, kn_ref, kpe_ref, v_ref, ycg_ref, sga_ref, ubuf):
    i = pl.program_id(0)
    tm = x_ref.shape[0]
    xb = x_ref[...].astype(_BF)

    @pl.when(i % tiles_per_seq == 0)
    def _():
        ubuf[0:HALO, :] = jnp.zeros((HALO, CONV_DIM), _F32)

    g3 = _dot(xb, wg3_ref[...])
    u = g3[:, CONV_DIM:2 * CONV_DIM] * g3[:, 2 * CONV_DIM:]
    ubuf[HALO:HALO + tm, :] = u
    cw = convw_ref[...]
    conv = (cw[2:3, :] * u + cw[1:2, :] * ubuf[HALO - 1:HALO - 1 + tm, :]
            + cw[0:1, :] * ubuf[HALO - 2:HALO - 2 + tm, :])
    ubuf[0:HALO, :] = ubuf[tm:tm + HALO, :]
    yc = _dot((g3[:, :CONV_DIM] * conv).astype(_BF), wcb_ref[...])
    gates = _dot(xb, wgates_ref[...])
    ycg_ref[...] = (jax.nn.sigmoid(gates[:, :D_MODEL]) * yc).astype(_BF)
    sga_ref[...] = jax.nn.sigmoid(gates[:, D_MODEL:]).astype(_BF)

    lat = _dot(xb, wlat_ref[...])
    ang = pos_ref[...].astype(_F32) * rope_ref[0:1, :]
    cosb = jnp.cos(ang) * rope_ref[1:2, :]
    sinb = jnp.sin(ang) * rope_ref[2:3, :]
    q_lat = lat[:, :Q_LORA_RANK]
    rq = q_lat * lax.rsqrt(jnp.mean(q_lat * q_lat, axis=-1, keepdims=True) + RMS_EPS) * qg_ref[...]
    kv_lat = lat[:, Q_LORA_RANK:Q_LORA_RANK + KV_LORA_RANK]
    ckv = (kv_lat * lax.rsqrt(jnp.mean(kv_lat * kv_lat, axis=-1, keepdims=True) + RMS_EPS)
           * kvg_ref[...]).astype(_BF)
    o = Q_LORA_RANK + KV_LORA_RANK
    kpe_ref[...] = (lat[:, o:o + ROPE_PAD] * cosb + lat[:, o + ROPE_PAD:] * sinb).astype(_BF)
    kn = _dot(ckv, wuk_ref[...]).astype(_BF)
    vv = _dot(ckv, wuv_ref[...]).astype(_BF)
    for h in range(N_HEADS):
        kn_ref[h] = kn[:, h * QK_NOPE_DIM:(h + 1) * QK_NOPE_DIM]
        v_ref[h] = vv[:, h * V_HEAD_DIM:(h + 1) * V_HEAD_DIM]

    rqb = rq.astype(_BF)
    qa = _dot(rqb, wuqa_ref[...])
    qb = _dot(rqb, wuqb_ref[...])
    scale = QK_HEAD_DIM ** -0.5 * LOG2E
    for h in range(N_HEADS):
        lo = h * QK_PAD
        q_ref[h, :, :QK_NOPE_DIM] = (qa[:, lo:lo + QK_NOPE_DIM] * scale).astype(_BF)
        hi = (qa[:, lo + QK_NOPE_DIM:lo + QK_PAD] * cosb
              + qb[:, h * ROPE_PAD:(h + 1) * ROPE_PAD] * sinb)
        q_ref[h, :, QK_NOPE_DIM:] = (hi * scale).astype(_BF)


def _proj_call(x2d, pos, rope, wg3, wlat, wgates, convw, qg, kvg, wuqa, wuqb, wuk, wuv, wcb, seq):
    t = x2d.shape[0]
    tm = PROJ_TM
    tps = seq // tm
    row = lambda w: pl.BlockSpec((tm, w), lambda i: (i, 0))
    heads = lambda w: pl.BlockSpec((None, N_HEADS, tm, w), lambda i: (i // tps, 0, i % tps, 0))
    head_major = lambda w: jax.ShapeDtypeStruct((t // seq, N_HEADS, seq, w), _BF)
    outs = [
        head_major(QK_PAD),
        head_major(QK_NOPE_DIM),
        jax.ShapeDtypeStruct((t, ROPE_PAD), _BF),
        head_major(V_HEAD_DIM),
        jax.ShapeDtypeStruct((t, D_MODEL), _BF),
        jax.ShapeDtypeStruct((t, D_MODEL), _BF),
    ]
    return pl.pallas_call(
        functools.partial(_proj_kernel, tps),
        grid=(t // tm,),
        in_specs=[row(D_MODEL), row(1)]
        + [_const_spec(a.shape)
           for a in (rope, wg3, wlat, wgates, convw, qg, kvg, wuqa, wuqb, wuk, wuv, wcb)],
        out_specs=[heads(QK_PAD), heads(QK_NOPE_DIM), row(ROPE_PAD), heads(V_HEAD_DIM),
                   row(D_MODEL), row(D_MODEL)],
        out_shape=outs,
        scratch_shapes=[pltpu.VMEM((tm + HALO, CONV_DIM), _F32)],
        compiler_params=pltpu.CompilerParams(dimension_semantics=("arbitrary",),
                                             vmem_limit_bytes=VMEM_LIMIT),
        name="proj_conv_qkv",
    )(x2d, pos, rope, wg3, wlat, wgates, convw, qg, kvg, wuqa, wuqb, wuk, wuv, wcb)


def _attn_kernel(q_ref, kn_ref, kpe_ref, v_ref, o_ref, kfull, vext, s0, s1, p0, p1, a0, a1, m_ref, acc):
    i = pl.program_id(2)
    tq = q_ref.shape[0]
    tk = ATTN_TK
    s_bufs, p_bufs, a_bufs = (s0, s1), (p0, p1), (a0, a1)

    @pl.when(i == 0)
    def _():
        kfull[:, :QK_NOPE_DIM] = kn_ref[...]
        kfull[:, QK_NOPE_DIM:] = kpe_ref[...]
        vext[:, :V_HEAD_DIM] = v_ref[...]
        vext[:, V_HEAD_DIM:] = jnp.ones((vext.shape[0], vext.shape[1] - V_HEAD_DIM), _BF)

    def scores_t(j, dst):
        k = kfull[pl.ds(pl.multiple_of(j * tk, tk), tk), :]
        dst[...] = lax.dot_general(k, q_ref[...], (((1,), (1,)), ((), ())),
                                   preferred_element_type=_F32)

    def softmax_t(par, mask):
        s_t = s_bufs[par][...]
        if mask is not None:
            s_t = jnp.where(mask, s_t, NEG_BIG)
        m_old = m_ref[...]
        m_new = jnp.maximum(m_old, jnp.max(s_t, axis=0, keepdims=True))
        m_ref[...] = m_new
        a_bufs[par][...] = jnp.exp2(m_old - m_new)
        p_bufs[par][...] = jnp.exp2(s_t - m_new).astype(_BF)

    def accumulate(j, par):
        v_blk = vext[pl.ds(pl.multiple_of(j * tk, tk), tk), :]
        pv = lax.dot_general(p_bufs[par][...], v_blk, (((0,), (0,)), ((), ())),
                             preferred_element_type=_F32)
        a_col = jnp.broadcast_to(a_bufs[par][...], (V_HEAD_DIM, tq)).T
        acc[...] = jnp.concatenate([a_col, a_col], axis=1) * acc[...] + pv

    def stage(j, par):
        softmax_t(par, None)
        scores_t(j + 1, s_bufs[1 - par])
        accumulate(jnp.maximum(j - 1, 0), 1 - par)

    def finish(par):
        accumulate(jnp.maximum(i - 1, 0), 1 - par)
        key_chunk = lax.broadcasted_iota(jnp.int32, (tk, tq), 0) // CHUNK
        qry_chunk = lax.broadcasted_iota(jnp.int32, (tk, tq), 1) // CHUNK
        softmax_t(par, key_chunk <= qry_chunk)
        accumulate(i, par)
        o_ref[...] = (acc[:, :V_HEAD_DIM] / acc[:, V_HEAD_DIM:V_HEAD_DIM + 1]).astype(_BF)

    scores_t(0, s0)
    p1[...] = jnp.zeros(p1.shape, _BF)
    a1[...] = jnp.ones(a1.shape, _F32)
    m_ref[...] = jnp.full(m_ref.shape, NEG_BIG, _F32)
    acc[...] = jnp.zeros(acc.shape, _F32)

    @pl.loop(0, i // 2)
    def _(jj):
        stage(2 * jj, 0)
        stage(2 * jj + 1, 1)

    @pl.when(i % 2 == 0)
    def _():
        finish(0)

    @pl.when(i % 2 == 1)
    def _():
        stage(i - 1, 0)
        finish(1)


def _attn_call(q, kn, kpe, v):
    b, _, s, _ = q.shape
    tq = ATTN_TQ
    return pl.pallas_call(
        _attn_kernel,
        grid=(b, N_HEADS, s // tq),
        in_specs=[
            pl.BlockSpec((None, None, tq, QK_PAD), lambda bi, h, i: (bi, h, i, 0)),
            pl.BlockSpec((None, None, s, QK_NOPE_DIM), lambda bi, h, i: (bi, h, 0, 0)),
            pl.BlockSpec((None, s, ROPE_PAD), lambda bi, h, i: (bi, 0, 0)),
            pl.BlockSpec((None, None, s, V_HEAD_DIM), lambda bi, h, i: (bi, h, 0, 0)),
        ],
        out_specs=pl.BlockSpec((None, None, tq, V_HEAD_DIM), lambda bi, h, i: (bi, h, i, 0)),
        out_shape=jax.ShapeDtypeStruct((b, N_HEADS, s, V_HEAD_DIM), _BF),
        scratch_shapes=[pltpu.VMEM((s, QK_PAD), _BF), pltpu.VMEM((s, 2 * V_HEAD_DIM), _BF),
                        pltpu.VMEM((ATTN_TK, tq), _F32), pltpu.VMEM((ATTN_TK, tq), _F32),
                        pltpu.VMEM((ATTN_TK, tq), _BF), pltpu.VMEM((ATTN_TK, tq), _BF),
                        pltpu.VMEM((1, tq), _F32), pltpu.VMEM((1, tq), _F32), pltpu.VMEM((1, tq), _F32),
                        pltpu.VMEM((tq, 2 * V_HEAD_DIM), _F32)],
        compiler_params=pltpu.CompilerParams(
            dimension_semantics=("arbitrary", "arbitrary", "arbitrary"),
            vmem_limit_bytes=VMEM_LIMIT),
        name="chunk_causal_attention",
    )(q, kn, kpe, v)


def _layer_norm(z, g, b):
    mu = jnp.mean(z, axis=-1, keepdims=True)
    zc = z - mu
    var = jnp.mean(zc * zc, axis=-1, keepdims=True)
    return zc * lax.rsqrt(var + LN_EPS) * g + b


def _post_kernel(expert_cap, attn_ref, ycg_ref, sga_ref, x_ref, wab_ref, wout_ref, g1_ref, b1_ref,
                 wr_ref, br_ref, tri_ref, h1_ref, h1pa_ref, h1pb_ref, slot_ref, tw_ref, cnt_ref, run):
    tm = x_ref.shape[0]

    @pl.when(pl.program_id(0) == 0)
    def _():
        run[...] = jnp.zeros(run.shape, _F32)

    attn = jnp.concatenate([attn_ref[h] for h in range(N_HEADS)], axis=1)
    y_attn = _dot(attn, wab_ref[...])
    merged = ycg_ref[...].astype(_F32) + sga_ref[...].astype(_F32) * y_attn
    mix = _dot(merged.astype(_BF), wout_ref[...])
    h1 = _layer_norm(DEEPNORM_ALPHA * x_ref[...] + mix, g1_ref[...], b1_ref[...])
    h1_ref[...] = h1
    h1p = _pack_rows(h1)
    h1pa_ref[...] = h1p[:, :HALF]
    h1pb_ref[...] = h1p[:, HALF:]

    nt = (((1,), (1,)), ((), ()))
    h_hi = h1.astype(_BF)
    h_lo = (h1 - h_hi.astype(_F32)).astype(_BF)
    both = lax.dot_general(wr_ref[...], h_hi, nt, preferred_element_type=_F32)
    logits = (both[:N_EXPERTS] + both[N_EXPERTS:] + br_ref[:, 0:1]
              + lax.dot_general(wr_ref[:N_EXPERTS, :], h_lo, nt, preferred_element_type=_F32))
    expert = lax.broadcasted_iota(jnp.int32, (N_EXPERTS, tm), 0)
    work = logits
    chosen = jnp.zeros((N_EXPERTS, tm), jnp.bool_)
    sels, exps = [], []
    for k in range(TOP_K):
        mx = jnp.max(work, axis=0, keepdims=True)
        sel = jnp.min(jnp.where(work == mx, expert, N_EXPERTS), axis=0, keepdims=True)
        hit = expert == sel
        sels.append(sel)
        exps.append(jnp.exp(mx - (mx if k == 0 else top0)))
        if k == 0:
            top0 = mx
        chosen = jnp.logical_or(chosen, hit)
        work = jnp.where(hit, -jnp.inf, work)
    denom = exps[0] + exps[1] + exps[2] + exps[3]

    onehot = chosen.astype(_F32)
    before = _dot(onehot.astype(_BF), tri_ref[...]) + run[:, 0:1]
    rows = []
    for k in range(TOP_K):
        rank = jnp.sum(jnp.where(expert == sels[k], before, 0.0), axis=0, keepdims=True)
        rows.append(sels[k] * expert_cap + rank.astype(jnp.int32))
    slot_ref[...] = jnp.concatenate(rows + [jnp.zeros((8 - TOP_K, tm), jnp.int32)], axis=0)
    run[...] = run[...] + jnp.sum(onehot, axis=1, keepdims=True)
    cnt_ref[...] = run[...]
    tw_t = jnp.concatenate([e / denom for e in exps] + [jnp.zeros((128 - TOP_K, tm), _F32)], axis=0)
    tw_ref[...] = tw_t.T


def _post_call(attn, ycg, sga, x2d, wab, wout, g1, b1, wr, br):
    t = x2d.shape[0]
    tm = POST_TM
    row = lambda w: pl.BlockSpec((tm, w), lambda i: (i, 0))
    tps = attn.shape[2] // tm
    tri = (jnp.arange(tm)[:, None] < jnp.arange(tm)[None, :]).astype(_BF)
    outs = [
        jax.ShapeDtypeStruct((t, D_MODEL), _F32),
        jax.ShapeDtypeStruct((t, HALF), jnp.int32),
        jax.ShapeDtypeStruct((t, HALF), jnp.int32),
        jax.ShapeDtypeStruct((8, t), jnp.int32),
        jax.ShapeDtypeStruct((t, 128), _F32),
        jax.ShapeDtypeStruct((N_EXPERTS, 128), _F32),
    ]
    out_specs = [row(D_MODEL), row(HALF), row(HALF), pl.BlockSpec((8, tm), lambda i: (0, i)),
                 row(128), pl.BlockSpec((N_EXPERTS, 128), lambda i: (0, 0))]
    return pl.pallas_call(
        functools.partial(_post_kernel, t),
        grid=(t // tm,),
        in_specs=[pl.BlockSpec((None, N_HEADS, tm, V_HEAD_DIM), lambda i: (i // tps, 0, i % tps, 0))]
        + [row(D_MODEL)] * 3
        + [_const_spec(a.shape) for a in (wab, wout, g1, b1, wr, br, tri)],
        out_specs=out_specs,
        out_shape=outs,
        scratch_shapes=[pltpu.VMEM((N_EXPERTS, 128), _F32)],
        compiler_params=pltpu.CompilerParams(dimension_semantics=("arbitrary",),
                                             vmem_limit_bytes=VMEM_LIMIT),
        name="merge_out_ln1_router",
    )(attn, ycg, sga, x2d, wab, wout, g1, b1, wr, br, tri)


def _moe_kernel(be_ref, rb_ref, nused_ref, xa_ref, xb_ref, wgu_ref, bgu_ref, wd_ref, bd_ref,
                ya_ref, yb_ref, wgu_bf, wd_bf):
    i = pl.program_id(0)
    prev = be_ref[jnp.maximum(i - 1, 0)]
    fresh = jnp.logical_or(i == 0, be_ref[i] != prev)
    used = i < nused_ref[0]

    @pl.when(jnp.logical_and(fresh, used))
    def _():
        wgu_bf[...] = wgu_ref[...].astype(_BF)
        wd_bf[...] = wd_ref[...].astype(_BF)

    @pl.when(used)
    def _():
        a_lo, a_hi = _unpack_rows(xa_ref[...])
        b_lo, b_hi = _unpack_rows(xb_ref[...])
        xs = jnp.concatenate([a_lo, b_lo, a_hi, b_hi], axis=1).astype(_BF)
        gu = _dot(xs, wgu_bf[...]) + bgu_ref[...]
        gate = jnp.minimum(gu[:, :D_FF_EXPERT], SWIGLU_LIMIT)
        up = jnp.clip(gu[:, D_FF_EXPERT:], -SWIGLU_LIMIT, SWIGLU_LIMIT)
        hid = (up + 1.0) * (gate * jax.nn.sigmoid(SWIGLU_ALPHA * gate))
        yp = _pack_rows(_dot(hid.astype(_BF), wd_bf[...]) + bd_ref[...])
        ya_ref[...] = yp[:, :HALF]
        yb_ref[...] = yp[:, HALF:]


def _moe_call(block_e, block_row, n_used, xa, xb, wgu, bgu, wd, bd):
    n_slots = xa.shape[0]
    bm = MOE_BM
    n_blocks = block_e.shape[0]
    grid_spec = pltpu.PrefetchScalarGridSpec(
        num_scalar_prefetch=3,
        grid=(n_blocks,),
        in_specs=[
            pl.BlockSpec((bm, HALF), lambda i, be, rb, nu: (rb[i], 0)),
            pl.BlockSpec((bm, HALF), lambda i, be, rb, nu: (rb[i], 0)),
            pl.BlockSpec((None, D_MODEL, 2 * D_FF_EXPERT), lambda i, be, rb, nu: (be[i], 0, 0)),
            pl.BlockSpec((None, 1, 2 * D_FF_EXPERT), lambda i, be, rb, nu: (be[i], 0, 0)),
            pl.BlockSpec((None, D_FF_EXPERT, D_MODEL), lambda i, be, rb, nu: (be[i], 0, 0)),
            pl.BlockSpec((None, 1, D_MODEL), lambda i, be, rb, nu: (be[i], 0, 0)),
        ],
        out_specs=[pl.BlockSpec((bm, HALF), lambda i, be, rb, nu: (rb[i], 0))] * 2,
        scratch_shapes=[pltpu.VMEM((D_MODEL, 2 * D_FF_EXPERT), _BF),
                        pltpu.VMEM((D_FF_EXPERT, D_MODEL), _BF)],
    )
    return pl.pallas_call(
        _moe_kernel,
        grid_spec=grid_spec,
        out_shape=[jax.ShapeDtypeStruct((n_slots, HALF), jnp.int32)] * 2,
        compiler_params=pltpu.CompilerParams(dimension_semantics=("arbitrary",),
                                             vmem_limit_bytes=VMEM_LIMIT),
        name="expert_ffn",
    )(block_e, block_row, n_used, xa, xb, wgu, bgu, wd, bd)


def _combine_kernel(*refs):
    ya_refs, yb_refs = refs[:TOP_K], refs[TOP_K:2 * TOP_K]
    tw_ref, h1_ref, g2_ref, b2_ref, o_ref = refs[2 * TOP_K:]
    tw = tw_ref[...]
    parts = [jnp.zeros((tw.shape[0], HALF), _F32) for _ in range(4)]
    for k in range(TOP_K):
        a_lo, a_hi = _unpack_rows(ya_refs[k][...])
        b_lo, b_hi = _unpack_rows(yb_refs[k][...])
        w = tw[:, k:k + 1]
        parts = [p + w * y for p, y in zip(parts, (a_lo, b_lo, a_hi, b_hi))]
    ffn = jnp.concatenate(parts, axis=1)
    o_ref[...] = _layer_norm(DEEPNORM_ALPHA * h1_ref[...] + ffn, g2_ref[...], b2_ref[...])


def _combine_call(yga, ygb, tw, h1, g2, b2):
    t = h1.shape[0]
    tm = COMB_TM
    row = lambda w: pl.BlockSpec((tm, w), lambda i: (i, 0))
    krow = lambda k: pl.BlockSpec((tm, HALF), lambda i: (k * (t // tm) + i, 0))
    return pl.pallas_call(
        _combine_kernel,
        grid=(t // tm,),
        in_specs=[krow(k) for k in range(TOP_K)] * 2 + [row(128), row(D_MODEL),
                  _const_spec(g2.shape), _const_spec(b2.shape)],
        out_specs=row(D_MODEL),
        out_shape=jax.ShapeDtypeStruct((t, D_MODEL), _F32),
        compiler_params=pltpu.CompilerParams(dimension_semantics=("arbitrary",),
                                             vmem_limit_bytes=VMEM_LIMIT),
        name="combine_ln2",
    )(*([yga] * TOP_K + [ygb] * TOP_K), tw, h1, g2, b2)


def _prepare_weights(w_in, w_uq):
    o_q = 3 * CONV_DIM
    o_kv = o_q + Q_LORA_RANK
    o_pe = o_kv + KV_LORA_RANK
    o_gc = o_pe + QK_ROPE_DIM
    half = QK_ROPE_DIM // 2
    wg3 = w_in[:, :o_q]
    kpe = w_in[:, o_pe:o_gc]
    kpe_sw = jnp.concatenate([kpe[:, half:], kpe[:, :half]], axis=1)
    zpad = jnp.zeros((D_MODEL, ROPE_PAD - QK_ROPE_DIM), w_in.dtype)
    wlat = jnp.concatenate([w_in[:, o_q:o_pe], kpe, zpad, kpe_sw, zpad], axis=1)
    wgates = w_in[:, o_gc:]
    wq = w_uq.reshape(Q_LORA_RANK, N_HEADS, QK_HEAD_DIM)
    wuqa = jnp.pad(wq, ((0, 0), (0, 0), (0, QK_PAD - QK_HEAD_DIM)))
    pe = wq[:, :, QK_NOPE_DIM:]
    pe_sw = jnp.concatenate([pe[:, :, half:], pe[:, :, :half]], axis=2)
    wuqb = jnp.pad(pe_sw, ((0, 0), (0, 0), (0, ROPE_PAD - QK_ROPE_DIM)))
    return (wg3.astype(_BF), wlat.astype(_BF), wgates.astype(_BF),
            wuqa.reshape(Q_LORA_RANK, N_HEADS * QK_PAD).astype(_BF),
            wuqb.reshape(Q_LORA_RANK, N_HEADS * ROPE_PAD).astype(_BF))


def _rope_rows():
    half = QK_ROPE_DIM // 2
    inv_freq = ROPE_BASE ** (-jnp.arange(0, QK_ROPE_DIM, 2, dtype=_F32) / QK_ROPE_DIM)
    z = jnp.zeros((ROPE_PAD - QK_ROPE_DIM,), _F32)
    ones = jnp.ones((half,), _F32)
    rows = jnp.stack([jnp.concatenate([inv_freq, inv_freq, z]),
                      jnp.concatenate([ones, ones, z]),
                      jnp.concatenate([-ones, ones, z])])
    return jnp.pad(rows, ((0, 8 - rows.shape[0]), (0, 0)))


def _block_tables(counts, n_tok):
    bm = MOE_BM
    n_blocks = n_tok * TOP_K // bm + N_EXPERTS
    nblk = (counts + bm - 1) // bm
    end = jnp.cumsum(nblk)
    start = end - nblk
    n_used = end[-1]
    i = jnp.minimum(jnp.arange(n_blocks, dtype=jnp.int32), n_used - 1)
    block_e = (end[None, :] <= i[:, None]).sum(axis=1).astype(jnp.int32)
    first = jnp.sum(jnp.where(jnp.arange(N_EXPERTS)[None, :] == block_e[:, None], start[None, :], 0), axis=1)
    block_row = block_e * (n_tok // bm) + (i - first)
    return block_e, block_row.astype(jnp.int32), n_used.astype(jnp.int32).reshape(1)


def kernel(x, positions, w_in, conv_w, q_norm_g, w_uq, kv_norm_g, w_uk, w_uv, w_conv_branch,
           w_attn_branch, w_out, ln1_g, ln1_b, w_router, b_router, w_gate_up, b_gate_up, w_down,
           b_down, ln2_g, ln2_b):
    b, s, d = x.shape
    t = b * s
    x2d = x.reshape(t, d)
    pos = positions.reshape(t, 1).astype(jnp.int32)
    rope = _rope_rows()
    for l in range(DEPTH):
        wg3, wlat, wgates, wuqa, wuqb = _prepare_weights(w_in[l], w_uq[l])
        q, kn, kpe, v, ycg, sga = _proj_call(
            x2d, pos, rope, wg3, wlat, wgates, conv_w[l], q_norm_g[l][None, :], kv_norm_g[l][None, :],
            wuqa, wuqb, w_uk[l].astype(_BF), w_uv[l].astype(_BF), w_conv_branch[l].astype(_BF), s)
        attn = _attn_call(q, kn, kpe.reshape(b, s, -1), v)
        wr_t = w_router[l].T
        wr_hi = wr_t.astype(_BF)
        wr_lo = (wr_t - wr_hi.astype(_F32)).astype(_BF)
        h1, h1pa, h1pb, slots, tw_pad, counts = _post_call(
            attn, ycg, sga, x2d, w_attn_branch[l].astype(_BF), w_out[l].astype(_BF),
            ln1_g[l][None, :], ln1_b[l][None, :], jnp.concatenate([wr_hi, wr_lo], axis=0),
            jnp.broadcast_to(b_router[l][:, None], (N_EXPERTS, 128)))
        block_e, block_row, n_used = _block_tables(counts[:, 0].astype(jnp.int32), t)
        xa, xb = _sc_scatter_rows([h1pa, h1pb], slots, TOP_K, N_EXPERTS * t)
        ya, yb = _moe_call(block_e, block_row, n_used, xa, xb, w_gate_up[l], b_gate_up[l][:, None, :],
                           w_down[l], b_down[l][:, None, :])
        yga, ygb = _sc_gather_rows([ya, yb], slots, TOP_K)
        x2d = _combine_call(yga, ygb, tw_pad, h1, ln2_g[l][None, :], ln2_b[l][None, :])
    return x2d.reshape(b, s, d)
```

```python
import functools

import numpy as np
import jax
import jax.numpy as jnp
from jax import lax
from jax.experimental import pallas as pl
from jax.experimental.pallas import tpu as pltpu
from jax.experimental.pallas import tpu_sc as plsc

D_MODEL = 1024
CHUNK = 64
CONV_DIM = D_MODEL
CONV_WIDTH = 3
N_HEADS = 8
QK_NOPE_DIM = 128
QK_ROPE_DIM = 64
V_HEAD_DIM = 128
QK_HEAD_DIM = QK_NOPE_DIM + QK_ROPE_DIM
Q_LORA_RANK = 384
KV_LORA_RANK = 256
ROPE_BASE = 10000.0
N_EXPERTS = 32
TOP_K = 4
D_FF_EXPERT = D_MODEL
SWIGLU_LIMIT = 7.0
SWIGLU_ALPHA = 1.702
LN_EPS = 1e-5
RMS_EPS = 1e-6
DEPTH = 1
DEEPNORM_ALPHA = (2 * DEPTH) ** 0.25

QK_PAD = 256
ROPE_PAD = 128
LAT_DIM = Q_LORA_RANK + KV_LORA_RANK + 2 * ROPE_PAD

PROJ_TM = 512
ATTN_TQ = 512
ATTN_TK = 512
ONES_ROWS = 16
POST_TM = 512
MOE_BM = 512
COMB_TM = 512
SC_WINDOW = 128
PACKED = D_MODEL // 2
HALF = PACKED // 2
HALO = 8
NEG_BIG = -1e30
LOG2E = 1.4426950408889634
VMEM_LIMIT = 56 * 1024 * 1024

_BF = jnp.bfloat16
_F32 = jnp.float32


def _dot(a, b):
    return jnp.dot(a, b, preferred_element_type=_F32)


def _const_spec(shape):
    nd = len(shape)
    return pl.BlockSpec(shape, lambda *_: (0,) * nd, pipeline_mode=pl.Buffered(1))


def _pack_rows(a):
    half = a.shape[1] // 2
    lo = lax.bitcast_convert_type(a[:, :half].astype(_BF).astype(_F32), jnp.int32)
    hi = lax.bitcast_convert_type(a[:, half:].astype(_BF).astype(_F32), jnp.int32)
    return lax.shift_right_logical(lo, 16) | (hi & jnp.int32(-65536))


def _unpack_rows(p):
    lo = lax.bitcast_convert_type(lax.shift_left(p, 16), _F32)
    hi = lax.bitcast_convert_type(p & jnp.int32(-65536), _F32)
    return lo, hi


def _sc_mesh():
    return plsc.VectorSubcoreMesh(core_axis_name="core", subcore_axis_name="subcore")


def _sc_scatter_rows(tables, idx, n_copies, n_out):
    n_rows, width = tables[0].shape
    n_tab = len(tables)
    window = SC_WINDOW

    @functools.partial(pl.kernel, mesh=_sc_mesh(),
                       out_type=[jax.ShapeDtypeStruct((n_out, width), tables[0].dtype)] * n_tab)
    def scatter_kernel(*refs):
        idx_hbm = refs[n_tab]
        for src_hbm, out_hbm in zip(refs[:n_tab], refs[n_tab + 1:]):
            def body(src_vmem, idx_vmem, out_hbm=out_hbm):
                for k in range(n_copies):
                    pltpu.sync_copy(src_vmem, out_hbm.at[idx_vmem.at[k]])

            pltpu.emit_pipeline(
                body,
                grid=(n_rows // window,),
                in_specs=[pl.BlockSpec((window, width), lambda i: (i, 0)),
                          pl.BlockSpec((idx.shape[0], window), lambda i: (0, i))],
                out_specs=[],
                core_axis_name=("core", "subcore"),
                dimension_semantics=(pltpu.PARALLEL,),
            )(src_hbm, idx_hbm)

    return scatter_kernel(*tables, idx)


def _sc_gather_rows(tables, idx, n_copies):
    n_rows = idx.shape[1]
    width = tables[0].shape[1]
    n_tab = len(tables)
    window = SC_WINDOW
    steps = n_rows // window

    @functools.partial(pl.kernel, mesh=_sc_mesh(),
                       out_type=[jax.ShapeDtypeStruct((n_copies * n_rows, width), tables[0].dtype)] * n_tab)
    def gather_kernel(*refs):
        idx_hbm = refs[n_tab]
        for table_hbm, out_hbm in zip(refs[:n_tab], refs[n_tab + 1:]):
            def body(idx_vmem, out_vmem, table_hbm=table_hbm):
                pltpu.sync_copy(table_hbm.at[idx_vmem.at[0]], out_vmem)

            pltpu.emit_pipeline(
                body,
                grid=(n_copies, steps),
                in_specs=[pl.BlockSpec((1, window), lambda k, i: (k, i))],
                out_specs=[pl.BlockSpec((window, width), lambda k, i: (k * steps + i, 0))],
                core_axis_name=("core", "subcore"),
                dimension_semantics=(pltpu.PARALLEL, pltpu.PARALLEL),
            )(idx_hbm, out_hbm)

    return gather_kernel(*tables, idx)


def _proj_kernel(tiles_per_seq,
                 x_ref, pos_ref, rope_ref, wg3_ref, wlat_ref, wgates_ref, convw_ref,
                 qg_ref, kvg_ref, wuqa_ref, wuqb_ref, wuk_ref, wuv_ref, wcb_ref,
                 q_ref, kn_ref, kpe_ref, v_ref, ycg_ref, sga_ref, ubuf):
    i = pl.program_id(0)
    tm = x_ref.shape[0]
    xb = x_ref[...].astype(_BF)

    @pl.when(i % tiles_per_seq == 0)
    def _():
        ubuf[0:HALO, :] = jnp.zeros((HALO, CONV_DIM), _F32)

    g3 = _dot(xb, wg3_ref[...])
    u = g3[:, CONV_DIM:2 * CONV_DIM] * g3[:, 2 * CONV_DIM:]
    ubuf[HALO:HALO + tm, :] = u
    cw = convw_ref[...]
    conv = (cw[2:3, :] * u + cw[1:2, :] * ubuf[HALO - 1:HALO - 1 + tm, :]
            + cw[0:1, :] * ubuf[HALO - 2:HALO - 2 + tm, :])
    ubuf[0:HALO, :] = ubuf[tm:tm + HALO, :]
    yc = _dot((g3[:, :CONV_DIM] * conv).astype(_BF), wcb_ref[...])
    gates = _dot(xb, wgates_ref[...])
    ycg_ref[...] = (jax.nn.sigmoid(gates[:, :D_MODEL]) * yc).astype(_BF)
    sga_ref[...] = jax.nn.sigmoid(gates[:, D_MODEL:]).astype(_BF)

    lat = _dot(xb, wlat_ref[...])
    ang = pos_ref[...].astype(_F32) * rope_ref[0:1, :]
    cosb = jnp.cos(ang) * rope_ref[1:2, :]
    sinb = jnp.sin(ang) * rope_ref[2:3, :]
    q_lat = lat[:, :Q_LORA_RANK]
    rq = q_lat * lax.rsqrt(jnp.mean(q_lat * q_lat, axis=-1, keepdims=True) + RMS_EPS) * qg_ref[...]
    kv_lat = lat[:, Q_LORA_RANK:Q_LORA_RANK + KV_LORA_RANK]
    ckv = (kv_lat * lax.rsqrt(jnp.mean(kv_lat * kv_lat, axis=-1, keepdims=True) + RMS_EPS)
           * kvg_ref[...]).astype(_BF)
    o = Q_LORA_RANK + KV_LORA_RANK
    kpe_ref[...] = (lat[:, o:o + ROPE_PAD] * cosb + lat[:, o + ROPE_PAD:] * sinb).astype(_BF)
    kn = _dot(ckv, wuk_ref[...]).astype(_BF)
    vv = _dot(ckv, wuv_ref[...]).astype(_BF)
    for h in range(N_HEADS):
        kn_ref[h] = kn[:, h * QK_NOPE_DIM:(h + 1) * QK_NOPE_DIM]
        v_ref[h] = vv[:, h * V_HEAD_DIM:(h + 1) * V_HEAD_DIM]

    rqb = rq.astype(_BF)
    qa = _dot(rqb, wuqa_ref[...])
    qb = _dot(rqb, wuqb_ref[...])
    scale = QK_HEAD_DIM ** -0.5 * LOG2E
    for h in range(N_HEADS):
        lo = h * QK_PAD
        q_ref[h, :, :QK_NOPE_DIM] = (qa[:, lo:lo + QK_NOPE_DIM] * scale).astype(_BF)
        hi = (qa[:, lo + QK_NOPE_DIM:lo + QK_PAD] * cosb
              + qb[:, h * ROPE_PAD:(h + 1) * ROPE_PAD] * sinb)
        q_ref[h, :, QK_NOPE_DIM:] = (hi * scale).astype(_BF)


def _proj_call(x2d, pos, rope, wg3, wlat, wgates, convw, qg, kvg, wuqa, wuqb, wuk, wuv, wcb, seq):
    t = x2d.shape[0]
    tm = PROJ_TM
    tps = seq // tm
    row = lambda w: pl.BlockSpec((tm, w), lambda i: (i, 0))
    heads = lambda w: pl.BlockSpec((None, N_HEADS, tm, w), lambda i: (i // tps, 0, i % tps, 0))
    head_major = lambda w: jax.ShapeDtypeStruct((t // seq, N_HEADS, seq, w), _BF)
    outs = [
        head_major(QK_PAD),
        head_major(QK_NOPE_DIM),
        jax.ShapeDtypeStruct((t, ROPE_PAD), _BF),
        head_major(V_HEAD_DIM),
        jax.ShapeDtypeStruct((t, D_MODEL), _BF),
        jax.ShapeDtypeStruct((t, D_MODEL), _BF),
    ]
    return pl.pallas_call(
        functools.partial(_proj_kernel, tps),
        grid=(t // tm,),
        in_specs=[row(D_MODEL), row(1)]
        + [_const_spec(a.shape)
           for a in (rope, wg3, wlat, wgates, convw, qg, kvg, wuqa, wuqb, wuk, wuv, wcb)],
        out_specs=[heads(QK_PAD), heads(QK_NOPE_DIM), row(ROPE_PAD), heads(V_HEAD_DIM),
                   row(D_MODEL), row(D_MODEL)],
        out_shape=outs,
        scratch_shapes=[pltpu.VMEM((tm + HALO, CONV_DIM), _F32)],
        compiler_params=pltpu.CompilerParams(dimension_semantics=("arbitrary",),
                                             vmem_limit_bytes=VMEM_LIMIT),
        name="proj_conv_qkv",
    )(x2d, pos, rope, wg3, wlat, wgates, convw, qg, kvg, wuqa, wuqb, wuk, wuv, wcb)


def _attn_kernel(q_ref, kn_ref, kpe_ref, v_ref, o_ref, kfull, vt, s0, s1, p0, p1, a0, a1, x0, x1, m_ref, acc):
    i = pl.program_id(2)
    tq = q_ref.shape[0]
    tk = ATTN_TK
    s_bufs, p_bufs, a_bufs, x_bufs = (s0, s1), (p0, p1), (a0, a1), (x0, x1)

    @pl.when(i == 0)
    def _():
        kfull[:, :QK_NOPE_DIM] = kn_ref[...]
        kfull[:, QK_NOPE_DIM:] = kpe_ref[...]
        vt[:V_HEAD_DIM, :] = v_ref[...].T
        vt[V_HEAD_DIM:, :] = jnp.ones((vt.shape[0] - V_HEAD_DIM, vt.shape[1]), _BF)

    def scores_t(j, dst, mx):
        k = kfull[pl.ds(pl.multiple_of(j * tk, tk), tk), :]
        s_t = lax.dot_general(k, q_ref[...], (((1,), (1,)), ((), ())), preferred_element_type=_F32)
        dst[...] = s_t
        mx[...] = jnp.max(s_t, axis=0, keepdims=True)

    def softmax_t(par, mask):
        s_t = s_bufs[par][...]
        if mask is None:
            blk_max = x_bufs[par][...]
        else:
            s_t = jnp.where(mask, s_t, NEG_BIG)
            blk_max = jnp.max(s_t, axis=0, keepdims=True)
        m_old = m_ref[...]
        m_new = jnp.maximum(m_old, blk_max)
        m_ref[...] = m_new
        a_bufs[par][...] = jnp.exp2(m_old - m_new)
        p_bufs[par][...] = jnp.exp2(s_t - m_new).astype(_BF)

    def accumulate(j, par):
        v_t = vt[:, pl.ds(pl.multiple_of(j * tk, tk), tk)]
        acc[...] = a_bufs[par][...] * acc[...] + _dot(v_t, p_bufs[par][...])

    def stage(j, par):
        scores_t(j + 1, s_bufs[1 - par], x_bufs[1 - par])
        softmax_t(par, None)
        accumulate(j, par)

    def finish(par):
        key_chunk = lax.broadcasted_iota(jnp.int32, (tk, tq), 0) // CHUNK
        qry_chunk = lax.broadcasted_iota(jnp.int32, (tk, tq), 1) // CHUNK
        softmax_t(par, key_chunk <= qry_chunk)
        accumulate(i, par)
        out_t = acc[:V_HEAD_DIM, :] / acc[V_HEAD_DIM:V_HEAD_DIM + 1, :]
        o_ref[...] = out_t.T.astype(_BF)

    scores_t(0, s0, x0)
    m_ref[...] = jnp.full(m_ref.shape, NEG_BIG, _F32)
    acc[...] = jnp.zeros(acc.shape, _F32)

    @pl.loop(0, i // 2)
    def _(jj):
        stage(2 * jj, 0)
        stage(2 * jj + 1, 1)

    @pl.when(i % 2 == 0)
    def _():
        finish(0)

    @pl.when(i % 2 == 1)
    def _():
        stage(i - 1, 0)
        finish(1)


def _attn_call(q, kn, kpe, v):
    b, _, s, _ = q.shape
    tq = ATTN_TQ
    return pl.pallas_call(
        _attn_kernel,
        grid=(b, N_HEADS, s // tq),
        in_specs=[
            pl.BlockSpec((None, None, tq, QK_PAD), lambda bi, h, i: (bi, h, i, 0)),
            pl.BlockSpec((None, None, s, QK_NOPE_DIM), lambda bi, h, i: (bi, h, 0, 0)),
            pl.BlockSpec((None, s, ROPE_PAD), lambda bi, h, i: (bi, 0, 0)),
            pl.BlockSpec((None, None, s, V_HEAD_DIM), lambda bi, h, i: (bi, h, 0, 0)),
        ],
        out_specs=pl.BlockSpec((None, None, tq, V_HEAD_DIM), lambda bi, h, i: (bi, h, i, 0)),
        out_shape=jax.ShapeDtypeStruct((b, N_HEADS, s, V_HEAD_DIM), _BF),
        scratch_shapes=[pltpu.VMEM((s, QK_PAD), _BF), pltpu.VMEM((V_HEAD_DIM + ONES_ROWS, s), _BF),
                        pltpu.VMEM((ATTN_TK, tq), _F32), pltpu.VMEM((ATTN_TK, tq), _F32),
                        pltpu.VMEM((ATTN_TK, tq), _BF), pltpu.VMEM((ATTN_TK, tq), _BF),
                        pltpu.VMEM((1, tq), _F32), pltpu.VMEM((1, tq), _F32),
                        pltpu.VMEM((1, tq), _F32), pltpu.VMEM((1, tq), _F32), pltpu.VMEM((1, tq), _F32),
                        pltpu.VMEM((V_HEAD_DIM + ONES_ROWS, tq), _F32)],
        compiler_params=pltpu.CompilerParams(
            dimension_semantics=("arbitrary", "arbitrary", "arbitrary"),
            vmem_limit_bytes=VMEM_LIMIT),
        name="chunk_causal_attention",
    )(q, kn, kpe, v)


def _layer_norm(z, g, b):
    mu = jnp.mean(z, axis=-1, keepdims=True)
    zc = z - mu
    var = jnp.mean(zc * zc, axis=-1, keepdims=True)
    return zc * lax.rsqrt(var + LN_EPS) * g + b


def _post_kernel(expert_cap, attn_ref, ycg_ref, sga_ref, x_ref, wab_ref, wout_ref, g1_ref, b1_ref,
                 wr_ref, br_ref, tri_ref, h1_ref, h1pa_ref, h1pb_ref, slot_ref, tw_ref, cnt_ref, run):
    tm = x_ref.shape[0]

    @pl.when(pl.program_id(0) == 0)
    def _():
        run[...] = jnp.zeros(run.shape, _F32)

    attn = jnp.concatenate([attn_ref[h] for h in range(N_HEADS)], axis=1)
    y_attn = _dot(attn, wab_ref[...])
    merged = ycg_ref[...].astype(_F32) + sga_ref[...].astype(_F32) * y_attn
    mix = _dot(merged.astype(_BF), wout_ref[...])
    h1 = _layer_norm(DEEPNORM_ALPHA * x_ref[...] + mix, g1_ref[...], b1_ref[...])
    h1_ref[...] = h1
    h1p = _pack_rows(h1)
    h1pa_ref[...] = h1p[:, :HALF]
    h1pb_ref[...] = h1p[:, HALF:]

    nt = (((1,), (1,)), ((), ()))
    h_hi = h1.astype(_BF)
    h_lo = (h1 - h_hi.astype(_F32)).astype(_BF)
    both = lax.dot_general(wr_ref[...], h_hi, nt, preferred_element_type=_F32)
    logits = (both[:N_EXPERTS] + both[N_EXPERTS:] + br_ref[:, 0:1]
              + lax.dot_general(wr_ref[:N_EXPERTS, :], h_lo, nt, preferred_element_type=_F32))
    expert = lax.broadcasted_iota(jnp.int32, (N_EXPERTS, tm), 0)
    work = logits
    chosen = jnp.zeros((N_EXPERTS, tm), jnp.bool_)
    sels, exps = [], []
    for k in range(TOP_K):
        mx = jnp.max(work, axis=0, keepdims=True)
        sel = jnp.min(jnp.where(work == mx, expert, N_EXPERTS), axis=0, keepdims=True)
        hit = expert == sel
        sels.append(sel)
        exps.append(jnp.exp(mx - (mx if k == 0 else top0)))
        if k == 0:
            top0 = mx
        chosen = jnp.logical_or(chosen, hit)
        work = jnp.where(hit, -jnp.inf, work)
    denom = exps[0] + exps[1] + exps[2] + exps[3]

    onehot = chosen.astype(_F32)
    before = _dot(onehot.astype(_BF), tri_ref[...]) + run[:, 0:1]
    rows = []
    for k in range(TOP_K):
        rank = jnp.sum(jnp.where(expert == sels[k], before, 0.0), axis=0, keepdims=True)
        rows.append(sels[k] * expert_cap + rank.astype(jnp.int32))
    slot_ref[...] = jnp.concatenate(rows + [jnp.zeros((8 - TOP_K, tm), jnp.int32)], axis=0)
    run[...] = run[...] + jnp.sum(onehot, axis=1, keepdims=True)
    cnt_ref[...] = run[...]
    tw_t = jnp.concatenate([e / denom for e in exps] + [jnp.zeros((128 - TOP_K, tm), _F32)], axis=0)
    tw_ref[...] = tw_t.T


def _post_call(attn, ycg, sga, x2d, wab, wout, g1, b1, wr, br):
    t = x2d.shape[0]
    tm = POST_TM
    row = lambda w: pl.BlockSpec((tm, w), lambda i: (i, 0))
    tps = attn.shape[2] // tm
    tri = (jnp.arange(tm)[:, None] < jnp.arange(tm)[None, :]).astype(_BF)
    outs = [
        jax.ShapeDtypeStruct((t, D_MODEL), _F32),
        jax.ShapeDtypeStruct((t, HALF), jnp.int32),
        jax.ShapeDtypeStruct((t, HALF), jnp.int32),
        jax.ShapeDtypeStruct((8, t), jnp.int32),
        jax.ShapeDtypeStruct((t, 128), _F32),
        jax.ShapeDtypeStruct((N_EXPERTS, 128), _F32),
    ]
    out_specs = [row(D_MODEL), row(HALF), row(HALF), pl.BlockSpec((8, tm), lambda i: (0, i)),
                 row(128), pl.BlockSpec((N_EXPERTS, 128), lambda i: (0, 0))]
    return pl.pallas_call(
        functools.partial(_post_kernel, t),
        grid=(t // tm,),
        in_specs=[pl.BlockSpec((None, N_HEADS, tm, V_HEAD_DIM), lambda i: (i // tps, 0, i % tps, 0))]
        + [row(D_MODEL)] * 3
        + [_const_spec(a.shape) for a in (wab, wout, g1, b1, wr, br, tri)],
        out_specs=out_specs,
        out_shape=outs,
        scratch_shapes=[pltpu.VMEM((N_EXPERTS, 128), _F32)],
        compiler_params=pltpu.CompilerParams(dimension_semantics=("arbitrary",),
                                             vmem_limit_bytes=VMEM_LIMIT),
        name="merge_out_ln1_router",
    )(attn, ycg, sga, x2d, wab, wout, g1, b1, wr, br, tri)


def _moe_kernel(be_ref, rb_ref, nused_ref, xa_ref, xb_ref, wgu_ref, bgu_ref, wd_ref, bd_ref,
                ya_ref, yb_ref, wgu_bf, wd_bf):
    i = pl.program_id(0)
    prev = be_ref[jnp.maximum(i - 1, 0)]
    fresh = jnp.logical_or(i == 0, be_ref[i] != prev)
    used = i < nused_ref[0]

    @pl.when(jnp.logical_and(fresh, used))
    def _():
        wgu_bf[...] = wgu_ref[...].astype(_BF)
        wd_bf[...] = wd_ref[...].astype(_BF)

    @pl.when(used)
    def _():
        a_lo, a_hi = _unpack_rows(xa_ref[...])
        b_lo, b_hi = _unpack_rows(xb_ref[...])
        xs = jnp.concatenate([a_lo, b_lo, a_hi, b_hi], axis=1).astype(_BF)
        gu = _dot(xs, wgu_bf[...]) + bgu_ref[...]
        gate = jnp.minimum(gu[:, :D_FF_EXPERT], SWIGLU_LIMIT)
        up = jnp.clip(gu[:, D_FF_EXPERT:], -SWIGLU_LIMIT, SWIGLU_LIMIT)
        hid = (up + 1.0) * (gate * jax.nn.sigmoid(SWIGLU_ALPHA * gate))
        yp = _pack_rows(_dot(hid.astype(_BF), wd_bf[...]) + bd_ref[...])
        ya_ref[...] = yp[:, :HALF]
        yb_ref[...] = yp[:, HALF:]


def _moe_call(block_e, block_row, n_used, xa, xb, wgu, bgu, wd, bd):
    n_slots = xa.shape[0]
    bm = MOE_BM
    n_blocks = block_e.shape[0]
    grid_spec = pltpu.PrefetchScalarGridSpec(
        num_scalar_prefetch=3,
        grid=(n_blocks,),
        in_specs=[
            pl.BlockSpec((bm, HALF), lambda i, be, rb, nu: (rb[i], 0)),
            pl.BlockSpec((bm, HALF), lambda i, be, rb, nu: (rb[i], 0)),
            pl.BlockSpec((None, D_MODEL, 2 * D_FF_EXPERT), lambda i, be, rb, nu: (be[i], 0, 0)),
            pl.BlockSpec((None, 1, 2 * D_FF_EXPERT), lambda i, be, rb, nu: (be[i], 0, 0)),
            pl.BlockSpec((None, D_FF_EXPERT, D_MODEL), lambda i, be, rb, nu: (be[i], 0, 0)),
            pl.BlockSpec((None, 1, D_MODEL), lambda i, be, rb, nu: (be[i], 0, 0)),
        ],
        out_specs=[pl.BlockSpec((bm, HALF), lambda i, be, rb, nu: (rb[i], 0))] * 2,
        scratch_shapes=[pltpu.VMEM((D_MODEL, 2 * D_FF_EXPERT), _BF),
                        pltpu.VMEM((D_FF_EXPERT, D_MODEL), _BF)],
    )
    return pl.pallas_call(
        _moe_kernel,
        grid_spec=grid_spec,
        out_shape=[jax.ShapeDtypeStruct((n_slots, HALF), jnp.int32)] * 2,
        compiler_params=pltpu.CompilerParams(dimension_semantics=("arbitrary",),
                                             vmem_limit_bytes=VMEM_LIMIT),
        name="expert_ffn",
    )(block_e, block_row, n_used, xa, xb, wgu, bgu, wd, bd)


def _combine_kernel(*refs):
    ya_refs, yb_refs = refs[:TOP_K], refs[TOP_K:2 * TOP_K]
    tw_ref, h1_ref, g2_ref, b2_ref, o_ref = refs[2 * TOP_K:]
    tw = tw_ref[...]
    parts = [jnp.zeros((tw.shape[0], HALF), _F32) for _ in range(4)]
    for k in range(TOP_K):
        a_lo, a_hi = _unpack_rows(ya_refs[k][...])
        b_lo, b_hi = _unpack_rows(yb_refs[k][...])
        w = tw[:, k:k + 1]
        parts = [p + w * y for p, y in zip(parts, (a_lo, b_lo, a_hi, b_hi))]
    ffn = jnp.concatenate(parts, axis=1)
    o_ref[...] = _layer_norm(DEEPNORM_ALPHA * h1_ref[...] + ffn, g2_ref[...], b2_ref[...])


def _combine_call(yga, ygb, tw, h1, g2, b2):
    t = h1.shape[0]
    tm = COMB_TM
    row = lambda w: pl.BlockSpec((tm, w), lambda i: (i, 0))
    krow = lambda k: pl.BlockSpec((tm, HALF), lambda i: (k * (t // tm) + i, 0))
    return pl.pallas_call(
        _combine_kernel,
        grid=(t // tm,),
        in_specs=[krow(k) for k in range(TOP_K)] * 2 + [row(128), row(D_MODEL),
                  _const_spec(g2.shape), _const_spec(b2.shape)],
        out_specs=row(D_MODEL),
        out_shape=jax.ShapeDtypeStruct((t, D_MODEL), _F32),
        compiler_params=pltpu.CompilerParams(dimension_semantics=("arbitrary",),
                                             vmem_limit_bytes=VMEM_LIMIT),
        name="combine_ln2",
    )(*([yga] * TOP_K + [ygb] * TOP_K), tw, h1, g2, b2)


def _prepare_weights(w_in, w_uq):
    o_q = 3 * CONV_DIM
    o_kv = o_q + Q_LORA_RANK
    o_pe = o_kv + KV_LORA_RANK
    o_gc = o_pe + QK_ROPE_DIM
    half = QK_ROPE_DIM // 2
    wg3 = w_in[:, :o_q]
    kpe = w_in[:, o_pe:o_gc]
    kpe_sw = jnp.concatenate([kpe[:, half:], kpe[:, :half]], axis=1)
    zpad = jnp.zeros((D_MODEL, ROPE_PAD - QK_ROPE_DIM), w_in.dtype)
    wlat = jnp.concatenate([w_in[:, o_q:o_pe], kpe, zpad, kpe_sw, zpad], axis=1)
    wgates = w_in[:, o_gc:]
    wq = w_uq.reshape(Q_LORA_RANK, N_HEADS, QK_HEAD_DIM)
    wuqa = jnp.pad(wq, ((0, 0), (0, 0), (0, QK_PAD - QK_HEAD_DIM)))
    pe = wq[:, :, QK_NOPE_DIM:]
    pe_sw = jnp.concatenate([pe[:, :, half:], pe[:, :, :half]], axis=2)
    wuqb = jnp.pad(pe_sw, ((0, 0), (0, 0), (0, ROPE_PAD - QK_ROPE_DIM)))
    return (wg3.astype(_BF), wlat.astype(_BF), wgates.astype(_BF),
            wuqa.reshape(Q_LORA_RANK, N_HEADS * QK_PAD).astype(_BF),
            wuqb.reshape(Q_LORA_RANK, N_HEADS * ROPE_PAD).astype(_BF))


def _rope_rows():
    half = QK_ROPE_DIM // 2
    inv_freq = ROPE_BASE ** (-jnp.arange(0, QK_ROPE_DIM, 2, dtype=_F32) / QK_ROPE_DIM)
    z = jnp.zeros((ROPE_PAD - QK_ROPE_DIM,), _F32)
    ones = jnp.ones((half,), _F32)
    rows = jnp.stack([jnp.concatenate([inv_freq, inv_freq, z]),
                      jnp.concatenate([ones, ones, z]),
                      jnp.concatenate([-ones, ones, z])])
    return jnp.pad(rows, ((0, 8 - rows.shape[0]), (0, 0)))


def _block_tables(counts, n_tok):
    bm = MOE_BM
    n_blocks = n_tok * TOP_K // bm + N_EXPERTS
    nblk = (counts + bm - 1) // bm
    end = jnp.cumsum(nblk)
    start = end - nblk
    n_used = end[-1]
    i = jnp.minimum(jnp.arange(n_blocks, dtype=jnp.int32), n_used - 1)
    block_e = (end[None, :] <= i[:, None]).sum(axis=1).astype(jnp.int32)
    first = jnp.sum(jnp.where(jnp.arange(N_EXPERTS)[None, :] == block_e[:, None], start[None, :], 0), axis=1)
    block_row = block_e * (n_tok // bm) + (i - first)
    return block_e, block_row.astype(jnp.int32), n_used.astype(jnp.int32).reshape(1)


def kernel(x, positions, w_in, conv_w, q_norm_g, w_uq, kv_norm_g, w_uk, w_uv, w_conv_branch,
           w_attn_branch, w_out, ln1_g, ln1_b, w_router, b_router, w_gate_up, b_gate_up, w_down,
           b_down, ln2_g, ln2_b):
    b, s, d = x.shape
    t = b * s
    x2d = x.reshape(t, d)
    pos = positions.reshape(t, 1).astype(jnp.int32)
    rope = _rope_rows()
    for l in range(DEPTH):
        wg3, wlat, wgates, wuqa, wuqb = _prepare_weights(w_in[l], w_uq[l])
        q, kn, kpe, v, ycg, sga = _proj_call(
            x2d, pos, rope, wg3, wlat, wgates, conv_w[l], q_norm_g[l][None, :], kv_norm_g[l][None, :],
            wuqa, wuqb, w_uk[l].astype(_BF), w_uv[l].astype(_BF), w_conv_branch[l].astype(_BF), s)
        attn = _attn_call(q, kn, kpe.reshape(b, s, -1), v)
        wr_t = w_router[l].T
        wr_hi = wr_t.astype(_BF)
        wr_lo = (wr_t - wr_hi.astype(_F32)).astype(_BF)
        h1, h1pa, h1pb, slots, tw_pad, counts = _post_call(
            attn, ycg, sga, x2d, w_attn_branch[l].astype(_BF), w_out[l].astype(_BF),
            ln1_g[l][None, :], ln1_b[l][None, :], jnp.concatenate([wr_hi, wr_lo], axis=0),
            jnp.broadcast_to(b_router[l][:, None], (N_EXPERTS, 128)))
        block_e, block_row, n_used = _block_tables(counts[:, 0].astype(jnp.int32), t)
        xa, xb = _sc_scatter_rows([h1pa, h1pb], slots, TOP_K, N_EXPERTS * t)
        ya, yb = _moe_call(block_e, block_row, n_used, xa, xb, w_gate_up[l], b_gate_up[l][:, None, :],
                           w_down[l], b_down[l][:, None, :])
        yga, ygb = _sc_gather_rows([ya, yb], slots, TOP_K)
        x2d = _combine_call(yga, ygb, tw_pad, h1, ln2_g[l][None, :], ln2_b[l][None, :])
    return x2d.reshape(b, s, d)
```

```python
import functools

import numpy as np
import jax
import jax.numpy as jnp
from jax import lax
from jax.experimental import pallas as pl
from jax.experimental.pallas import tpu as pltpu
from jax.experimental.pallas import tpu_sc as plsc

D_MODEL = 1024
CHUNK = 64
CONV_DIM = D_MODEL
CONV_WIDTH = 3
N_HEADS = 8
QK_NOPE_DIM = 128
QK_ROPE_DIM = 64
V_HEAD_DIM = 128
QK_HEAD_DIM = QK_NOPE_DIM + QK_ROPE_DIM
Q_LORA_RANK = 384
KV_LORA_RANK = 256
ROPE_BASE = 10000.0
N_EXPERTS = 32
TOP_K = 4
D_FF_EXPERT = D_MODEL
SWIGLU_LIMIT = 7.0
SWIGLU_ALPHA = 1.702
LN_EPS = 1e-5
RMS_EPS = 1e-6
DEPTH = 1
DEEPNORM_ALPHA = (2 * DEPTH) ** 0.25

QK_PAD = 256
ROPE_PAD = 128
LAT_DIM = Q_LORA_RANK + KV_LORA_RANK + 2 * ROPE_PAD

PROJ_TM = 512
ATTN_TQ = 512
ATTN_TK = 512
ONES_ROWS = 16
POST_TM = 512
MOE_BM = 512
COMB_TM = 512
SC_WINDOW = 128
PACKED = D_MODEL // 2
HALF = PACKED // 2
HALO = 8
NEG_BIG = -1e30
LOG2E = 1.4426950408889634
VMEM_LIMIT = 56 * 1024 * 1024

_BF = jnp.bfloat16
_F32 = jnp.float32


def _dot(a, b):
    return jnp.dot(a, b, preferred_element_type=_F32)


def _const_spec(shape):
    nd = len(shape)
    return pl.BlockSpec(shape, lambda *_: (0,) * nd, pipeline_mode=pl.Buffered(1))


def _pack_rows(a):
    half = a.shape[1] // 2
    lo = lax.bitcast_convert_type(a[:, :half].astype(_BF).astype(_F32), jnp.int32)
    hi = lax.bitcast_convert_type(a[:, half:].astype(_BF).astype(_F32), jnp.int32)
    return lax.shift_right_logical(lo, 16) | (hi & jnp.int32(-65536))


def _unpack_rows(p):
    lo = lax.bitcast_convert_type(lax.shift_left(p, 16), _F32)
    hi = lax.bitcast_convert_type(p & jnp.int32(-65536), _F32)
    return lo, hi


def _sc_mesh():
    return plsc.VectorSubcoreMesh(core_axis_name="core", subcore_axis_name="subcore")


def _sc_scatter_rows(tables, idx, n_copies, n_out):
    n_rows, width = tables[0].shape
    n_tab = len(tables)
    window = SC_WINDOW

    @functools.partial(pl.kernel, mesh=_sc_mesh(),
                       out_type=[jax.ShapeDtypeStruct((n_out, width), tables[0].dtype)] * n_tab)
    def scatter_kernel(*refs):
        idx_hbm = refs[n_tab]
        for src_hbm, out_hbm in zip(refs[:n_tab], refs[n_tab + 1:]):
            def body(src_vmem, idx_vmem, out_hbm=out_hbm):
                for k in range(n_copies):
                    pltpu.sync_copy(src_vmem, out_hbm.at[idx_vmem.at[k]])

            pltpu.emit_pipeline(
                body,
                grid=(n_rows // window,),
                in_specs=[pl.BlockSpec((window, width), lambda i: (i, 0)),
                          pl.BlockSpec((idx.shape[0], window), lambda i: (0, i))],
                out_specs=[],
                core_axis_name=("core", "subcore"),
                dimension_semantics=(pltpu.PARALLEL,),
            )(src_hbm, idx_hbm)

    return scatter_kernel(*tables, idx)


def _sc_gather_rows(tables, idx, n_copies):
    n_rows = idx.shape[1]
    width = tables[0].shape[1]
    n_tab = len(tables)
    window = SC_WINDOW
    steps = n_rows // window

    @functools.partial(pl.kernel, mesh=_sc_mesh(),
                       out_type=[jax.ShapeDtypeStruct((n_copies * n_rows, width), tables[0].dtype)] * n_tab)
    def gather_kernel(*refs):
        idx_hbm = refs[n_tab]
        for table_hbm, out_hbm in zip(refs[:n_tab], refs[n_tab + 1:]):
            def body(idx_vmem, out_vmem, table_hbm=table_hbm):
                pltpu.sync_copy(table_hbm.at[idx_vmem.at[0]], out_vmem)

            pltpu.emit_pipeline(
                body,
                grid=(n_copies, steps),
                in_specs=[pl.BlockSpec((1, window), lambda k, i: (k, i))],
                out_specs=[pl.BlockSpec((window, width), lambda k, i: (k * steps + i, 0))],
                core_axis_name=("core", "subcore"),
                dimension_semantics=(pltpu.PARALLEL, pltpu.PARALLEL),
            )(idx_hbm, out_hbm)

    return gather_kernel(*tables, idx)


def _proj_kernel(tiles_per_seq,
                 x_ref, pos_ref, rope_ref, wg3_ref, wlat_ref, wgates_ref, convw_ref,
                 qg_ref, kvg_ref, wuqa_ref, wuqb_ref, wuk_ref, wuv_ref, wcb_ref,
                 q_ref, kn_ref, kpe_ref, v_ref, ycg_ref, sga_ref, ubuf):
    i = pl.program_id(0)
    tm = x_ref.shape[0]
    xb = x_ref[...].astype(_BF)

    @pl.when(i % tiles_per_seq == 0)
    def _():
        ubuf[0:HALO, :] = jnp.zeros((HALO, CONV_DIM), _F32)

    g3 = _dot(xb, wg3_ref[...])
    u = g3[:, CONV_DIM:2 * CONV_DIM] * g3[:, 2 * CONV_DIM:]
    ubuf[HALO:HALO + tm, :] = u
    cw = convw_ref[...]
    conv = (cw[2:3, :] * u + cw[1:2, :] * ubuf[HALO - 1:HALO - 1 + tm, :]
            + cw[0:1, :] * ubuf[HALO - 2:HALO - 2 + tm, :])
    ubuf[0:HALO, :] = ubuf[tm:tm + HALO, :]
    yc = _dot((g3[:, :CONV_DIM] * conv).astype(_BF), wcb_ref[...])
    gates = _dot(xb, wgates_ref[...])
    ycg_ref[...] = (jax.nn.sigmoid(gates[:, :D_MODEL]) * yc).astype(_BF)
    sga_ref[...] = jax.nn.sigmoid(gates[:, D_MODEL:]).astype(_BF)

    lat = _dot(xb, wlat_ref[...])
    ang = pos_ref[...].astype(_F32) * rope_ref[0:1, :]
    cosb = jnp.cos(ang) * rope_ref[1:2, :]
    sinb = jnp.sin(ang) * rope_ref[2:3, :]
    q_lat = lat[:, :Q_LORA_RANK]
    rq = q_lat * lax.rsqrt(jnp.mean(q_lat * q_lat, axis=-1, keepdims=True) + RMS_EPS) * qg_ref[...]
    kv_lat = lat[:, Q_LORA_RANK:Q_LORA_RANK + KV_LORA_RANK]
    ckv = (kv_lat * lax.rsqrt(jnp.mean(kv_lat * kv_lat, axis=-1, keepdims=True) + RMS_EPS)
           * kvg_ref[...]).astype(_BF)
    o = Q_LORA_RANK + KV_LORA_RANK
    kpe_ref[...] = (lat[:, o:o + ROPE_PAD] * cosb + lat[:, o + ROPE_PAD:] * sinb).astype(_BF)
    kn = _dot(ckv, wuk_ref[...]).astype(_BF)
    vv = _dot(ckv, wuv_ref[...]).astype(_BF)
    for h in range(N_HEADS):
        kn_ref[h] = kn[:, h * QK_NOPE_DIM:(h + 1) * QK_NOPE_DIM]
        v_ref[h] = vv[:, h * V_HEAD_DIM:(h + 1) * V_HEAD_DIM]

    rqb = rq.astype(_BF)
    qa = _dot(rqb, wuqa_ref[...])
    qb = _dot(rqb, wuqb_ref[...])
    scale = QK_HEAD_DIM ** -0.5 * LOG2E
    for h in range(N_HEADS):
        lo = h * QK_PAD
        q_ref[h, :, :QK_NOPE_DIM] = (qa[:, lo:lo + QK_NOPE_DIM] * scale).astype(_BF)
        hi = (qa[:, lo + QK_NOPE_DIM:lo + QK_PAD] * cosb
              + qb[:, h * ROPE_PAD:(h + 1) * ROPE_PAD] * sinb)
        q_ref[h, :, QK_NOPE_DIM:] = (hi * scale).astype(_BF)


def _proj_call(x2d, pos, rope, wg3, wlat, wgates, convw, qg, kvg, wuqa, wuqb, wuk, wuv, wcb, seq):
    t = x2d.shape[0]
    tm = PROJ_TM
    tps = seq // tm
    row = lambda w: pl.BlockSpec((tm, w), lambda i: (i, 0))
    heads = lambda w: pl.BlockSpec((None, N_HEADS, tm, w), lambda i: (i // tps, 0, i % tps, 0))
    paired = pl.BlockSpec((None, N_HEADS, tm, QK_PAD),
                          lambda i: (i // tps, 0, _paired_tile_pos(i % tps, tps), 0))
    head_major = lambda w: jax.ShapeDtypeStruct((t // seq, N_HEADS, seq, w), _BF)
    outs = [
        head_major(QK_PAD),
        head_major(QK_NOPE_DIM),
        jax.ShapeDtypeStruct((t, ROPE_PAD), _BF),
        head_major(V_HEAD_DIM),
        jax.ShapeDtypeStruct((t, D_MODEL), _BF),
        jax.ShapeDtypeStruct((t, D_MODEL), _BF),
    ]
    return pl.pallas_call(
        functools.partial(_proj_kernel, tps),
        grid=(t // tm,),
        in_specs=[row(D_MODEL), row(1)]
        + [_const_spec(a.shape)
           for a in (rope, wg3, wlat, wgates, convw, qg, kvg, wuqa, wuqb, wuk, wuv, wcb)],
        out_specs=[paired, heads(QK_NOPE_DIM), row(ROPE_PAD), heads(V_HEAD_DIM),
                   row(D_MODEL), row(D_MODEL)],
        out_shape=outs,
        scratch_shapes=[pltpu.VMEM((tm + HALO, CONV_DIM), _F32)],
        compiler_params=pltpu.CompilerParams(dimension_semantics=("arbitrary",),
                                             vmem_limit_bytes=VMEM_LIMIT),
        name="proj_conv_qkv",
    )(x2d, pos, rope, wg3, wlat, wgates, convw, qg, kvg, wuqa, wuqb, wuk, wuv, wcb)


def _paired_tile_pos(tile, n_tiles):
    return jnp.where(tile < n_tiles // 2, 2 * tile, 2 * (n_tiles - 1 - tile) + 1)


def _attn_kernel(n_tiles, q_ref, kn_ref, kpe_ref, v_ref, o_ref, kfull, vt, s0, s1, p0, p1, a0, a1,
                 m_ref, acc):
    pr = pl.program_id(2)
    tq = q_ref.shape[0] // 2
    tk = ATTN_TK
    s_bufs, p_bufs, a_bufs = (s0, s1), (p0, p1), (a0, a1)
    n_stages = n_tiles + 1

    @pl.when(pr == 0)
    def _():
        kfull[:, :QK_NOPE_DIM] = kn_ref[...]
        kfull[:, QK_NOPE_DIM:] = kpe_ref[...]
        vt[:V_HEAD_DIM, :] = v_ref[...].T
        vt[V_HEAD_DIM:, :] = jnp.ones((vt.shape[0] - V_HEAD_DIM, vt.shape[1]), _BF)

    def sel_of(t):
        return (t > pr).astype(jnp.int32)

    def block_of(t):
        return jnp.where(t <= pr, t, t - pr - 1)

    def scores_t(t):
        k = kfull[pl.ds(pl.multiple_of(block_of(t) * tk, tk), tk), :]
        q = q_ref[pl.ds(pl.multiple_of(sel_of(t) * tq, tq), tq), :]
        s_bufs[t % 2][...] = lax.dot_general(k, q, (((1,), (1,)), ((), ())),
                                             preferred_element_type=_F32)

    key_chunk = lax.broadcasted_iota(jnp.int32, (tk, tq), 0) // CHUNK
    qry_chunk = lax.broadcasted_iota(jnp.int32, (tk, tq), 1) // CHUNK
    causal = key_chunk <= qry_chunk

    def softmax_t(t):
        par = t % 2
        diag = True if t == n_stages - 1 else (pr == t)
        s_t = jnp.where(jnp.logical_or(causal, jnp.logical_not(diag)), s_bufs[par][...], NEG_BIG)
        m_old = m_ref[sel_of(t)]
        m_new = jnp.maximum(m_old, jnp.max(s_t, axis=0, keepdims=True))
        m_ref[sel_of(t)] = m_new
        a_bufs[par][...] = jnp.exp2(m_old - m_new)
        p_bufs[par][...] = jnp.exp2(s_t - m_new).astype(_BF)

    def accumulate(t):
        par = t % 2
        v_t = vt[:, pl.ds(pl.multiple_of(block_of(t) * tk, tk), tk)]
        sel = sel_of(t)
        acc[sel] = a_bufs[par][...] * acc[sel] + _dot(v_t, p_bufs[par][...])

    m_ref[...] = jnp.full(m_ref.shape, NEG_BIG, _F32)
    acc[...] = jnp.zeros(acc.shape, _F32)
    scores_t(0)
    for t in range(n_stages):
        if t + 1 < n_stages:
            scores_t(t + 1)
        softmax_t(t)
        accumulate(t)
    for half in range(2):
        out_t = acc[half, :V_HEAD_DIM, :] / acc[half, V_HEAD_DIM:V_HEAD_DIM + 1, :]
        o_ref[half * tq:(half + 1) * tq, :] = out_t.T.astype(_BF)


def _attn_call(q, kn, kpe, v):
    b, _, s, _ = q.shape
    tq = ATTN_TQ
    n_tiles = s // tq
    return pl.pallas_call(
        functools.partial(_attn_kernel, n_tiles),
        grid=(b, N_HEADS, n_tiles // 2),
        in_specs=[
            pl.BlockSpec((None, None, 2 * tq, QK_PAD), lambda bi, h, i: (bi, h, i, 0)),
            pl.BlockSpec((None, None, s, QK_NOPE_DIM), lambda bi, h, i: (bi, h, 0, 0)),
            pl.BlockSpec((None, s, ROPE_PAD), lambda bi, h, i: (bi, 0, 0)),
            pl.BlockSpec((None, None, s, V_HEAD_DIM), lambda bi, h, i: (bi, h, 0, 0)),
        ],
        out_specs=pl.BlockSpec((None, None, 2 * tq, V_HEAD_DIM), lambda bi, h, i: (bi, h, i, 0)),
        out_shape=jax.ShapeDtypeStruct((b, N_HEADS, s, V_HEAD_DIM), _BF),
        scratch_shapes=[pltpu.VMEM((s, QK_PAD), _BF), pltpu.VMEM((V_HEAD_DIM + ONES_ROWS, s), _BF),
                        pltpu.VMEM((ATTN_TK, tq), _F32), pltpu.VMEM((ATTN_TK, tq), _F32),
                        pltpu.VMEM((ATTN_TK, tq), _BF), pltpu.VMEM((ATTN_TK, tq), _BF),
                        pltpu.VMEM((1, tq), _F32), pltpu.VMEM((1, tq), _F32),
                        pltpu.VMEM((2, 1, tq), _F32),
                        pltpu.VMEM((2, V_HEAD_DIM + ONES_ROWS, tq), _F32)],
        compiler_params=pltpu.CompilerParams(
            dimension_semantics=("arbitrary", "arbitrary", "arbitrary"),
            vmem_limit_bytes=VMEM_LIMIT),
        name="chunk_causal_attention",
    )(q, kn, kpe, v)


def _layer_norm(z, g, b):
    mu = jnp.mean(z, axis=-1, keepdims=True)
    zc = z - mu
    var = jnp.mean(zc * zc, axis=-1, keepdims=True)
    return zc * lax.rsqrt(var + LN_EPS) * g + b


def _post_kernel(expert_cap, attn_ref, ycg_ref, sga_ref, x_ref, wab_ref, wout_ref, g1_ref, b1_ref,
                 wr_ref, br_ref, tri_ref, h1_ref, h1pa_ref, h1pb_ref, slot_ref, tw_ref, cnt_ref, run):
    tm = x_ref.shape[0]

    @pl.when(pl.program_id(0) == 0)
    def _():
        run[...] = jnp.zeros(run.shape, _F32)

    attn = jnp.concatenate([attn_ref[h] for h in range(N_HEADS)], axis=1)
    y_attn = _dot(attn, wab_ref[...])
    merged = ycg_ref[...].astype(_F32) + sga_ref[...].astype(_F32) * y_attn
    mix = _dot(merged.astype(_BF), wout_ref[...])
    h1 = _layer_norm(DEEPNORM_ALPHA * x_ref[...] + mix, g1_ref[...], b1_ref[...])
    h1_ref[...] = h1
    h1p = _pack_rows(h1)
    h1pa_ref[...] = h1p[:, :HALF]
    h1pb_ref[...] = h1p[:, HALF:]

    nt = (((1,), (1,)), ((), ()))
    h_hi = h1.astype(_BF)
    h_lo = (h1 - h_hi.astype(_F32)).astype(_BF)
    both = lax.dot_general(wr_ref[...], h_hi, nt, preferred_element_type=_F32)
    logits = (both[:N_EXPERTS] + both[N_EXPERTS:] + br_ref[:, 0:1]
              + lax.dot_general(wr_ref[:N_EXPERTS, :], h_lo, nt, preferred_element_type=_F32))
    expert = lax.broadcasted_iota(jnp.int32, (N_EXPERTS, tm), 0)
    work = logits
    chosen = jnp.zeros((N_EXPERTS, tm), jnp.bool_)
    sels, exps = [], []
    for k in range(TOP_K):
        mx = jnp.max(work, axis=0, keepdims=True)
        sel = jnp.min(jnp.where(work == mx, expert, N_EXPERTS), axis=0, keepdims=True)
        hit = expert == sel
        sels.append(sel)
        exps.append(jnp.exp(mx - (mx if k == 0 else top0)))
        if k == 0:
            top0 = mx
        chosen = jnp.logical_or(chosen, hit)
        work = jnp.where(hit, -jnp.inf, work)
    denom = exps[0] + exps[1] + exps[2] + exps[3]

    onehot = chosen.astype(_F32)
    before = _dot(onehot.astype(_BF), tri_ref[...]) + run[:, 0:1]
    rows = []
    for k in range(TOP_K):
        rank = jnp.sum(jnp.where(expert == sels[k], before, 0.0), axis=0, keepdims=True)
        rows.append(sels[k] * expert_cap + rank.astype(jnp.int32))
    slot_ref[...] = jnp.concatenate(rows + [jnp.zeros((8 - TOP_K, tm), jnp.int32)], axis=0)
    run[...] = run[...] + jnp.sum(onehot, axis=1, keepdims=True)
    cnt_ref[...] = run[...]
    tw_t = jnp.concatenate([e / denom for e in exps] + [jnp.zeros((128 - TOP_K, tm), _F32)], axis=0)
    tw_ref[...] = tw_t.T


def _post_call(attn, ycg, sga, x2d, wab, wout, g1, b1, wr, br):
    t = x2d.shape[0]
    tm = POST_TM
    row = lambda w: pl.BlockSpec((tm, w), lambda i: (i, 0))
    tps = attn.shape[2] // tm
    tri = (jnp.arange(tm)[:, None] < jnp.arange(tm)[None, :]).astype(_BF)
    outs = [
        jax.ShapeDtypeStruct((t, D_MODEL), _F32),
        jax.ShapeDtypeStruct((t, HALF), jnp.int32),
        jax.ShapeDtypeStruct((t, HALF), jnp.int32),
        jax.ShapeDtypeStruct((8, t), jnp.int32),
        jax.ShapeDtypeStruct((t, 128), _F32),
        jax.ShapeDtypeStruct((N_EXPERTS, 128), _F32),
    ]
    out_specs = [row(D_MODEL), row(HALF), row(HALF), pl.BlockSpec((8, tm), lambda i: (0, i)),
                 row(128), pl.BlockSpec((N_EXPERTS, 128), lambda i: (0, 0))]
    return pl.pallas_call(
        functools.partial(_post_kernel, t),
        grid=(t // tm,),
        in_specs=[pl.BlockSpec((None, N_HEADS, tm, V_HEAD_DIM),
                               lambda i: (i // tps, 0, _paired_tile_pos(i % tps, tps), 0))]
        + [row(D_MODEL)] * 3
        + [_const_spec(a.shape) for a in (wab, wout, g1, b1, wr, br, tri)],
        out_specs=out_specs,
        out_shape=outs,
        scratch_shapes=[pltpu.VMEM((N_EXPERTS, 128), _F32)],
        compiler_params=pltpu.CompilerParams(dimension_semantics=("arbitrary",),
                                             vmem_limit_bytes=VMEM_LIMIT),
        name="merge_out_ln1_router",
    )(attn, ycg, sga, x2d, wab, wout, g1, b1, wr, br, tri)


def _moe_kernel(be_ref, rb_ref, nused_ref, xa_ref, xb_ref, wgu_ref, bgu_ref, wd_ref, bd_ref,
                ya_ref, yb_ref, wgu_bf, wd_bf):
    i = pl.program_id(0)
    prev = be_ref[jnp.maximum(i - 1, 0)]
    fresh = jnp.logical_or(i == 0, be_ref[i] != prev)
    used = i < nused_ref[0]

    @pl.when(jnp.logical_and(fresh, used))
    def _():
        wgu_bf[...] = wgu_ref[...].astype(_BF)
        wd_bf[...] = wd_ref[...].astype(_BF)

    @pl.when(used)
    def _():
        a_lo, a_hi = _unpack_rows(xa_ref[...])
        b_lo, b_hi = _unpack_rows(xb_ref[...])
        xs = jnp.concatenate([a_lo, b_lo, a_hi, b_hi], axis=1).astype(_BF)
        gu = _dot(xs, wgu_bf[...]) + bgu_ref[...]
        gate = jnp.minimum(gu[:, :D_FF_EXPERT], SWIGLU_LIMIT)
        up = jnp.clip(gu[:, D_FF_EXPERT:], -SWIGLU_LIMIT, SWIGLU_LIMIT)
        hid = (up + 1.0) * (gate * jax.nn.sigmoid(SWIGLU_ALPHA * gate))
        yp = _pack_rows(_dot(hid.astype(_BF), wd_bf[...]) + bd_ref[...])
        ya_ref[...] = yp[:, :HALF]
        yb_ref[...] = yp[:, HALF:]


def _moe_call(block_e, block_row, n_used, xa, xb, wgu, bgu, wd, bd):
    n_slots = xa.shape[0]
    bm = MOE_BM
    n_blocks = block_e.shape[0]
    grid_spec = pltpu.PrefetchScalarGridSpec(
        num_scalar_prefetch=3,
        grid=(n_blocks,),
        in_specs=[
            pl.BlockSpec((bm, HALF), lambda i, be, rb, nu: (rb[i], 0)),
            pl.BlockSpec((bm, HALF), lambda i, be, rb, nu: (rb[i], 0)),
            pl.BlockSpec((None, D_MODEL, 2 * D_FF_EXPERT), lambda i, be, rb, nu: (be[i], 0, 0)),
            pl.BlockSpec((None, 1, 2 * D_FF_EXPERT), lambda i, be, rb, nu: (be[i], 0, 0)),
            pl.BlockSpec((None, D_FF_EXPERT, D_MODEL), lambda i, be, rb, nu: (be[i], 0, 0)),
            pl.BlockSpec((None, 1, D_MODEL), lambda i, be, rb, nu: (be[i], 0, 0)),
        ],
        out_specs=[pl.BlockSpec((bm, HALF), lambda i, be, rb, nu: (rb[i], 0))] * 2,
        scratch_shapes=[pltpu.VMEM((D_MODEL, 2 * D_FF_EXPERT), _BF),
                        pltpu.VMEM((D_FF_EXPERT, D_MODEL), _BF)],
    )
    return pl.pallas_call(
        _moe_kernel,
        grid_spec=grid_spec,
        out_shape=[jax.ShapeDtypeStruct((n_slots, HALF), jnp.int32)] * 2,
        compiler_params=pltpu.CompilerParams(dimension_semantics=("arbitrary",),
                                             vmem_limit_bytes=VMEM_LIMIT),
        name="expert_ffn",
    )(block_e, block_row, n_used, xa, xb, wgu, bgu, wd, bd)


def _combine_kernel(*refs):
    ya_refs, yb_refs = refs[:TOP_K], refs[TOP_K:2 * TOP_K]
    tw_ref, h1_ref, g2_ref, b2_ref, o_ref = refs[2 * TOP_K:]
    tw = tw_ref[...]
    parts = [jnp.zeros((tw.shape[0], HALF), _F32) for _ in range(4)]
    for k in range(TOP_K):
        a_lo, a_hi = _unpack_rows(ya_refs[k][...])
        b_lo, b_hi = _unpack_rows(yb_refs[k][...])
        w = tw[:, k:k + 1]
        parts = [p + w * y for p, y in zip(parts, (a_lo, b_lo, a_hi, b_hi))]
    ffn = jnp.concatenate(parts, axis=1)
    o_ref[...] = _layer_norm(DEEPNORM_ALPHA * h1_ref[...] + ffn, g2_ref[...], b2_ref[...])


def _combine_call(yga, ygb, tw, h1, g2, b2):
    t = h1.shape[0]
    tm = COMB_TM
    row = lambda w: pl.BlockSpec((tm, w), lambda i: (i, 0))
    krow = lambda k: pl.BlockSpec((tm, HALF), lambda i: (k * (t // tm) + i, 0))
    return pl.pallas_call(
        _combine_kernel,
        grid=(t // tm,),
        in_specs=[krow(k) for k in range(TOP_K)] * 2 + [row(128), row(D_MODEL),
                  _const_spec(g2.shape), _const_spec(b2.shape)],
        out_specs=row(D_MODEL),
        out_shape=jax.ShapeDtypeStruct((t, D_MODEL), _F32),
        compiler_params=pltpu.CompilerParams(dimension_semantics=("arbitrary",),
                                             vmem_limit_bytes=VMEM_LIMIT),
        name="combine_ln2",
    )(*([yga] * TOP_K + [ygb] * TOP_K), tw, h1, g2, b2)


def _prepare_weights(w_in, w_uq):
    o_q = 3 * CONV_DIM
    o_kv = o_q + Q_LORA_RANK
    o_pe = o_kv + KV_LORA_RANK
    o_gc = o_pe + QK_ROPE_DIM
    half = QK_ROPE_DIM // 2
    wg3 = w_in[:, :o_q]
    kpe = w_in[:, o_pe:o_gc]
    kpe_sw = jnp.concatenate([kpe[:, half:], kpe[:, :half]], axis=1)
    zpad = jnp.zeros((D_MODEL, ROPE_PAD - QK_ROPE_DIM), w_in.dtype)
    wlat = jnp.concatenate([w_in[:, o_q:o_pe], kpe, zpad, kpe_sw, zpad], axis=1)
    wgates = w_in[:, o_gc:]
    wq = w_uq.reshape(Q_LORA_RANK, N_HEADS, QK_HEAD_DIM)
    wuqa = jnp.pad(wq, ((0, 0), (0, 0), (0, QK_PAD - QK_HEAD_DIM)))
    pe = wq[:, :, QK_NOPE_DIM:]
    pe_sw = jnp.concatenate([pe[:, :, half:], pe[:, :, :half]], axis=2)
    wuqb = jnp.pad(pe_sw, ((0, 0), (0, 0), (0, ROPE_PAD - QK_ROPE_DIM)))
    return (wg3.astype(_BF), wlat.astype(_BF), wgates.astype(_BF),
            wuqa.reshape(Q_LORA_RANK, N_HEADS * QK_PAD).astype(_BF),
            wuqb.reshape(Q_LORA_RANK, N_HEADS * ROPE_PAD).astype(_BF))


def _rope_rows():
    half = QK_ROPE_DIM // 2
    inv_freq = ROPE_BASE ** (-jnp.arange(0, QK_ROPE_DIM, 2, dtype=_F32) / QK_ROPE_DIM)
    z = jnp.zeros((ROPE_PAD - QK_ROPE_DIM,), _F32)
    ones = jnp.ones((half,), _F32)
    rows = jnp.stack([jnp.concatenate([inv_freq, inv_freq, z]),
                      jnp.concatenate([ones, ones, z]),
                      jnp.concatenate([-ones, ones, z])])
    return jnp.pad(rows, ((0, 8 - rows.shape[0]), (0, 0)))


def _block_tables(counts, n_tok):
    bm = MOE_BM
    n_blocks = n_tok * TOP_K // bm + N_EXPERTS
    nblk = (counts + bm - 1) // bm
    end = jnp.cumsum(nblk)
    start = end - nblk
    n_used = end[-1]
    i = jnp.minimum(jnp.arange(n_blocks, dtype=jnp.int32), n_used - 1)
    block_e = (end[None, :] <= i[:, None]).sum(axis=1).astype(jnp.int32)
    first = jnp.sum(jnp.where(jnp.arange(N_EXPERTS)[None, :] == block_e[:, None], start[None, :], 0), axis=1)
    block_row = block_e * (n_tok // bm) + (i - first)
    return block_e, block_row.astype(jnp.int32), n_used.astype(jnp.int32).reshape(1)


def kernel(x, positions, w_in, conv_w, q_norm_g, w_uq, kv_norm_g, w_uk, w_uv, w_conv_branch,
           w_attn_branch, w_out, ln1_g, ln1_b, w_router, b_router, w_gate_up, b_gate_up, w_down,
           b_down, ln2_g, ln2_b):
    b, s, d = x.shape
    t = b * s
    x2d = x.reshape(t, d)
    pos = positions.reshape(t, 1).astype(jnp.int32)
    rope = _rope_rows()
    for l in range(DEPTH):
        wg3, wlat, wgates, wuqa, wuqb = _prepare_weights(w_in[l], w_uq[l])
        q, kn, kpe, v, ycg, sga = _proj_call(
            x2d, pos, rope, wg3, wlat, wgates, conv_w[l], q_norm_g[l][None, :], kv_norm_g[l][None, :],
            wuqa, wuqb, w_uk[l].astype(_BF), w_uv[l].astype(_BF), w_conv_branch[l].astype(_BF), s)
        attn = _attn_call(q, kn, kpe.reshape(b, s, -1), v)
        wr_t = w_router[l].T
        wr_hi = wr_t.astype(_BF)
        wr_lo = (wr_t - wr_hi.astype(_F32)).astype(_BF)
        h1, h1pa, h1pb, slots, tw_pad, counts = _post_call(
            attn, ycg, sga, x2d, w_attn_branch[l].astype(_BF), w_out[l].astype(_BF),
            ln1_g[l][None, :], ln1_b[l][None, :], jnp.concatenate([wr_hi, wr_lo], axis=0),
            jnp.broadcast_to(b_router[l][:, None], (N_EXPERTS, 128)))
        block_e, block_row, n_used = _block_tables(counts[:, 0].astype(jnp.int32), t)
        xa, xb = _sc_scatter_rows([h1pa, h1pb], slots, TOP_K, N_EXPERTS * t)
        ya, yb = _moe_call(block_e, block_row, n_used, xa, xb, w_gate_up[l], b_gate_up[l][:, None, :],
                           w_down[l], b_down[l][:, None, :])
        yga, ygb = _sc_gather_rows([ya, yb], slots, TOP_K)
        x2d = _combine_call(yga, ygb, tw_pad, h1, ln2_g[l][None, :], ln2_b[l][None, :])
    return x2d.reshape(b, s, d)
```

```python
import functools

import numpy as np
import jax
import jax.numpy as jnp
from jax import lax
from jax.experimental import pallas as pl
from jax.experimental.pallas import tpu as pltpu
from jax.experimental.pallas import tpu_sc as plsc

D_MODEL = 1024
CHUNK = 64
CONV_DIM = D_MODEL
CONV_WIDTH = 3
N_HEADS = 8
QK_NOPE_DIM = 128
QK_ROPE_DIM = 64
V_HEAD_DIM = 128
QK_HEAD_DIM = QK_NOPE_DIM + QK_ROPE_DIM
Q_LORA_RANK = 384
KV_LORA_RANK = 256
ROPE_BASE = 10000.0
N_EXPERTS = 32
TOP_K = 4
D_FF_EXPERT = D_MODEL
SWIGLU_LIMIT = 7.0
SWIGLU_ALPHA = 1.702
LN_EPS = 1e-5
RMS_EPS = 1e-6
DEPTH = 1
DEEPNORM_ALPHA = (2 * DEPTH) ** 0.25

QK_PAD = 256
ROPE_PAD = 128
LAT_DIM = Q_LORA_RANK + KV_LORA_RANK + 2 * ROPE_PAD

PROJ_TM = 512
ATTN_TQ = 512
ATTN_TK = 512
ONES_ROWS = 16
POST_TM = 512
MOE_BM = 512
COMB_TM = 512
SC_WINDOW = 128
PACKED = D_MODEL // 2
HALF = PACKED // 2
HALO = 8
NEG_BIG = -1e30
LOG2E = 1.4426950408889634
VMEM_LIMIT = 56 * 1024 * 1024

_BF = jnp.bfloat16
_F32 = jnp.float32


def _dot(a, b):
    return jnp.dot(a, b, preferred_element_type=_F32)


def _const_spec(shape):
    nd = len(shape)
    return pl.BlockSpec(shape, lambda *_: (0,) * nd, pipeline_mode=pl.Buffered(1))


def _pack_rows(a):
    half = a.shape[1] // 2
    lo = lax.bitcast_convert_type(a[:, :half].astype(_BF).astype(_F32), jnp.int32)
    hi = lax.bitcast_convert_type(a[:, half:].astype(_BF).astype(_F32), jnp.int32)
    return lax.shift_right_logical(lo, 16) | (hi & jnp.int32(-65536))


def _unpack_rows(p):
    lo = lax.bitcast_convert_type(lax.shift_left(p, 16), _F32)
    hi = lax.bitcast_convert_type(p & jnp.int32(-65536), _F32)
    return lo, hi


def _sc_mesh():
    return plsc.VectorSubcoreMesh(core_axis_name="core", subcore_axis_name="subcore")


def _sc_scatter_rows(tables, idx, n_copies, n_out):
    n_rows, width = tables[0].shape
    n_tab = len(tables)
    window = SC_WINDOW

    @functools.partial(pl.kernel, mesh=_sc_mesh(),
                       out_type=[jax.ShapeDtypeStruct((n_out, width), tables[0].dtype)] * n_tab)
    def scatter_kernel(*refs):
        idx_hbm = refs[n_tab]
        for src_hbm, out_hbm in zip(refs[:n_tab], refs[n_tab + 1:]):
            def body(src_vmem, idx_vmem, out_hbm=out_hbm):
                for k in range(n_copies):
                    pltpu.sync_copy(src_vmem, out_hbm.at[idx_vmem.at[k]])

            pltpu.emit_pipeline(
                body,
                grid=(n_rows // window,),
                in_specs=[pl.BlockSpec((window, width), lambda i: (i, 0)),
                          pl.BlockSpec((idx.shape[0], window), lambda i: (0, i))],
                out_specs=[],
                core_axis_name=("core", "subcore"),
                dimension_semantics=(pltpu.PARALLEL,),
            )(src_hbm, idx_hbm)

    return scatter_kernel(*tables, idx)


def _sc_gather_rows(tables, idx, n_copies):
    n_rows = idx.shape[1]
    width = tables[0].shape[1]
    n_tab = len(tables)
    window = SC_WINDOW
    steps = n_rows // window

    @functools.partial(pl.kernel, mesh=_sc_mesh(),
                       out_type=[jax.ShapeDtypeStruct((n_copies * n_rows, width), tables[0].dtype)] * n_tab)
    def gather_kernel(*refs):
        idx_hbm = refs[n_tab]
        for table_hbm, out_hbm in zip(refs[:n_tab], refs[n_tab + 1:]):
            def body(idx_vmem, out_vmem, table_hbm=table_hbm):
                pltpu.sync_copy(table_hbm.at[idx_vmem.at[0]], out_vmem)

            pltpu.emit_pipeline(
                body,
                grid=(n_copies, steps),
                in_specs=[pl.BlockSpec((1, window), lambda k, i: (k, i))],
                out_specs=[pl.BlockSpec((window, width), lambda k, i: (k * steps + i, 0))],
                core_axis_name=("core", "subcore"),
                dimension_semantics=(pltpu.PARALLEL, pltpu.PARALLEL),
            )(idx_hbm, out_hbm)

    return gather_kernel(*tables, idx)


def _proj_kernel(tiles_per_seq,
                 x_ref, pos_ref, rope_ref, wg3_ref, wlat_ref, wgates_ref, convw_ref,
                 qg_ref, kvg_ref, wuqa_ref, wuqb_ref, wuk_ref, wuv_ref, wcb_ref,
                 q_ref, kn_ref, kpe_ref, v_ref, ycg_ref, sga_ref, ubuf):
    i = pl.program_id(0)
    tm = x_ref.shape[0]
    xb = x_ref[...].astype(_BF)

    @pl.when(i % tiles_per_seq == 0)
    def _():
        ubuf[0:HALO, :] = jnp.zeros((HALO, CONV_DIM), _F32)

    g3 = _dot(xb, wg3_ref[...])
    u = g3[:, CONV_DIM:2 * CONV_DIM] * g3[:, 2 * CONV_DIM:]
    ubuf[HALO:HALO + tm, :] = u
    cw = convw_ref[...]
    conv = (cw[2:3, :] * u + cw[1:2, :] * ubuf[HALO - 1:HALO - 1 + tm, :]
            + cw[0:1, :] * ubuf[HALO - 2:HALO - 2 + tm, :])
    ubuf[0:HALO, :] = ubuf[tm:tm + HALO, :]
    yc = _dot((g3[:, :CONV_DIM] * conv).astype(_BF), wcb_ref[...])
    gates = _dot(xb, wgates_ref[...])
    ycg_ref[...] = (jax.nn.sigmoid(gates[:, :D_MODEL]) * yc).astype(_BF)
    sga_ref[...] = jax.nn.sigmoid(gates[:, D_MODEL:]).astype(_BF)

    lat = _dot(xb, wlat_ref[...])
    ang = pos_ref[...].astype(_F32) * rope_ref[0:1, :]
    cosb = jnp.cos(ang) * rope_ref[1:2, :]
    sinb = jnp.sin(ang) * rope_ref[2:3, :]
    q_lat = lat[:, :Q_LORA_RANK]
    rq = q_lat * lax.rsqrt(jnp.mean(q_lat * q_lat, axis=-1, keepdims=True) + RMS_EPS) * qg_ref[...]
    kv_lat = lat[:, Q_LORA_RANK:Q_LORA_RANK + KV_LORA_RANK]
    ckv = (kv_lat * lax.rsqrt(jnp.mean(kv_lat * kv_lat, axis=-1, keepdims=True) + RMS_EPS)
           * kvg_ref[...]).astype(_BF)
    o = Q_LORA_RANK + KV_LORA_RANK
    kpe_ref[...] = (lat[:, o:o + ROPE_PAD] * cosb + lat[:, o + ROPE_PAD:] * sinb).astype(_BF)
    kn = _dot(ckv, wuk_ref[...]).astype(_BF)
    vv = _dot(ckv, wuv_ref[...]).astype(_BF)
    for h in range(N_HEADS):
        kn_ref[h] = kn[:, h * QK_NOPE_DIM:(h + 1) * QK_NOPE_DIM]
        v_ref[h] = vv[:, h * V_HEAD_DIM:(h + 1) * V_HEAD_DIM]

    rqb = rq.astype(_BF)
    qa = _dot(rqb, wuqa_ref[...])
    qb = _dot(rqb, wuqb_ref[...])
    scale = QK_HEAD_DIM ** -0.5 * LOG2E
    for h in range(N_HEADS):
        lo = h * QK_PAD
        q_ref[h, :, :QK_NOPE_DIM] = (qa[:, lo:lo + QK_NOPE_DIM] * scale).astype(_BF)
        hi = (qa[:, lo + QK_NOPE_DIM:lo + QK_PAD] * cosb
              + qb[:, h * ROPE_PAD:(h + 1) * ROPE_PAD] * sinb)
        q_ref[h, :, QK_NOPE_DIM:] = (hi * scale).astype(_BF)


def _proj_call(x2d, pos, rope, wg3, wlat, wgates, convw, qg, kvg, wuqa, wuqb, wuk, wuv, wcb, seq):
    t = x2d.shape[0]
    tm = PROJ_TM
    tps = seq // tm
    row = lambda w: pl.BlockSpec((tm, w), lambda i: (i, 0))
    heads = lambda w: pl.BlockSpec((None, N_HEADS, tm, w), lambda i: (i // tps, 0, i % tps, 0))
    paired = pl.BlockSpec((None, N_HEADS, tm, QK_PAD),
                          lambda i: (i // tps, 0, _paired_tile_pos(i % tps, tps), 0))
    head_major = lambda w: jax.ShapeDtypeStruct((t // seq, N_HEADS, seq, w), _BF)
    outs = [
        head_major(QK_PAD),
        head_major(QK_NOPE_DIM),
        jax.ShapeDtypeStruct((t, ROPE_PAD), _BF),
        head_major(V_HEAD_DIM),
        jax.ShapeDtypeStruct((t, D_MODEL), _BF),
        jax.ShapeDtypeStruct((t, D_MODEL), _BF),
    ]
    return pl.pallas_call(
        functools.partial(_proj_kernel, tps),
        grid=(t // tm,),
        in_specs=[row(D_MODEL), row(1)]
        + [_const_spec(a.shape)
           for a in (rope, wg3, wlat, wgates, convw, qg, kvg, wuqa, wuqb, wuk, wuv, wcb)],
        out_specs=[paired, heads(QK_NOPE_DIM), row(ROPE_PAD), heads(V_HEAD_DIM),
                   row(D_MODEL), row(D_MODEL)],
        out_shape=outs,
        scratch_shapes=[pltpu.VMEM((tm + HALO, CONV_DIM), _F32)],
        compiler_params=pltpu.CompilerParams(dimension_semantics=("arbitrary",),
                                             vmem_limit_bytes=VMEM_LIMIT),
        name="proj_conv_qkv",
    )(x2d, pos, rope, wg3, wlat, wgates, convw, qg, kvg, wuqa, wuqb, wuk, wuv, wcb)


def _paired_tile_pos(tile, n_tiles):
    return jnp.where(tile < n_tiles // 2, 2 * tile, 2 * (n_tiles - 1 - tile) + 1)


def _attn_kernel(n_tiles, q_ref, kn_ref, kpe_ref, v_ref, o_ref, kfull, vt, s0, s1, p0, p1, a0, a1,
                 m_ref, acc):
    pr = pl.program_id(2)
    tq = q_ref.shape[0] // 2
    tk = ATTN_TK
    s_bufs, p_bufs, a_bufs = (s0, s1), (p0, p1), (a0, a1)
    n_stages = n_tiles + 1

    @pl.when(pr == 0)
    def _():
        kfull[:, :QK_NOPE_DIM] = kn_ref[...]
        kfull[:, QK_NOPE_DIM:] = kpe_ref[...]
        vt[:V_HEAD_DIM, :] = v_ref[...].T
        vt[V_HEAD_DIM:, :] = jnp.ones((vt.shape[0] - V_HEAD_DIM, vt.shape[1]), _BF)

    def sel_of(t):
        return (t > pr).astype(jnp.int32)

    def block_of(t):
        return jnp.where(t <= pr, t, t - pr - 1)

    def scores_t(t):
        k = kfull[pl.ds(pl.multiple_of(block_of(t) * tk, tk), tk), :]
        q = q_ref[pl.ds(pl.multiple_of(sel_of(t) * tq, tq), tq), :]
        s_bufs[t % 2][...] = lax.dot_general(k, q, (((1,), (1,)), ((), ())),
                                             preferred_element_type=_F32)

    key_chunk = lax.broadcasted_iota(jnp.int32, (tk, tq), 0) // CHUNK
    qry_chunk = lax.broadcasted_iota(jnp.int32, (tk, tq), 1) // CHUNK
    causal = key_chunk <= qry_chunk

    def softmax_t(t):
        par = t % 2
        s_t = s_bufs[par][...]
        if t == n_stages - 1:
            s_t = jnp.where(causal, s_t, NEG_BIG)
        elif t < n_tiles // 2:
            s_t = jnp.where(jnp.logical_or(causal, pr != t), s_t, NEG_BIG)
        m_old = m_ref[sel_of(t)]
        m_new = jnp.maximum(m_old, jnp.max(s_t, axis=0, keepdims=True))
        m_ref[sel_of(t)] = m_new
        a_bufs[par][...] = jnp.exp2(m_old - m_new)
        p_bufs[par][...] = jnp.exp2(s_t - m_new).astype(_BF)

    def accumulate(t):
        par = t % 2
        v_t = vt[:, pl.ds(pl.multiple_of(block_of(t) * tk, tk), tk)]
        sel = sel_of(t)
        acc[sel] = a_bufs[par][...] * acc[sel] + _dot(v_t, p_bufs[par][...])

    m_ref[...] = jnp.full(m_ref.shape, NEG_BIG, _F32)
    acc[...] = jnp.zeros(acc.shape, _F32)
    scores_t(0)
    for t in range(n_stages):
        if t + 1 < n_stages:
            scores_t(t + 1)
        if t > 0:
            accumulate(t - 1)
        softmax_t(t)
    accumulate(n_stages - 1)
    for half in range(2):
        out_t = acc[half, :V_HEAD_DIM, :] / acc[half, V_HEAD_DIM:V_HEAD_DIM + 1, :]
        o_ref[half * tq:(half + 1) * tq, :] = out_t.T.astype(_BF)


def _attn_call(q, kn, kpe, v):
    b, _, s, _ = q.shape
    tq = ATTN_TQ
    n_tiles = s // tq
    return pl.pallas_call(
        functools.partial(_attn_kernel, n_tiles),
        grid=(b, N_HEADS, n_tiles // 2),
        in_specs=[
            pl.BlockSpec((None, None, 2 * tq, QK_PAD), lambda bi, h, i: (bi, h, i, 0)),
            pl.BlockSpec((None, None, s, QK_NOPE_DIM), lambda bi, h, i: (bi, h, 0, 0)),
            pl.BlockSpec((None, s, ROPE_PAD), lambda bi, h, i: (bi, 0, 0)),
            pl.BlockSpec((None, None, s, V_HEAD_DIM), lambda bi, h, i: (bi, h, 0, 0)),
        ],
        out_specs=pl.BlockSpec((None, None, 2 * tq, V_HEAD_DIM), lambda bi, h, i: (bi, h, i, 0)),
        out_shape=jax.ShapeDtypeStruct((b, N_HEADS, s, V_HEAD_DIM), _BF),
        scratch_shapes=[pltpu.VMEM((s, QK_PAD), _BF), pltpu.VMEM((V_HEAD_DIM + ONES_ROWS, s), _BF),
                        pltpu.VMEM((ATTN_TK, tq), _F32), pltpu.VMEM((ATTN_TK, tq), _F32),
                        pltpu.VMEM((ATTN_TK, tq), _BF), pltpu.VMEM((ATTN_TK, tq), _BF),
                        pltpu.VMEM((1, tq), _F32), pltpu.VMEM((1, tq), _F32),
                        pltpu.VMEM((2, 1, tq), _F32),
                        pltpu.VMEM((2, V_HEAD_DIM + ONES_ROWS, tq), _F32)],
        compiler_params=pltpu.CompilerParams(
            dimension_semantics=("arbitrary", "arbitrary", "arbitrary"),
            vmem_limit_bytes=VMEM_LIMIT),
        name="chunk_causal_attention",
    )(q, kn, kpe, v)


def _layer_norm(z, g, b):
    mu = jnp.mean(z, axis=-1, keepdims=True)
    zc = z - mu
    var = jnp.mean(zc * zc, axis=-1, keepdims=True)
    return zc * lax.rsqrt(var + LN_EPS) * g + b


def _post_kernel(expert_cap, attn_ref, ycg_ref, sga_ref, x_ref, wab_ref, wout_ref, g1_ref, b1_ref,
                 wr_ref, br_ref, tri_ref, h1_ref, h1pa_ref, h1pb_ref, slot_ref, tw_ref, cnt_ref, run):
    tm = x_ref.shape[0]

    @pl.when(pl.program_id(0) == 0)
    def _():
        run[...] = jnp.zeros(run.shape, _F32)

    attn = jnp.concatenate([attn_ref[h] for h in range(N_HEADS)], axis=1)
    y_attn = _dot(attn, wab_ref[...])
    merged = ycg_ref[...].astype(_F32) + sga_ref[...].astype(_F32) * y_attn
    mix = _dot(merged.astype(_BF), wout_ref[...])
    h1 = _layer_norm(DEEPNORM_ALPHA * x_ref[...] + mix, g1_ref[...], b1_ref[...])
    h1_ref[...] = h1
    h1p = _pack_rows(h1)
    h1pa_ref[...] = h1p[:, :HALF]
    h1pb_ref[...] = h1p[:, HALF:]

    nt = (((1,), (1,)), ((), ()))
    h_hi = h1.astype(_BF)
    h_lo = (h1 - h_hi.astype(_F32)).astype(_BF)
    both = lax.dot_general(wr_ref[...], h_hi, nt, preferred_element_type=_F32)
    logits = (both[:N_EXPERTS] + both[N_EXPERTS:] + br_ref[:, 0:1]
              + lax.dot_general(wr_ref[:N_EXPERTS, :], h_lo, nt, preferred_element_type=_F32))
    expert = lax.broadcasted_iota(jnp.int32, (N_EXPERTS, tm), 0)
    work = logits
    chosen = jnp.zeros((N_EXPERTS, tm), jnp.bool_)
    sels, exps = [], []
    for k in range(TOP_K):
        mx = jnp.max(work, axis=0, keepdims=True)
        sel = jnp.min(jnp.where(work == mx, expert, N_EXPERTS), axis=0, keepdims=True)
        hit = expert == sel
        sels.append(sel)
        exps.append(jnp.exp(mx - (mx if k == 0 else top0)))
        if k == 0:
            top0 = mx
        chosen = jnp.logical_or(chosen, hit)
        work = jnp.where(hit, -jnp.inf, work)
    denom = exps[0] + exps[1] + exps[2] + exps[3]

    onehot = chosen.astype(_F32)
    before = _dot(onehot.astype(_BF), tri_ref[...]) + run[:, 0:1]
    rows = []
    for k in range(TOP_K):
        rank = jnp.sum(jnp.where(expert == sels[k], before, 0.0), axis=0, keepdims=True)
        rows.append(sels[k] * expert_cap + rank.astype(jnp.int32))
    slot_ref[...] = jnp.concatenate(rows + [jnp.zeros((8 - TOP_K, tm), jnp.int32)], axis=0)
    run[...] = run[...] + jnp.sum(onehot, axis=1, keepdims=True)
    cnt_ref[...] = run[...]
    tw_t = jnp.concatenate([e / denom for e in exps] + [jnp.zeros((128 - TOP_K, tm), _F32)], axis=0)
    tw_ref[...] = tw_t.T


def _post_call(attn, ycg, sga, x2d, wab, wout, g1, b1, wr, br):
    t = x2d.shape[0]
    tm = POST_TM
    row = lambda w: pl.BlockSpec((tm, w), lambda i: (i, 0))
    tps = attn.shape[2] // tm
    tri = (jnp.arange(tm)[:, None] < jnp.arange(tm)[None, :]).astype(_BF)
    outs = [
        jax.ShapeDtypeStruct((t, D_MODEL), _F32),
        jax.ShapeDtypeStruct((t, HALF), jnp.int32),
        jax.ShapeDtypeStruct((t, HALF), jnp.int32),
        jax.ShapeDtypeStruct((8, t), jnp.int32),
        jax.ShapeDtypeStruct((t, 128), _F32),
        jax.ShapeDtypeStruct((N_EXPERTS, 128), _F32),
    ]
    out_specs = [row(D_MODEL), row(HALF), row(HALF), pl.BlockSpec((8, tm), lambda i: (0, i)),
                 row(128), pl.BlockSpec((N_EXPERTS, 128), lambda i: (0, 0))]
    return pl.pallas_call(
        functools.partial(_post_kernel, t),
        grid=(t // tm,),
        in_specs=[pl.BlockSpec((None, N_HEADS, tm, V_HEAD_DIM),
                               lambda i: (i // tps, 0, _paired_tile_pos(i % tps, tps), 0))]
        + [row(D_MODEL)] * 3
        + [_const_spec(a.shape) for a in (wab, wout, g1, b1, wr, br, tri)],
        out_specs=out_specs,
        out_shape=outs,
        scratch_shapes=[pltpu.VMEM((N_EXPERTS, 128), _F32)],
        compiler_params=pltpu.CompilerParams(dimension_semantics=("arbitrary",),
                                             vmem_limit_bytes=VMEM_LIMIT),
        name="merge_out_ln1_router",
    )(attn, ycg, sga, x2d, wab, wout, g1, b1, wr, br, tri)


def _moe_kernel(be_ref, rb_ref, nused_ref, xa_ref, xb_ref, wgu_ref, bgu_ref, wd_ref, bd_ref,
                ya_ref, yb_ref, wgu_bf, wd_bf):
    i = pl.program_id(0)
    prev = be_ref[jnp.maximum(i - 1, 0)]
    fresh = jnp.logical_or(i == 0, be_ref[i] != prev)
    used = i < nused_ref[0]

    @pl.when(jnp.logical_and(fresh, used))
    def _():
        wgu_bf[...] = wgu_ref[...].astype(_BF)
        wd_bf[...] = wd_ref[...].astype(_BF)

    @pl.when(used)
    def _():
        a_lo, a_hi = _unpack_rows(xa_ref[...])
        b_lo, b_hi = _unpack_rows(xb_ref[...])
        xs = jnp.concatenate([a_lo, b_lo, a_hi, b_hi], axis=1).astype(_BF)
        gu = _dot(xs, wgu_bf[...]) + bgu_ref[...]
        gate = jnp.minimum(gu[:, :D_FF_EXPERT], SWIGLU_LIMIT)
        up = jnp.clip(gu[:, D_FF_EXPERT:], -SWIGLU_LIMIT, SWIGLU_LIMIT)
        hid = (up + 1.0) * (gate * jax.nn.sigmoid(SWIGLU_ALPHA * gate))
        yp = _pack_rows(_dot(hid.astype(_BF), wd_bf[...]) + bd_ref[...])
        ya_ref[...] = yp[:, :HALF]
        yb_ref[...] = yp[:, HALF:]


def _moe_call(block_e, block_row, n_used, xa, xb, wgu, bgu, wd, bd):
    n_slots = xa.shape[0]
    bm = MOE_BM
    n_blocks = block_e.shape[0]
    grid_spec = pltpu.PrefetchScalarGridSpec(
        num_scalar_prefetch=3,
        grid=(n_blocks,),
        in_specs=[
            pl.BlockSpec((bm, HALF), lambda i, be, rb, nu: (rb[i], 0)),
            pl.BlockSpec((bm, HALF), lambda i, be, rb, nu: (rb[i], 0)),
            pl.BlockSpec((None, D_MODEL, 2 * D_FF_EXPERT), lambda i, be, rb, nu: (be[i], 0, 0)),
            pl.BlockSpec((None, 1, 2 * D_FF_EXPERT), lambda i, be, rb, nu: (be[i], 0, 0)),
            pl.BlockSpec((None, D_FF_EXPERT, D_MODEL), lambda i, be, rb, nu: (be[i], 0, 0)),
            pl.BlockSpec((None, 1, D_MODEL), lambda i, be, rb, nu: (be[i], 0, 0)),
        ],
        out_specs=[pl.BlockSpec((bm, HALF), lambda i, be, rb, nu: (rb[i], 0))] * 2,
        scratch_shapes=[pltpu.VMEM((D_MODEL, 2 * D_FF_EXPERT), _BF),
                        pltpu.VMEM((D_FF_EXPERT, D_MODEL), _BF)],
    )
    return pl.pallas_call(
        _moe_kernel,
        grid_spec=grid_spec,
        out_shape=[jax.ShapeDtypeStruct((n_slots, HALF), jnp.int32)] * 2,
        compiler_params=pltpu.CompilerParams(dimension_semantics=("arbitrary",),
                                             vmem_limit_bytes=VMEM_LIMIT),
        name="expert_ffn",
    )(block_e, block_row, n_used, xa, xb, wgu, bgu, wd, bd)


def _combine_kernel(*refs):
    ya_refs, yb_refs = refs[:TOP_K], refs[TOP_K:2 * TOP_K]
    tw_ref, h1_ref, g2_ref, b2_ref, o_ref = refs[2 * TOP_K:]
    tw = tw_ref[...]
    parts = [jnp.zeros((tw.shape[0], HALF), _F32) for _ in range(4)]
    for k in range(TOP_K):
        a_lo, a_hi = _unpack_rows(ya_refs[k][...])
        b_lo, b_hi = _unpack_rows(yb_refs[k][...])
        w = tw[:, k:k + 1]
        parts = [p + w * y for p, y in zip(parts, (a_lo, b_lo, a_hi, b_hi))]
    ffn = jnp.concatenate(parts, axis=1)
    o_ref[...] = _layer_norm(DEEPNORM_ALPHA * h1_ref[...] + ffn, g2_ref[...], b2_ref[...])


def _combine_call(yga, ygb, tw, h1, g2, b2):
    t = h1.shape[0]
    tm = COMB_TM
    row = lambda w: pl.BlockSpec((tm, w), lambda i: (i, 0))
    krow = lambda k: pl.BlockSpec((tm, HALF), lambda i: (k * (t // tm) + i, 0))
    return pl.pallas_call(
        _combine_kernel,
        grid=(t // tm,),
        in_specs=[krow(k) for k in range(TOP_K)] * 2 + [row(128), row(D_MODEL),
                  _const_spec(g2.shape), _const_spec(b2.shape)],
        out_specs=row(D_MODEL),
        out_shape=jax.ShapeDtypeStruct((t, D_MODEL), _F32),
        compiler_params=pltpu.CompilerParams(dimension_semantics=("arbitrary",),
                                             vmem_limit_bytes=VMEM_LIMIT),
        name="combine_ln2",
    )(*([yga] * TOP_K + [ygb] * TOP_K), tw, h1, g2, b2)


def _prepare_weights(w_in, w_uq):
    o_q = 3 * CONV_DIM
    o_kv = o_q + Q_LORA_RANK
    o_pe = o_kv + KV_LORA_RANK
    o_gc = o_pe + QK_ROPE_DIM
    half = QK_ROPE_DIM // 2
    wg3 = w_in[:, :o_q]
    kpe = w_in[:, o_pe:o_gc]
    kpe_sw = jnp.concatenate([kpe[:, half:], kpe[:, :half]], axis=1)
    zpad = jnp.zeros((D_MODEL, ROPE_PAD - QK_ROPE_DIM), w_in.dtype)
    wlat = jnp.concatenate([w_in[:, o_q:o_pe], kpe, zpad, kpe_sw, zpad], axis=1)
    wgates = w_in[:, o_gc:]
    wq = w_uq.reshape(Q_LORA_RANK, N_HEADS, QK_HEAD_DIM)
    wuqa = jnp.pad(wq, ((0, 0), (0, 0), (0, QK_PAD - QK_HEAD_DIM)))
    pe = wq[:, :, QK_NOPE_DIM:]
    pe_sw = jnp.concatenate([pe[:, :, half:], pe[:, :, :half]], axis=2)
    wuqb = jnp.pad(pe_sw, ((0, 0), (0, 0), (0, ROPE_PAD - QK_ROPE_DIM)))
    return (wg3.astype(_BF), wlat.astype(_BF), wgates.astype(_BF),
            wuqa.reshape(Q_LORA_RANK, N_HEADS * QK_PAD).astype(_BF),
            wuqb.reshape(Q_LORA_RANK, N_HEADS * ROPE_PAD).astype(_BF))


def _rope_rows():
    half = QK_ROPE_DIM // 2
    inv_freq = ROPE_BASE ** (-jnp.arange(0, QK_ROPE_DIM, 2, dtype=_F32) / QK_ROPE_DIM)
    z = jnp.zeros((ROPE_PAD - QK_ROPE_DIM,), _F32)
    ones = jnp.ones((half,), _F32)
    rows = jnp.stack([jnp.concatenate([inv_freq, inv_freq, z]),
                      jnp.concatenate([ones, ones, z]),
                      jnp.concatenate([-ones, ones, z])])
    return jnp.pad(rows, ((0, 8 - rows.shape[0]), (0, 0)))


def _block_tables(counts, n_tok):
    bm = MOE_BM
    n_blocks = n_tok * TOP_K // bm + N_EXPERTS
    nblk = (counts + bm - 1) // bm
    end = jnp.cumsum(nblk)
    start = end - nblk
    n_used = end[-1]
    i = jnp.minimum(jnp.arange(n_blocks, dtype=jnp.int32), n_used - 1)
    block_e = (end[None, :] <= i[:, None]).sum(axis=1).astype(jnp.int32)
    first = jnp.sum(jnp.where(jnp.arange(N_EXPERTS)[None, :] == block_e[:, None], start[None, :], 0), axis=1)
    block_row = block_e * (n_tok // bm) + (i - first)
    return block_e, block_row.astype(jnp.int32), n_used.astype(jnp.int32).reshape(1)


def kernel(x, positions, w_in, conv_w, q_norm_g, w_uq, kv_norm_g, w_uk, w_uv, w_conv_branch,
           w_attn_branch, w_out, ln1_g, ln1_b, w_router, b_router, w_gate_up, b_gate_up, w_down,
           b_down, ln2_g, ln2_b):
    b, s, d = x.shape
    t = b * s
    x2d = x.reshape(t, d)
    pos = positions.reshape(t, 1).astype(jnp.int32)
    rope = _rope_rows()
    for l in range(DEPTH):
        wg3, wlat, wgates, wuqa, wuqb = _prepare_weights(w_in[l], w_uq[l])
        q, kn, kpe, v, ycg, sga = _proj_call(
            x2d, pos, rope, wg3, wlat, wgates, conv_w[l], q_norm_g[l][None, :], kv_norm_g[l][None, :],
            wuqa, wuqb, w_uk[l].astype(_BF), w_uv[l].astype(_BF), w_conv_branch[l].astype(_BF), s)
        attn = _attn_call(q, kn, kpe.reshape(b, s, -1), v)
        wr_t = w_router[l].T
        wr_hi = wr_t.astype(_BF)
        wr_lo = (wr_t - wr_hi.astype(_F32)).astype(_BF)
        h1, h1pa, h1pb, slots, tw_pad, counts = _post_call(
            attn, ycg, sga, x2d, w_attn_branch[l].astype(_BF), w_out[l].astype(_BF),
            ln1_g[l][None, :], ln1_b[l][None, :], jnp.concatenate([wr_hi, wr_lo], axis=0),
            jnp.broadcast_to(b_router[l][:, None], (N_EXPERTS, 128)))
        block_e, block_row, n_used = _block_tables(counts[:, 0].astype(jnp.int32), t)
        xa, xb = _sc_scatter_rows([h1pa, h1pb], slots, TOP_K, N_EXPERTS * t)
        ya, yb = _moe_call(block_e, block_row, n_used, xa, xb, w_gate_up[l], b_gate_up[l][:, None, :],
                           w_down[l], b_down[l][:, None, :])
        yga, ygb = _sc_gather_rows([ya, yb], slots, TOP_K)
        x2d = _combine_call(yga, ygb, tw_pad, h1, ln2_g[l][None, :], ln2_b[l][None, :])
    return x2d.reshape(b, s, d)
```

```python
import functools

import numpy as np
import jax
import jax.numpy as jnp
from jax import lax
from jax.experimental import pallas as pl
from jax.experimental.pallas import tpu as pltpu
from jax.experimental.pallas import tpu_sc as plsc

D_MODEL = 1024
CHUNK = 64
CONV_DIM = D_MODEL
CONV_WIDTH = 3
N_HEADS = 8
QK_NOPE_DIM = 128
QK_ROPE_DIM = 64
V_HEAD_DIM = 128
QK_HEAD_DIM = QK_NOPE_DIM + QK_ROPE_DIM
Q_LORA_RANK = 384
KV_LORA_RANK = 256
ROPE_BASE = 10000.0
N_EXPERTS = 32
TOP_K = 4
D_FF_EXPERT = D_MODEL
SWIGLU_LIMIT = 7.0
SWIGLU_ALPHA = 1.702
LN_EPS = 1e-5
RMS_EPS = 1e-6
DEPTH = 1
DEEPNORM_ALPHA = (2 * DEPTH) ** 0.25

QK_PAD = 256
ROPE_PAD = 128
LAT_DIM = Q_LORA_RANK + KV_LORA_RANK + 2 * ROPE_PAD

PROJ_TM = 512
ATTN_TQ = 512
ATTN_TK = 512
ONES_ROWS = 16
POST_TM = 512
MOE_BM = 512
COMB_TM = 512
SC_WINDOW = 128
PACKED = D_MODEL // 2
HALF = PACKED // 2
HALO = 8
NEG_BIG = -1e30
LOG2E = 1.4426950408889634
VMEM_LIMIT = 56 * 1024 * 1024

_BF = jnp.bfloat16
_F32 = jnp.float32


def _dot(a, b):
    return jnp.dot(a, b, preferred_element_type=_F32)


def _const_spec(shape):
    nd = len(shape)
    return pl.BlockSpec(shape, lambda *_: (0,) * nd, pipeline_mode=pl.Buffered(1))


def _pack_rows(a):
    half = a.shape[1] // 2
    lo = lax.bitcast_convert_type(a[:, :half].astype(_BF).astype(_F32), jnp.int32)
    hi = lax.bitcast_convert_type(a[:, half:].astype(_BF).astype(_F32), jnp.int32)
    return lax.shift_right_logical(lo, 16) | (hi & jnp.int32(-65536))


def _unpack_rows(p):
    lo = lax.bitcast_convert_type(lax.shift_left(p, 16), _F32)
    hi = lax.bitcast_convert_type(p & jnp.int32(-65536), _F32)
    return lo, hi


def _sc_mesh():
    return plsc.VectorSubcoreMesh(core_axis_name="core", subcore_axis_name="subcore")


def _sc_scatter_rows(tables, idx, n_copies, n_out):
    n_rows, width = tables[0].shape
    n_tab = len(tables)
    window = SC_WINDOW

    @functools.partial(pl.kernel, mesh=_sc_mesh(),
                       out_type=[jax.ShapeDtypeStruct((n_out, width), tables[0].dtype)] * n_tab)
    def scatter_kernel(*refs):
        idx_hbm = refs[n_tab]
        for src_hbm, out_hbm in zip(refs[:n_tab], refs[n_tab + 1:]):
            def body(src_vmem, idx_vmem, out_hbm=out_hbm):
                for k in range(n_copies):
                    pltpu.sync_copy(src_vmem, out_hbm.at[idx_vmem.at[k]])

            pltpu.emit_pipeline(
                body,
                grid=(n_rows // window,),
                in_specs=[pl.BlockSpec((window, width), lambda i: (i, 0)),
                          pl.BlockSpec((idx.shape[0], window), lambda i: (0, i))],
                out_specs=[],
                core_axis_name=("core", "subcore"),
                dimension_semantics=(pltpu.PARALLEL,),
            )(src_hbm, idx_hbm)

    return scatter_kernel(*tables, idx)


def _sc_gather_rows(tables, idx, n_copies):
    n_rows = idx.shape[1]
    width = tables[0].shape[1]
    n_tab = len(tables)
    window = SC_WINDOW
    steps = n_rows // window

    @functools.partial(pl.kernel, mesh=_sc_mesh(),
                       out_type=[jax.ShapeDtypeStruct((n_copies * n_rows, width), tables[0].dtype)] * n_tab)
    def gather_kernel(*refs):
        idx_hbm = refs[n_tab]
        for table_hbm, out_hbm in zip(refs[:n_tab], refs[n_tab + 1:]):
            def body(idx_vmem, out_vmem, table_hbm=table_hbm):
                pltpu.sync_copy(table_hbm.at[idx_vmem.at[0]], out_vmem)

            pltpu.emit_pipeline(
                body,
                grid=(n_copies, steps),
                in_specs=[pl.BlockSpec((1, window), lambda k, i: (k, i))],
                out_specs=[pl.BlockSpec((window, width), lambda k, i: (k * steps + i, 0))],
                core_axis_name=("core", "subcore"),
                dimension_semantics=(pltpu.PARALLEL, pltpu.PARALLEL),
            )(idx_hbm, out_hbm)

    return gather_kernel(*tables, idx)


def _proj_kernel(tiles_per_seq,
                 x_ref, pos_ref, rope_ref, wg3_ref, wlat_ref, wgates_ref, convw_ref,
                 qg_ref, kvg_ref, wuqa_ref, wuqb_ref, wuk_ref, wuv_ref, wcb_ref,
                 q_ref, kn_ref, kpe_ref, v_ref, ycg_ref, sga_ref, ubuf):
    i = pl.program_id(0)
    tm = x_ref.shape[0]
    xb = x_ref[...].astype(_BF)

    @pl.when(i % tiles_per_seq == 0)
    def _():
        ubuf[0:HALO, :] = jnp.zeros((HALO, CONV_DIM), _F32)

    g3 = _dot(xb, wg3_ref[...])
    u = g3[:, CONV_DIM:2 * CONV_DIM] * g3[:, 2 * CONV_DIM:]
    ubuf[HALO:HALO + tm, :] = u
    cw = convw_ref[...]
    conv = (cw[2:3, :] * u + cw[1:2, :] * ubuf[HALO - 1:HALO - 1 + tm, :]
            + cw[0:1, :] * ubuf[HALO - 2:HALO - 2 + tm, :])
    ubuf[0:HALO, :] = ubuf[tm:tm + HALO, :]
    yc = _dot((g3[:, :CONV_DIM] * conv).astype(_BF), wcb_ref[...])
    gates = _dot(xb, wgates_ref[...])
    ycg_ref[...] = (jax.nn.sigmoid(gates[:, :D_MODEL]) * yc).astype(_BF)
    sga_ref[...] = jax.nn.sigmoid(gates[:, D_MODEL:]).astype(_BF)

    lat = _dot(xb, wlat_ref[...])
    ang = pos_ref[...].astype(_F32) * rope_ref[0:1, :]
    cosb = jnp.cos(ang) * rope_ref[1:2, :]
    sinb = jnp.sin(ang) * rope_ref[2:3, :]
    q_lat = lat[:, :Q_LORA_RANK]
    rq = q_lat * lax.rsqrt(jnp.mean(q_lat * q_lat, axis=-1, keepdims=True) + RMS_EPS) * qg_ref[...]
    kv_lat = lat[:, Q_LORA_RANK:Q_LORA_RANK + KV_LORA_RANK]
    ckv = (kv_lat * lax.rsqrt(jnp.mean(kv_lat * kv_lat, axis=-1, keepdims=True) + RMS_EPS)
           * kvg_ref[...]).astype(_BF)
    o = Q_LORA_RANK + KV_LORA_RANK
    kpe_ref[...] = (lat[:, o:o + ROPE_PAD] * cosb + lat[:, o + ROPE_PAD:] * sinb).astype(_BF)
    kn = _dot(ckv, wuk_ref[...]).astype(_BF)
    vv = _dot(ckv, wuv_ref[...]).astype(_BF)
    for h in range(N_HEADS):
        kn_ref[h] = kn[:, h * QK_NOPE_DIM:(h + 1) * QK_NOPE_DIM]
        v_ref[h] = vv[:, h * V_HEAD_DIM:(h + 1) * V_HEAD_DIM]

    rqb = rq.astype(_BF)
    qa = _dot(rqb, wuqa_ref[...])
    qb = _dot(rqb, wuqb_ref[...])
    scale = QK_HEAD_DIM ** -0.5 * LOG2E
    for h in range(N_HEADS):
        lo = h * QK_PAD
        q_ref[h, :, :QK_NOPE_DIM] = (qa[:, lo:lo + QK_NOPE_DIM] * scale).astype(_BF)
        hi = (qa[:, lo + QK_NOPE_DIM:lo + QK_PAD] * cosb
              + qb[:, h * ROPE_PAD:(h + 1) * ROPE_PAD] * sinb)
        q_ref[h, :, QK_NOPE_DIM:] = (hi * scale).astype(_BF)


def _proj_call(x2d, pos, rope, wg3, wlat, wgates, convw, qg, kvg, wuqa, wuqb, wuk, wuv, wcb, seq):
    t = x2d.shape[0]
    tm = PROJ_TM
    tps = seq // tm
    row = lambda w: pl.BlockSpec((tm, w), lambda i: (i, 0))
    heads = lambda w: pl.BlockSpec((None, N_HEADS, tm, w), lambda i: (i // tps, 0, i % tps, 0))
    paired = pl.BlockSpec((None, N_HEADS, tm, QK_PAD),
                          lambda i: (i // tps, 0, _paired_tile_pos(i % tps, tps), 0))
    head_major = lambda w: jax.ShapeDtypeStruct((t // seq, N_HEADS, seq, w), _BF)
    outs = [
        head_major(QK_PAD),
        head_major(QK_NOPE_DIM),
        jax.ShapeDtypeStruct((t, ROPE_PAD), _BF),
        head_major(V_HEAD_DIM),
        jax.ShapeDtypeStruct((t, D_MODEL), _BF),
        jax.ShapeDtypeStruct((t, D_MODEL), _BF),
    ]
    return pl.pallas_call(
        functools.partial(_proj_kernel, tps),
        grid=(t // tm,),
        in_specs=[row(D_MODEL), row(1)]
        + [_const_spec(a.shape)
           for a in (rope, wg3, wlat, wgates, convw, qg, kvg, wuqa, wuqb, wuk, wuv, wcb)],
        out_specs=[paired, heads(QK_NOPE_DIM), row(ROPE_PAD), heads(V_HEAD_DIM),
                   row(D_MODEL), row(D_MODEL)],
        out_shape=outs,
        scratch_shapes=[pltpu.VMEM((tm + HALO, CONV_DIM), _F32)],
        compiler_params=pltpu.CompilerParams(dimension_semantics=("arbitrary",),
                                             vmem_limit_bytes=VMEM_LIMIT),
        name="proj_conv_qkv",
    )(x2d, pos, rope, wg3, wlat, wgates, convw, qg, kvg, wuqa, wuqb, wuk, wuv, wcb)


def _paired_tile_pos(tile, n_tiles):
    return jnp.where(tile < n_tiles // 2, 2 * tile, 2 * (n_tiles - 1 - tile) + 1)


def _attn_kernel(n_tiles, q_ref, kn_ref, kpe_ref, v_ref, o_ref, kfull, vt, s0, s1, p0, p1, a0, a1,
                 m_ref, acc):
    pr = pl.program_id(2)
    tq = q_ref.shape[0] // 2
    tk = ATTN_TK
    s_bufs, p_bufs, a_bufs = (s0, s1), (p0, p1), (a0, a1)
    n_stages = n_tiles + 1

    @pl.when(pr == 0)
    def _():
        kfull[:, :QK_NOPE_DIM] = kn_ref[...]
        kfull[:, QK_NOPE_DIM:] = kpe_ref[...]
        vt[:V_HEAD_DIM, :] = v_ref[...].T
        vt[V_HEAD_DIM:, :] = jnp.ones((vt.shape[0] - V_HEAD_DIM, vt.shape[1]), _BF)

    def sel_of(t):
        return (t > pr).astype(jnp.int32)

    def block_of(t):
        return jnp.where(t <= pr, t, t - pr - 1)

    def scores_t(t):
        k = kfull[pl.ds(pl.multiple_of(block_of(t) * tk, tk), tk), :]
        q = q_ref[pl.ds(pl.multiple_of(sel_of(t) * tq, tq), tq), :]
        s_bufs[t % 2][...] = lax.dot_general(k, q, (((1,), (1,)), ((), ())),
                                             preferred_element_type=_F32)

    key_chunk = lax.broadcasted_iota(jnp.int32, (tk, tq), 0) // CHUNK
    qry_chunk = lax.broadcasted_iota(jnp.int32, (tk, tq), 1) // CHUNK
    causal = key_chunk <= qry_chunk

    def softmax_t(t):
        par = t % 2
        s_t = s_bufs[par][...]
        if t == n_stages - 1:
            s_t = jnp.where(causal, s_t, NEG_BIG)
        elif t < n_tiles // 2:
            s_t = jnp.where(jnp.logical_or(causal, pr != t), s_t, NEG_BIG)
        m_old = m_ref[sel_of(t)]
        m_new = jnp.maximum(m_old, jnp.max(s_t, axis=0, keepdims=True))
        m_ref[sel_of(t)] = m_new
        a_bufs[par][...] = jnp.exp2(m_old - m_new)
        p_bufs[par][...] = jnp.exp2(s_t - m_new).astype(_BF)

    def accumulate(t):
        par = t % 2
        v_t = vt[:, pl.ds(pl.multiple_of(block_of(t) * tk, tk), tk)]
        sel = sel_of(t)
        acc[sel] = a_bufs[par][...] * acc[sel] + _dot(v_t, p_bufs[par][...])

    m_ref[...] = jnp.full(m_ref.shape, NEG_BIG, _F32)
    acc[...] = jnp.zeros(acc.shape, _F32)
    scores_t(0)
    for t in range(n_stages):
        if t + 1 < n_stages:
            scores_t(t + 1)
        if t > 0:
            accumulate(t - 1)
        softmax_t(t)
    accumulate(n_stages - 1)
    for half in range(2):
        out_t = acc[half, :V_HEAD_DIM, :] / acc[half, V_HEAD_DIM:V_HEAD_DIM + 1, :]
        o_ref[half * tq:(half + 1) * tq, :] = out_t.T.astype(_BF)


def _attn_call(q, kn, kpe, v):
    b, _, s, _ = q.shape
    tq = ATTN_TQ
    n_tiles = s // tq
    return pl.pallas_call(
        functools.partial(_attn_kernel, n_tiles),
        grid=(b, N_HEADS, n_tiles // 2),
        in_specs=[
            pl.BlockSpec((None, None, 2 * tq, QK_PAD), lambda bi, h, i: (bi, h, i, 0)),
            pl.BlockSpec((None, None, s, QK_NOPE_DIM), lambda bi, h, i: (bi, h, 0, 0)),
            pl.BlockSpec((None, s, ROPE_PAD), lambda bi, h, i: (bi, 0, 0)),
            pl.BlockSpec((None, None, s, V_HEAD_DIM), lambda bi, h, i: (bi, h, 0, 0)),
        ],
        out_specs=pl.BlockSpec((None, None, 2 * tq, V_HEAD_DIM), lambda bi, h, i: (bi, h, i, 0)),
        out_shape=jax.ShapeDtypeStruct((b, N_HEADS, s, V_HEAD_DIM), _BF),
        scratch_shapes=[pltpu.VMEM((s, QK_PAD), _BF), pltpu.VMEM((V_HEAD_DIM + ONES_ROWS, s), _BF),
                        pltpu.VMEM((ATTN_TK, tq), _F32), pltpu.VMEM((ATTN_TK, tq), _F32),
                        pltpu.VMEM((ATTN_TK, tq), _BF), pltpu.VMEM((ATTN_TK, tq), _BF),
                        pltpu.VMEM((1, tq), _F32), pltpu.VMEM((1, tq), _F32),
                        pltpu.VMEM((2, 1, tq), _F32),
                        pltpu.VMEM((2, V_HEAD_DIM + ONES_ROWS, tq), _F32)],
        compiler_params=pltpu.CompilerParams(
            dimension_semantics=("arbitrary", "arbitrary", "arbitrary"),
            vmem_limit_bytes=VMEM_LIMIT),
        name="chunk_causal_attention",
    )(q, kn, kpe, v)


def _layer_norm(z, g, b):
    mu = jnp.mean(z, axis=-1, keepdims=True)
    zc = z - mu
    var = jnp.mean(zc * zc, axis=-1, keepdims=True)
    return zc * lax.rsqrt(var + LN_EPS) * g + b


def _post_kernel(expert_cap, attn_ref, ycg_ref, sga_ref, x_ref, wab_ref, wout_ref, g1_ref, b1_ref,
                 wr_ref, br_ref, tri_ref, h1_ref, h1pa_ref, h1pb_ref, slot_ref, tw_ref, cnt_ref, run):
    tm = x_ref.shape[0]

    @pl.when(pl.program_id(0) == 0)
    def _():
        run[...] = jnp.zeros(run.shape, _F32)

    attn = jnp.concatenate([attn_ref[h] for h in range(N_HEADS)], axis=1)
    y_attn = _dot(attn, wab_ref[...])
    merged = ycg_ref[...].astype(_F32) + sga_ref[...].astype(_F32) * y_attn
    mix = _dot(merged.astype(_BF), wout_ref[...])
    h1 = _layer_norm(DEEPNORM_ALPHA * x_ref[...] + mix, g1_ref[...], b1_ref[...])
    h1_ref[...] = h1
    h1p = _pack_rows(h1)
    h1pa_ref[...] = h1p[:, :HALF]
    h1pb_ref[...] = h1p[:, HALF:]

    nt = (((1,), (1,)), ((), ()))
    h_hi = h1.astype(_BF)
    h_lo = (h1 - h_hi.astype(_F32)).astype(_BF)
    both = lax.dot_general(wr_ref[...], h_hi, nt, preferred_element_type=_F32)
    logits = (both[:N_EXPERTS] + both[N_EXPERTS:] + br_ref[:, 0:1]
              + lax.dot_general(wr_ref[:N_EXPERTS, :], h_lo, nt, preferred_element_type=_F32))
    expert = lax.broadcasted_iota(jnp.int32, (N_EXPERTS, tm), 0)
    work = logits
    chosen = jnp.zeros((N_EXPERTS, tm), jnp.bool_)
    sels, exps = [], []
    for k in range(TOP_K):
        mx = jnp.max(work, axis=0, keepdims=True)
        sel = jnp.min(jnp.where(work == mx, expert, N_EXPERTS), axis=0, keepdims=True)
        hit = expert == sel
        sels.append(sel)
        exps.append(jnp.exp(mx - (mx if k == 0 else top0)))
        if k == 0:
            top0 = mx
        chosen = jnp.logical_or(chosen, hit)
        work = jnp.where(hit, -jnp.inf, work)
    denom = exps[0] + exps[1] + exps[2] + exps[3]

    onehot = chosen.astype(_F32)
    before = _dot(onehot.astype(_BF), tri_ref[...]) + run[:, 0:1]
    rows = []
    for k in range(TOP_K):
        rank = jnp.sum(jnp.where(expert == sels[k], before, 0.0), axis=0, keepdims=True)
        rows.append(sels[k] * expert_cap + rank.astype(jnp.int32))
    slot_ref[...] = jnp.concatenate(rows + [jnp.zeros((8 - TOP_K, tm), jnp.int32)], axis=0)
    run[...] = run[...] + jnp.sum(onehot, axis=1, keepdims=True)
    cnt_ref[...] = run[...]
    tw_t = jnp.concatenate([e / denom for e in exps] + [jnp.zeros((128 - TOP_K, tm), _F32)], axis=0)
    tw_ref[...] = tw_t.T


def _post_call(attn, ycg, sga, x2d, wab, wout, g1, b1, wr, br):
    t = x2d.shape[0]
    tm = POST_TM
    row = lambda w: pl.BlockSpec((tm, w), lambda i: (i, 0))
    tps = attn.shape[2] // tm
    tri = (jnp.arange(tm)[:, None] < jnp.arange(tm)[None, :]).astype(_BF)
    outs = [
        jax.ShapeDtypeStruct((t, D_MODEL), _F32),
        jax.ShapeDtypeStruct((t, HALF), jnp.int32),
        jax.ShapeDtypeStruct((t, HALF), jnp.int32),
        jax.ShapeDtypeStruct((8, t), jnp.int32),
        jax.ShapeDtypeStruct((t, 128), _F32),
        jax.ShapeDtypeStruct((N_EXPERTS, 128), _F32),
    ]
    out_specs = [row(D_MODEL), row(HALF), row(HALF), pl.BlockSpec((8, tm), lambda i: (0, i)),
                 row(128), pl.BlockSpec((N_EXPERTS, 128), lambda i: (0, 0))]
    return pl.pallas_call(
        functools.partial(_post_kernel, t),
        grid=(t // tm,),
        in_specs=[pl.BlockSpec((None, N_HEADS, tm, V_HEAD_DIM),
                               lambda i: (i // tps, 0, _paired_tile_pos(i % tps, tps), 0))]
        + [row(D_MODEL)] * 3
        + [_const_spec(a.shape) for a in (wab, wout, g1, b1, wr, br, tri)],
        out_specs=out_specs,
        out_shape=outs,
        scratch_shapes=[pltpu.VMEM((N_EXPERTS, 128), _F32)],
        compiler_params=pltpu.CompilerParams(dimension_semantics=("arbitrary",),
                                             vmem_limit_bytes=VMEM_LIMIT),
        name="merge_out_ln1_router",
    )(attn, ycg, sga, x2d, wab, wout, g1, b1, wr, br, tri)


def _moe_kernel(be_ref, we_ref, rb_ref, nused_ref, xa_ref, xb_ref, wgu_ref, bgu_ref, wd_ref, bd_ref,
                ya_ref, yb_ref, wgu_bf, wd_bf):
    i = pl.program_id(0)
    prev = be_ref[jnp.maximum(i - 1, 0)]
    fresh = jnp.logical_or(i == 0, be_ref[i] != prev)
    used = i < nused_ref[0]

    @pl.when(jnp.logical_and(fresh, used))
    def _():
        wgu_bf[...] = wgu_ref[...].astype(_BF)
        wd_bf[...] = wd_ref[...].astype(_BF)

    @pl.when(used)
    def _():
        a_lo, a_hi = _unpack_rows(xa_ref[...])
        b_lo, b_hi = _unpack_rows(xb_ref[...])
        xs = jnp.concatenate([a_lo, b_lo, a_hi, b_hi], axis=1).astype(_BF)
        gu = _dot(xs, wgu_bf[...]) + bgu_ref[...]
        gate = jnp.minimum(gu[:, :D_FF_EXPERT], SWIGLU_LIMIT)
        up = jnp.clip(gu[:, D_FF_EXPERT:], -SWIGLU_LIMIT, SWIGLU_LIMIT)
        hid = (up + 1.0) * (gate * jax.nn.sigmoid(SWIGLU_ALPHA * gate))
        yp = _pack_rows(_dot(hid.astype(_BF), wd_bf[...]) + bd_ref[...])
        ya_ref[...] = yp[:, :HALF]
        yb_ref[...] = yp[:, HALF:]


def _moe_call(block_e, weight_e, block_row, n_used, xa, xb, wgu, bgu, wd, bd):
    n_slots = xa.shape[0]
    bm = MOE_BM
    n_blocks = block_e.shape[0]
    grid_spec = pltpu.PrefetchScalarGridSpec(
        num_scalar_prefetch=4,
        grid=(n_blocks,),
        in_specs=[
            pl.BlockSpec((bm, HALF), lambda i, be, we, rb, nu: (rb[i], 0)),
            pl.BlockSpec((bm, HALF), lambda i, be, we, rb, nu: (rb[i], 0)),
            pl.BlockSpec((None, D_MODEL, 2 * D_FF_EXPERT), lambda i, be, we, rb, nu: (we[i], 0, 0)),
            pl.BlockSpec((None, 1, 2 * D_FF_EXPERT), lambda i, be, we, rb, nu: (be[i], 0, 0)),
            pl.BlockSpec((None, D_FF_EXPERT, D_MODEL), lambda i, be, we, rb, nu: (we[i], 0, 0)),
            pl.BlockSpec((None, 1, D_MODEL), lambda i, be, we, rb, nu: (be[i], 0, 0)),
        ],
        out_specs=[pl.BlockSpec((bm, HALF), lambda i, be, we, rb, nu: (rb[i], 0))] * 2,
        scratch_shapes=[pltpu.VMEM((D_MODEL, 2 * D_FF_EXPERT), _BF),
                        pltpu.VMEM((D_FF_EXPERT, D_MODEL), _BF)],
    )
    return pl.pallas_call(
        _moe_kernel,
        grid_spec=grid_spec,
        out_shape=[jax.ShapeDtypeStruct((n_slots, HALF), jnp.int32)] * 2,
        compiler_params=pltpu.CompilerParams(dimension_semantics=("arbitrary",),
                                             vmem_limit_bytes=VMEM_LIMIT),
        name="expert_ffn",
    )(block_e, weight_e, block_row, n_used, xa, xb, wgu, bgu, wd, bd)


def _combine_kernel(*refs):
    ya_refs, yb_refs = refs[:TOP_K], refs[TOP_K:2 * TOP_K]
    tw_ref, h1_ref, g2_ref, b2_ref, o_ref = refs[2 * TOP_K:]
    tw = tw_ref[...]
    parts = [jnp.zeros((tw.shape[0], HALF), _F32) for _ in range(4)]
    for k in range(TOP_K):
        a_lo, a_hi = _unpack_rows(ya_refs[k][...])
        b_lo, b_hi = _unpack_rows(yb_refs[k][...])
        w = tw[:, k:k + 1]
        parts = [p + w * y for p, y in zip(parts, (a_lo, b_lo, a_hi, b_hi))]
    ffn = jnp.concatenate(parts, axis=1)
    o_ref[...] = _layer_norm(DEEPNORM_ALPHA * h1_ref[...] + ffn, g2_ref[...], b2_ref[...])


def _combine_call(yga, ygb, tw, h1, g2, b2):
    t = h1.shape[0]
    tm = COMB_TM
    row = lambda w: pl.BlockSpec((tm, w), lambda i: (i, 0))
    krow = lambda k: pl.BlockSpec((tm, HALF), lambda i: (k * (t // tm) + i, 0))
    return pl.pallas_call(
        _combine_kernel,
        grid=(t // tm,),
        in_specs=[krow(k) for k in range(TOP_K)] * 2 + [row(128), row(D_MODEL),
                  _const_spec(g2.shape), _const_spec(b2.shape)],
        out_specs=row(D_MODEL),
        out_shape=jax.ShapeDtypeStruct((t, D_MODEL), _F32),
        compiler_params=pltpu.CompilerParams(dimension_semantics=("arbitrary",),
                                             vmem_limit_bytes=VMEM_LIMIT),
        name="combine_ln2",
    )(*([yga] * TOP_K + [ygb] * TOP_K), tw, h1, g2, b2)


def _prepare_weights(w_in, w_uq):
    o_q = 3 * CONV_DIM
    o_kv = o_q + Q_LORA_RANK
    o_pe = o_kv + KV_LORA_RANK
    o_gc = o_pe + QK_ROPE_DIM
    half = QK_ROPE_DIM // 2
    wg3 = w_in[:, :o_q]
    kpe = w_in[:, o_pe:o_gc]
    kpe_sw = jnp.concatenate([kpe[:, half:], kpe[:, :half]], axis=1)
    zpad = jnp.zeros((D_MODEL, ROPE_PAD - QK_ROPE_DIM), w_in.dtype)
    wlat = jnp.concatenate([w_in[:, o_q:o_pe], kpe, zpad, kpe_sw, zpad], axis=1)
    wgates = w_in[:, o_gc:]
    wq = w_uq.reshape(Q_LORA_RANK, N_HEADS, QK_HEAD_DIM)
    wuqa = jnp.pad(wq, ((0, 0), (0, 0), (0, QK_PAD - QK_HEAD_DIM)))
    pe = wq[:, :, QK_NOPE_DIM:]
    pe_sw = jnp.concatenate([pe[:, :, half:], pe[:, :, :half]], axis=2)
    wuqb = jnp.pad(pe_sw, ((0, 0), (0, 0), (0, ROPE_PAD - QK_ROPE_DIM)))
    return (wg3.astype(_BF), wlat.astype(_BF), wgates.astype(_BF),
            wuqa.reshape(Q_LORA_RANK, N_HEADS * QK_PAD).astype(_BF),
            wuqb.reshape(Q_LORA_RANK, N_HEADS * ROPE_PAD).astype(_BF))


def _rope_rows():
    half = QK_ROPE_DIM // 2
    inv_freq = ROPE_BASE ** (-jnp.arange(0, QK_ROPE_DIM, 2, dtype=_F32) / QK_ROPE_DIM)
    z = jnp.zeros((ROPE_PAD - QK_ROPE_DIM,), _F32)
    ones = jnp.ones((half,), _F32)
    rows = jnp.stack([jnp.concatenate([inv_freq, inv_freq, z]),
                      jnp.concatenate([ones, ones, z]),
                      jnp.concatenate([-ones, ones, z])])
    return jnp.pad(rows, ((0, 8 - rows.shape[0]), (0, 0)))


def _block_tables(counts, n_tok):
    bm = MOE_BM
    n_blocks = n_tok * TOP_K // bm + N_EXPERTS
    nblk = (counts + bm - 1) // bm
    end = jnp.cumsum(nblk)
    start = end - nblk
    n_used = end[-1]
    i = jnp.minimum(jnp.arange(n_blocks, dtype=jnp.int32), n_used - 1)
    block_e = (end[None, :] <= i[:, None]).sum(axis=1).astype(jnp.int32)
    first = jnp.sum(jnp.where(jnp.arange(N_EXPERTS)[None, :] == block_e[:, None], start[None, :], 0), axis=1)
    block_row = block_e * (n_tok // bm) + (i - first)
    fresh = jnp.concatenate([jnp.ones((1,), jnp.bool_), block_e[1:] != block_e[:-1]])
    later = jnp.where(block_e[None, :] > block_e[:, None], block_e[None, :], N_EXPERTS)
    nxt = jnp.min(later, axis=1)
    weight_e = jnp.where(jnp.logical_or(fresh, nxt == N_EXPERTS), block_e, nxt).astype(jnp.int32)
    return block_e, weight_e, block_row.astype(jnp.int32), n_used.astype(jnp.int32).reshape(1)


def kernel(x, positions, w_in, conv_w, q_norm_g, w_uq, kv_norm_g, w_uk, w_uv, w_conv_branch,
           w_attn_branch, w_out, ln1_g, ln1_b, w_router, b_router, w_gate_up, b_gate_up, w_down,
           b_down, ln2_g, ln2_b):
    b, s, d = x.shape
    t = b * s
    x2d = x.reshape(t, d)
    pos = positions.reshape(t, 1).astype(jnp.int32)
    rope = _rope_rows()
    for l in range(DEPTH):
        wg3, wlat, wgates, wuqa, wuqb = _prepare_weights(w_in[l], w_uq[l])
        q, kn, kpe, v, ycg, sga = _proj_call(
            x2d, pos, rope, wg3, wlat, wgates, conv_w[l], q_norm_g[l][None, :], kv_norm_g[l][None, :],
            wuqa, wuqb, w_uk[l].astype(_BF), w_uv[l].astype(_BF), w_conv_branch[l].astype(_BF), s)
        attn = _attn_call(q, kn, kpe.reshape(b, s, -1), v)
        wr_t = w_router[l].T
        wr_hi = wr_t.astype(_BF)
        wr_lo = (wr_t - wr_hi.astype(_F32)).astype(_BF)
        h1, h1pa, h1pb, slots, tw_pad, counts = _post_call(
            attn, ycg, sga, x2d, w_attn_branch[l].astype(_BF), w_out[l].astype(_BF),
            ln1_g[l][None, :], ln1_b[l][None, :], jnp.concatenate([wr_hi, wr_lo], axis=0),
            jnp.broadcast_to(b_router[l][:, None], (N_EXPERTS, 128)))
        block_e, weight_e, block_row, n_used = _block_tables(counts[:, 0].astype(jnp.int32), t)
        xa, xb = _sc_scatter_rows([h1pa, h1pb], slots, TOP_K, N_EXPERTS * t)
        ya, yb = _moe_call(block_e, weight_e, block_row, n_used, xa, xb, w_gate_up[l], b_gate_up[l][:, None, :],
                           w_down[l], b_down[l][:, None, :])
        yga, ygb = _sc_gather_rows([ya, yb], slots, TOP_K)
        x2d = _combine_call(yga, ygb, tw_pad, h1, ln2_g[l][None, :], ln2_b[l][None, :])
    return x2d.reshape(b, s, d)
```

```python
import functools

import numpy as np
import jax
import jax.numpy as jnp
from jax import lax
from jax.experimental import pallas as pl
from jax.experimental.pallas import tpu as pltpu
from jax.experimental.pallas import tpu_sc as plsc

D_MODEL = 1024
CHUNK = 64
CONV_DIM = D_MODEL
CONV_WIDTH = 3
N_HEADS = 8
QK_NOPE_DIM = 128
QK_ROPE_DIM = 64
V_HEAD_DIM = 128
QK_HEAD_DIM = QK_NOPE_DIM + QK_ROPE_DIM
Q_LORA_RANK = 384
KV_LORA_RANK = 256
ROPE_BASE = 10000.0
N_EXPERTS = 32
TOP_K = 4
D_FF_EXPERT = D_MODEL
SWIGLU_LIMIT = 7.0
SWIGLU_ALPHA = 1.702
LN_EPS = 1e-5
RMS_EPS = 1e-6
DEPTH = 1
DEEPNORM_ALPHA = (2 * DEPTH) ** 0.25

QK_PAD = 256
ROPE_PAD = 128

PROJ_TM = 512
ATTN_TQ = 512
ATTN_TK = 512
ONES_ROWS = 16
POST_TM = 512
MOE_BM = 512
COMB_TM = 512
SC_WINDOW = 128
PACKED = D_MODEL // 2
HALF = PACKED // 2
HALO = 8
NEG_BIG = -1e30
LOG2E = 1.4426950408889634
VMEM_LIMIT = 56 * 1024 * 1024

_BF = jnp.bfloat16
_F32 = jnp.float32


def _dot(a, b):
    return jnp.dot(a, b, preferred_element_type=_F32)


def _const_spec(shape):
    nd = len(shape)
    return pl.BlockSpec(shape, lambda *_: (0,) * nd, pipeline_mode=pl.Buffered(1))


def _pack_rows(a):
    half = a.shape[1] // 2
    lo = lax.bitcast_convert_type(a[:, :half].astype(_BF).astype(_F32), jnp.int32)
    hi = lax.bitcast_convert_type(a[:, half:].astype(_BF).astype(_F32), jnp.int32)
    return lax.shift_right_logical(lo, 16) | (hi & jnp.int32(-65536))


def _unpack_rows(p):
    lo = lax.bitcast_convert_type(lax.shift_left(p, 16), _F32)
    hi = lax.bitcast_convert_type(p & jnp.int32(-65536), _F32)
    return lo, hi


def _sc_mesh():
    return plsc.VectorSubcoreMesh(core_axis_name="core", subcore_axis_name="subcore")


def _sc_scatter_rows(tables, idx, n_copies, n_out):
    n_rows, width = tables[0].shape
    n_tab = len(tables)
    window = SC_WINDOW

    @functools.partial(pl.kernel, mesh=_sc_mesh(),
                       out_type=[jax.ShapeDtypeStruct((n_out, width), tables[0].dtype)] * n_tab)
    def scatter_kernel(*refs):
        idx_hbm = refs[n_tab]
        for src_hbm, out_hbm in zip(refs[:n_tab], refs[n_tab + 1:]):
            def body(src_vmem, idx_vmem, out_hbm=out_hbm):
                for k in range(n_copies):
                    pltpu.sync_copy(src_vmem, out_hbm.at[idx_vmem.at[k]])

            pltpu.emit_pipeline(
                body,
                grid=(n_rows // window,),
                in_specs=[pl.BlockSpec((window, width), lambda i: (i, 0)),
                          pl.BlockSpec((idx.shape[0], window), lambda i: (0, i))],
                out_specs=[],
                core_axis_name=("core", "subcore"),
                dimension_semantics=(pltpu.PARALLEL,),
            )(src_hbm, idx_hbm)

    return scatter_kernel(*tables, idx)


def _sc_gather_rows(tables, idx, n_copies):
    n_rows = idx.shape[1]
    width = tables[0].shape[1]
    n_tab = len(tables)
    window = SC_WINDOW
    steps = n_rows // window

    @functools.partial(pl.kernel, mesh=_sc_mesh(),
                       out_type=[jax.ShapeDtypeStruct((n_copies * n_rows, width), tables[0].dtype)] * n_tab)
    def gather_kernel(*refs):
        idx_hbm = refs[n_tab]
        for table_hbm, out_hbm in zip(refs[:n_tab], refs[n_tab + 1:]):
            def body(idx_vmem, out_vmem, table_hbm=table_hbm):
                pltpu.sync_copy(table_hbm.at[idx_vmem.at[0]], out_vmem)

            pltpu.emit_pipeline(
                body,
                grid=(n_copies, steps),
                in_specs=[pl.BlockSpec((1, window), lambda k, i: (k, i))],
                out_specs=[pl.BlockSpec((window, width), lambda k, i: (k * steps + i, 0))],
                core_axis_name=("core", "subcore"),
                dimension_semantics=(pltpu.PARALLEL, pltpu.PARALLEL),
            )(idx_hbm, out_hbm)

    return gather_kernel(*tables, idx)


def _proj_kernel(tiles_per_seq,
                 x_ref, pos_ref, rope_ref, wg3_ref, wlat_ref, wgates_ref, convw_ref,
                 qg_ref, kvg_ref, wuqa_ref, wuk_ref, wuv_ref, wcb_ref,
                 q_ref, kn_ref, kpe_ref, v_ref, ycg_ref, sga_ref, ubuf):
    i = pl.program_id(0)
    tm = x_ref.shape[0]
    xb = x_ref[...].astype(_BF)

    @pl.when(i % tiles_per_seq == 0)
    def _():
        ubuf[0:HALO, :] = jnp.zeros((HALO, CONV_DIM), _F32)

    g3 = _dot(xb, wg3_ref[...])
    u = g3[:, CONV_DIM:2 * CONV_DIM] * g3[:, 2 * CONV_DIM:]
    ubuf[HALO:HALO + tm, :] = u
    cw = convw_ref[...]
    conv = (cw[2:3, :] * u + cw[1:2, :] * ubuf[HALO - 1:HALO - 1 + tm, :]
            + cw[0:1, :] * ubuf[HALO - 2:HALO - 2 + tm, :])
    ubuf[0:HALO, :] = ubuf[tm:tm + HALO, :]
    yc = _dot((g3[:, :CONV_DIM] * conv).astype(_BF), wcb_ref[...])
    gates = _dot(xb, wgates_ref[...])
    ycg_ref[...] = (jax.nn.sigmoid(gates[:, :D_MODEL]) * yc).astype(_BF)
    sga_ref[...] = jax.nn.sigmoid(gates[:, D_MODEL:]).astype(_BF)

    lat = _dot(xb, wlat_ref[...])
    ang = pos_ref[...].astype(_F32) * rope_ref[0:1, :]
    cosb = jnp.cos(ang) * rope_ref[1:2, :]
    sinb = jnp.sin(ang) * rope_ref[2:3, :]
    q_lat = lat[:, :Q_LORA_RANK]
    rq = q_lat * lax.rsqrt(jnp.mean(q_lat * q_lat, axis=-1, keepdims=True) + RMS_EPS) * qg_ref[...]
    kv_lat = lat[:, Q_LORA_RANK:Q_LORA_RANK + KV_LORA_RANK]
    ckv = (kv_lat * lax.rsqrt(jnp.mean(kv_lat * kv_lat, axis=-1, keepdims=True) + RMS_EPS)
           * kvg_ref[...]).astype(_BF)
    o = Q_LORA_RANK + KV_LORA_RANK
    lane = lax.broadcasted_iota(jnp.int32, (tm, ROPE_PAD), 1)
    half = QK_ROPE_DIM // 2

    def swap_halves(u):
        return jnp.where(lane < half, pltpu.roll(u, ROPE_PAD - half, axis=1), pltpu.roll(u, half, axis=1))

    k_pe = lat[:, o:o + ROPE_PAD]
    kpe_ref[...] = (k_pe * cosb + swap_halves(k_pe) * sinb).astype(_BF)
    kn = _dot(ckv, wuk_ref[...]).astype(_BF)
    vv = _dot(ckv, wuv_ref[...]).astype(_BF)
    for h in range(N_HEADS):
        kn_ref[h] = kn[:, h * QK_NOPE_DIM:(h + 1) * QK_NOPE_DIM]
        v_ref[h] = vv[:, h * V_HEAD_DIM:(h + 1) * V_HEAD_DIM]

    rqb = rq.astype(_BF)
    qa = _dot(rqb, wuqa_ref[...])
    scale = QK_HEAD_DIM ** -0.5 * LOG2E
    for h in range(N_HEADS):
        lo = h * QK_PAD
        q_ref[h, :, :QK_NOPE_DIM] = (qa[:, lo:lo + QK_NOPE_DIM] * scale).astype(_BF)
        q_pe = qa[:, lo + QK_NOPE_DIM:lo + QK_PAD]
        hi = q_pe * cosb + swap_halves(q_pe) * sinb
        q_ref[h, :, QK_NOPE_DIM:] = (hi * scale).astype(_BF)


def _proj_call(x2d, pos, rope, wg3, wlat, wgates, convw, qg, kvg, wuqa, wuk, wuv, wcb, seq):
    t = x2d.shape[0]
    tm = PROJ_TM
    tps = seq // tm
    row = lambda w: pl.BlockSpec((tm, w), lambda i: (i, 0))
    heads = lambda w: pl.BlockSpec((None, N_HEADS, tm, w), lambda i: (i // tps, 0, i % tps, 0))
    paired = pl.BlockSpec((None, N_HEADS, tm, QK_PAD),
                          lambda i: (i // tps, 0, _paired_tile_pos(i % tps, tps), 0))
    head_major = lambda w: jax.ShapeDtypeStruct((t // seq, N_HEADS, seq, w), _BF)
    outs = [
        head_major(QK_PAD),
        head_major(QK_NOPE_DIM),
        jax.ShapeDtypeStruct((t, ROPE_PAD), _BF),
        head_major(V_HEAD_DIM),
        jax.ShapeDtypeStruct((t, D_MODEL), _BF),
        jax.ShapeDtypeStruct((t, D_MODEL), _BF),
    ]
    return pl.pallas_call(
        functools.partial(_proj_kernel, tps),
        grid=(t // tm,),
        in_specs=[row(D_MODEL), row(1)]
        + [_const_spec(a.shape)
           for a in (rope, wg3, wlat, wgates, convw, qg, kvg, wuqa, wuk, wuv, wcb)],
        out_specs=[paired, heads(QK_NOPE_DIM), row(ROPE_PAD), heads(V_HEAD_DIM),
                   row(D_MODEL), row(D_MODEL)],
        out_shape=outs,
        scratch_shapes=[pltpu.VMEM((tm + HALO, CONV_DIM), _F32)],
        compiler_params=pltpu.CompilerParams(dimension_semantics=("arbitrary",),
                                             vmem_limit_bytes=VMEM_LIMIT),
        name="proj_conv_qkv",
    )(x2d, pos, rope, wg3, wlat, wgates, convw, qg, kvg, wuqa, wuk, wuv, wcb)


def _paired_tile_pos(tile, n_tiles):
    return jnp.where(tile < n_tiles // 2, 2 * tile, 2 * (n_tiles - 1 - tile) + 1)


def _attn_kernel(n_tiles, q_ref, kn_ref, kpe_ref, v_ref, o_ref, kfull, vt, s0, s1, p0, p1, a0, a1,
                 m_ref, acc):
    pr = pl.program_id(2)
    tq = q_ref.shape[0] // 2
    tk = ATTN_TK
    s_bufs, p_bufs, a_bufs = (s0, s1), (p0, p1), (a0, a1)
    n_stages = n_tiles + 1

    @pl.when(pr == 0)
    def _():
        kfull[:, :QK_NOPE_DIM] = kn_ref[...]
        kfull[:, QK_NOPE_DIM:] = kpe_ref[...]
        vt[:V_HEAD_DIM, :] = v_ref[...].T
        vt[V_HEAD_DIM:, :] = jnp.ones((vt.shape[0] - V_HEAD_DIM, vt.shape[1]), _BF)

    def sel_of(t):
        return (t > pr).astype(jnp.int32)

    def block_of(t):
        return jnp.where(t <= pr, t, t - pr - 1)

    def scores_t(t):
        k = kfull[pl.ds(pl.multiple_of(block_of(t) * tk, tk), tk), :]
        q = q_ref[pl.ds(pl.multiple_of(sel_of(t) * tq, tq), tq), :]
        s_bufs[t % 2][...] = lax.dot_general(k, q, (((1,), (1,)), ((), ())),
                                             preferred_element_type=_F32)

    key_chunk = lax.broadcasted_iota(jnp.int32, (tk, tq), 0) // CHUNK
    qry_chunk = lax.broadcasted_iota(jnp.int32, (tk, tq), 1) // CHUNK
    causal = key_chunk <= qry_chunk

    def softmax_t(t):
        par = t % 2
        s_t = s_bufs[par][...]
        if t == n_stages - 1:
            s_t = jnp.where(causal, s_t, NEG_BIG)
        elif t < n_tiles // 2:
            s_t = jnp.where(jnp.logical_or(causal, pr != t), s_t, NEG_BIG)
        m_old = m_ref[sel_of(t)]
        m_new = jnp.maximum(m_old, jnp.max(s_t, axis=0, keepdims=True))
        m_ref[sel_of(t)] = m_new
        a_bufs[par][...] = jnp.exp2(m_old - m_new)
        p_bufs[par][...] = jnp.exp2(s_t - m_new).astype(_BF)

    def accumulate(t):
        par = t % 2
        v_t = vt[:, pl.ds(pl.multiple_of(block_of(t) * tk, tk), tk)]
        sel = sel_of(t)
        acc[sel] = a_bufs[par][...] * acc[sel] + _dot(v_t, p_bufs[par][...])

    m_ref[...] = jnp.full(m_ref.shape, NEG_BIG, _F32)
    acc[...] = jnp.zeros(acc.shape, _F32)
    scores_t(0)
    for t in range(n_stages):
        if t + 1 < n_stages:
            scores_t(t + 1)
        if t > 0:
            accumulate(t - 1)
        softmax_t(t)
    accumulate(n_stages - 1)
    for half in range(2):
        out_t = acc[half, :V_HEAD_DIM, :] / acc[half, V_HEAD_DIM:V_HEAD_DIM + 1, :]
        o_ref[half * tq:(half + 1) * tq, :] = out_t.T.astype(_BF)


def _attn_call(q, kn, kpe, v):
    b, _, s, _ = q.shape
    tq = ATTN_TQ
    n_tiles = s // tq
    return pl.pallas_call(
        functools.partial(_attn_kernel, n_tiles),
        grid=(b, N_HEADS, n_tiles // 2),
        in_specs=[
            pl.BlockSpec((None, None, 2 * tq, QK_PAD), lambda bi, h, i: (bi, h, i, 0)),
            pl.BlockSpec((None, None, s, QK_NOPE_DIM), lambda bi, h, i: (bi, h, 0, 0)),
            pl.BlockSpec((None, s, ROPE_PAD), lambda bi, h, i: (bi, 0, 0)),
            pl.BlockSpec((None, None, s, V_HEAD_DIM), lambda bi, h, i: (bi, h, 0, 0)),
        ],
        out_specs=pl.BlockSpec((None, None, 2 * tq, V_HEAD_DIM), lambda bi, h, i: (bi, h, i, 0)),
        out_shape=jax.ShapeDtypeStruct((b, N_HEADS, s, V_HEAD_DIM), _BF),
        scratch_shapes=[pltpu.VMEM((s, QK_PAD), _BF), pltpu.VMEM((V_HEAD_DIM + ONES_ROWS, s), _BF),
                        pltpu.VMEM((ATTN_TK, tq), _F32), pltpu.VMEM((ATTN_TK, tq), _F32),
                        pltpu.VMEM((ATTN_TK, tq), _BF), pltpu.VMEM((ATTN_TK, tq), _BF),
                        pltpu.VMEM((1, tq), _F32), pltpu.VMEM((1, tq), _F32),
                        pltpu.VMEM((2, 1, tq), _F32),
                        pltpu.VMEM((2, V_HEAD_DIM + ONES_ROWS, tq), _F32)],
        compiler_params=pltpu.CompilerParams(
            dimension_semantics=("arbitrary", "arbitrary", "arbitrary"),
            vmem_limit_bytes=VMEM_LIMIT),
        name="chunk_causal_attention",
    )(q, kn, kpe, v)


def _layer_norm(z, g, b):
    mu = jnp.mean(z, axis=-1, keepdims=True)
    zc = z - mu
    var = jnp.mean(zc * zc, axis=-1, keepdims=True)
    return zc * lax.rsqrt(var + LN_EPS) * g + b


def _post_kernel(expert_cap, attn_ref, ycg_ref, sga_ref, x_ref, wab_ref, wout_ref, g1_ref, b1_ref,
                 wr_ref, br_ref, tri_ref, h1_ref, h1pa_ref, h1pb_ref, slot_ref, tw_ref, cnt_ref, run):
    tm = x_ref.shape[0]

    @pl.when(pl.program_id(0) == 0)
    def _():
        run[...] = jnp.zeros(run.shape, _F32)

    attn = jnp.concatenate([attn_ref[h] for h in range(N_HEADS)], axis=1)
    y_attn = _dot(attn, wab_ref[...])
    merged = ycg_ref[...].astype(_F32) + sga_ref[...].astype(_F32) * y_attn
    mix = _dot(merged.astype(_BF), wout_ref[...])
    h1 = _layer_norm(DEEPNORM_ALPHA * x_ref[...] + mix, g1_ref[...], b1_ref[...])
    h1_ref[...] = h1
    h1p = _pack_rows(h1)
    h1pa_ref[...] = h1p[:, :HALF]
    h1pb_ref[...] = h1p[:, HALF:]

    nt = (((1,), (1,)), ((), ()))
    h_hi = h1.astype(_BF)
    h_lo = (h1 - h_hi.astype(_F32)).astype(_BF)
    both = lax.dot_general(wr_ref[...], h_hi, nt, preferred_element_type=_F32)
    logits = (both[:N_EXPERTS] + both[N_EXPERTS:] + br_ref[:, 0:1]
              + lax.dot_general(wr_ref[:N_EXPERTS, :], h_lo, nt, preferred_element_type=_F32))
    expert = lax.broadcasted_iota(jnp.int32, (N_EXPERTS, tm), 0)
    work = logits
    chosen = jnp.zeros((N_EXPERTS, tm), jnp.bool_)
    sels, exps = [], []
    for k in range(TOP_K):
        mx = jnp.max(work, axis=0, keepdims=True)
        sel = jnp.min(jnp.where(work == mx, expert, N_EXPERTS), axis=0, keepdims=True)
        hit = expert == sel
        sels.append(sel)
        exps.append(jnp.exp(mx - (mx if k == 0 else top0)))
        if k == 0:
            top0 = mx
        chosen = jnp.logical_or(chosen, hit)
        work = jnp.where(hit, -jnp.inf, work)
    denom = exps[0] + exps[1] + exps[2] + exps[3]

    onehot = chosen.astype(_F32)
    before = _dot(onehot.astype(_BF), tri_ref[...]) + run[:, 0:1]
    rows = []
    for k in range(TOP_K):
        rank = jnp.sum(jnp.where(expert == sels[k], before, 0.0), axis=0, keepdims=True)
        rows.append(sels[k] * expert_cap + rank.astype(jnp.int32))
    slot_ref[...] = jnp.concatenate(rows + [jnp.zeros((8 - TOP_K, tm), jnp.int32)], axis=0)
    run[...] = run[...] + jnp.sum(onehot, axis=1, keepdims=True)
    cnt_ref[...] = run[...]
    tw_t = jnp.concatenate([e / denom for e in exps] + [jnp.zeros((128 - TOP_K, tm), _F32)], axis=0)
    tw_ref[...] = tw_t.T


def _post_call(attn, ycg, sga, x2d, wab, wout, g1, b1, wr, br):
    t = x2d.shape[0]
    tm = POST_TM
    row = lambda w: pl.BlockSpec((tm, w), lambda i: (i, 0))
    tps = attn.shape[2] // tm
    tri = (jnp.arange(tm)[:, None] < jnp.arange(tm)[None, :]).astype(_BF)
    outs = [
        jax.ShapeDtypeStruct((t, D_MODEL), _F32),
        jax.ShapeDtypeStruct((t, HALF), jnp.int32),
        jax.ShapeDtypeStruct((t, HALF), jnp.int32),
        jax.ShapeDtypeStruct((8, t), jnp.int32),
        jax.ShapeDtypeStruct((t, 128), _F32),
        jax.ShapeDtypeStruct((N_EXPERTS, 128), _F32),
    ]
    out_specs = [row(D_MODEL), row(HALF), row(HALF), pl.BlockSpec((8, tm), lambda i: (0, i)),
                 row(128), pl.BlockSpec((N_EXPERTS, 128), lambda i: (0, 0))]
    return pl.pallas_call(
        functools.partial(_post_kernel, t),
        grid=(t // tm,),
        in_specs=[pl.BlockSpec((None, N_HEADS, tm, V_HEAD_DIM),
                               lambda i: (i // tps, 0, _paired_tile_pos(i % tps, tps), 0))]
        + [row(D_MODEL)] * 3
        + [_const_spec(a.shape) for a in (wab, wout, g1, b1, wr, br, tri)],
        out_specs=out_specs,
        out_shape=outs,
        scratch_shapes=[pltpu.VMEM((N_EXPERTS, 128), _F32)],
        compiler_params=pltpu.CompilerParams(dimension_semantics=("arbitrary",),
                                             vmem_limit_bytes=VMEM_LIMIT),
        name="merge_out_ln1_router",
    )(attn, ycg, sga, x2d, wab, wout, g1, b1, wr, br, tri)


def _moe_kernel(be_ref, we_ref, rb_ref, nused_ref, xa_ref, xb_ref, wgu_ref, bgu_ref, wd_ref, bd_ref,
                ya_ref, yb_ref, wgu_bf, wd_bf):
    i = pl.program_id(0)
    prev = be_ref[jnp.maximum(i - 1, 0)]
    fresh = jnp.logical_or(i == 0, be_ref[i] != prev)
    used = i < nused_ref[0]

    @pl.when(jnp.logical_and(fresh, used))
    def _():
        wgu_bf[...] = wgu_ref[...].astype(_BF)
        wd_bf[...] = wd_ref[...].astype(_BF)

    @pl.when(used)
    def _():
        a_lo, a_hi = _unpack_rows(xa_ref[...])
        b_lo, b_hi = _unpack_rows(xb_ref[...])
        xs = jnp.concatenate([a_lo, b_lo, a_hi, b_hi], axis=1).astype(_BF)
        gu = _dot(xs, wgu_bf[...]) + bgu_ref[...]
        gate = jnp.minimum(gu[:, :D_FF_EXPERT], SWIGLU_LIMIT)
        up = jnp.clip(gu[:, D_FF_EXPERT:], -SWIGLU_LIMIT, SWIGLU_LIMIT)
        hid = (up + 1.0) * (gate * jax.nn.sigmoid(SWIGLU_ALPHA * gate))
        yp = _pack_rows(_dot(hid.astype(_BF), wd_bf[...]) + bd_ref[...])
        ya_ref[...] = yp[:, :HALF]
        yb_ref[...] = yp[:, HALF:]


def _moe_call(block_e, weight_e, block_row, n_used, xa, xb, wgu, bgu, wd, bd):
    n_slots = xa.shape[0]
    bm = MOE_BM
    n_blocks = block_e.shape[0]
    grid_spec = pltpu.PrefetchScalarGridSpec(
        num_scalar_prefetch=4,
        grid=(n_blocks,),
        in_specs=[
            pl.BlockSpec((bm, HALF), lambda i, be, we, rb, nu: (rb[i], 0)),
            pl.BlockSpec((bm, HALF), lambda i, be, we, rb, nu: (rb[i], 0)),
            pl.BlockSpec((None, D_MODEL, 2 * D_FF_EXPERT), lambda i, be, we, rb, nu: (we[i], 0, 0)),
            pl.BlockSpec((None, 1, 2 * D_FF_EXPERT), lambda i, be, we, rb, nu: (be[i], 0, 0)),
            pl.BlockSpec((None, D_FF_EXPERT, D_MODEL), lambda i, be, we, rb, nu: (we[i], 0, 0)),
            pl.BlockSpec((None, 1, D_MODEL), lambda i, be, we, rb, nu: (be[i], 0, 0)),
        ],
        out_specs=[pl.BlockSpec((bm, HALF), lambda i, be, we, rb, nu: (rb[i], 0))] * 2,
        scratch_shapes=[pltpu.VMEM((D_MODEL, 2 * D_FF_EXPERT), _BF),
                        pltpu.VMEM((D_FF_EXPERT, D_MODEL), _BF)],
    )
    return pl.pallas_call(
        _moe_kernel,
        grid_spec=grid_spec,
        out_shape=[jax.ShapeDtypeStruct((n_slots, HALF), jnp.int32)] * 2,
        compiler_params=pltpu.CompilerParams(dimension_semantics=("arbitrary",),
                                             vmem_limit_bytes=VMEM_LIMIT),
        name="expert_ffn",
    )(block_e, weight_e, block_row, n_used, xa, xb, wgu, bgu, wd, bd)


def _combine_kernel(*refs):
    ya_refs, yb_refs = refs[:TOP_K], refs[TOP_K:2 * TOP_K]
    tw_ref, h1_ref, g2_ref, b2_ref, o_ref = refs[2 * TOP_K:]
    tw = tw_ref[...]
    parts = [jnp.zeros((tw.shape[0], HALF), _F32) for _ in range(4)]
    for k in range(TOP_K):
        a_lo, a_hi = _unpack_rows(ya_refs[k][...])
        b_lo, b_hi = _unpack_rows(yb_refs[k][...])
        w = tw[:, k:k + 1]
        parts = [p + w * y for p, y in zip(parts, (a_lo, b_lo, a_hi, b_hi))]
    ffn = jnp.concatenate(parts, axis=1)
    o_ref[...] = _layer_norm(DEEPNORM_ALPHA * h1_ref[...] + ffn, g2_ref[...], b2_ref[...])


def _combine_call(yga, ygb, tw, h1, g2, b2):
    t = h1.shape[0]
    tm = COMB_TM
    row = lambda w: pl.BlockSpec((tm, w), lambda i: (i, 0))
    krow = lambda k: pl.BlockSpec((tm, HALF), lambda i: (k * (t // tm) + i, 0))
    return pl.pallas_call(
        _combine_kernel,
        grid=(t // tm,),
        in_specs=[krow(k) for k in range(TOP_K)] * 2 + [row(128), row(D_MODEL),
                  _const_spec(g2.shape), _const_spec(b2.shape)],
        out_specs=row(D_MODEL),
        out_shape=jax.ShapeDtypeStruct((t, D_MODEL), _F32),
        compiler_params=pltpu.CompilerParams(dimension_semantics=("arbitrary",),
                                             vmem_limit_bytes=VMEM_LIMIT),
        name="combine_ln2",
    )(*([yga] * TOP_K + [ygb] * TOP_K), tw, h1, g2, b2)


def _prepare_weights(w_in, w_uq):
    o_q = 3 * CONV_DIM
    o_kv = o_q + Q_LORA_RANK
    o_pe = o_kv + KV_LORA_RANK
    o_gc = o_pe + QK_ROPE_DIM
    wg3 = w_in[:, :o_q]
    kpe = w_in[:, o_pe:o_gc]
    zpad = jnp.zeros((D_MODEL, ROPE_PAD - QK_ROPE_DIM), w_in.dtype)
    wlat = jnp.concatenate([w_in[:, o_q:o_pe], kpe, zpad], axis=1)
    wgates = w_in[:, o_gc:]
    wq = w_uq.reshape(Q_LORA_RANK, N_HEADS, QK_HEAD_DIM)
    wuqa = jnp.pad(wq, ((0, 0), (0, 0), (0, QK_PAD - QK_HEAD_DIM)))
    return (wg3.astype(_BF), wlat.astype(_BF), wgates.astype(_BF),
            wuqa.reshape(Q_LORA_RANK, N_HEADS * QK_PAD).astype(_BF))


def _rope_rows():
    half = QK_ROPE_DIM // 2
    inv_freq = ROPE_BASE ** (-jnp.arange(0, QK_ROPE_DIM, 2, dtype=_F32) / QK_ROPE_DIM)
    z = jnp.zeros((ROPE_PAD - QK_ROPE_DIM,), _F32)
    ones = jnp.ones((half,), _F32)
    rows = jnp.stack([jnp.concatenate([inv_freq, inv_freq, z]),
                      jnp.concatenate([ones, ones, z]),
                      jnp.concatenate([-ones, ones, z])])
    return jnp.pad(rows, ((0, 8 - rows.shape[0]), (0, 0)))


def _block_tables(counts, n_tok):
    bm = MOE_BM
    n_blocks = n_tok * TOP_K // bm + N_EXPERTS
    nblk = (counts + bm - 1) // bm
    end = jnp.cumsum(nblk)
    start = end - nblk
    n_used = end[-1]
    i = jnp.minimum(jnp.arange(n_blocks, dtype=jnp.int32), n_used - 1)
    block_e = (end[None, :] <= i[:, None]).sum(axis=1).astype(jnp.int32)
    first = jnp.sum(jnp.where(jnp.arange(N_EXPERTS)[None, :] == block_e[:, None], start[None, :], 0), axis=1)
    block_row = block_e * (n_tok // bm) + (i - first)
    fresh = jnp.concatenate([jnp.ones((1,), jnp.bool_), block_e[1:] != block_e[:-1]])
    later = jnp.where(block_e[None, :] > block_e[:, None], block_e[None, :], N_EXPERTS)
    nxt = jnp.min(later, axis=1)
    weight_e = jnp.where(jnp.logical_or(fresh, nxt == N_EXPERTS), block_e, nxt).astype(jnp.int32)
    return block_e, weight_e, block_row.astype(jnp.int32), n_used.astype(jnp.int32).reshape(1)


def kernel(x, positions, w_in, conv_w, q_norm_g, w_uq, kv_norm_g, w_uk, w_uv, w_conv_branch,
           w_attn_branch, w_out, ln1_g, ln1_b, w_router, b_router, w_gate_up, b_gate_up, w_down,
           b_down, ln2_g, ln2_b):
    b, s, d = x.shape
    t = b * s
    x2d = x.reshape(t, d)
    pos = positions.reshape(t, 1).astype(jnp.int32)
    rope = _rope_rows()
    for l in range(DEPTH):
        wg3, wlat, wgates, wuqa = _prepare_weights(w_in[l], w_uq[l])
        q, kn, kpe, v, ycg, sga = _proj_call(
            x2d, pos, rope, wg3, wlat, wgates, conv_w[l], q_norm_g[l][None, :], kv_norm_g[l][None, :],
            wuqa, w_uk[l].astype(_BF), w_uv[l].astype(_BF), w_conv_branch[l].astype(_BF), s)
        attn = _attn_call(q, kn, kpe.reshape(b, s, -1), v)
        wr_t = w_router[l].T
        wr_hi = wr_t.astype(_BF)
        wr_lo = (wr_t - wr_hi.astype(_F32)).astype(_BF)
        h1, h1pa, h1pb, slots, tw_pad, counts = _post_call(
            attn, ycg, sga, x2d, w_attn_branch[l].astype(_BF), w_out[l].astype(_BF),
            ln1_g[l][None, :], ln1_b[l][None, :], jnp.concatenate([wr_hi, wr_lo], axis=0),
            jnp.broadcast_to(b_router[l][:, None], (N_EXPERTS, 128)))
        block_e, weight_e, block_row, n_used = _block_tables(counts[:, 0].astype(jnp.int32), t)
        xa, xb = _sc_scatter_rows([h1pa, h1pb], slots, TOP_K, N_EXPERTS * t)
        ya, yb = _moe_call(block_e, weight_e, block_row, n_used, xa, xb, w_gate_up[l], b_gate_up[l][:, None, :],
                           w_down[l], b_down[l][:, None, :])
        yga, ygb = _sc_gather_rows([ya, yb], slots, TOP_K)
        x2d = _combine_call(yga, ygb, tw_pad, h1, ln2_g[l][None, :], ln2_b[l][None, :])
    return x2d.reshape(b, s, d)
```

```python
import functools

import numpy as np
import jax
import jax.numpy as jnp
from jax import lax
from jax.experimental import pallas as pl
from jax.experimental.pallas import tpu as pltpu
from jax.experimental.pallas import tpu_sc as plsc

D_MODEL = 1024
CHUNK = 64
CONV_DIM = D_MODEL
CONV_WIDTH = 3
N_HEADS = 8
QK_NOPE_DIM = 128
QK_ROPE_DIM = 64
V_HEAD_DIM = 128
QK_HEAD_DIM = QK_NOPE_DIM + QK_ROPE_DIM
Q_LORA_RANK = 384
KV_LORA_RANK = 256
ROPE_BASE = 10000.0
N_EXPERTS = 32
TOP_K = 4
D_FF_EXPERT = D_MODEL
SWIGLU_LIMIT = 7.0
SWIGLU_ALPHA = 1.702
LN_EPS = 1e-5
RMS_EPS = 1e-6
DEPTH = 1
DEEPNORM_ALPHA = (2 * DEPTH) ** 0.25

QK_PAD = 256
ROPE_PAD = 128

PROJ_TM = 512
ATTN_TQ = 512
ATTN_TK = 512
ONES_ROWS = 16
POST_TM = 512
MOE_BM = 512
COMB_TM = 512
SC_WINDOW = 128
PACKED = D_MODEL // 2
HALF = PACKED // 2
HALO = 8
NEG_BIG = -1e30
LOG2E = 1.4426950408889634
VMEM_LIMIT = 56 * 1024 * 1024

_BF = jnp.bfloat16
_F32 = jnp.float32


def _dot(a, b):
    return jnp.dot(a, b, preferred_element_type=_F32)


def _const_spec(shape):
    nd = len(shape)
    return pl.BlockSpec(shape, lambda *_: (0,) * nd, pipeline_mode=pl.Buffered(1))


def _pack_rows(a):
    half = a.shape[1] // 2
    lo = lax.bitcast_convert_type(a[:, :half].astype(_BF).astype(_F32), jnp.int32)
    hi = lax.bitcast_convert_type(a[:, half:].astype(_BF).astype(_F32), jnp.int32)
    return lax.shift_right_logical(lo, 16) | (hi & jnp.int32(-65536))


def _unpack_rows(p):
    lo = lax.bitcast_convert_type(lax.shift_left(p, 16), _F32)
    hi = lax.bitcast_convert_type(p & jnp.int32(-65536), _F32)
    return lo, hi


def _sc_mesh():
    return plsc.VectorSubcoreMesh(core_axis_name="core", subcore_axis_name="subcore")


def _sc_scatter_rows(tables, idx, n_copies, n_out):
    n_rows, width = tables[0].shape
    n_tab = len(tables)
    window = SC_WINDOW

    @functools.partial(pl.kernel, mesh=_sc_mesh(),
                       out_type=[jax.ShapeDtypeStruct((n_out, width), tables[0].dtype)] * n_tab)
    def scatter_kernel(*refs):
        idx_hbm = refs[n_tab]
        for src_hbm, out_hbm in zip(refs[:n_tab], refs[n_tab + 1:]):
            def body(src_vmem, idx_vmem, out_hbm=out_hbm):
                for k in range(n_copies):
                    pltpu.sync_copy(src_vmem, out_hbm.at[idx_vmem.at[k]])

            pltpu.emit_pipeline(
                body,
                grid=(n_rows // window,),
                in_specs=[pl.BlockSpec((window, width), lambda i: (i, 0)),
                          pl.BlockSpec((idx.shape[0], window), lambda i: (0, i))],
                out_specs=[],
                core_axis_name=("core", "subcore"),
                dimension_semantics=(pltpu.PARALLEL,),
            )(src_hbm, idx_hbm)

    return scatter_kernel(*tables, idx)


def _sc_gather_rows(tables, idx, n_copies):
    n_rows = idx.shape[1]
    width = tables[0].shape[1]
    n_tab = len(tables)
    window = SC_WINDOW
    steps = n_rows // window

    @functools.partial(pl.kernel, mesh=_sc_mesh(),
                       out_type=[jax.ShapeDtypeStruct((n_copies * n_rows, width), tables[0].dtype)] * n_tab)
    def gather_kernel(*refs):
        idx_hbm = refs[n_tab]
        for table_hbm, out_hbm in zip(refs[:n_tab], refs[n_tab + 1:]):
            def body(idx_vmem, out_vmem, table_hbm=table_hbm):
                pltpu.sync_copy(table_hbm.at[idx_vmem.at[0]], out_vmem)

            pltpu.emit_pipeline(
                body,
                grid=(n_copies, steps),
                in_specs=[pl.BlockSpec((1, window), lambda k, i: (k, i))],
                out_specs=[pl.BlockSpec((window, width), lambda k, i: (k * steps + i, 0))],
                core_axis_name=("core", "subcore"),
                dimension_semantics=(pltpu.PARALLEL, pltpu.PARALLEL),
            )(idx_hbm, out_hbm)

    return gather_kernel(*tables, idx)


def _proj_kernel(tiles_per_seq,
                 x_ref, pos_ref, rope_ref, wg3_ref, wlat_ref, wgates_ref, convw_ref,
                 qg_ref, kvg_ref, wuqa_ref, wuk_ref, wuv_ref, wcb_ref,
                 q_ref, kn_ref, kpe_ref, v_ref, ycg_ref, sga_ref, ubuf):
    i = pl.program_id(0)
    tm = x_ref.shape[0]
    xb = x_ref[...].astype(_BF)

    @pl.when(i % tiles_per_seq == 0)
    def _():
        ubuf[0:HALO, :] = jnp.zeros((HALO, CONV_DIM), _F32)

    g3 = _dot(xb, wg3_ref[...])
    u = g3[:, CONV_DIM:2 * CONV_DIM] * g3[:, 2 * CONV_DIM:]
    ubuf[HALO:HALO + tm, :] = u
    cw = convw_ref[...]
    conv = (cw[2:3, :] * u + cw[1:2, :] * ubuf[HALO - 1:HALO - 1 + tm, :]
            + cw[0:1, :] * ubuf[HALO - 2:HALO - 2 + tm, :])
    ubuf[0:HALO, :] = ubuf[tm:tm + HALO, :]
    yc = _dot((g3[:, :CONV_DIM] * conv).astype(_BF), wcb_ref[...])
    gates = _dot(xb, wgates_ref[...])
    ycg_ref[...] = (jax.nn.sigmoid(gates[:, :D_MODEL]) * yc).astype(_BF)
    sga_ref[...] = jax.nn.sigmoid(gates[:, D_MODEL:]).astype(_BF)

    lat = _dot(xb, wlat_ref[...])
    trig = jnp.sin(pos_ref[...].astype(_F32) * rope_ref[0:1, :] + rope_ref[3:4, :])
    cosb = trig * rope_ref[1:2, :]
    sinb = pltpu.roll(trig, QK_ROPE_DIM, axis=1) * rope_ref[2:3, :]
    q_lat = lat[:, :Q_LORA_RANK]
    rq = q_lat * lax.rsqrt(jnp.mean(q_lat * q_lat, axis=-1, keepdims=True) + RMS_EPS) * qg_ref[...]
    kv_lat = lat[:, Q_LORA_RANK:Q_LORA_RANK + KV_LORA_RANK]
    ckv = (kv_lat * lax.rsqrt(jnp.mean(kv_lat * kv_lat, axis=-1, keepdims=True) + RMS_EPS)
           * kvg_ref[...]).astype(_BF)
    o = Q_LORA_RANK + KV_LORA_RANK
    lane = lax.broadcasted_iota(jnp.int32, (tm, ROPE_PAD), 1)
    half = QK_ROPE_DIM // 2

    def swap_halves(u):
        return jnp.where(lane < half, pltpu.roll(u, ROPE_PAD - half, axis=1), pltpu.roll(u, half, axis=1))

    k_pe = lat[:, o:o + ROPE_PAD]
    kpe_ref[...] = (k_pe * cosb + swap_halves(k_pe) * sinb).astype(_BF)
    kn = _dot(ckv, wuk_ref[...]).astype(_BF)
    vv = _dot(ckv, wuv_ref[...]).astype(_BF)
    for h in range(N_HEADS):
        kn_ref[h] = kn[:, h * QK_NOPE_DIM:(h + 1) * QK_NOPE_DIM]
        v_ref[h] = vv[:, h * V_HEAD_DIM:(h + 1) * V_HEAD_DIM]

    rqb = rq.astype(_BF)
    qa = _dot(rqb, wuqa_ref[...])
    scale = QK_HEAD_DIM ** -0.5 * LOG2E
    for h in range(N_HEADS):
        lo = h * QK_PAD
        q_ref[h, :, :QK_NOPE_DIM] = (qa[:, lo:lo + QK_NOPE_DIM] * scale).astype(_BF)
        q_pe = qa[:, lo + QK_NOPE_DIM:lo + QK_PAD]
        hi = q_pe * cosb + swap_halves(q_pe) * sinb
        q_ref[h, :, QK_NOPE_DIM:] = (hi * scale).astype(_BF)


def _proj_call(x2d, pos, rope, wg3, wlat, wgates, convw, qg, kvg, wuqa, wuk, wuv, wcb, seq):
    t = x2d.shape[0]
    tm = PROJ_TM
    tps = seq // tm
    row = lambda w: pl.BlockSpec((tm, w), lambda i: (i, 0))
    heads = lambda w: pl.BlockSpec((None, N_HEADS, tm, w), lambda i: (i // tps, 0, i % tps, 0))
    paired = pl.BlockSpec((None, N_HEADS, tm, QK_PAD),
                          lambda i: (i // tps, 0, _paired_tile_pos(i % tps, tps), 0))
    head_major = lambda w: jax.ShapeDtypeStruct((t // seq, N_HEADS, seq, w), _BF)
    outs = [
        head_major(QK_PAD),
        head_major(QK_NOPE_DIM),
        jax.ShapeDtypeStruct((t, ROPE_PAD), _BF),
        head_major(V_HEAD_DIM),
        jax.ShapeDtypeStruct((t, D_MODEL), _BF),
        jax.ShapeDtypeStruct((t, D_MODEL), _BF),
    ]
    return pl.pallas_call(
        functools.partial(_proj_kernel, tps),
        grid=(t // tm,),
        in_specs=[row(D_MODEL), row(1)]
        + [_const_spec(a.shape)
           for a in (rope, wg3, wlat, wgates, convw, qg, kvg, wuqa, wuk, wuv, wcb)],
        out_specs=[paired, heads(QK_NOPE_DIM), row(ROPE_PAD), heads(V_HEAD_DIM),
                   row(D_MODEL), row(D_MODEL)],
        out_shape=outs,
        scratch_shapes=[pltpu.VMEM((tm + HALO, CONV_DIM), _F32)],
        compiler_params=pltpu.CompilerParams(dimension_semantics=("arbitrary",),
                                             vmem_limit_bytes=VMEM_LIMIT),
        name="proj_conv_qkv",
    )(x2d, pos, rope, wg3, wlat, wgates, convw, qg, kvg, wuqa, wuk, wuv, wcb)


def _paired_tile_pos(tile, n_tiles):
    return jnp.where(tile < n_tiles // 2, 2 * tile, 2 * (n_tiles - 1 - tile) + 1)


def _attn_kernel(n_tiles, q_ref, kn_ref, kpe_ref, v_ref, o_ref, kfull, vt, s0, s1, p0, p1, a0, a1,
                 m_ref, acc):
    pr = pl.program_id(2)
    tq = q_ref.shape[0] // 2
    tk = ATTN_TK
    s_bufs, p_bufs, a_bufs = (s0, s1), (p0, p1), (a0, a1)
    n_stages = n_tiles + 1

    @pl.when(pr == 0)
    def _():
        kfull[:, :QK_NOPE_DIM] = kn_ref[...]
        kfull[:, QK_NOPE_DIM:] = kpe_ref[...]
        vt[:V_HEAD_DIM, :] = v_ref[...].T
        vt[V_HEAD_DIM:, :] = jnp.ones((vt.shape[0] - V_HEAD_DIM, vt.shape[1]), _BF)

    def sel_of(t):
        return (t > pr).astype(jnp.int32)

    def block_of(t):
        return jnp.where(t <= pr, t, t - pr - 1)

    def scores_t(t):
        k = kfull[pl.ds(pl.multiple_of(block_of(t) * tk, tk), tk), :]
        q = q_ref[pl.ds(pl.multiple_of(sel_of(t) * tq, tq), tq), :]
        s_bufs[t % 2][...] = lax.dot_general(k, q, (((1,), (1,)), ((), ())),
                                             preferred_element_type=_F32)

    key_chunk = lax.broadcasted_iota(jnp.int32, (tk, tq), 0) // CHUNK
    qry_chunk = lax.broadcasted_iota(jnp.int32, (tk, tq), 1) // CHUNK
    causal = key_chunk <= qry_chunk

    def softmax_t(t):
        par = t % 2
        s_t = s_bufs[par][...]
        if t == n_stages - 1:
            s_t = jnp.where(causal, s_t, NEG_BIG)
        elif t < n_tiles // 2:
            s_t = jnp.where(jnp.logical_or(causal, pr != t), s_t, NEG_BIG)
        m_old = m_ref[sel_of(t)]
        m_new = jnp.maximum(m_old, jnp.max(s_t, axis=0, keepdims=True))
        m_ref[sel_of(t)] = m_new
        a_bufs[par][...] = jnp.exp2(m_old - m_new)
        p_bufs[par][...] = jnp.exp2(s_t - m_new).astype(_BF)

    def accumulate(t):
        par = t % 2
        v_t = vt[:, pl.ds(pl.multiple_of(block_of(t) * tk, tk), tk)]
        sel = sel_of(t)
        acc[sel] = a_bufs[par][...] * acc[sel] + _dot(v_t, p_bufs[par][...])

    m_ref[...] = jnp.full(m_ref.shape, NEG_BIG, _F32)
    acc[...] = jnp.zeros(acc.shape, _F32)
    scores_t(0)
    for t in range(n_stages):
        if t + 1 < n_stages:
            scores_t(t + 1)
        if t > 0:
            accumulate(t - 1)
        softmax_t(t)
    accumulate(n_stages - 1)
    for half in range(2):
        out_t = acc[half, :V_HEAD_DIM, :] / acc[half, V_HEAD_DIM:V_HEAD_DIM + 1, :]
        o_ref[half * tq:(half + 1) * tq, :] = out_t.T.astype(_BF)


def _attn_call(q, kn, kpe, v):
    b, _, s, _ = q.shape
    tq = ATTN_TQ
    n_tiles = s // tq
    return pl.pallas_call(
        functools.partial(_attn_kernel, n_tiles),
        grid=(b, N_HEADS, n_tiles // 2),
        in_specs=[
            pl.BlockSpec((None, None, 2 * tq, QK_PAD), lambda bi, h, i: (bi, h, i, 0)),
            pl.BlockSpec((None, None, s, QK_NOPE_DIM), lambda bi, h, i: (bi, h, 0, 0)),
            pl.BlockSpec((None, s, ROPE_PAD), lambda bi, h, i: (bi, 0, 0)),
            pl.BlockSpec((None, None, s, V_HEAD_DIM), lambda bi, h, i: (bi, h, 0, 0)),
        ],
        out_specs=pl.BlockSpec((None, None, 2 * tq, V_HEAD_DIM), lambda bi, h, i: (bi, h, i, 0)),
        out_shape=jax.ShapeDtypeStruct((b, N_HEADS, s, V_HEAD_DIM), _BF),
        scratch_shapes=[pltpu.VMEM((s, QK_PAD), _BF), pltpu.VMEM((V_HEAD_DIM + ONES_ROWS, s), _BF),
                        pltpu.VMEM((ATTN_TK, tq), _F32), pltpu.VMEM((ATTN_TK, tq), _F32),
                        pltpu.VMEM((ATTN_TK, tq), _BF), pltpu.VMEM((ATTN_TK, tq), _BF),
                        pltpu.VMEM((1, tq), _F32), pltpu.VMEM((1, tq), _F32),
                        pltpu.VMEM((2, 1, tq), _F32),
                        pltpu.VMEM((2, V_HEAD_DIM + ONES_ROWS, tq), _F32)],
        compiler_params=pltpu.CompilerParams(
            dimension_semantics=("arbitrary", "arbitrary", "arbitrary"),
            vmem_limit_bytes=VMEM_LIMIT),
        name="chunk_causal_attention",
    )(q, kn, kpe, v)


def _layer_norm(z, g, b):
    mu = jnp.mean(z, axis=-1, keepdims=True)
    zc = z - mu
    var = jnp.mean(zc * zc, axis=-1, keepdims=True)
    return zc * lax.rsqrt(var + LN_EPS) * g + b


def _post_kernel(expert_cap, attn_ref, ycg_ref, sga_ref, x_ref, wab_ref, wout_ref, g1_ref, b1_ref,
                 wr_ref, br_ref, tri_ref, h1_ref, h1pa_ref, h1pb_ref, slot_ref, tw_ref, cnt_ref, run):
    tm = x_ref.shape[0]

    @pl.when(pl.program_id(0) == 0)
    def _():
        run[...] = jnp.zeros(run.shape, _F32)

    attn = jnp.concatenate([attn_ref[h] for h in range(N_HEADS)], axis=1)
    y_attn = _dot(attn, wab_ref[...])
    merged = ycg_ref[...].astype(_F32) + sga_ref[...].astype(_F32) * y_attn
    mix = _dot(merged.astype(_BF), wout_ref[...])
    h1 = _layer_norm(DEEPNORM_ALPHA * x_ref[...] + mix, g1_ref[...], b1_ref[...])
    h1_ref[...] = h1
    h1p = _pack_rows(h1)
    h1pa_ref[...] = h1p[:, :HALF]
    h1pb_ref[...] = h1p[:, HALF:]

    nt = (((1,), (1,)), ((), ()))
    h_hi = h1.astype(_BF)
    h_lo = (h1 - h_hi.astype(_F32)).astype(_BF)
    both = lax.dot_general(wr_ref[...], h_hi, nt, preferred_element_type=_F32)
    logits = (both[:N_EXPERTS] + both[N_EXPERTS:] + br_ref[:, 0:1]
              + lax.dot_general(wr_ref[:N_EXPERTS, :], h_lo, nt, preferred_element_type=_F32))
    expert = lax.broadcasted_iota(jnp.int32, (N_EXPERTS, tm), 0)
    work = logits
    chosen = jnp.zeros((N_EXPERTS, tm), jnp.bool_)
    sels, exps = [], []
    for k in range(TOP_K):
        mx = jnp.max(work, axis=0, keepdims=True)
        sel = jnp.min(jnp.where(work == mx, expert, N_EXPERTS), axis=0, keepdims=True)
        hit = expert == sel
        sels.append(sel)
        exps.append(jnp.exp(mx - (mx if k == 0 else top0)))
        if k == 0:
            top0 = mx
        chosen = jnp.logical_or(chosen, hit)
        work = jnp.where(hit, -jnp.inf, work)
    denom = exps[0] + exps[1] + exps[2] + exps[3]

    onehot = chosen.astype(_F32)
    before = _dot(onehot.astype(_BF), tri_ref[...]) + run[:, 0:1]
    rows = []
    for k in range(TOP_K):
        rank = jnp.sum(jnp.where(expert == sels[k], before, 0.0), axis=0, keepdims=True)
        rows.append(sels[k] * expert_cap + rank.astype(jnp.int32))
    slot_ref[...] = jnp.concatenate(rows + [jnp.zeros((8 - TOP_K, tm), jnp.int32)], axis=0)
    run[...] = run[...] + jnp.sum(onehot, axis=1, keepdims=True)
    cnt_ref[...] = run[...]
    tw_t = jnp.concatenate([e / denom for e in exps] + [jnp.zeros((128 - TOP_K, tm), _F32)], axis=0)
    tw_ref[...] = tw_t.T


def _post_call(attn, ycg, sga, x2d, wab, wout, g1, b1, wr, br):
    t = x2d.shape[0]
    tm = POST_TM
    row = lambda w: pl.BlockSpec((tm, w), lambda i: (i, 0))
    tps = attn.shape[2] // tm
    tri = (jnp.arange(tm)[:, None] < jnp.arange(tm)[None, :]).astype(_BF)
    outs = [
        jax.ShapeDtypeStruct((t, D_MODEL), _F32),
        jax.ShapeDtypeStruct((t, HALF), jnp.int32),
        jax.ShapeDtypeStruct((t, HALF), jnp.int32),
        jax.ShapeDtypeStruct((8, t), jnp.int32),
        jax.ShapeDtypeStruct((t, 128), _F32),
        jax.ShapeDtypeStruct((N_EXPERTS, 128), _F32),
    ]
    out_specs = [row(D_MODEL), row(HALF), row(HALF), pl.BlockSpec((8, tm), lambda i: (0, i)),
                 row(128), pl.BlockSpec((N_EXPERTS, 128), lambda i: (0, 0))]
    return pl.pallas_call(
        functools.partial(_post_kernel, t),
        grid=(t // tm,),
        in_specs=[pl.BlockSpec((None, N_HEADS, tm, V_HEAD_DIM),
                               lambda i: (i // tps, 0, _paired_tile_pos(i % tps, tps), 0))]
        + [row(D_MODEL)] * 3
        + [_const_spec(a.shape) for a in (wab, wout, g1, b1, wr, br, tri)],
        out_specs=out_specs,
        out_shape=outs,
        scratch_shapes=[pltpu.VMEM((N_EXPERTS, 128), _F32)],
        compiler_params=pltpu.CompilerParams(dimension_semantics=("arbitrary",),
                                             vmem_limit_bytes=VMEM_LIMIT),
        name="merge_out_ln1_router",
    )(attn, ycg, sga, x2d, wab, wout, g1, b1, wr, br, tri)


def _moe_kernel(be_ref, we_ref, rb_ref, bv_ref, nused_ref, xa_ref, xb_ref, wgu_ref, bgu_ref, wd_ref, bd_ref,
                ya_ref, yb_ref, wgu_bf, wd_bf):
    i = pl.program_id(0)
    prev = be_ref[jnp.maximum(i - 1, 0)]
    fresh = jnp.logical_or(i == 0, be_ref[i] != prev)
    used = i < nused_ref[0]

    @pl.when(jnp.logical_and(fresh, used))
    def _():
        wgu_bf[...] = wgu_ref[...].astype(_BF)
        wd_bf[...] = wd_ref[...].astype(_BF)

    def ffn(rows):
        a_lo, a_hi = _unpack_rows(xa_ref[0:rows, :])
        b_lo, b_hi = _unpack_rows(xb_ref[0:rows, :])
        xs = jnp.concatenate([a_lo, b_lo, a_hi, b_hi], axis=1).astype(_BF)
        gu = _dot(xs, wgu_bf[...]) + bgu_ref[...]
        gate = jnp.minimum(gu[:, :D_FF_EXPERT], SWIGLU_LIMIT)
        up = jnp.clip(gu[:, D_FF_EXPERT:], -SWIGLU_LIMIT, SWIGLU_LIMIT)
        hid = (up + 1.0) * (gate * jax.nn.sigmoid(SWIGLU_ALPHA * gate))
        yp = _pack_rows(_dot(hid.astype(_BF), wd_bf[...]) + bd_ref[...])
        ya_ref[0:rows, :] = yp[:, :HALF]
        yb_ref[0:rows, :] = yp[:, HALF:]

    bm = xa_ref.shape[0]
    small = bv_ref[i] <= bm // 2

    @pl.when(jnp.logical_and(used, jnp.logical_not(small)))
    def _():
        ffn(bm)

    @pl.when(jnp.logical_and(used, small))
    def _():
        ffn(bm // 2)


def _moe_call(block_e, weight_e, block_row, block_valid, n_used, xa, xb, wgu, bgu, wd, bd):
    n_slots = xa.shape[0]
    bm = MOE_BM
    n_blocks = block_e.shape[0]
    grid_spec = pltpu.PrefetchScalarGridSpec(
        num_scalar_prefetch=5,
        grid=(n_blocks,),
        in_specs=[
            pl.BlockSpec((bm, HALF), lambda i, be, we, rb, bv, nu: (rb[i], 0)),
            pl.BlockSpec((bm, HALF), lambda i, be, we, rb, bv, nu: (rb[i], 0)),
            pl.BlockSpec((None, D_MODEL, 2 * D_FF_EXPERT), lambda i, be, we, rb, bv, nu: (we[i], 0, 0)),
            pl.BlockSpec((None, 1, 2 * D_FF_EXPERT), lambda i, be, we, rb, bv, nu: (be[i], 0, 0)),
            pl.BlockSpec((None, D_FF_EXPERT, D_MODEL), lambda i, be, we, rb, bv, nu: (we[i], 0, 0)),
            pl.BlockSpec((None, 1, D_MODEL), lambda i, be, we, rb, bv, nu: (be[i], 0, 0)),
        ],
        out_specs=[pl.BlockSpec((bm, HALF), lambda i, be, we, rb, bv, nu: (rb[i], 0))] * 2,
        scratch_shapes=[pltpu.VMEM((D_MODEL, 2 * D_FF_EXPERT), _BF),
                        pltpu.VMEM((D_FF_EXPERT, D_MODEL), _BF)],
    )
    return pl.pallas_call(
        _moe_kernel,
        grid_spec=grid_spec,
        out_shape=[jax.ShapeDtypeStruct((n_slots, HALF), jnp.int32)] * 2,
        compiler_params=pltpu.CompilerParams(dimension_semantics=("arbitrary",),
                                             vmem_limit_bytes=VMEM_LIMIT),
        name="expert_ffn",
    )(block_e, weight_e, block_row, block_valid, n_used, xa, xb, wgu, bgu, wd, bd)


def _combine_kernel(*refs):
    ya_refs, yb_refs = refs[:TOP_K], refs[TOP_K:2 * TOP_K]
    tw_ref, h1_ref, g2_ref, b2_ref, o_ref = refs[2 * TOP_K:]
    tw = tw_ref[...]
    parts = [jnp.zeros((tw.shape[0], HALF), _F32) for _ in range(4)]
    for k in range(TOP_K):
        a_lo, a_hi = _unpack_rows(ya_refs[k][...])
        b_lo, b_hi = _unpack_rows(yb_refs[k][...])
        w = tw[:, k:k + 1]
        parts = [p + w * y for p, y in zip(parts, (a_lo, b_lo, a_hi, b_hi))]
    ffn = jnp.concatenate(parts, axis=1)
    o_ref[...] = _layer_norm(DEEPNORM_ALPHA * h1_ref[...] + ffn, g2_ref[...], b2_ref[...])


def _combine_call(yga, ygb, tw, h1, g2, b2):
    t = h1.shape[0]
    tm = COMB_TM
    row = lambda w: pl.BlockSpec((tm, w), lambda i: (i, 0))
    krow = lambda k: pl.BlockSpec((tm, HALF), lambda i: (k * (t // tm) + i, 0))
    return pl.pallas_call(
        _combine_kernel,
        grid=(t // tm,),
        in_specs=[krow(k) for k in range(TOP_K)] * 2 + [row(128), row(D_MODEL),
                  _const_spec(g2.shape), _const_spec(b2.shape)],
        out_specs=row(D_MODEL),
        out_shape=jax.ShapeDtypeStruct((t, D_MODEL), _F32),
        compiler_params=pltpu.CompilerParams(dimension_semantics=("arbitrary",),
                                             vmem_limit_bytes=VMEM_LIMIT),
        name="combine_ln2",
    )(*([yga] * TOP_K + [ygb] * TOP_K), tw, h1, g2, b2)


def _prepare_weights(w_in, w_uq):
    o_q = 3 * CONV_DIM
    o_kv = o_q + Q_LORA_RANK
    o_pe = o_kv + KV_LORA_RANK
    o_gc = o_pe + QK_ROPE_DIM
    wg3 = w_in[:, :o_q]
    kpe = w_in[:, o_pe:o_gc]
    zpad = jnp.zeros((D_MODEL, ROPE_PAD - QK_ROPE_DIM), w_in.dtype)
    wlat = jnp.concatenate([w_in[:, o_q:o_pe], kpe, zpad], axis=1)
    wgates = w_in[:, o_gc:]
    wq = w_uq.reshape(Q_LORA_RANK, N_HEADS, QK_HEAD_DIM)
    wuqa = jnp.pad(wq, ((0, 0), (0, 0), (0, QK_PAD - QK_HEAD_DIM)))
    return (wg3.astype(_BF), wlat.astype(_BF), wgates.astype(_BF),
            wuqa.reshape(Q_LORA_RANK, N_HEADS * QK_PAD).astype(_BF))


def _rope_rows():
    half = QK_ROPE_DIM // 2
    inv_freq = ROPE_BASE ** (-jnp.arange(0, QK_ROPE_DIM, 2, dtype=_F32) / QK_ROPE_DIM)
    z = jnp.zeros((ROPE_PAD - QK_ROPE_DIM,), _F32)
    ones = jnp.ones((half,), _F32)
    rows = jnp.stack([jnp.concatenate([inv_freq] * 4),
                      jnp.concatenate([ones, ones, z]),
                      jnp.concatenate([-ones, ones, z]),
                      jnp.concatenate([ones * (np.pi / 2), ones * (np.pi / 2), z])])
    return jnp.pad(rows, ((0, 8 - rows.shape[0]), (0, 0)))


def _block_tables(counts, n_tok):
    bm = MOE_BM
    n_blocks = n_tok * TOP_K // bm + N_EXPERTS
    nblk = (counts + bm - 1) // bm
    end = jnp.cumsum(nblk)
    start = end - nblk
    n_used = end[-1]
    i = jnp.minimum(jnp.arange(n_blocks, dtype=jnp.int32), n_used - 1)
    block_e = (end[None, :] <= i[:, None]).sum(axis=1).astype(jnp.int32)
    first = jnp.sum(jnp.where(jnp.arange(N_EXPERTS)[None, :] == block_e[:, None], start[None, :], 0), axis=1)
    block_row = block_e * (n_tok // bm) + (i - first)
    fresh = jnp.concatenate([jnp.ones((1,), jnp.bool_), block_e[1:] != block_e[:-1]])
    later = jnp.where(block_e[None, :] > block_e[:, None], block_e[None, :], N_EXPERTS)
    nxt = jnp.min(later, axis=1)
    weight_e = jnp.where(jnp.logical_or(fresh, nxt == N_EXPERTS), block_e, nxt).astype(jnp.int32)
    cnt = jnp.sum(jnp.where(jnp.arange(N_EXPERTS)[None, :] == block_e[:, None], counts[None, :], 0), axis=1)
    block_valid = jnp.clip(cnt - (i - first) * bm, 0, bm).astype(jnp.int32)
    return (block_e, weight_e, block_row.astype(jnp.int32), block_valid,
            n_used.astype(jnp.int32).reshape(1))


def kernel(x, positions, w_in, conv_w, q_norm_g, w_uq, kv_norm_g, w_uk, w_uv, w_conv_branch,
           w_attn_branch, w_out, ln1_g, ln1_b, w_router, b_router, w_gate_up, b_gate_up, w_down,
           b_down, ln2_g, ln2_b):
    b, s, d = x.shape
    t = b * s
    x2d = x.reshape(t, d)
    pos = positions.reshape(t, 1).astype(jnp.int32)
    rope = _rope_rows()
    for l in range(DEPTH):
        wg3, wlat, wgates, wuqa = _prepare_weights(w_in[l], w_uq[l])
        q, kn, kpe, v, ycg, sga = _proj_call(
            x2d, pos, rope, wg3, wlat, wgates, conv_w[l], q_norm_g[l][None, :], kv_norm_g[l][None, :],
            wuqa, w_uk[l].astype(_BF), w_uv[l].astype(_BF), w_conv_branch[l].astype(_BF), s)
        attn = _attn_call(q, kn, kpe.reshape(b, s, -1), v)
        wr_t = w_router[l].T
        wr_hi = wr_t.astype(_BF)
        wr_lo = (wr_t - wr_hi.astype(_F32)).astype(_BF)
        h1, h1pa, h1pb, slots, tw_pad, counts = _post_call(
            attn, ycg, sga, x2d, w_attn_branch[l].astype(_BF), w_out[l].astype(_BF),
            ln1_g[l][None, :], ln1_b[l][None, :], jnp.concatenate([wr_hi, wr_lo], axis=0),
            jnp.broadcast_to(b_router[l][:, None], (N_EXPERTS, 128)))
        block_e, weight_e, block_row, block_valid, n_used = _block_tables(counts[:, 0].astype(jnp.int32), t)
        xa, xb = _sc_scatter_rows([h1pa, h1pb], slots, TOP_K, N_EXPERTS * t)
        ya, yb = _moe_call(block_e, weight_e, block_row, block_valid, n_used, xa, xb, w_gate_up[l], b_gate_up[l][:, None, :],
                           w_down[l], b_down[l][:, None, :])
        yga, ygb = _sc_gather_rows([ya, yb], slots, TOP_K)
        x2d = _combine_call(yga, ygb, tw_pad, h1, ln2_g[l][None, :], ln2_b[l][None, :])
    return x2d.reshape(b, s, d)
```

```python
import functools

import numpy as np
import jax
import jax.numpy as jnp
from jax import lax
from jax.experimental import pallas as pl
from jax.experimental.pallas import tpu as pltpu
from jax.experimental.pallas import tpu_sc as plsc

D_MODEL = 1024
CHUNK = 64
CONV_DIM = D_MODEL
CONV_WIDTH = 3
N_HEADS = 8
QK_NOPE_DIM = 128
QK_ROPE_DIM = 64
V_HEAD_DIM = 128
QK_HEAD_DIM = QK_NOPE_DIM + QK_ROPE_DIM
Q_LORA_RANK = 384
KV_LORA_RANK = 256
ROPE_BASE = 10000.0
N_EXPERTS = 32
TOP_K = 4
D_FF_EXPERT = D_MODEL
SWIGLU_LIMIT = 7.0
SWIGLU_ALPHA = 1.702
LN_EPS = 1e-5
RMS_EPS = 1e-6
DEPTH = 1
DEEPNORM_ALPHA = (2 * DEPTH) ** 0.25

QK_PAD = 256
ROPE_PAD = 128

PROJ_TM = 512
ATTN_TQ = 512
ATTN_TK = 512
ONES_ROWS = 16
POST_TM = 512
MOE_BM = 512
COMB_TM = 512
SC_WINDOW = 128
PACKED = D_MODEL // 2
HALF = PACKED // 2
HALO = 8
NEG_BIG = -1e30
LOG2E = 1.4426950408889634
VMEM_LIMIT = 56 * 1024 * 1024

_BF = jnp.bfloat16
_F32 = jnp.float32


def _dot(a, b):
    return jnp.dot(a, b, preferred_element_type=_F32)


def _const_spec(shape):
    nd = len(shape)
    return pl.BlockSpec(shape, lambda *_: (0,) * nd, pipeline_mode=pl.Buffered(1))


def _pack_rows(a):
    half = a.shape[1] // 2
    lo = lax.bitcast_convert_type(a[:, :half].astype(_BF).astype(_F32), jnp.int32)
    hi = lax.bitcast_convert_type(a[:, half:].astype(_BF).astype(_F32), jnp.int32)
    return lax.shift_right_logical(lo, 16) | (hi & jnp.int32(-65536))


def _unpack_rows(p):
    lo = lax.bitcast_convert_type(lax.shift_left(p, 16), _F32)
    hi = lax.bitcast_convert_type(p & jnp.int32(-65536), _F32)
    return lo, hi


def _sc_mesh():
    return plsc.VectorSubcoreMesh(core_axis_name="core", subcore_axis_name="subcore")


def _sc_scatter_rows(tables, idx, n_copies, n_out):
    n_rows, width = tables[0].shape
    n_tab = len(tables)
    window = SC_WINDOW

    @functools.partial(pl.kernel, mesh=_sc_mesh(),
                       out_type=[jax.ShapeDtypeStruct((n_out, width), tables[0].dtype)] * n_tab)
    def scatter_kernel(*refs):
        idx_hbm = refs[n_tab]
        for src_hbm, out_hbm in zip(refs[:n_tab], refs[n_tab + 1:]):
            def body(src_vmem, idx_vmem, out_hbm=out_hbm):
                for k in range(n_copies):
                    pltpu.sync_copy(src_vmem, out_hbm.at[idx_vmem.at[k]])

            pltpu.emit_pipeline(
                body,
                grid=(n_rows // window,),
                in_specs=[pl.BlockSpec((window, width), lambda i: (i, 0)),
                          pl.BlockSpec((idx.shape[0], window), lambda i: (0, i))],
                out_specs=[],
                core_axis_name=("core", "subcore"),
                dimension_semantics=(pltpu.PARALLEL,),
            )(src_hbm, idx_hbm)

    return scatter_kernel(*tables, idx)


def _sc_gather_rows(tables, idx, n_copies):
    n_rows = idx.shape[1]
    width = tables[0].shape[1]
    n_tab = len(tables)
    window = SC_WINDOW
    steps = n_rows // window

    @functools.partial(pl.kernel, mesh=_sc_mesh(),
                       out_type=[jax.ShapeDtypeStruct((n_copies * n_rows, width), tables[0].dtype)] * n_tab)
    def gather_kernel(*refs):
        idx_hbm = refs[n_tab]
        for table_hbm, out_hbm in zip(refs[:n_tab], refs[n_tab + 1:]):
            def body(idx_vmem, out_vmem, table_hbm=table_hbm):
                pltpu.sync_copy(table_hbm.at[idx_vmem.at[0]], out_vmem)

            pltpu.emit_pipeline(
                body,
                grid=(n_copies, steps),
                in_specs=[pl.BlockSpec((1, window), lambda k, i: (k, i))],
                out_specs=[pl.BlockSpec((window, width), lambda k, i: (k * steps + i, 0))],
                core_axis_name=("core", "subcore"),
                dimension_semantics=(pltpu.PARALLEL, pltpu.PARALLEL),
            )(idx_hbm, out_hbm)

    return gather_kernel(*tables, idx)


def _proj_kernel(tiles_per_seq,
                 x_ref, pos_ref, rope_ref, wg3_ref, wlat_ref, wgates_ref, convw_ref,
                 qg_ref, kvg_ref, wuqa_ref, wuk_ref, wuv_ref, wcb_ref,
                 q_ref, kn_ref, kpe_ref, v_ref, ycg_ref, sga_ref, ubuf):
    i = pl.program_id(0)
    tm = x_ref.shape[0]
    xb = x_ref[...].astype(_BF)

    @pl.when(i % tiles_per_seq == 0)
    def _():
        ubuf[0:HALO, :] = jnp.zeros((HALO, CONV_DIM), _F32)

    g3 = _dot(xb, wg3_ref[...])
    u = g3[:, CONV_DIM:2 * CONV_DIM] * g3[:, 2 * CONV_DIM:]
    ubuf[HALO:HALO + tm, :] = u
    cw = convw_ref[...]
    conv = (cw[2:3, :] * u + cw[1:2, :] * ubuf[HALO - 1:HALO - 1 + tm, :]
            + cw[0:1, :] * ubuf[HALO - 2:HALO - 2 + tm, :])
    ubuf[0:HALO, :] = ubuf[tm:tm + HALO, :]
    yc = _dot((g3[:, :CONV_DIM] * conv).astype(_BF), wcb_ref[...])
    gates = _dot(xb, wgates_ref[...])
    ycg_ref[...] = (jax.nn.sigmoid(gates[:, :D_MODEL]) * yc).astype(_BF)
    sga_ref[...] = jax.nn.sigmoid(gates[:, D_MODEL:]).astype(_BF)

    lat = _dot(xb, wlat_ref[...])
    trig = jnp.sin(pos_ref[...].astype(_F32) * rope_ref[0:1, :] + rope_ref[3:4, :])
    cosb = trig * rope_ref[1:2, :]
    sinb = pltpu.roll(trig, QK_ROPE_DIM, axis=1) * rope_ref[2:3, :]
    q_lat = lat[:, :Q_LORA_RANK]
    rq = q_lat * lax.rsqrt(jnp.mean(q_lat * q_lat, axis=-1, keepdims=True) + RMS_EPS) * qg_ref[...]
    kv_lat = lat[:, Q_LORA_RANK:Q_LORA_RANK + KV_LORA_RANK]
    ckv = (kv_lat * lax.rsqrt(jnp.mean(kv_lat * kv_lat, axis=-1, keepdims=True) + RMS_EPS)
           * kvg_ref[...]).astype(_BF)
    o = Q_LORA_RANK + KV_LORA_RANK
    lane = lax.broadcasted_iota(jnp.int32, (tm, ROPE_PAD), 1)
    half = QK_ROPE_DIM // 2

    def swap_halves(u):
        return jnp.where(lane < half, pltpu.roll(u, ROPE_PAD - half, axis=1), pltpu.roll(u, half, axis=1))

    k_pe = lat[:, o:o + ROPE_PAD]
    kpe_ref[...] = (k_pe * cosb + swap_halves(k_pe) * sinb).astype(_BF)
    kn = _dot(ckv, wuk_ref[...]).astype(_BF)
    vv = _dot(ckv, wuv_ref[...]).astype(_BF)
    for h in range(N_HEADS):
        kn_ref[h] = kn[:, h * QK_NOPE_DIM:(h + 1) * QK_NOPE_DIM]
        v_ref[h] = vv[:, h * V_HEAD_DIM:(h + 1) * V_HEAD_DIM]

    rqb = rq.astype(_BF)
    qa = _dot(rqb, wuqa_ref[...])
    scale = QK_HEAD_DIM ** -0.5 * LOG2E
    for h in range(N_HEADS):
        lo = h * QK_PAD
        q_ref[h, :, :QK_NOPE_DIM] = (qa[:, lo:lo + QK_NOPE_DIM] * scale).astype(_BF)
        q_pe = qa[:, lo + QK_NOPE_DIM:lo + QK_PAD]
        hi = q_pe * cosb + swap_halves(q_pe) * sinb
        q_ref[h, :, QK_NOPE_DIM:] = (hi * scale).astype(_BF)


def _proj_call(x2d, pos, rope, wg3, wlat, wgates, convw, qg, kvg, wuqa, wuk, wuv, wcb, seq):
    t = x2d.shape[0]
    tm = PROJ_TM
    tps = seq // tm
    row = lambda w: pl.BlockSpec((tm, w), lambda i: (i, 0))
    heads = lambda w: pl.BlockSpec((None, N_HEADS, tm, w), lambda i: (i // tps, 0, i % tps, 0))
    head_major = lambda w: jax.ShapeDtypeStruct((t // seq, N_HEADS, seq, w), _BF)
    outs = [
        head_major(QK_PAD),
        head_major(QK_NOPE_DIM),
        jax.ShapeDtypeStruct((t, ROPE_PAD), _BF),
        head_major(V_HEAD_DIM),
        jax.ShapeDtypeStruct((t, D_MODEL), _BF),
        jax.ShapeDtypeStruct((t, D_MODEL), _BF),
    ]
    return pl.pallas_call(
        functools.partial(_proj_kernel, tps),
        grid=(t // tm,),
        in_specs=[row(D_MODEL), row(1)]
        + [_const_spec(a.shape)
           for a in (rope, wg3, wlat, wgates, convw, qg, kvg, wuqa, wuk, wuv, wcb)],
        out_specs=[heads(QK_PAD), heads(QK_NOPE_DIM), row(ROPE_PAD), heads(V_HEAD_DIM),
                   row(D_MODEL), row(D_MODEL)],
        out_shape=outs,
        scratch_shapes=[pltpu.VMEM((tm + HALO, CONV_DIM), _F32)],
        compiler_params=pltpu.CompilerParams(dimension_semantics=("arbitrary",),
                                             vmem_limit_bytes=VMEM_LIMIT),
        name="proj_conv_qkv",
    )(x2d, pos, rope, wg3, wlat, wgates, convw, qg, kvg, wuqa, wuk, wuv, wcb)


def _attn_kernel(q_ref, kn_ref, kpe_ref, v_ref, o_ref, kfull, vt, s0, s1, p0, p1, a0, a1, m_ref, acc):
    tq, tk = ATTN_TQ, ATTN_TK
    n_tiles = q_ref.shape[0] // tq
    s_bufs, p_bufs, a_bufs = (s0, s1), (p0, p1), (a0, a1)
    stages = [(tile, blk) for tile in range(n_tiles) for blk in range(tile + 1)]

    kfull[:, :QK_NOPE_DIM] = kn_ref[...]
    kfull[:, QK_NOPE_DIM:] = kpe_ref[...]
    vt[:V_HEAD_DIM, :] = v_ref[...].T
    vt[V_HEAD_DIM:, :] = jnp.ones((vt.shape[0] - V_HEAD_DIM, vt.shape[1]), _BF)

    def scores_t(t):
        tile, blk = stages[t]
        s_bufs[t % 2][...] = lax.dot_general(
            kfull[blk * tk:(blk + 1) * tk, :], q_ref[tile * tq:(tile + 1) * tq, :],
            (((1,), (1,)), ((), ())), preferred_element_type=_F32)

    key_chunk = lax.broadcasted_iota(jnp.int32, (tk, tq), 0) // CHUNK
    qry_chunk = lax.broadcasted_iota(jnp.int32, (tk, tq), 1) // CHUNK
    causal = key_chunk <= qry_chunk

    def softmax_t(t):
        tile, blk = stages[t]
        par = t % 2
        s_t = s_bufs[par][...]
        if blk == tile:
            s_t = jnp.where(causal, s_t, NEG_BIG)
        blk_max = jnp.max(s_t, axis=0, keepdims=True)
        if blk == 0:
            m_new = blk_max
        else:
            m_old = m_ref[...]
            m_new = jnp.maximum(m_old, blk_max)
            a_bufs[par][...] = jnp.exp2(m_old - m_new)
        m_ref[...] = m_new
        p_bufs[par][...] = jnp.exp2(s_t - m_new).astype(_BF)

    def accumulate(t):
        tile, blk = stages[t]
        par = t % 2
        pv = _dot(vt[:, blk * tk:(blk + 1) * tk], p_bufs[par][...])
        acc[...] = pv if blk == 0 else a_bufs[par][...] * acc[...] + pv
        if blk == tile:
            out_t = acc[:V_HEAD_DIM, :] / acc[V_HEAD_DIM:V_HEAD_DIM + 1, :]
            o_ref[tile * tq:(tile + 1) * tq, :] = out_t.T.astype(_BF)

    scores_t(0)
    for t in range(len(stages)):
        if t + 1 < len(stages):
            scores_t(t + 1)
        if t > 0:
            accumulate(t - 1)
        softmax_t(t)
    accumulate(len(stages) - 1)


def _attn_call(q, kn, kpe, v):
    b, _, s, _ = q.shape
    tq = ATTN_TQ
    assert ATTN_TQ == ATTN_TK and s % tq == 0
    per_head = lambda w: pl.BlockSpec((None, None, s, w), lambda bi, h: (bi, h, 0, 0))
    return pl.pallas_call(
        _attn_kernel,
        grid=(b, N_HEADS),
        in_specs=[per_head(QK_PAD), per_head(QK_NOPE_DIM),
                  pl.BlockSpec((None, s, ROPE_PAD), lambda bi, h: (bi, 0, 0)), per_head(V_HEAD_DIM)],
        out_specs=per_head(V_HEAD_DIM),
        out_shape=jax.ShapeDtypeStruct((b, N_HEADS, s, V_HEAD_DIM), _BF),
        scratch_shapes=[pltpu.VMEM((s, QK_PAD), _BF), pltpu.VMEM((V_HEAD_DIM + ONES_ROWS, s), _BF),
                        pltpu.VMEM((ATTN_TK, tq), _F32), pltpu.VMEM((ATTN_TK, tq), _F32),
                        pltpu.VMEM((ATTN_TK, tq), _BF), pltpu.VMEM((ATTN_TK, tq), _BF),
                        pltpu.VMEM((1, tq), _F32), pltpu.VMEM((1, tq), _F32),
                        pltpu.VMEM((1, tq), _F32),
                        pltpu.VMEM((V_HEAD_DIM + ONES_ROWS, tq), _F32)],
        compiler_params=pltpu.CompilerParams(dimension_semantics=("arbitrary", "arbitrary"),
                                             vmem_limit_bytes=VMEM_LIMIT),
        name="chunk_causal_attention",
    )(q, kn, kpe, v)


def _layer_norm(z, g, b):
    mu = jnp.mean(z, axis=-1, keepdims=True)
    zc = z - mu
    var = jnp.mean(zc * zc, axis=-1, keepdims=True)
    return zc * lax.rsqrt(var + LN_EPS) * g + b


def _post_kernel(expert_cap, attn_ref, ycg_ref, sga_ref, x_ref, wab_ref, wout_ref, g1_ref, b1_ref,
                 wr_ref, br_ref, tri_ref, h1_ref, h1pa_ref, h1pb_ref, slot_ref, tw_ref, cnt_ref, run):
    tm = x_ref.shape[0]

    @pl.when(pl.program_id(0) == 0)
    def _():
        run[...] = jnp.zeros(run.shape, _F32)

    attn = jnp.concatenate([attn_ref[h] for h in range(N_HEADS)], axis=1)
    y_attn = _dot(attn, wab_ref[...])
    merged = ycg_ref[...].astype(_F32) + sga_ref[...].astype(_F32) * y_attn
    mix = _dot(merged.astype(_BF), wout_ref[...])
    h1 = _layer_norm(DEEPNORM_ALPHA * x_ref[...] + mix, g1_ref[...], b1_ref[...])
    h1_ref[...] = h1
    h1p = _pack_rows(h1)
    h1pa_ref[...] = h1p[:, :HALF]
    h1pb_ref[...] = h1p[:, HALF:]

    nt = (((1,), (1,)), ((), ()))
    h_hi = h1.astype(_BF)
    h_lo = (h1 - h_hi.astype(_F32)).astype(_BF)
    both = lax.dot_general(wr_ref[...], h_hi, nt, preferred_element_type=_F32)
    logits = (both[:N_EXPERTS] + both[N_EXPERTS:] + br_ref[:, 0:1]
              + lax.dot_general(wr_ref[:N_EXPERTS, :], h_lo, nt, preferred_element_type=_F32))
    expert = lax.broadcasted_iota(jnp.int32, (N_EXPERTS, tm), 0)
    work = logits
    chosen = jnp.zeros((N_EXPERTS, tm), jnp.bool_)
    sels, exps = [], []
    for k in range(TOP_K):
        mx = jnp.max(work, axis=0, keepdims=True)
        sel = jnp.min(jnp.where(work == mx, expert, N_EXPERTS), axis=0, keepdims=True)
        hit = expert == sel
        sels.append(sel)
        exps.append(jnp.exp(mx - (mx if k == 0 else top0)))
        if k == 0:
            top0 = mx
        chosen = jnp.logical_or(chosen, hit)
        work = jnp.where(hit, -jnp.inf, work)
    denom = exps[0] + exps[1] + exps[2] + exps[3]

    onehot = chosen.astype(_F32)
    before = _dot(onehot.astype(_BF), tri_ref[...]) + run[:, 0:1]
    rows = []
    for k in range(TOP_K):
        rank = jnp.sum(jnp.where(expert == sels[k], before, 0.0), axis=0, keepdims=True)
        rows.append(sels[k] * expert_cap + rank.astype(jnp.int32))
    slot_ref[...] = jnp.concatenate(rows + [jnp.zeros((8 - TOP_K, tm), jnp.int32)], axis=0)
    run[...] = run[...] + jnp.sum(onehot, axis=1, keepdims=True)
    cnt_ref[...] = run[...]
    tw_t = jnp.concatenate([e / denom for e in exps] + [jnp.zeros((128 - TOP_K, tm), _F32)], axis=0)
    tw_ref[...] = tw_t.T


def _post_call(attn, ycg, sga, x2d, wab, wout, g1, b1, wr, br):
    t = x2d.shape[0]
    tm = POST_TM
    row = lambda w: pl.BlockSpec((tm, w), lambda i: (i, 0))
    tps = attn.shape[2] // tm
    tri = (jnp.arange(tm)[:, None] < jnp.arange(tm)[None, :]).astype(_BF)
    outs = [
        jax.ShapeDtypeStruct((t, D_MODEL), _F32),
        jax.ShapeDtypeStruct((t, HALF), jnp.int32),
        jax.ShapeDtypeStruct((t, HALF), jnp.int32),
        jax.ShapeDtypeStruct((8, t), jnp.int32),
        jax.ShapeDtypeStruct((t, 128), _F32),
        jax.ShapeDtypeStruct((N_EXPERTS, 128), _F32),
    ]
    out_specs = [row(D_MODEL), row(HALF), row(HALF), pl.BlockSpec((8, tm), lambda i: (0, i)),
                 row(128), pl.BlockSpec((N_EXPERTS, 128), lambda i: (0, 0))]
    return pl.pallas_call(
        functools.partial(_post_kernel, t),
        grid=(t // tm,),
        in_specs=[pl.BlockSpec((None, N_HEADS, tm, V_HEAD_DIM), lambda i: (i // tps, 0, i % tps, 0))]
        + [row(D_MODEL)] * 3
        + [_const_spec(a.shape) for a in (wab, wout, g1, b1, wr, br, tri)],
        out_specs=out_specs,
        out_shape=outs,
        scratch_shapes=[pltpu.VMEM((N_EXPERTS, 128), _F32)],
        compiler_params=pltpu.CompilerParams(dimension_semantics=("arbitrary",),
                                             vmem_limit_bytes=VMEM_LIMIT),
        name="merge_out_ln1_router",
    )(attn, ycg, sga, x2d, wab, wout, g1, b1, wr, br, tri)


def _moe_kernel(be_ref, we_ref, rb_ref, bv_ref, nused_ref, xa_ref, xb_ref, wgu_ref, bgu_ref, wd_ref, bd_ref,
                ya_ref, yb_ref, wgu_bf, wd_bf):
    i = pl.program_id(0)
    prev = be_ref[jnp.maximum(i - 1, 0)]
    fresh = jnp.logical_or(i == 0, be_ref[i] != prev)
    used = i < nused_ref[0]

    @pl.when(jnp.logical_and(fresh, used))
    def _():
        wgu_bf[...] = wgu_ref[...].astype(_BF)
        wd_bf[...] = wd_ref[...].astype(_BF)

    def ffn(r0, rows):
        a_lo, a_hi = _unpack_rows(xa_ref[r0:r0 + rows, :])
        b_lo, b_hi = _unpack_rows(xb_ref[r0:r0 + rows, :])
        xs = jnp.concatenate([a_lo, b_lo, a_hi, b_hi], axis=1).astype(_BF)
        gu = _dot(xs, wgu_bf[...]) + bgu_ref[...]
        gate = jnp.minimum(gu[:, :D_FF_EXPERT], SWIGLU_LIMIT)
        up = jnp.clip(gu[:, D_FF_EXPERT:], -SWIGLU_LIMIT, SWIGLU_LIMIT)
        hid = (up + 1.0) * (gate * jax.nn.sigmoid(SWIGLU_ALPHA * gate))
        yp = _pack_rows(_dot(hid.astype(_BF), wd_bf[...]) + bd_ref[...])
        ya_ref[r0:r0 + rows, :] = yp[:, :HALF]
        yb_ref[r0:r0 + rows, :] = yp[:, HALF:]

    bm = xa_ref.shape[0]
    small = bv_ref[i] <= bm // 2

    @pl.when(jnp.logical_and(used, jnp.logical_not(small)))
    def _():
        ffn(0, bm)

    @pl.when(jnp.logical_and(used, small))
    def _():
        ffn(0, bm // 2)


def _moe_call(block_e, weight_e, block_row, block_valid, n_used, xa, xb, wgu, bgu, wd, bd):
    n_slots = xa.shape[0]
    bm = MOE_BM
    n_blocks = block_e.shape[0]
    grid_spec = pltpu.PrefetchScalarGridSpec(
        num_scalar_prefetch=5,
        grid=(n_blocks,),
        in_specs=[
            pl.BlockSpec((bm, HALF), lambda i, be, we, rb, bv, nu: (rb[i], 0)),
            pl.BlockSpec((bm, HALF), lambda i, be, we, rb, bv, nu: (rb[i], 0)),
            pl.BlockSpec((None, D_MODEL, 2 * D_FF_EXPERT), lambda i, be, we, rb, bv, nu: (we[i], 0, 0)),
            pl.BlockSpec((None, 1, 2 * D_FF_EXPERT), lambda i, be, we, rb, bv, nu: (be[i], 0, 0)),
            pl.BlockSpec((None, D_FF_EXPERT, D_MODEL), lambda i, be, we, rb, bv, nu: (we[i], 0, 0)),
            pl.BlockSpec((None, 1, D_MODEL), lambda i, be, we, rb, bv, nu: (be[i], 0, 0)),
        ],
        out_specs=[pl.BlockSpec((bm, HALF), lambda i, be, we, rb, bv, nu: (rb[i], 0))] * 2,
        scratch_shapes=[pltpu.VMEM((D_MODEL, 2 * D_FF_EXPERT), _BF),
                        pltpu.VMEM((D_FF_EXPERT, D_MODEL), _BF)],
    )
    return pl.pallas_call(
        _moe_kernel,
        grid_spec=grid_spec,
        out_shape=[jax.ShapeDtypeStruct((n_slots, HALF), jnp.int32)] * 2,
        compiler_params=pltpu.CompilerParams(dimension_semantics=("arbitrary",),
                                             vmem_limit_bytes=VMEM_LIMIT),
        name="expert_ffn",
    )(block_e, weight_e, block_row, block_valid, n_used, xa, xb, wgu, bgu, wd, bd)


def _combine_kernel(*refs):
    ya_refs, yb_refs = refs[:TOP_K], refs[TOP_K:2 * TOP_K]
    tw_ref, h1_ref, g2_ref, b2_ref, o_ref = refs[2 * TOP_K:]
    tw = tw_ref[...]
    parts = [jnp.zeros((tw.shape[0], HALF), _F32) for _ in range(4)]
    for k in range(TOP_K):
        a_lo, a_hi = _unpack_rows(ya_refs[k][...])
        b_lo, b_hi = _unpack_rows(yb_refs[k][...])
        w = tw[:, k:k + 1]
        parts = [p + w * y for p, y in zip(parts, (a_lo, b_lo, a_hi, b_hi))]
    ffn = jnp.concatenate(parts, axis=1)
    o_ref[...] = _layer_norm(DEEPNORM_ALPHA * h1_ref[...] + ffn, g2_ref[...], b2_ref[...])


def _combine_call(yga, ygb, tw, h1, g2, b2):
    t = h1.shape[0]
    tm = COMB_TM
    row = lambda w: pl.BlockSpec((tm, w), lambda i: (i, 0))
    krow = lambda k: pl.BlockSpec((tm, HALF), lambda i: (k * (t // tm) + i, 0))
    return pl.pallas_call(
        _combine_kernel,
        grid=(t // tm,),
        in_specs=[krow(k) for k in range(TOP_K)] * 2 + [row(128), row(D_MODEL),
                  _const_spec(g2.shape), _const_spec(b2.shape)],
        out_specs=row(D_MODEL),
        out_shape=jax.ShapeDtypeStruct((t, D_MODEL), _F32),
        compiler_params=pltpu.CompilerParams(dimension_semantics=("arbitrary",),
                                             vmem_limit_bytes=VMEM_LIMIT),
        name="combine_ln2",
    )(*([yga] * TOP_K + [ygb] * TOP_K), tw, h1, g2, b2)


def _prepare_weights(w_in, w_uq):
    o_q = 3 * CONV_DIM
    o_kv = o_q + Q_LORA_RANK
    o_pe = o_kv + KV_LORA_RANK
    o_gc = o_pe + QK_ROPE_DIM
    wg3 = w_in[:, :o_q]
    kpe = w_in[:, o_pe:o_gc]
    zpad = jnp.zeros((D_MODEL, ROPE_PAD - QK_ROPE_DIM), w_in.dtype)
    wlat = jnp.concatenate([w_in[:, o_q:o_pe], kpe, zpad], axis=1)
    wgates = w_in[:, o_gc:]
    wq = w_uq.reshape(Q_LORA_RANK, N_HEADS, QK_HEAD_DIM)
    wuqa = jnp.pad(wq, ((0, 0), (0, 0), (0, QK_PAD - QK_HEAD_DIM)))
    return (wg3.astype(_BF), wlat.astype(_BF), wgates.astype(_BF),
            wuqa.reshape(Q_LORA_RANK, N_HEADS * QK_PAD).astype(_BF))


def _rope_rows():
    half = QK_ROPE_DIM // 2
    inv_freq = ROPE_BASE ** (-jnp.arange(0, QK_ROPE_DIM, 2, dtype=_F32) / QK_ROPE_DIM)
    z = jnp.zeros((ROPE_PAD - QK_ROPE_DIM,), _F32)
    ones = jnp.ones((half,), _F32)
    rows = jnp.stack([jnp.concatenate([inv_freq] * 4),
                      jnp.concatenate([ones, ones, z]),
                      jnp.concatenate([-ones, ones, z]),
                      jnp.concatenate([ones * (np.pi / 2), ones * (np.pi / 2), z])])
    return jnp.pad(rows, ((0, 8 - rows.shape[0]), (0, 0)))


def _block_tables(counts, n_tok):
    bm = MOE_BM
    n_blocks = n_tok * TOP_K // bm + N_EXPERTS
    nblk = (counts + bm - 1) // bm
    end = jnp.cumsum(nblk)
    start = end - nblk
    n_used = end[-1]
    i = jnp.minimum(jnp.arange(n_blocks, dtype=jnp.int32), n_used - 1)
    block_e = (end[None, :] <= i[:, None]).sum(axis=1).astype(jnp.int32)
    first = jnp.sum(jnp.where(jnp.arange(N_EXPERTS)[None, :] == block_e[:, None], start[None, :], 0), axis=1)
    block_row = block_e * (n_tok // bm) + (i - first)
    fresh = jnp.concatenate([jnp.ones((1,), jnp.bool_), block_e[1:] != block_e[:-1]])
    later = jnp.where(block_e[None, :] > block_e[:, None], block_e[None, :], N_EXPERTS)
    nxt = jnp.min(later, axis=1)
    weight_e = jnp.where(jnp.logical_or(fresh, nxt == N_EXPERTS), block_e, nxt).astype(jnp.int32)
    cnt = jnp.sum(jnp.where(jnp.arange(N_EXPERTS)[None, :] == block_e[:, None], counts[None, :], 0), axis=1)
    block_valid = jnp.clip(cnt - (i - first) * bm, 0, bm).astype(jnp.int32)
    return (block_e, weight_e, block_row.astype(jnp.int32), block_valid,
            n_used.astype(jnp.int32).reshape(1))


def kernel(x, positions, w_in, conv_w, q_norm_g, w_uq, kv_norm_g, w_uk, w_uv, w_conv_branch,
           w_attn_branch, w_out, ln1_g, ln1_b, w_router, b_router, w_gate_up, b_gate_up, w_down,
           b_down, ln2_g, ln2_b):
    b, s, d = x.shape
    t = b * s
    x2d = x.reshape(t, d)
    pos = positions.reshape(t, 1).astype(jnp.int32)
    rope = _rope_rows()
    for l in range(DEPTH):
        wg3, wlat, wgates, wuqa = _prepare_weights(w_in[l], w_uq[l])
        q, kn, kpe, v, ycg, sga = _proj_call(
            x2d, pos, rope, wg3, wlat, wgates, conv_w[l], q_norm_g[l][None, :], kv_norm_g[l][None, :],
            wuqa, w_uk[l].astype(_BF), w_uv[l].astype(_BF), w_conv_branch[l].astype(_BF), s)
        attn = _attn_call(q, kn, kpe.reshape(b, s, -1), v)
        wr_t = w_router[l].T
        wr_hi = wr_t.astype(_BF)
        wr_lo = (wr_t - wr_hi.astype(_F32)).astype(_BF)
        h1, h1pa, h1pb, slots, tw_pad, counts = _post_call(
            attn, ycg, sga, x2d, w_attn_branch[l].astype(_BF), w_out[l].astype(_BF),
            ln1_g[l][None, :], ln1_b[l][None, :], jnp.concatenate([wr_hi, wr_lo], axis=0),
            jnp.broadcast_to(b_router[l][:, None], (N_EXPERTS, 128)))
        block_e, weight_e, block_row, block_valid, n_used = _block_tables(counts[:, 0].astype(jnp.int32), t)
        xa, xb = _sc_scatter_rows([h1pa, h1pb], slots, TOP_K, N_EXPERTS * t)
        ya, yb = _moe_call(block_e, weight_e, block_row, block_valid, n_used, xa, xb,
                           w_gate_up[l], b_gate_up[l][:, None, :], w_down[l], b_down[l][:, None, :])
        yga, ygb = _sc_gather_rows([ya, yb], slots, TOP_K)
        x2d = _combine_call(yga, ygb, tw_pad, h1, ln2_g[l][None, :], ln2_b[l][None, :])
    return x2d.reshape(b, s, d)
```

```python
import functools

import numpy as np
import jax
import jax.numpy as jnp
from jax import lax
from jax.experimental import pallas as pl
from jax.experimental.pallas import tpu as pltpu
from jax.experimental.pallas import tpu_sc as plsc

D_MODEL = 1024
CHUNK = 64
CONV_DIM = D_MODEL
CONV_WIDTH = 3
N_HEADS = 8
QK_NOPE_DIM = 128
QK_ROPE_DIM = 64
V_HEAD_DIM = 128
QK_HEAD_DIM = QK_NOPE_DIM + QK_ROPE_DIM
Q_LORA_RANK = 384
KV_LORA_RANK = 256
ROPE_BASE = 10000.0
N_EXPERTS = 32
TOP_K = 4
D_FF_EXPERT = D_MODEL
SWIGLU_LIMIT = 7.0
SWIGLU_ALPHA = 1.702
LN_EPS = 1e-5
RMS_EPS = 1e-6
DEPTH = 1
DEEPNORM_ALPHA = (2 * DEPTH) ** 0.25

QK_PAD = 256
ROPE_PAD = 128

PROJ_TM = 512
ATTN_TQ = 512
ATTN_TK = 512
ONES_ROWS = 16
POST_TM = 512
MOE_BM = 512
MOE_ROW_STEPS = 4
COMB_TM = 512
SC_WINDOW = 128
PACKED = D_MODEL // 2
HALF = PACKED // 2
HALO = 8
NEG_BIG = -1e30
LOG2E = 1.4426950408889634
VMEM_LIMIT = 56 * 1024 * 1024

_BF = jnp.bfloat16
_F32 = jnp.float32


def _dot(a, b):
    return jnp.dot(a, b, preferred_element_type=_F32)


def _const_spec(shape):
    nd = len(shape)
    return pl.BlockSpec(shape, lambda *_: (0,) * nd, pipeline_mode=pl.Buffered(1))


def _pack_rows(a):
    half = a.shape[1] // 2
    lo = lax.bitcast_convert_type(a[:, :half].astype(_BF).astype(_F32), jnp.int32)
    hi = lax.bitcast_convert_type(a[:, half:].astype(_BF).astype(_F32), jnp.int32)
    return lax.shift_right_logical(lo, 16) | (hi & jnp.int32(-65536))


def _unpack_rows(p):
    lo = lax.bitcast_convert_type(lax.shift_left(p, 16), _F32)
    hi = lax.bitcast_convert_type(p & jnp.int32(-65536), _F32)
    return lo, hi


def _sc_mesh():
    return plsc.VectorSubcoreMesh(core_axis_name="core", subcore_axis_name="subcore")


def _sc_scatter_rows(tables, idx, n_copies, n_out):
    n_rows, width = tables[0].shape
    n_tab = len(tables)
    window = SC_WINDOW

    @functools.partial(pl.kernel, mesh=_sc_mesh(),
                       out_type=[jax.ShapeDtypeStruct((n_out, width), tables[0].dtype)] * n_tab)
    def scatter_kernel(*refs):
        idx_hbm = refs[n_tab]
        for src_hbm, out_hbm in zip(refs[:n_tab], refs[n_tab + 1:]):
            def body(src_vmem, idx_vmem, out_hbm=out_hbm):
                for k in range(n_copies):
                    pltpu.sync_copy(src_vmem, out_hbm.at[idx_vmem.at[k]])

            pltpu.emit_pipeline(
                body,
                grid=(n_rows // window,),
                in_specs=[pl.BlockSpec((window, width), lambda i: (i, 0)),
                          pl.BlockSpec((idx.shape[0], window), lambda i: (0, i))],
                out_specs=[],
                core_axis_name=("core", "subcore"),
                dimension_semantics=(pltpu.PARALLEL,),
            )(src_hbm, idx_hbm)

    return scatter_kernel(*tables, idx)


def _sc_gather_rows(tables, idx, n_copies):
    n_rows = idx.shape[1]
    width = tables[0].shape[1]
    n_tab = len(tables)
    window = SC_WINDOW
    steps = n_rows // window

    @functools.partial(pl.kernel, mesh=_sc_mesh(),
                       out_type=[jax.ShapeDtypeStruct((n_copies * n_rows, width), tables[0].dtype)] * n_tab)
    def gather_kernel(*refs):
        idx_hbm = refs[n_tab]
        for table_hbm, out_hbm in zip(refs[:n_tab], refs[n_tab + 1:]):
            def body(idx_vmem, out_vmem, table_hbm=table_hbm):
                pltpu.sync_copy(table_hbm.at[idx_vmem.at[0]], out_vmem)

            pltpu.emit_pipeline(
                body,
                grid=(n_copies, steps),
                in_specs=[pl.BlockSpec((1, window), lambda k, i: (k, i))],
                out_specs=[pl.BlockSpec((window, width), lambda k, i: (k * steps + i, 0))],
                core_axis_name=("core", "subcore"),
                dimension_semantics=(pltpu.PARALLEL, pltpu.PARALLEL),
            )(idx_hbm, out_hbm)

    return gather_kernel(*tables, idx)


def _proj_kernel(tiles_per_seq,
                 x_ref, pos_ref, rope_ref, wg3_ref, wlat_ref, wgates_ref, convw_ref,
                 qg_ref, kvg_ref, wuqa_ref, wuk_ref, wuv_ref, wcb_ref,
                 q_ref, kn_ref, kpe_ref, v_ref, ycg_ref, sga_ref, ubuf):
    i = pl.program_id(0)
    tm = x_ref.shape[0]
    xb = x_ref[...].astype(_BF)

    @pl.when(i % tiles_per_seq == 0)
    def _():
        ubuf[0:HALO, :] = jnp.zeros((HALO, CONV_DIM), _F32)

    g3 = _dot(xb, wg3_ref[...])
    u = g3[:, CONV_DIM:2 * CONV_DIM] * g3[:, 2 * CONV_DIM:]
    ubuf[HALO:HALO + tm, :] = u
    cw = convw_ref[...]
    conv = (cw[2:3, :] * u + cw[1:2, :] * ubuf[HALO - 1:HALO - 1 + tm, :]
            + cw[0:1, :] * ubuf[HALO - 2:HALO - 2 + tm, :])
    ubuf[0:HALO, :] = ubuf[tm:tm + HALO, :]
    yc = _dot((g3[:, :CONV_DIM] * conv).astype(_BF), wcb_ref[...])
    gates = _dot(xb, wgates_ref[...])
    ycg_ref[...] = (jax.nn.sigmoid(gates[:, :D_MODEL]) * yc).astype(_BF)
    sga_ref[...] = jax.nn.sigmoid(gates[:, D_MODEL:]).astype(_BF)

    lat = _dot(xb, wlat_ref[...])
    trig = jnp.sin(pos_ref[...].astype(_F32) * rope_ref[0:1, :] + rope_ref[3:4, :])
    cosb = trig * rope_ref[1:2, :]
    sinb = pltpu.roll(trig, QK_ROPE_DIM, axis=1) * rope_ref[2:3, :]
    q_lat = lat[:, :Q_LORA_RANK]
    rq = q_lat * lax.rsqrt(jnp.mean(q_lat * q_lat, axis=-1, keepdims=True) + RMS_EPS) * qg_ref[...]
    kv_lat = lat[:, Q_LORA_RANK:Q_LORA_RANK + KV_LORA_RANK]
    ckv = (kv_lat * lax.rsqrt(jnp.mean(kv_lat * kv_lat, axis=-1, keepdims=True) + RMS_EPS)
           * kvg_ref[...]).astype(_BF)
    o = Q_LORA_RANK + KV_LORA_RANK
    lane = lax.broadcasted_iota(jnp.int32, (tm, ROPE_PAD), 1)
    half = QK_ROPE_DIM // 2

    def swap_halves(u):
        return jnp.where(lane < half, pltpu.roll(u, ROPE_PAD - half, axis=1), pltpu.roll(u, half, axis=1))

    k_pe = lat[:, o:o + ROPE_PAD]
    kpe_ref[...] = (k_pe * cosb + swap_halves(k_pe) * sinb).astype(_BF)
    kn = _dot(ckv, wuk_ref[...]).astype(_BF)
    vv = _dot(ckv, wuv_ref[...]).astype(_BF)
    for h in range(N_HEADS):
        kn_ref[h] = kn[:, h * QK_NOPE_DIM:(h + 1) * QK_NOPE_DIM]
        v_ref[h] = vv[:, h * V_HEAD_DIM:(h + 1) * V_HEAD_DIM]

    rqb = rq.astype(_BF)
    qa = _dot(rqb, wuqa_ref[...])
    scale = QK_HEAD_DIM ** -0.5 * LOG2E
    for h in range(N_HEADS):
        lo = h * QK_PAD
        q_ref[h, :, :QK_NOPE_DIM] = (qa[:, lo:lo + QK_NOPE_DIM] * scale).astype(_BF)
        q_pe = qa[:, lo + QK_NOPE_DIM:lo + QK_PAD]
        hi = q_pe * cosb + swap_halves(q_pe) * sinb
        q_ref[h, :, QK_NOPE_DIM:] = (hi * scale).astype(_BF)


def _proj_call(x2d, pos, rope, wg3, wlat, wgates, convw, qg, kvg, wuqa, wuk, wuv, wcb, seq):
    t = x2d.shape[0]
    tm = PROJ_TM
    tps = seq // tm
    row = lambda w: pl.BlockSpec((tm, w), lambda i: (i, 0))
    heads = lambda w: pl.BlockSpec((None, N_HEADS, tm, w), lambda i: (i // tps, 0, i % tps, 0))
    head_major = lambda w: jax.ShapeDtypeStruct((t // seq, N_HEADS, seq, w), _BF)
    outs = [
        head_major(QK_PAD),
        head_major(QK_NOPE_DIM),
        jax.ShapeDtypeStruct((t, ROPE_PAD), _BF),
        head_major(V_HEAD_DIM),
        jax.ShapeDtypeStruct((t, D_MODEL), _BF),
        jax.ShapeDtypeStruct((t, D_MODEL), _BF),
    ]
    return pl.pallas_call(
        functools.partial(_proj_kernel, tps),
        grid=(t // tm,),
        in_specs=[row(D_MODEL), row(1)]
        + [_const_spec(a.shape)
           for a in (rope, wg3, wlat, wgates, convw, qg, kvg, wuqa, wuk, wuv, wcb)],
        out_specs=[heads(QK_PAD), heads(QK_NOPE_DIM), row(ROPE_PAD), heads(V_HEAD_DIM),
                   row(D_MODEL), row(D_MODEL)],
        out_shape=outs,
        scratch_shapes=[pltpu.VMEM((tm + HALO, CONV_DIM), _F32)],
        compiler_params=pltpu.CompilerParams(dimension_semantics=("arbitrary",),
                                             vmem_limit_bytes=VMEM_LIMIT),
        name="proj_conv_qkv",
    )(x2d, pos, rope, wg3, wlat, wgates, convw, qg, kvg, wuqa, wuk, wuv, wcb)


def _attn_kernel(q_ref, kn_ref, kpe_ref, v_ref, o_ref, kfull, vt, s0, s1, p0, p1, a0, a1, m_ref, acc):
    tq, tk = ATTN_TQ, ATTN_TK
    n_tiles = q_ref.shape[0] // tq
    s_bufs, p_bufs, a_bufs = (s0, s1), (p0, p1), (a0, a1)
    stages = [(tile, blk) for tile in range(n_tiles) for blk in range(tile + 1)]

    kfull[:, :QK_NOPE_DIM] = kn_ref[...]
    kfull[:, QK_NOPE_DIM:] = kpe_ref[...]
    vt[:V_HEAD_DIM, :] = v_ref[...].T
    vt[V_HEAD_DIM:, :] = jnp.ones((vt.shape[0] - V_HEAD_DIM, vt.shape[1]), _BF)

    def scores_t(t):
        tile, blk = stages[t]
        s_bufs[t % 2][...] = lax.dot_general(
            kfull[blk * tk:(blk + 1) * tk, :], q_ref[tile * tq:(tile + 1) * tq, :],
            (((1,), (1,)), ((), ())), preferred_element_type=_F32)

    key_chunk = lax.broadcasted_iota(jnp.int32, (tk, tq), 0) // CHUNK
    qry_chunk = lax.broadcasted_iota(jnp.int32, (tk, tq), 1) // CHUNK
    causal = key_chunk <= qry_chunk

    def softmax_t(t):
        tile, blk = stages[t]
        par = t % 2
        s_t = s_bufs[par][...]
        if blk == tile:
            s_t = jnp.where(causal, s_t, NEG_BIG)
        blk_max = jnp.max(s_t, axis=0, keepdims=True)
        if blk == 0:
            m_new = blk_max
        else:
            m_old = m_ref[...]
            m_new = jnp.maximum(m_old, blk_max)
            a_bufs[par][...] = jnp.exp2(m_old - m_new)
        m_ref[...] = m_new
        p_bufs[par][...] = jnp.exp2(s_t - m_new).astype(_BF)

    def accumulate(t):
        tile, blk = stages[t]
        par = t % 2
        pv = _dot(vt[:, blk * tk:(blk + 1) * tk], p_bufs[par][...])
        acc[...] = pv if blk == 0 else a_bufs[par][...] * acc[...] + pv
        if blk == tile:
            out_t = acc[:V_HEAD_DIM, :] / acc[V_HEAD_DIM:V_HEAD_DIM + 1, :]
            o_ref[tile * tq:(tile + 1) * tq, :] = out_t.T.astype(_BF)

    scores_t(0)
    for t in range(len(stages)):
        if t + 1 < len(stages):
            scores_t(t + 1)
        if t > 0:
            accumulate(t - 1)
        softmax_t(t)
    accumulate(len(stages) - 1)


def _attn_call(q, kn, kpe, v):
    b, _, s, _ = q.shape
    tq = ATTN_TQ
    assert ATTN_TQ == ATTN_TK and s % tq == 0
    per_head = lambda w: pl.BlockSpec((None, None, s, w), lambda bi, h: (bi, h, 0, 0))
    return pl.pallas_call(
        _attn_kernel,
        grid=(b, N_HEADS),
        in_specs=[per_head(QK_PAD), per_head(QK_NOPE_DIM),
                  pl.BlockSpec((None, s, ROPE_PAD), lambda bi, h: (bi, 0, 0)), per_head(V_HEAD_DIM)],
        out_specs=per_head(V_HEAD_DIM),
        out_shape=jax.ShapeDtypeStruct((b, N_HEADS, s, V_HEAD_DIM), _BF),
        scratch_shapes=[pltpu.VMEM((s, QK_PAD), _BF), pltpu.VMEM((V_HEAD_DIM + ONES_ROWS, s), _BF),
                        pltpu.VMEM((ATTN_TK, tq), _F32), pltpu.VMEM((ATTN_TK, tq), _F32),
                        pltpu.VMEM((ATTN_TK, tq), _BF), pltpu.VMEM((ATTN_TK, tq), _BF),
                        pltpu.VMEM((1, tq), _F32), pltpu.VMEM((1, tq), _F32),
                        pltpu.VMEM((1, tq), _F32),
                        pltpu.VMEM((V_HEAD_DIM + ONES_ROWS, tq), _F32)],
        compiler_params=pltpu.CompilerParams(dimension_semantics=("arbitrary", "arbitrary"),
                                             vmem_limit_bytes=VMEM_LIMIT),
        name="chunk_causal_attention",
    )(q, kn, kpe, v)


def _layer_norm(z, g, b):
    mu = jnp.mean(z, axis=-1, keepdims=True)
    zc = z - mu
    var = jnp.mean(zc * zc, axis=-1, keepdims=True)
    return zc * lax.rsqrt(var + LN_EPS) * g + b


def _post_kernel(expert_cap, attn_ref, ycg_ref, sga_ref, x_ref, wab_ref, wout_ref, g1_ref, b1_ref,
                 wr_ref, br_ref, tri_ref, h1_ref, h1pa_ref, h1pb_ref, slot_ref, tw_ref, cnt_ref, run):
    tm = x_ref.shape[0]

    @pl.when(pl.program_id(0) == 0)
    def _():
        run[...] = jnp.zeros(run.shape, _F32)

    attn = jnp.concatenate([attn_ref[h] for h in range(N_HEADS)], axis=1)
    y_attn = _dot(attn, wab_ref[...])
    merged = ycg_ref[...].astype(_F32) + sga_ref[...].astype(_F32) * y_attn
    mix = _dot(merged.astype(_BF), wout_ref[...])
    h1 = _layer_norm(DEEPNORM_ALPHA * x_ref[...] + mix, g1_ref[...], b1_ref[...])
    h1_ref[...] = h1
    h1p = _pack_rows(h1)
    h1pa_ref[...] = h1p[:, :HALF]
    h1pb_ref[...] = h1p[:, HALF:]

    nt = (((1,), (1,)), ((), ()))
    h_hi = h1.astype(_BF)
    h_lo = (h1 - h_hi.astype(_F32)).astype(_BF)
    both = lax.dot_general(wr_ref[...], h_hi, nt, preferred_element_type=_F32)
    logits = (both[:N_EXPERTS] + both[N_EXPERTS:] + br_ref[:, 0:1]
              + lax.dot_general(wr_ref[:N_EXPERTS, :], h_lo, nt, preferred_element_type=_F32))
    expert = lax.broadcasted_iota(jnp.int32, (N_EXPERTS, tm), 0)
    work = logits
    chosen = jnp.zeros((N_EXPERTS, tm), jnp.bool_)
    sels, exps = [], []
    for k in range(TOP_K):
        mx = jnp.max(work, axis=0, keepdims=True)
        sel = jnp.min(jnp.where(work == mx, expert, N_EXPERTS), axis=0, keepdims=True)
        hit = expert == sel
        sels.append(sel)
        exps.append(jnp.exp(mx - (mx if k == 0 else top0)))
        if k == 0:
            top0 = mx
        chosen = jnp.logical_or(chosen, hit)
        work = jnp.where(hit, -jnp.inf, work)
    denom = exps[0] + exps[1] + exps[2] + exps[3]

    onehot = chosen.astype(_F32)
    before = _dot(onehot.astype(_BF), tri_ref[...]) + run[:, 0:1]
    rows = []
    for k in range(TOP_K):
        rank = jnp.sum(jnp.where(expert == sels[k], before, 0.0), axis=0, keepdims=True)
        rows.append(sels[k] * expert_cap + rank.astype(jnp.int32))
    slot_ref[...] = jnp.concatenate(rows + [jnp.zeros((8 - TOP_K, tm), jnp.int32)], axis=0)
    run[...] = run[...] + jnp.sum(onehot, axis=1, keepdims=True)
    cnt_ref[...] = run[...]
    tw_t = jnp.concatenate([e / denom for e in exps] + [jnp.zeros((128 - TOP_K, tm), _F32)], axis=0)
    tw_ref[...] = tw_t.T


def _post_call(attn, ycg, sga, x2d, wab, wout, g1, b1, wr, br):
    t = x2d.shape[0]
    tm = POST_TM
    row = lambda w: pl.BlockSpec((tm, w), lambda i: (i, 0))
    tps = attn.shape[2] // tm
    tri = (jnp.arange(tm)[:, None] < jnp.arange(tm)[None, :]).astype(_BF)
    outs = [
        jax.ShapeDtypeStruct((t, D_MODEL), _F32),
        jax.ShapeDtypeStruct((t, HALF), jnp.int32),
        jax.ShapeDtypeStruct((t, HALF), jnp.int32),
        jax.ShapeDtypeStruct((8, t), jnp.int32),
        jax.ShapeDtypeStruct((t, 128), _F32),
        jax.ShapeDtypeStruct((N_EXPERTS, 128), _F32),
    ]
    out_specs = [row(D_MODEL), row(HALF), row(HALF), pl.BlockSpec((8, tm), lambda i: (0, i)),
                 row(128), pl.BlockSpec((N_EXPERTS, 128), lambda i: (0, 0))]
    return pl.pallas_call(
        functools.partial(_post_kernel, t),
        grid=(t // tm,),
        in_specs=[pl.BlockSpec((None, N_HEADS, tm, V_HEAD_DIM), lambda i: (i // tps, 0, i % tps, 0))]
        + [row(D_MODEL)] * 3
        + [_const_spec(a.shape) for a in (wab, wout, g1, b1, wr, br, tri)],
        out_specs=out_specs,
        out_shape=outs,
        scratch_shapes=[pltpu.VMEM((N_EXPERTS, 128), _F32)],
        compiler_params=pltpu.CompilerParams(dimension_semantics=("arbitrary",),
                                             vmem_limit_bytes=VMEM_LIMIT),
        name="merge_out_ln1_router",
    )(attn, ycg, sga, x2d, wab, wout, g1, b1, wr, br, tri)


def _moe_kernel(be_ref, we_ref, rb_ref, bv_ref, nused_ref, xa_ref, xb_ref, wgu_ref, bgu_ref, wd_ref, bd_ref,
                ya_ref, yb_ref, wgu_bf, wd_bf):
    i = pl.program_id(0)
    prev = be_ref[jnp.maximum(i - 1, 0)]
    fresh = jnp.logical_or(i == 0, be_ref[i] != prev)
    used = i < nused_ref[0]

    @pl.when(jnp.logical_and(fresh, used))
    def _():
        wgu_bf[...] = wgu_ref[...].astype(_BF)
        wd_bf[...] = wd_ref[...].astype(_BF)

    def ffn(r0, rows):
        a_lo, a_hi = _unpack_rows(xa_ref[r0:r0 + rows, :])
        b_lo, b_hi = _unpack_rows(xb_ref[r0:r0 + rows, :])
        xs = jnp.concatenate([a_lo, b_lo, a_hi, b_hi], axis=1).astype(_BF)
        gu = _dot(xs, wgu_bf[...]) + bgu_ref[...]
        gate = jnp.minimum(gu[:, :D_FF_EXPERT], SWIGLU_LIMIT)
        up = jnp.clip(gu[:, D_FF_EXPERT:], -SWIGLU_LIMIT, SWIGLU_LIMIT)
        hid = (up + 1.0) * (gate * jax.nn.sigmoid(SWIGLU_ALPHA * gate))
        yp = _pack_rows(_dot(hid.astype(_BF), wd_bf[...]) + bd_ref[...])
        ya_ref[r0:r0 + rows, :] = yp[:, :HALF]
        yb_ref[r0:r0 + rows, :] = yp[:, HALF:]

    bm = xa_ref.shape[0]
    step = bm // MOE_ROW_STEPS
    need = (bv_ref[i] + step - 1) // step
    for n in range(1, MOE_ROW_STEPS + 1):
        @pl.when(jnp.logical_and(used, need == n))
        def _(n=n):
            ffn(0, n * step)


def _moe_call(block_e, weight_e, block_row, block_valid, n_used, xa, xb, wgu, bgu, wd, bd):
    n_slots = xa.shape[0]
    bm = MOE_BM
    n_blocks = block_e.shape[0]
    grid_spec = pltpu.PrefetchScalarGridSpec(
        num_scalar_prefetch=5,
        grid=(n_blocks,),
        in_specs=[
            pl.BlockSpec((bm, HALF), lambda i, be, we, rb, bv, nu: (rb[i], 0)),
            pl.BlockSpec((bm, HALF), lambda i, be, we, rb, bv, nu: (rb[i], 0)),
            pl.BlockSpec((None, D_MODEL, 2 * D_FF_EXPERT), lambda i, be, we, rb, bv, nu: (we[i], 0, 0)),
            pl.BlockSpec((None, 1, 2 * D_FF_EXPERT), lambda i, be, we, rb, bv, nu: (be[i], 0, 0)),
            pl.BlockSpec((None, D_FF_EXPERT, D_MODEL), lambda i, be, we, rb, bv, nu: (we[i], 0, 0)),
            pl.BlockSpec((None, 1, D_MODEL), lambda i, be, we, rb, bv, nu: (be[i], 0, 0)),
        ],
        out_specs=[pl.BlockSpec((bm, HALF), lambda i, be, we, rb, bv, nu: (rb[i], 0))] * 2,
        scratch_shapes=[pltpu.VMEM((D_MODEL, 2 * D_FF_EXPERT), _BF),
                        pltpu.VMEM((D_FF_EXPERT, D_MODEL), _BF)],
    )
    return pl.pallas_call(
        _moe_kernel,
        grid_spec=grid_spec,
        out_shape=[jax.ShapeDtypeStruct((n_slots, HALF), jnp.int32)] * 2,
        compiler_params=pltpu.CompilerParams(dimension_semantics=("arbitrary",),
                                             vmem_limit_bytes=VMEM_LIMIT),
        name="expert_ffn",
    )(block_e, weight_e, block_row, block_valid, n_used, xa, xb, wgu, bgu, wd, bd)


def _combine_kernel(*refs):
    ya_refs, yb_refs = refs[:TOP_K], refs[TOP_K:2 * TOP_K]
    tw_ref, h1_ref, g2_ref, b2_ref, o_ref = refs[2 * TOP_K:]
    tw = tw_ref[...]
    parts = [jnp.zeros((tw.shape[0], HALF), _F32) for _ in range(4)]
    for k in range(TOP_K):
        a_lo, a_hi = _unpack_rows(ya_refs[k][...])
        b_lo, b_hi = _unpack_rows(yb_refs[k][...])
        w = tw[:, k:k + 1]
        parts = [p + w * y for p, y in zip(parts, (a_lo, b_lo, a_hi, b_hi))]
    ffn = jnp.concatenate(parts, axis=1)
    o_ref[...] = _layer_norm(DEEPNORM_ALPHA * h1_ref[...] + ffn, g2_ref[...], b2_ref[...])


def _combine_call(yga, ygb, tw, h1, g2, b2):
    t = h1.shape[0]
    tm = COMB_TM
    row = lambda w: pl.BlockSpec((tm, w), lambda i: (i, 0))
    krow = lambda k: pl.BlockSpec((tm, HALF), lambda i: (k * (t // tm) + i, 0))
    return pl.pallas_call(
        _combine_kernel,
        grid=(t // tm,),
        in_specs=[krow(k) for k in range(TOP_K)] * 2 + [row(128), row(D_MODEL),
                  _const_spec(g2.shape), _const_spec(b2.shape)],
        out_specs=row(D_MODEL),
        out_shape=jax.ShapeDtypeStruct((t, D_MODEL), _F32),
        compiler_params=pltpu.CompilerParams(dimension_semantics=("arbitrary",),
                                             vmem_limit_bytes=VMEM_LIMIT),
        name="combine_ln2",
    )(*([yga] * TOP_K + [ygb] * TOP_K), tw, h1, g2, b2)


def _prepare_weights(w_in, w_uq):
    o_q = 3 * CONV_DIM
    o_kv = o_q + Q_LORA_RANK
    o_pe = o_kv + KV_LORA_RANK
    o_gc = o_pe + QK_ROPE_DIM
    wg3 = w_in[:, :o_q]
    kpe = w_in[:, o_pe:o_gc]
    zpad = jnp.zeros((D_MODEL, ROPE_PAD - QK_ROPE_DIM), w_in.dtype)
    wlat = jnp.concatenate([w_in[:, o_q:o_pe], kpe, zpad], axis=1)
    wgates = w_in[:, o_gc:]
    wq = w_uq.reshape(Q_LORA_RANK, N_HEADS, QK_HEAD_DIM)
    wuqa = jnp.pad(wq, ((0, 0), (0, 0), (0, QK_PAD - QK_HEAD_DIM)))
    return (wg3.astype(_BF), wlat.astype(_BF), wgates.astype(_BF),
            wuqa.reshape(Q_LORA_RANK, N_HEADS * QK_PAD).astype(_BF))


def _rope_rows():
    half = QK_ROPE_DIM // 2
    inv_freq = ROPE_BASE ** (-jnp.arange(0, QK_ROPE_DIM, 2, dtype=_F32) / QK_ROPE_DIM)
    z = jnp.zeros((ROPE_PAD - QK_ROPE_DIM,), _F32)
    ones = jnp.ones((half,), _F32)
    rows = jnp.stack([jnp.concatenate([inv_freq] * 4),
                      jnp.concatenate([ones, ones, z]),
                      jnp.concatenate([-ones, ones, z]),
                      jnp.concatenate([ones * (np.pi / 2), ones * (np.pi / 2), z])])
    return jnp.pad(rows, ((0, 8 - rows.shape[0]), (0, 0)))


def _block_tables(counts, n_tok):
    bm = MOE_BM
    n_blocks = n_tok * TOP_K // bm + N_EXPERTS
    nblk = (counts + bm - 1) // bm
    end = jnp.cumsum(nblk)
    start = end - nblk
    n_used = end[-1]
    i = jnp.minimum(jnp.arange(n_blocks, dtype=jnp.int32), n_used - 1)
    block_e = (end[None, :] <= i[:, None]).sum(axis=1).astype(jnp.int32)
    first = jnp.sum(jnp.where(jnp.arange(N_EXPERTS)[None, :] == block_e[:, None], start[None, :], 0), axis=1)
    block_row = block_e * (n_tok // bm) + (i - first)
    fresh = jnp.concatenate([jnp.ones((1,), jnp.bool_), block_e[1:] != block_e[:-1]])
    later = jnp.where(block_e[None, :] > block_e[:, None], block_e[None, :], N_EXPERTS)
    nxt = jnp.min(later, axis=1)
    weight_e = jnp.where(jnp.logical_or(fresh, nxt == N_EXPERTS), block_e, nxt).astype(jnp.int32)
    cnt = jnp.sum(jnp.where(jnp.arange(N_EXPERTS)[None, :] == block_e[:, None], counts[None, :], 0), axis=1)
    block_valid = jnp.clip(cnt - (i - first) * bm, 0, bm).astype(jnp.int32)
    return (block_e, weight_e, block_row.astype(jnp.int32), block_valid,
            n_used.astype(jnp.int32).reshape(1))


def kernel(x, positions, w_in, conv_w, q_norm_g, w_uq, kv_norm_g, w_uk, w_uv, w_conv_branch,
           w_attn_branch, w_out, ln1_g, ln1_b, w_router, b_router, w_gate_up, b_gate_up, w_down,
           b_down, ln2_g, ln2_b):
    b, s, d = x.shape
    t = b * s
    x2d = x.reshape(t, d)
    pos = positions.reshape(t, 1).astype(jnp.int32)
    rope = _rope_rows()
    for l in range(DEPTH):
        wg3, wlat, wgates, wuqa = _prepare_weights(w_in[l], w_uq[l])
        q, kn, kpe, v, ycg, sga = _proj_call(
            x2d, pos, rope, wg3, wlat, wgates, conv_w[l], q_norm_g[l][None, :], kv_norm_g[l][None, :],
            wuqa, w_uk[l].astype(_BF), w_uv[l].astype(_BF), w_conv_branch[l].astype(_BF), s)
        attn = _attn_call(q, kn, kpe.reshape(b, s, -1), v)
        wr_t = w_router[l].T
        wr_hi = wr_t.astype(_BF)
        wr_lo = (wr_t - wr_hi.astype(_F32)).astype(_BF)
        h1, h1pa, h1pb, slots, tw_pad, counts = _post_call(
            attn, ycg, sga, x2d, w_attn_branch[l].astype(_BF), w_out[l].astype(_BF),
            ln1_g[l][None, :], ln1_b[l][None, :], jnp.concatenate([wr_hi, wr_lo], axis=0),
            jnp.broadcast_to(b_router[l][:, None], (N_EXPERTS, 128)))
        block_e, weight_e, block_row, block_valid, n_used = _block_tables(counts[:, 0].astype(jnp.int32), t)
        xa, xb = _sc_scatter_rows([h1pa, h1pb], slots, TOP_K, N_EXPERTS * t)
        ya, yb = _moe_call(block_e, weight_e, block_row, block_valid, n_used, xa, xb,
                           w_gate_up[l], b_gate_up[l][:, None, :], w_down[l], b_down[l][:, None, :])
        yga, ygb = _sc_gather_rows([ya, yb], slots, TOP_K)
        x2d = _combine_call(yga, ygb, tw_pad, h1, ln2_g[l][None, :], ln2_b[l][None, :])
    return x2d.reshape(b, s, d)
```

```python
import functools

import numpy as np
import jax
import jax.numpy as jnp
from jax import lax
from jax.experimental import pallas as pl
from jax.experimental.pallas import tpu as pltpu
from jax.experimental.pallas import tpu_sc as plsc

D_MODEL = 1024
CHUNK = 64
CONV_DIM = D_MODEL
CONV_WIDTH = 3
N_HEADS = 8
QK_NOPE_DIM = 128
QK_ROPE_DIM = 64
V_HEAD_DIM = 128
QK_HEAD_DIM = QK_NOPE_DIM + QK_ROPE_DIM
Q_LORA_RANK = 384
KV_LORA_RANK = 256
ROPE_BASE = 10000.0
N_EXPERTS = 32
TOP_K = 4
D_FF_EXPERT = D_MODEL
SWIGLU_LIMIT = 7.0
SWIGLU_ALPHA = 1.702
LN_EPS = 1e-5
RMS_EPS = 1e-6
DEPTH = 1
DEEPNORM_ALPHA = (2 * DEPTH) ** 0.25

QK_PAD = 256
ROPE_PAD = 128

PROJ_TM = 512
ATTN_TQ = 512
ATTN_TK = 512
ONES_ROWS = 16
POST_TM = 512
MOE_BM = 512
MOE_ROW_STEPS = 4
COMB_TM = 512
COMB_GROUPS = 2
SC_WINDOW = 128
PACKED = D_MODEL // 2
HALF = PACKED // 2
HALO = 8
NEG_BIG = -1e30
LOG2E = 1.4426950408889634
VMEM_LIMIT = 56 * 1024 * 1024

_BF = jnp.bfloat16
_F32 = jnp.float32


def _dot(a, b):
    return jnp.dot(a, b, preferred_element_type=_F32)


def _const_spec(shape):
    nd = len(shape)
    return pl.BlockSpec(shape, lambda *_: (0,) * nd, pipeline_mode=pl.Buffered(1))


def _pack_rows(a):
    half = a.shape[1] // 2
    lo = lax.bitcast_convert_type(a[:, :half].astype(_BF).astype(_F32), jnp.int32)
    hi = lax.bitcast_convert_type(a[:, half:].astype(_BF).astype(_F32), jnp.int32)
    return lax.shift_right_logical(lo, 16) | (hi & jnp.int32(-65536))


def _unpack_rows(p):
    lo = lax.bitcast_convert_type(lax.shift_left(p, 16), _F32)
    hi = lax.bitcast_convert_type(p & jnp.int32(-65536), _F32)
    return lo, hi


def _sc_mesh():
    return plsc.VectorSubcoreMesh(core_axis_name="core", subcore_axis_name="subcore")


def _sc_scatter_rows(tables, idx, n_copies, n_out):
    n_rows, width = tables[0].shape
    n_tab = len(tables)
    window = SC_WINDOW

    @functools.partial(pl.kernel, mesh=_sc_mesh(),
                       out_type=[jax.ShapeDtypeStruct((n_out, width), tables[0].dtype)] * n_tab)
    def scatter_kernel(*refs):
        idx_hbm = refs[n_tab]
        for src_hbm, out_hbm in zip(refs[:n_tab], refs[n_tab + 1:]):
            def body(src_vmem, idx_vmem, out_hbm=out_hbm):
                for k in range(n_copies):
                    pltpu.sync_copy(src_vmem, out_hbm.at[idx_vmem.at[k]])

            pltpu.emit_pipeline(
                body,
                grid=(n_rows // window,),
                in_specs=[pl.BlockSpec((window, width), lambda i: (i, 0)),
                          pl.BlockSpec((idx.shape[0], window), lambda i: (0, i))],
                out_specs=[],
                core_axis_name=("core", "subcore"),
                dimension_semantics=(pltpu.PARALLEL,),
            )(src_hbm, idx_hbm)

    return scatter_kernel(*tables, idx)


def _sc_gather_rows(tables, idx, n_copies, col0, n_rows):
    width = tables[0].shape[1]
    n_tab = len(tables)
    window = SC_WINDOW
    steps = n_rows // window
    first = col0 // window

    @functools.partial(pl.kernel, mesh=_sc_mesh(),
                       out_type=[jax.ShapeDtypeStruct((n_copies * n_rows, width), tables[0].dtype)] * n_tab)
    def gather_kernel(*refs):
        idx_hbm = refs[n_tab]
        for table_hbm, out_hbm in zip(refs[:n_tab], refs[n_tab + 1:]):
            def body(idx_vmem, out_vmem, table_hbm=table_hbm):
                pltpu.sync_copy(table_hbm.at[idx_vmem.at[0]], out_vmem)

            pltpu.emit_pipeline(
                body,
                grid=(n_copies, steps),
                in_specs=[pl.BlockSpec((1, window), lambda k, i: (k, first + i))],
                out_specs=[pl.BlockSpec((window, width), lambda k, i: (k * steps + i, 0))],
                core_axis_name=("core", "subcore"),
                dimension_semantics=(pltpu.PARALLEL, pltpu.PARALLEL),
            )(idx_hbm, out_hbm)

    return gather_kernel(*tables, idx)


def _proj_kernel(tiles_per_seq,
                 x_ref, pos_ref, rope_ref, wg3_ref, wlat_ref, wgates_ref, convw_ref,
                 qg_ref, kvg_ref, wuqa_ref, wuk_ref, wuv_ref, wcb_ref,
                 q_ref, kn_ref, kpe_ref, v_ref, ycg_ref, sga_ref, ubuf):
    i = pl.program_id(0)
    tm = x_ref.shape[0]
    xb = x_ref[...].astype(_BF)

    @pl.when(i % tiles_per_seq == 0)
    def _():
        ubuf[0:HALO, :] = jnp.zeros((HALO, CONV_DIM), _F32)

    g3 = _dot(xb, wg3_ref[...])
    u = g3[:, CONV_DIM:2 * CONV_DIM] * g3[:, 2 * CONV_DIM:]
    ubuf[HALO:HALO + tm, :] = u
    cw = convw_ref[...]
    conv = (cw[2:3, :] * u + cw[1:2, :] * ubuf[HALO - 1:HALO - 1 + tm, :]
            + cw[0:1, :] * ubuf[HALO - 2:HALO - 2 + tm, :])
    ubuf[0:HALO, :] = ubuf[tm:tm + HALO, :]
    yc = _dot((g3[:, :CONV_DIM] * conv).astype(_BF), wcb_ref[...])
    gates = _dot(xb, wgates_ref[...])
    ycg_ref[...] = (jax.nn.sigmoid(gates[:, :D_MODEL]) * yc).astype(_BF)
    sga_ref[...] = jax.nn.sigmoid(gates[:, D_MODEL:]).astype(_BF)

    lat = _dot(xb, wlat_ref[...])
    trig = jnp.sin(pos_ref[...].astype(_F32) * rope_ref[0:1, :] + rope_ref[3:4, :])
    cosb = trig * rope_ref[1:2, :]
    sinb = pltpu.roll(trig, QK_ROPE_DIM, axis=1) * rope_ref[2:3, :]
    q_lat = lat[:, :Q_LORA_RANK]
    rq = q_lat * lax.rsqrt(jnp.mean(q_lat * q_lat, axis=-1, keepdims=True) + RMS_EPS) * qg_ref[...]
    kv_lat = lat[:, Q_LORA_RANK:Q_LORA_RANK + KV_LORA_RANK]
    ckv = (kv_lat * lax.rsqrt(jnp.mean(kv_lat * kv_lat, axis=-1, keepdims=True) + RMS_EPS)
           * kvg_ref[...]).astype(_BF)
    o = Q_LORA_RANK + KV_LORA_RANK
    lane = lax.broadcasted_iota(jnp.int32, (tm, ROPE_PAD), 1)
    half = QK_ROPE_DIM // 2

    def swap_halves(u):
        return jnp.where(lane < half, pltpu.roll(u, ROPE_PAD - half, axis=1), pltpu.roll(u, half, axis=1))

    k_pe = lat[:, o:o + ROPE_PAD]
    kpe_ref[...] = (k_pe * cosb + swap_halves(k_pe) * sinb).astype(_BF)
    kn = _dot(ckv, wuk_ref[...]).astype(_BF)
    vv = _dot(ckv, wuv_ref[...]).astype(_BF)
    for h in range(N_HEADS):
        kn_ref[h] = kn[:, h * QK_NOPE_DIM:(h + 1) * QK_NOPE_DIM]
        v_ref[h] = vv[:, h * V_HEAD_DIM:(h + 1) * V_HEAD_DIM]

    rqb = rq.astype(_BF)
    qa = _dot(rqb, wuqa_ref[...])
    scale = QK_HEAD_DIM ** -0.5 * LOG2E
    for h in range(N_HEADS):
        lo = h * QK_PAD
        q_ref[h, :, :QK_NOPE_DIM] = (qa[:, lo:lo + QK_NOPE_DIM] * scale).astype(_BF)
        q_pe = qa[:, lo + QK_NOPE_DIM:lo + QK_PAD]
        hi = q_pe * cosb + swap_halves(q_pe) * sinb
        q_ref[h, :, QK_NOPE_DIM:] = (hi * scale).astype(_BF)


def _proj_call(x2d, pos, rope, wg3, wlat, wgates, convw, qg, kvg, wuqa, wuk, wuv, wcb, seq):
    t = x2d.shape[0]
    tm = PROJ_TM
    tps = seq // tm
    row = lambda w: pl.BlockSpec((tm, w), lambda i: (i, 0))
    heads = lambda w: pl.BlockSpec((None, N_HEADS, tm, w), lambda i: (i // tps, 0, i % tps, 0))
    head_major = lambda w: jax.ShapeDtypeStruct((t // seq, N_HEADS, seq, w), _BF)
    outs = [
        head_major(QK_PAD),
        head_major(QK_NOPE_DIM),
        jax.ShapeDtypeStruct((t, ROPE_PAD), _BF),
        head_major(V_HEAD_DIM),
        jax.ShapeDtypeStruct((t, D_MODEL), _BF),
        jax.ShapeDtypeStruct((t, D_MODEL), _BF),
    ]
    return pl.pallas_call(
        functools.partial(_proj_kernel, tps),
        grid=(t // tm,),
        in_specs=[row(D_MODEL), row(1)]
        + [_const_spec(a.shape)
           for a in (rope, wg3, wlat, wgates, convw, qg, kvg, wuqa, wuk, wuv, wcb)],
        out_specs=[heads(QK_PAD), heads(QK_NOPE_DIM), row(ROPE_PAD), heads(V_HEAD_DIM),
                   row(D_MODEL), row(D_MODEL)],
        out_shape=outs,
        scratch_shapes=[pltpu.VMEM((tm + HALO, CONV_DIM), _F32)],
        compiler_params=pltpu.CompilerParams(dimension_semantics=("arbitrary",),
                                             vmem_limit_bytes=VMEM_LIMIT),
        name="proj_conv_qkv",
    )(x2d, pos, rope, wg3, wlat, wgates, convw, qg, kvg, wuqa, wuk, wuv, wcb)


def _attn_kernel(q_ref, kn_ref, kpe_ref, v_ref, o_ref, kfull, vt, s0, s1, p0, p1, a0, a1, m_ref, acc):
    tq, tk = ATTN_TQ, ATTN_TK
    n_tiles = q_ref.shape[0] // tq
    s_bufs, p_bufs, a_bufs = (s0, s1), (p0, p1), (a0, a1)
    stages = [(tile, blk) for tile in range(n_tiles) for blk in range(tile + 1)]

    kfull[:, :QK_NOPE_DIM] = kn_ref[...]
    kfull[:, QK_NOPE_DIM:] = kpe_ref[...]
    vt[:V_HEAD_DIM, :] = v_ref[...].T
    vt[V_HEAD_DIM:, :] = jnp.ones((vt.shape[0] - V_HEAD_DIM, vt.shape[1]), _BF)

    def scores_t(t):
        tile, blk = stages[t]
        s_bufs[t % 2][...] = lax.dot_general(
            kfull[blk * tk:(blk + 1) * tk, :], q_ref[tile * tq:(tile + 1) * tq, :],
            (((1,), (1,)), ((), ())), preferred_element_type=_F32)

    key_chunk = lax.broadcasted_iota(jnp.int32, (tk, tq), 0) // CHUNK
    qry_chunk = lax.broadcasted_iota(jnp.int32, (tk, tq), 1) // CHUNK
    causal = key_chunk <= qry_chunk

    def softmax_t(t):
        tile, blk = stages[t]
        par = t % 2
        s_t = s_bufs[par][...]
        if blk == tile:
            s_t = jnp.where(causal, s_t, NEG_BIG)
        blk_max = jnp.max(s_t, axis=0, keepdims=True)
        if blk == 0:
            m_new = blk_max
        else:
            m_old = m_ref[...]
            m_new = jnp.maximum(m_old, blk_max)
            a_bufs[par][...] = jnp.exp2(m_old - m_new)
        m_ref[...] = m_new
        p_bufs[par][...] = jnp.exp2(s_t - m_new).astype(_BF)

    def accumulate(t):
        tile, blk = stages[t]
        par = t % 2
        pv = _dot(vt[:, blk * tk:(blk + 1) * tk], p_bufs[par][...])
        acc[...] = pv if blk == 0 else a_bufs[par][...] * acc[...] + pv
        if blk == tile:
            out_t = acc[:V_HEAD_DIM, :] / acc[V_HEAD_DIM:V_HEAD_DIM + 1, :]
            o_ref[tile * tq:(tile + 1) * tq, :] = out_t.T.astype(_BF)

    scores_t(0)
    for t in range(len(stages)):
        if t + 1 < len(stages):
            scores_t(t + 1)
        if t > 0:
            accumulate(t - 1)
        softmax_t(t)
    accumulate(len(stages) - 1)


def _attn_call(q, kn, kpe, v):
    b, _, s, _ = q.shape
    tq = ATTN_TQ
    assert ATTN_TQ == ATTN_TK and s % tq == 0
    per_head = lambda w: pl.BlockSpec((None, None, s, w), lambda bi, h: (bi, h, 0, 0))
    return pl.pallas_call(
        _attn_kernel,
        grid=(b, N_HEADS),
        in_specs=[per_head(QK_PAD), per_head(QK_NOPE_DIM),
                  pl.BlockSpec((None, s, ROPE_PAD), lambda bi, h: (bi, 0, 0)), per_head(V_HEAD_DIM)],
        out_specs=per_head(V_HEAD_DIM),
        out_shape=jax.ShapeDtypeStruct((b, N_HEADS, s, V_HEAD_DIM), _BF),
        scratch_shapes=[pltpu.VMEM((s, QK_PAD), _BF), pltpu.VMEM((V_HEAD_DIM + ONES_ROWS, s), _BF),
                        pltpu.VMEM((ATTN_TK, tq), _F32), pltpu.VMEM((ATTN_TK, tq), _F32),
                        pltpu.VMEM((ATTN_TK, tq), _BF), pltpu.VMEM((ATTN_TK, tq), _BF),
                        pltpu.VMEM((1, tq), _F32), pltpu.VMEM((1, tq), _F32),
                        pltpu.VMEM((1, tq), _F32),
                        pltpu.VMEM((V_HEAD_DIM + ONES_ROWS, tq), _F32)],
        compiler_params=pltpu.CompilerParams(dimension_semantics=("arbitrary", "arbitrary"),
                                             vmem_limit_bytes=VMEM_LIMIT),
        name="chunk_causal_attention",
    )(q, kn, kpe, v)


def _layer_norm(z, g, b):
    mu = jnp.mean(z, axis=-1, keepdims=True)
    zc = z - mu
    var = jnp.mean(zc * zc, axis=-1, keepdims=True)
    return zc * lax.rsqrt(var + LN_EPS) * g + b


def _post_kernel(expert_cap, attn_ref, ycg_ref, sga_ref, x_ref, wab_ref, wout_ref, g1_ref, b1_ref,
                 wr_ref, br_ref, tri_ref, h1_ref, h1pa_ref, h1pb_ref, slot_ref, tw_ref, cnt_ref, run):
    tm = x_ref.shape[0]

    @pl.when(pl.program_id(0) == 0)
    def _():
        run[...] = jnp.zeros(run.shape, _F32)

    attn = jnp.concatenate([attn_ref[h] for h in range(N_HEADS)], axis=1)
    y_attn = _dot(attn, wab_ref[...])
    merged = ycg_ref[...].astype(_F32) + sga_ref[...].astype(_F32) * y_attn
    mix = _dot(merged.astype(_BF), wout_ref[...])
    h1 = _layer_norm(DEEPNORM_ALPHA * x_ref[...] + mix, g1_ref[...], b1_ref[...])
    h1_ref[...] = h1
    h1p = _pack_rows(h1)
    h1pa_ref[...] = h1p[:, :HALF]
    h1pb_ref[...] = h1p[:, HALF:]

    nt = (((1,), (1,)), ((), ()))
    h_hi = h1.astype(_BF)
    h_lo = (h1 - h_hi.astype(_F32)).astype(_BF)
    both = lax.dot_general(wr_ref[...], h_hi, nt, preferred_element_type=_F32)
    logits = (both[:N_EXPERTS] + both[N_EXPERTS:] + br_ref[:, 0:1]
              + lax.dot_general(wr_ref[:N_EXPERTS, :], h_lo, nt, preferred_element_type=_F32))
    expert = lax.broadcasted_iota(jnp.int32, (N_EXPERTS, tm), 0)
    work = logits
    chosen = jnp.zeros((N_EXPERTS, tm), jnp.bool_)
    sels, exps = [], []
    for k in range(TOP_K):
        mx = jnp.max(work, axis=0, keepdims=True)
        sel = jnp.min(jnp.where(work == mx, expert, N_EXPERTS), axis=0, keepdims=True)
        hit = expert == sel
        sels.append(sel)
        exps.append(jnp.exp(mx - (mx if k == 0 else top0)))
        if k == 0:
            top0 = mx
        chosen = jnp.logical_or(chosen, hit)
        work = jnp.where(hit, -jnp.inf, work)
    denom = exps[0] + exps[1] + exps[2] + exps[3]

    onehot = chosen.astype(_F32)
    before = _dot(onehot.astype(_BF), tri_ref[...]) + run[:, 0:1]
    rows = []
    for k in range(TOP_K):
        rank = jnp.sum(jnp.where(expert == sels[k], before, 0.0), axis=0, keepdims=True)
        rows.append(sels[k] * expert_cap + rank.astype(jnp.int32))
    slot_ref[...] = jnp.concatenate(rows + [jnp.zeros((8 - TOP_K, tm), jnp.int32)], axis=0)
    run[...] = run[...] + jnp.sum(onehot, axis=1, keepdims=True)
    cnt_ref[...] = run[...]
    tw_t = jnp.concatenate([e / denom for e in exps] + [jnp.zeros((128 - TOP_K, tm), _F32)], axis=0)
    tw_ref[...] = tw_t.T


def _post_call(attn, ycg, sga, x2d, wab, wout, g1, b1, wr, br):
    t = x2d.shape[0]
    tm = POST_TM
    row = lambda w: pl.BlockSpec((tm, w), lambda i: (i, 0))
    tps = attn.shape[2] // tm
    tri = (jnp.arange(tm)[:, None] < jnp.arange(tm)[None, :]).astype(_BF)
    outs = [
        jax.ShapeDtypeStruct((t, D_MODEL), _F32),
        jax.ShapeDtypeStruct((t, HALF), jnp.int32),
        jax.ShapeDtypeStruct((t, HALF), jnp.int32),
        jax.ShapeDtypeStruct((8, t), jnp.int32),
        jax.ShapeDtypeStruct((t, 128), _F32),
        jax.ShapeDtypeStruct((N_EXPERTS, 128), _F32),
    ]
    out_specs = [row(D_MODEL), row(HALF), row(HALF), pl.BlockSpec((8, tm), lambda i: (0, i)),
                 row(128), pl.BlockSpec((N_EXPERTS, 128), lambda i: (0, 0))]
    return pl.pallas_call(
        functools.partial(_post_kernel, t),
        grid=(t // tm,),
        in_specs=[pl.BlockSpec((None, N_HEADS, tm, V_HEAD_DIM), lambda i: (i // tps, 0, i % tps, 0))]
        + [row(D_MODEL)] * 3
        + [_const_spec(a.shape) for a in (wab, wout, g1, b1, wr, br, tri)],
        out_specs=out_specs,
        out_shape=outs,
        scratch_shapes=[pltpu.VMEM((N_EXPERTS, 128), _F32)],
        compiler_params=pltpu.CompilerParams(dimension_semantics=("arbitrary",),
                                             vmem_limit_bytes=VMEM_LIMIT),
        name="merge_out_ln1_router",
    )(attn, ycg, sga, x2d, wab, wout, g1, b1, wr, br, tri)


def _moe_kernel(be_ref, we_ref, rb_ref, bv_ref, nused_ref, xa_ref, xb_ref, wgu_ref, bgu_ref, wd_ref, bd_ref,
                ya_ref, yb_ref, wgu_bf, wd_bf):
    i = pl.program_id(0)
    prev = be_ref[jnp.maximum(i - 1, 0)]
    fresh = jnp.logical_or(i == 0, be_ref[i] != prev)
    used = i < nused_ref[0]

    @pl.when(jnp.logical_and(fresh, used))
    def _():
        wgu_bf[...] = wgu_ref[...].astype(_BF)
        wd_bf[...] = wd_ref[...].astype(_BF)

    def ffn(r0, rows):
        a_lo, a_hi = _unpack_rows(xa_ref[r0:r0 + rows, :])
        b_lo, b_hi = _unpack_rows(xb_ref[r0:r0 + rows, :])
        xs = jnp.concatenate([a_lo, b_lo, a_hi, b_hi], axis=1).astype(_BF)
        gu = _dot(xs, wgu_bf[...]) + bgu_ref[...]
        gate = jnp.minimum(gu[:, :D_FF_EXPERT], SWIGLU_LIMIT)
        up = jnp.clip(gu[:, D_FF_EXPERT:], -SWIGLU_LIMIT, SWIGLU_LIMIT)
        hid = (up + 1.0) * (gate * jax.nn.sigmoid(SWIGLU_ALPHA * gate))
        yp = _pack_rows(_dot(hid.astype(_BF), wd_bf[...]) + bd_ref[...])
        ya_ref[r0:r0 + rows, :] = yp[:, :HALF]
        yb_ref[r0:r0 + rows, :] = yp[:, HALF:]

    bm = xa_ref.shape[0]
    step = bm // MOE_ROW_STEPS
    need = (bv_ref[i] + step - 1) // step
    for n in range(1, MOE_ROW_STEPS + 1):
        @pl.when(jnp.logical_and(used, need == n))
        def _(n=n):
            ffn(0, n * step)


def _moe_call(block_e, weight_e, block_row, block_valid, n_used, xa, xb, wgu, bgu, wd, bd):
    n_slots = xa.shape[0]
    bm = MOE_BM
    n_blocks = block_e.shape[0]
    grid_spec = pltpu.PrefetchScalarGridSpec(
        num_scalar_prefetch=5,
        grid=(n_blocks,),
        in_specs=[
            pl.BlockSpec((bm, HALF), lambda i, be, we, rb, bv, nu: (rb[i], 0)),
            pl.BlockSpec((bm, HALF), lambda i, be, we, rb, bv, nu: (rb[i], 0)),
            pl.BlockSpec((None, D_MODEL, 2 * D_FF_EXPERT), lambda i, be, we, rb, bv, nu: (we[i], 0, 0)),
            pl.BlockSpec((None, 1, 2 * D_FF_EXPERT), lambda i, be, we, rb, bv, nu: (be[i], 0, 0)),
            pl.BlockSpec((None, D_FF_EXPERT, D_MODEL), lambda i, be, we, rb, bv, nu: (we[i], 0, 0)),
            pl.BlockSpec((None, 1, D_MODEL), lambda i, be, we, rb, bv, nu: (be[i], 0, 0)),
        ],
        out_specs=[pl.BlockSpec((bm, HALF), lambda i, be, we, rb, bv, nu: (rb[i], 0))] * 2,
        scratch_shapes=[pltpu.VMEM((D_MODEL, 2 * D_FF_EXPERT), _BF),
                        pltpu.VMEM((D_FF_EXPERT, D_MODEL), _BF)],
    )
    return pl.pallas_call(
        _moe_kernel,
        grid_spec=grid_spec,
        out_shape=[jax.ShapeDtypeStruct((n_slots, HALF), jnp.int32)] * 2,
        compiler_params=pltpu.CompilerParams(dimension_semantics=("arbitrary",),
                                             vmem_limit_bytes=VMEM_LIMIT),
        name="expert_ffn",
    )(block_e, weight_e, block_row, block_valid, n_used, xa, xb, wgu, bgu, wd, bd)


def _combine_kernel(*refs):
    ya_refs, yb_refs = refs[:TOP_K], refs[TOP_K:2 * TOP_K]
    tw_ref, h1_ref, g2_ref, b2_ref = refs[2 * TOP_K:2 * TOP_K + 4]
    o_ref = refs[-1]
    tw = tw_ref[...]
    parts = [jnp.zeros((tw.shape[0], HALF), _F32) for _ in range(4)]
    for k in range(TOP_K):
        a_lo, a_hi = _unpack_rows(ya_refs[k][...])
        b_lo, b_hi = _unpack_rows(yb_refs[k][...])
        w = tw[:, k:k + 1]
        parts = [p + w * y for p, y in zip(parts, (a_lo, b_lo, a_hi, b_hi))]
    ffn = jnp.concatenate(parts, axis=1)
    o_ref[...] = _layer_norm(DEEPNORM_ALPHA * h1_ref[...] + ffn, g2_ref[...], b2_ref[...])


def _combine_call(yga, ygb, tw, h1, g2, b2, row0, out_prev):
    t = h1.shape[0]
    n = yga.shape[0] // TOP_K
    tm = COMB_TM
    off = row0 // tm
    row = lambda w: pl.BlockSpec((tm, w), lambda i: (off + i, 0))
    krow = lambda k: pl.BlockSpec((tm, HALF), lambda i: (k * (n // tm) + i, 0))
    prev = [] if out_prev is None else [out_prev]
    return pl.pallas_call(
        _combine_kernel,
        grid=(n // tm,),
        in_specs=[krow(k) for k in range(TOP_K)] * 2 + [row(128), row(D_MODEL),
                  _const_spec(g2.shape), _const_spec(b2.shape)]
        + [pl.BlockSpec(memory_space=pl.ANY)] * len(prev),
        out_specs=row(D_MODEL),
        out_shape=jax.ShapeDtypeStruct((t, D_MODEL), _F32),
        input_output_aliases={2 * TOP_K + 4: 0} if prev else {},
        compiler_params=pltpu.CompilerParams(dimension_semantics=("arbitrary",),
                                             vmem_limit_bytes=VMEM_LIMIT),
        name="combine_ln2",
    )(*([yga] * TOP_K + [ygb] * TOP_K), tw, h1, g2, b2, *prev)


def _prepare_weights(w_in, w_uq):
    o_q = 3 * CONV_DIM
    o_kv = o_q + Q_LORA_RANK
    o_pe = o_kv + KV_LORA_RANK
    o_gc = o_pe + QK_ROPE_DIM
    wg3 = w_in[:, :o_q]
    kpe = w_in[:, o_pe:o_gc]
    zpad = jnp.zeros((D_MODEL, ROPE_PAD - QK_ROPE_DIM), w_in.dtype)
    wlat = jnp.concatenate([w_in[:, o_q:o_pe], kpe, zpad], axis=1)
    wgates = w_in[:, o_gc:]
    wq = w_uq.reshape(Q_LORA_RANK, N_HEADS, QK_HEAD_DIM)
    wuqa = jnp.pad(wq, ((0, 0), (0, 0), (0, QK_PAD - QK_HEAD_DIM)))
    return (wg3.astype(_BF), wlat.astype(_BF), wgates.astype(_BF),
            wuqa.reshape(Q_LORA_RANK, N_HEADS * QK_PAD).astype(_BF))


def _rope_rows():
    half = QK_ROPE_DIM // 2
    inv_freq = ROPE_BASE ** (-jnp.arange(0, QK_ROPE_DIM, 2, dtype=_F32) / QK_ROPE_DIM)
    z = jnp.zeros((ROPE_PAD - QK_ROPE_DIM,), _F32)
    ones = jnp.ones((half,), _F32)
    rows = jnp.stack([jnp.concatenate([inv_freq] * 4),
                      jnp.concatenate([ones, ones, z]),
                      jnp.concatenate([-ones, ones, z]),
                      jnp.concatenate([ones * (np.pi / 2), ones * (np.pi / 2), z])])
    return jnp.pad(rows, ((0, 8 - rows.shape[0]), (0, 0)))


def _block_tables(counts, n_tok):
    bm = MOE_BM
    n_blocks = n_tok * TOP_K // bm + N_EXPERTS
    nblk = (counts + bm - 1) // bm
    end = jnp.cumsum(nblk)
    start = end - nblk
    n_used = end[-1]
    i = jnp.minimum(jnp.arange(n_blocks, dtype=jnp.int32), n_used - 1)
    block_e = (end[None, :] <= i[:, None]).sum(axis=1).astype(jnp.int32)
    first = jnp.sum(jnp.where(jnp.arange(N_EXPERTS)[None, :] == block_e[:, None], start[None, :], 0), axis=1)
    block_row = block_e * (n_tok // bm) + (i - first)
    fresh = jnp.concatenate([jnp.ones((1,), jnp.bool_), block_e[1:] != block_e[:-1]])
    later = jnp.where(block_e[None, :] > block_e[:, None], block_e[None, :], N_EXPERTS)
    nxt = jnp.min(later, axis=1)
    weight_e = jnp.where(jnp.logical_or(fresh, nxt == N_EXPERTS), block_e, nxt).astype(jnp.int32)
    cnt = jnp.sum(jnp.where(jnp.arange(N_EXPERTS)[None, :] == block_e[:, None], counts[None, :], 0), axis=1)
    block_valid = jnp.clip(cnt - (i - first) * bm, 0, bm).astype(jnp.int32)
    return (block_e, weight_e, block_row.astype(jnp.int32), block_valid,
            n_used.astype(jnp.int32).reshape(1))


def kernel(x, positions, w_in, conv_w, q_norm_g, w_uq, kv_norm_g, w_uk, w_uv, w_conv_branch,
           w_attn_branch, w_out, ln1_g, ln1_b, w_router, b_router, w_gate_up, b_gate_up, w_down,
           b_down, ln2_g, ln2_b):
    b, s, d = x.shape
    t = b * s
    x2d = x.reshape(t, d)
    pos = positions.reshape(t, 1).astype(jnp.int32)
    rope = _rope_rows()
    for l in range(DEPTH):
        wg3, wlat, wgates, wuqa = _prepare_weights(w_in[l], w_uq[l])
        q, kn, kpe, v, ycg, sga = _proj_call(
            x2d, pos, rope, wg3, wlat, wgates, conv_w[l], q_norm_g[l][None, :], kv_norm_g[l][None, :],
            wuqa, w_uk[l].astype(_BF), w_uv[l].astype(_BF), w_conv_branch[l].astype(_BF), s)
        attn = _attn_call(q, kn, kpe.reshape(b, s, -1), v)
        wr_t = w_router[l].T
        wr_hi = wr_t.astype(_BF)
        wr_lo = (wr_t - wr_hi.astype(_F32)).astype(_BF)
        h1, h1pa, h1pb, slots, tw_pad, counts = _post_call(
            attn, ycg, sga, x2d, w_attn_branch[l].astype(_BF), w_out[l].astype(_BF),
            ln1_g[l][None, :], ln1_b[l][None, :], jnp.concatenate([wr_hi, wr_lo], axis=0),
            jnp.broadcast_to(b_router[l][:, None], (N_EXPERTS, 128)))
        block_e, weight_e, block_row, block_valid, n_used = _block_tables(counts[:, 0].astype(jnp.int32), t)
        xa, xb = _sc_scatter_rows([h1pa, h1pb], slots, TOP_K, N_EXPERTS * t)
        ya, yb = _moe_call(block_e, weight_e, block_row, block_valid, n_used, xa, xb,
                           w_gate_up[l], b_gate_up[l][:, None, :], w_down[l], b_down[l][:, None, :])
        out = None
        for g in range(COMB_GROUPS):
            n = t // COMB_GROUPS
            yga, ygb = _sc_gather_rows([ya, yb], slots, TOP_K, g * n, n)
            out = _combine_call(yga, ygb, tw_pad, h1, ln2_g[l][None, :], ln2_b[l][None, :], g * n, out)
        x2d = out
    return x2d.reshape(b, s, d)
```

```python
import functools

import numpy as np
import jax
import jax.numpy as jnp
from jax import lax
from jax.experimental import pallas as pl
from jax.experimental.pallas import tpu as pltpu
from jax.experimental.pallas import tpu_sc as plsc

D_MODEL = 1024
CHUNK = 64
CONV_DIM = D_MODEL
CONV_WIDTH = 3
N_HEADS = 8
QK_NOPE_DIM = 128
QK_ROPE_DIM = 64
V_HEAD_DIM = 128
QK_HEAD_DIM = QK_NOPE_DIM + QK_ROPE_DIM
Q_LORA_RANK = 384
KV_LORA_RANK = 256
ROPE_BASE = 10000.0
N_EXPERTS = 32
TOP_K = 4
D_FF_EXPERT = D_MODEL
SWIGLU_LIMIT = 7.0
SWIGLU_ALPHA = 1.702
LN_EPS = 1e-5
RMS_EPS = 1e-6
DEPTH = 1
DEEPNORM_ALPHA = (2 * DEPTH) ** 0.25

QK_PAD = 256
ROPE_PAD = 128

PROJ_TM = 512
ATTN_TQ = 512
ATTN_TK = 512
ONES_ROWS = 16
POST_TM = 512
MOE_BM = 512
MOE_ROW_STEPS = 4
COMB_TM = 512
SC_WINDOW = 128
PACKED = D_MODEL // 2
HALF = PACKED // 2
HALO = 8
NEG_BIG = -1e30
LOG2E = 1.4426950408889634
VMEM_LIMIT = 56 * 1024 * 1024

_BF = jnp.bfloat16
_F32 = jnp.float32


def _dot(a, b):
    return jnp.dot(a, b, preferred_element_type=_F32)


def _const_spec(shape):
    nd = len(shape)
    return pl.BlockSpec(shape, lambda *_: (0,) * nd, pipeline_mode=pl.Buffered(1))


def _pack_rows(a):
    half = a.shape[1] // 2
    lo = lax.bitcast_convert_type(a[:, :half].astype(_BF).astype(_F32), jnp.int32)
    hi = lax.bitcast_convert_type(a[:, half:].astype(_BF).astype(_F32), jnp.int32)
    return lax.shift_right_logical(lo, 16) | (hi & jnp.int32(-65536))


def _unpack_rows(p):
    lo = lax.bitcast_convert_type(lax.shift_left(p, 16), _F32)
    hi = lax.bitcast_convert_type(p & jnp.int32(-65536), _F32)
    return lo, hi


def _sc_mesh():
    return plsc.VectorSubcoreMesh(core_axis_name="core", subcore_axis_name="subcore")


def _sc_scatter_rows(tables, idx, n_copies, n_out):
    n_rows, width = tables[0].shape
    n_tab = len(tables)
    window = SC_WINDOW

    @functools.partial(pl.kernel, mesh=_sc_mesh(),
                       out_type=[jax.ShapeDtypeStruct((n_out, width), tables[0].dtype)] * n_tab)
    def scatter_kernel(*refs):
        idx_hbm = refs[n_tab]
        for src_hbm, out_hbm in zip(refs[:n_tab], refs[n_tab + 1:]):
            def body(src_vmem, idx_vmem, out_hbm=out_hbm):
                for k in range(n_copies):
                    pltpu.sync_copy(src_vmem, out_hbm.at[idx_vmem.at[k]])

            pltpu.emit_pipeline(
                body,
                grid=(n_rows // window,),
                in_specs=[pl.BlockSpec((window, width), lambda i: (i, 0)),
                          pl.BlockSpec((idx.shape[0], window), lambda i: (0, i))],
                out_specs=[],
                core_axis_name=("core", "subcore"),
                dimension_semantics=(pltpu.PARALLEL,),
            )(src_hbm, idx_hbm)

    return scatter_kernel(*tables, idx)


def _sc_gather_rows(tables, idx, n_copies):
    n_rows = idx.shape[1]
    width = tables[0].shape[1]
    n_tab = len(tables)
    window = SC_WINDOW
    steps = n_rows // window

    @functools.partial(pl.kernel, mesh=_sc_mesh(),
                       out_type=[jax.ShapeDtypeStruct((n_copies * n_rows, width), tables[0].dtype)] * n_tab)
    def gather_kernel(*refs):
        idx_hbm = refs[n_tab]
        for table_hbm, out_hbm in zip(refs[:n_tab], refs[n_tab + 1:]):
            def body(idx_vmem, out_vmem, table_hbm=table_hbm):
                pltpu.sync_copy(table_hbm.at[idx_vmem.at[0]], out_vmem)

            pltpu.emit_pipeline(
                body,
                grid=(n_copies, steps),
                in_specs=[pl.BlockSpec((1, window), lambda k, i: (k, i))],
                out_specs=[pl.BlockSpec((window, width), lambda k, i: (k * steps + i, 0))],
                core_axis_name=("core", "subcore"),
                dimension_semantics=(pltpu.PARALLEL, pltpu.PARALLEL),
            )(idx_hbm, out_hbm)

    return gather_kernel(*tables, idx)


def _proj_kernel(tiles_per_seq,
                 x_ref, pos_ref, rope_ref, wg3_ref, wlat_ref, wgates_ref, convw_ref,
                 qg_ref, kvg_ref, wuqa_ref, wuk_ref, wuv_ref, wcb_ref,
                 q_ref, kn_ref, kpe_ref, v_ref, ycg_ref, sga_ref, ubuf):
    i = pl.program_id(0)
    tm = x_ref.shape[0]
    xb = x_ref[...].astype(_BF)

    @pl.when(i % tiles_per_seq == 0)
    def _():
        ubuf[0:HALO, :] = jnp.zeros((HALO, CONV_DIM), _F32)

    g3 = _dot(xb, wg3_ref[...])
    u = g3[:, CONV_DIM:2 * CONV_DIM] * g3[:, 2 * CONV_DIM:]
    ubuf[HALO:HALO + tm, :] = u
    cw = convw_ref[...]
    conv = (cw[2:3, :] * u + cw[1:2, :] * ubuf[HALO - 1:HALO - 1 + tm, :]
            + cw[0:1, :] * ubuf[HALO - 2:HALO - 2 + tm, :])
    ubuf[0:HALO, :] = ubuf[tm:tm + HALO, :]
    yc = _dot((g3[:, :CONV_DIM] * conv).astype(_BF), wcb_ref[...])
    gates = _dot(xb, wgates_ref[...])
    ycg_ref[...] = (jax.nn.sigmoid(gates[:, :D_MODEL]) * yc).astype(_BF)
    sga_ref[...] = jax.nn.sigmoid(gates[:, D_MODEL:]).astype(_BF)

    lat = _dot(xb, wlat_ref[...])
    trig = jnp.sin(pos_ref[...].astype(_F32) * rope_ref[0:1, :] + rope_ref[3:4, :])
    cosb = trig * rope_ref[1:2, :]
    sinb = pltpu.roll(trig, QK_ROPE_DIM, axis=1) * rope_ref[2:3, :]
    q_lat = lat[:, :Q_LORA_RANK]
    rq = q_lat * lax.rsqrt(jnp.mean(q_lat * q_lat, axis=-1, keepdims=True) + RMS_EPS) * qg_ref[...]
    kv_lat = lat[:, Q_LORA_RANK:Q_LORA_RANK + KV_LORA_RANK]
    ckv = (kv_lat * lax.rsqrt(jnp.mean(kv_lat * kv_lat, axis=-1, keepdims=True) + RMS_EPS)
           * kvg_ref[...]).astype(_BF)
    o = Q_LORA_RANK + KV_LORA_RANK
    lane = lax.broadcasted_iota(jnp.int32, (tm, ROPE_PAD), 1)
    half = QK_ROPE_DIM // 2

    def swap_halves(u):
        return jnp.where(lane < half, pltpu.roll(u, ROPE_PAD - half, axis=1), pltpu.roll(u, half, axis=1))

    k_pe = lat[:, o:o + ROPE_PAD]
    kpe_ref[...] = (k_pe * cosb + swap_halves(k_pe) * sinb).astype(_BF)
    kn = _dot(ckv, wuk_ref[...]).astype(_BF)
    vv = _dot(ckv, wuv_ref[...]).astype(_BF)
    for h in range(N_HEADS):
        kn_ref[h] = kn[:, h * QK_NOPE_DIM:(h + 1) * QK_NOPE_DIM]
        v_ref[h] = vv[:, h * V_HEAD_DIM:(h + 1) * V_HEAD_DIM]

    rqb = rq.astype(_BF)
    qa = _dot(rqb, wuqa_ref[...])
    scale = QK_HEAD_DIM ** -0.5 * LOG2E
    for h in range(N_HEADS):
        lo = h * QK_PAD
        q_ref[h, :, :QK_NOPE_DIM] = (qa[:, lo:lo + QK_NOPE_DIM] * scale).astype(_BF)
        q_pe = qa[:, lo + QK_NOPE_DIM:lo + QK_PAD]
        hi = q_pe * cosb + swap_halves(q_pe) * sinb
        q_ref[h, :, QK_NOPE_DIM:] = (hi * scale).astype(_BF)


def _proj_call(x2d, pos, rope, wg3, wlat, wgates, convw, qg, kvg, wuqa, wuk, wuv, wcb, seq):
    t = x2d.shape[0]
    tm = PROJ_TM
    tps = seq // tm
    row = lambda w: pl.BlockSpec((tm, w), lambda i: (i, 0))
    heads = lambda w: pl.BlockSpec((None, N_HEADS, tm, w), lambda i: (i // tps, 0, i % tps, 0))
    head_major = lambda w: jax.ShapeDtypeStruct((t // seq, N_HEADS, seq, w), _BF)
    outs = [
        head_major(QK_PAD),
        head_major(QK_NOPE_DIM),
        jax.ShapeDtypeStruct((t, ROPE_PAD), _BF),
        head_major(V_HEAD_DIM),
        jax.ShapeDtypeStruct((t, D_MODEL), _BF),
        jax.ShapeDtypeStruct((t, D_MODEL), _BF),
    ]
    return pl.pallas_call(
        functools.partial(_proj_kernel, tps),
        grid=(t // tm,),
        in_specs=[row(D_MODEL), row(1)]
        + [_const_spec(a.shape)
           for a in (rope, wg3, wlat, wgates, convw, qg, kvg, wuqa, wuk, wuv, wcb)],
        out_specs=[heads(QK_PAD), heads(QK_NOPE_DIM), row(ROPE_PAD), heads(V_HEAD_DIM),
                   row(D_MODEL), row(D_MODEL)],
        out_shape=outs,
        scratch_shapes=[pltpu.VMEM((tm + HALO, CONV_DIM), _F32)],
        compiler_params=pltpu.CompilerParams(dimension_semantics=("arbitrary",),
                                             vmem_limit_bytes=VMEM_LIMIT),
        name="proj_conv_qkv",
    )(x2d, pos, rope, wg3, wlat, wgates, convw, qg, kvg, wuqa, wuk, wuv, wcb)


def _attn_kernel(q_ref, kn_ref, kpe_ref, v_ref, o_ref, kfull, vt, s0, s1, p0, p1, a0, a1, m_ref, acc):
    tq, tk = ATTN_TQ, ATTN_TK
    n_tiles = q_ref.shape[0] // tq
    s_bufs, p_bufs, a_bufs = (s0, s1), (p0, p1), (a0, a1)
    stages = [(tile, blk) for tile in range(n_tiles) for blk in range(tile + 1)]

    kfull[:, :QK_NOPE_DIM] = kn_ref[...]
    kfull[:, QK_NOPE_DIM:] = kpe_ref[...]
    vt[:V_HEAD_DIM, :] = v_ref[...].T
    vt[V_HEAD_DIM:, :] = jnp.ones((vt.shape[0] - V_HEAD_DIM, vt.shape[1]), _BF)

    nt = (((1,), (1,)), ((), ()))
    hk, hq = tk // 2, tq // 2

    def scores_t(t):
        tile, blk = stages[t]
        s_ref = s_bufs[t % 2]
        k0, q0 = blk * tk, tile * tq
        if blk == tile:
            s_ref[0:hk, :] = lax.dot_general(kfull[k0:k0 + hk, :], q_ref[q0:q0 + tq, :], nt,
                                             preferred_element_type=_F32)
            s_ref[hk:tk, hq:tq] = lax.dot_general(kfull[k0 + hk:k0 + tk, :], q_ref[q0 + hq:q0 + tq, :], nt,
                                                  preferred_element_type=_F32)
        else:
            s_ref[...] = lax.dot_general(kfull[k0:k0 + tk, :], q_ref[q0:q0 + tq, :], nt,
                                         preferred_element_type=_F32)

    key_chunk = lax.broadcasted_iota(jnp.int32, (tk, tq), 0) // CHUNK
    qry_chunk = lax.broadcasted_iota(jnp.int32, (tk, tq), 1) // CHUNK
    causal = key_chunk <= qry_chunk

    def softmax_t(t):
        tile, blk = stages[t]
        par = t % 2
        if blk == tile:
            top = jnp.where(causal[0:hk, :], s_bufs[par][0:hk, :], NEG_BIG)
            bot = jnp.where(causal[hk:tk, hq:tq], s_bufs[par][hk:tk, hq:tq], NEG_BIG)
            bot_max = jnp.concatenate([jnp.full((1, hq), NEG_BIG, _F32),
                                       jnp.max(bot, axis=0, keepdims=True)], axis=1)
            blk_max = jnp.maximum(jnp.max(top, axis=0, keepdims=True), bot_max)
        else:
            s_t = s_bufs[par][...]
            blk_max = jnp.max(s_t, axis=0, keepdims=True)
        if blk == 0:
            m_new = blk_max
        else:
            m_old = m_ref[...]
            m_new = jnp.maximum(m_old, blk_max)
            a_bufs[par][...] = jnp.exp2(m_old - m_new)
        m_ref[...] = m_new
        if blk == tile:
            p_bufs[par][0:hk, :] = jnp.exp2(top - m_new).astype(_BF)
            p_bufs[par][hk:tk, hq:tq] = jnp.exp2(bot - m_new[:, hq:tq]).astype(_BF)
        else:
            p_bufs[par][...] = jnp.exp2(s_t - m_new).astype(_BF)

    def accumulate(t):
        tile, blk = stages[t]
        par = t % 2
        k0 = blk * tk
        if blk == tile:
            pv = jnp.concatenate([_dot(vt[:, k0:k0 + hk], p_bufs[par][0:hk, 0:hq]),
                                  _dot(vt[:, k0:k0 + tk], p_bufs[par][:, hq:tq])], axis=1)
        else:
            pv = _dot(vt[:, k0:k0 + tk], p_bufs[par][...])
        acc[...] = pv if blk == 0 else a_bufs[par][...] * acc[...] + pv
        if blk == tile:
            out_t = acc[:V_HEAD_DIM, :] / acc[V_HEAD_DIM:V_HEAD_DIM + 1, :]
            o_ref[tile * tq:(tile + 1) * tq, :] = out_t.T.astype(_BF)

    scores_t(0)
    for t in range(len(stages)):
        if t + 1 < len(stages):
            scores_t(t + 1)
        if t > 0:
            accumulate(t - 1)
        softmax_t(t)
    accumulate(len(stages) - 1)


def _attn_call(q, kn, kpe, v):
    b, _, s, _ = q.shape
    tq = ATTN_TQ
    assert ATTN_TQ == ATTN_TK and s % tq == 0
    per_head = lambda w: pl.BlockSpec((None, None, s, w), lambda bi, h: (bi, h, 0, 0))
    return pl.pallas_call(
        _attn_kernel,
        grid=(b, N_HEADS),
        in_specs=[per_head(QK_PAD), per_head(QK_NOPE_DIM),
                  pl.BlockSpec((None, s, ROPE_PAD), lambda bi, h: (bi, 0, 0)), per_head(V_HEAD_DIM)],
        out_specs=per_head(V_HEAD_DIM),
        out_shape=jax.ShapeDtypeStruct((b, N_HEADS, s, V_HEAD_DIM), _BF),
        scratch_shapes=[pltpu.VMEM((s, QK_PAD), _BF), pltpu.VMEM((V_HEAD_DIM + ONES_ROWS, s), _BF),
                        pltpu.VMEM((ATTN_TK, tq), _F32), pltpu.VMEM((ATTN_TK, tq), _F32),
                        pltpu.VMEM((ATTN_TK, tq), _BF), pltpu.VMEM((ATTN_TK, tq), _BF),
                        pltpu.VMEM((1, tq), _F32), pltpu.VMEM((1, tq), _F32),
                        pltpu.VMEM((1, tq), _F32),
                        pltpu.VMEM((V_HEAD_DIM + ONES_ROWS, tq), _F32)],
        compiler_params=pltpu.CompilerParams(dimension_semantics=("arbitrary", "arbitrary"),
                                             vmem_limit_bytes=VMEM_LIMIT),
        name="chunk_causal_attention",
    )(q, kn, kpe, v)


def _layer_norm(z, g, b):
    mu = jnp.mean(z, axis=-1, keepdims=True)
    zc = z - mu
    var = jnp.mean(zc * zc, axis=-1, keepdims=True)
    return zc * lax.rsqrt(var + LN_EPS) * g + b


def _post_kernel(expert_cap, attn_ref, ycg_ref, sga_ref, x_ref, wab_ref, wout_ref, g1_ref, b1_ref,
                 wr_ref, br_ref, tri_ref, h1_ref, h1pa_ref, h1pb_ref, slot_ref, tw_ref, cnt_ref, run):
    tm = x_ref.shape[0]

    @pl.when(pl.program_id(0) == 0)
    def _():
        run[...] = jnp.zeros(run.shape, _F32)

    attn = jnp.concatenate([attn_ref[h] for h in range(N_HEADS)], axis=1)
    y_attn = _dot(attn, wab_ref[...])
    merged = ycg_ref[...].astype(_F32) + sga_ref[...].astype(_F32) * y_attn
    mix = _dot(merged.astype(_BF), wout_ref[...])
    h1 = _layer_norm(DEEPNORM_ALPHA * x_ref[...] + mix, g1_ref[...], b1_ref[...])
    h1_ref[...] = h1
    h1p = _pack_rows(h1)
    h1pa_ref[...] = h1p[:, :HALF]
    h1pb_ref[...] = h1p[:, HALF:]

    nt = (((1,), (1,)), ((), ()))
    h_hi = h1.astype(_BF)
    h_lo = (h1 - h_hi.astype(_F32)).astype(_BF)
    both = lax.dot_general(wr_ref[...], h_hi, nt, preferred_element_type=_F32)
    logits = (both[:N_EXPERTS] + both[N_EXPERTS:] + br_ref[:, 0:1]
              + lax.dot_general(wr_ref[:N_EXPERTS, :], h_lo, nt, preferred_element_type=_F32))
    expert = lax.broadcasted_iota(jnp.int32, (N_EXPERTS, tm), 0)
    work = logits
    chosen = jnp.zeros((N_EXPERTS, tm), jnp.bool_)
    sels, exps = [], []
    for k in range(TOP_K):
        mx = jnp.max(work, axis=0, keepdims=True)
        sel = jnp.min(jnp.where(work == mx, expert, N_EXPERTS), axis=0, keepdims=True)
        hit = expert == sel
        sels.append(sel)
        exps.append(jnp.exp(mx - (mx if k == 0 else top0)))
        if k == 0:
            top0 = mx
        chosen = jnp.logical_or(chosen, hit)
        work = jnp.where(hit, -jnp.inf, work)
    denom = exps[0] + exps[1] + exps[2] + exps[3]

    onehot = chosen.astype(_F32)
    before = _dot(onehot.astype(_BF), tri_ref[...]) + run[:, 0:1]
    rows = []
    for k in range(TOP_K):
        rank = jnp.sum(jnp.where(expert == sels[k], before, 0.0), axis=0, keepdims=True)
        rows.append(sels[k] * expert_cap + rank.astype(jnp.int32))
    slot_ref[...] = jnp.concatenate(rows + [jnp.zeros((8 - TOP_K, tm), jnp.int32)], axis=0)
    run[...] = run[...] + jnp.sum(onehot, axis=1, keepdims=True)
    cnt_ref[...] = run[...]
    tw_t = jnp.concatenate([e / denom for e in exps] + [jnp.zeros((128 - TOP_K, tm), _F32)], axis=0)
    tw_ref[...] = tw_t.T


def _post_call(attn, ycg, sga, x2d, wab, wout, g1, b1, wr, br):
    t = x2d.shape[0]
    tm = POST_TM
    row = lambda w: pl.BlockSpec((tm, w), lambda i: (i, 0))
    tps = attn.shape[2] // tm
    tri = (jnp.arange(tm)[:, None] < jnp.arange(tm)[None, :]).astype(_BF)
    outs = [
        jax.ShapeDtypeStruct((t, D_MODEL), _F32),
        jax.ShapeDtypeStruct((t, HALF), jnp.int32),
        jax.ShapeDtypeStruct((t, HALF), jnp.int32),
        jax.ShapeDtypeStruct((8, t), jnp.int32),
        jax.ShapeDtypeStruct((t, 128), _F32),
        jax.ShapeDtypeStruct((N_EXPERTS, 128), _F32),
    ]
    out_specs = [row(D_MODEL), row(HALF), row(HALF), pl.BlockSpec((8, tm), lambda i: (0, i)),
                 row(128), pl.BlockSpec((N_EXPERTS, 128), lambda i: (0, 0))]
    return pl.pallas_call(
        functools.partial(_post_kernel, t),
        grid=(t // tm,),
        in_specs=[pl.BlockSpec((None, N_HEADS, tm, V_HEAD_DIM), lambda i: (i // tps, 0, i % tps, 0))]
        + [row(D_MODEL)] * 3
        + [_const_spec(a.shape) for a in (wab, wout, g1, b1, wr, br, tri)],
        out_specs=out_specs,
        out_shape=outs,
        scratch_shapes=[pltpu.VMEM((N_EXPERTS, 128), _F32)],
        compiler_params=pltpu.CompilerParams(dimension_semantics=("arbitrary",),
                                             vmem_limit_bytes=VMEM_LIMIT),
        name="merge_out_ln1_router",
    )(attn, ycg, sga, x2d, wab, wout, g1, b1, wr, br, tri)


def _moe_kernel(be_ref, we_ref, rb_ref, bv_ref, nused_ref, xa_ref, xb_ref, wgu_ref, bgu_ref, wd_ref, bd_ref,
                ya_ref, yb_ref, wgu_bf, wd_bf):
    i = pl.program_id(0)
    prev = be_ref[jnp.maximum(i - 1, 0)]
    fresh = jnp.logical_or(i == 0, be_ref[i] != prev)
    used = i < nused_ref[0]

    @pl.when(jnp.logical_and(fresh, used))
    def _():
        wgu_bf[...] = wgu_ref[...].astype(_BF)
        wd_bf[...] = wd_ref[...].astype(_BF)

    def ffn(r0, rows):
        a_lo, a_hi = _unpack_rows(xa_ref[r0:r0 + rows, :])
        b_lo, b_hi = _unpack_rows(xb_ref[r0:r0 + rows, :])
        xs = jnp.concatenate([a_lo, b_lo, a_hi, b_hi], axis=1).astype(_BF)
        gu = _dot(xs, wgu_bf[...]) + bgu_ref[...]
        gate = jnp.minimum(gu[:, :D_FF_EXPERT], SWIGLU_LIMIT)
        up = jnp.clip(gu[:, D_FF_EXPERT:], -SWIGLU_LIMIT, SWIGLU_LIMIT)
        hid = (up + 1.0) * (gate * jax.nn.sigmoid(SWIGLU_ALPHA * gate))
        yp = _pack_rows(_dot(hid.astype(_BF), wd_bf[...]) + bd_ref[...])
        ya_ref[r0:r0 + rows, :] = yp[:, :HALF]
        yb_ref[r0:r0 + rows, :] = yp[:, HALF:]

    bm = xa_ref.shape[0]
    step = bm // MOE_ROW_STEPS
    need = (bv_ref[i] + step - 1) // step
    for n in range(1, MOE_ROW_STEPS + 1):
        @pl.when(jnp.logical_and(used, need == n))
        def _(n=n):
            ffn(0, n * step)


def _moe_call(block_e, weight_e, block_row, block_valid, n_used, xa, xb, wgu, bgu, wd, bd):
    n_slots = xa.shape[0]
    bm = MOE_BM
    n_blocks = block_e.shape[0]
    grid_spec = pltpu.PrefetchScalarGridSpec(
        num_scalar_prefetch=5,
        grid=(n_blocks,),
        in_specs=[
            pl.BlockSpec((bm, HALF), lambda i, be, we, rb, bv, nu: (rb[i], 0)),
            pl.BlockSpec((bm, HALF), lambda i, be, we, rb, bv, nu: (rb[i], 0)),
            pl.BlockSpec((None, D_MODEL, 2 * D_FF_EXPERT), lambda i, be, we, rb, bv, nu: (we[i], 0, 0)),
            pl.BlockSpec((None, 1, 2 * D_FF_EXPERT), lambda i, be, we, rb, bv, nu: (be[i], 0, 0)),
            pl.BlockSpec((None, D_FF_EXPERT, D_MODEL), lambda i, be, we, rb, bv, nu: (we[i], 0, 0)),
            pl.BlockSpec((None, 1, D_MODEL), lambda i, be, we, rb, bv, nu: (be[i], 0, 0)),
        ],
        out_specs=[pl.BlockSpec((bm, HALF), lambda i, be, we, rb, bv, nu: (rb[i], 0))] * 2,
        scratch_shapes=[pltpu.VMEM((D_MODEL, 2 * D_FF_EXPERT), _BF),
                        pltpu.VMEM((D_FF_EXPERT, D_MODEL), _BF)],
    )
    return pl.pallas_call(
        _moe_kernel,
        grid_spec=grid_spec,
        out_shape=[jax.ShapeDtypeStruct((n_slots, HALF), jnp.int32)] * 2,
        compiler_params=pltpu.CompilerParams(dimension_semantics=("arbitrary",),
                                             vmem_limit_bytes=VMEM_LIMIT),
        name="expert_ffn",
    )(block_e, weight_e, block_row, block_valid, n_used, xa, xb, wgu, bgu, wd, bd)


def _combine_kernel(*refs):
    ya_refs, yb_refs = refs[:TOP_K], refs[TOP_K:2 * TOP_K]
    tw_ref, h1_ref, g2_ref, b2_ref, o_ref = refs[2 * TOP_K:]
    tw = tw_ref[...]
    parts = [jnp.zeros((tw.shape[0], HALF), _F32) for _ in range(4)]
    for k in range(TOP_K):
        a_lo, a_hi = _unpack_rows(ya_refs[k][...])
        b_lo, b_hi = _unpack_rows(yb_refs[k][...])
        w = tw[:, k:k + 1]
        parts = [p + w * y for p, y in zip(parts, (a_lo, b_lo, a_hi, b_hi))]
    ffn = jnp.concatenate(parts, axis=1)
    o_ref[...] = _layer_norm(DEEPNORM_ALPHA * h1_ref[...] + ffn, g2_ref[...], b2_ref[...])


def _combine_call(yga, ygb, tw, h1, g2, b2):
    t = h1.shape[0]
    tm = COMB_TM
    row = lambda w: pl.BlockSpec((tm, w), lambda i: (i, 0))
    krow = lambda k: pl.BlockSpec((tm, HALF), lambda i: (k * (t // tm) + i, 0))
    return pl.pallas_call(
        _combine_kernel,
        grid=(t // tm,),
        in_specs=[krow(k) for k in range(TOP_K)] * 2 + [row(128), row(D_MODEL),
                  _const_spec(g2.shape), _const_spec(b2.shape)],
        out_specs=row(D_MODEL),
        out_shape=jax.ShapeDtypeStruct((t, D_MODEL), _F32),
        compiler_params=pltpu.CompilerParams(dimension_semantics=("arbitrary",),
                                             vmem_limit_bytes=VMEM_LIMIT),
        name="combine_ln2",
    )(*([yga] * TOP_K + [ygb] * TOP_K), tw, h1, g2, b2)


def _prepare_weights(w_in, w_uq):
    o_q = 3 * CONV_DIM
    o_kv = o_q + Q_LORA_RANK
    o_pe = o_kv + KV_LORA_RANK
    o_gc = o_pe + QK_ROPE_DIM
    wg3 = w_in[:, :o_q]
    kpe = w_in[:, o_pe:o_gc]
    zpad = jnp.zeros((D_MODEL, ROPE_PAD - QK_ROPE_DIM), w_in.dtype)
    wlat = jnp.concatenate([w_in[:, o_q:o_pe], kpe, zpad], axis=1)
    wgates = w_in[:, o_gc:]
    wq = w_uq.reshape(Q_LORA_RANK, N_HEADS, QK_HEAD_DIM)
    wuqa = jnp.pad(wq, ((0, 0), (0, 0), (0, QK_PAD - QK_HEAD_DIM)))
    return (wg3.astype(_BF), wlat.astype(_BF), wgates.astype(_BF),
            wuqa.reshape(Q_LORA_RANK, N_HEADS * QK_PAD).astype(_BF))


def _rope_rows():
    half = QK_ROPE_DIM // 2
    inv_freq = ROPE_BASE ** (-jnp.arange(0, QK_ROPE_DIM, 2, dtype=_F32) / QK_ROPE_DIM)
    z = jnp.zeros((ROPE_PAD - QK_ROPE_DIM,), _F32)
    ones = jnp.ones((half,), _F32)
    rows = jnp.stack([jnp.concatenate([inv_freq] * 4),
                      jnp.concatenate([ones, ones, z]),
                      jnp.concatenate([-ones, ones, z]),
                      jnp.concatenate([ones * (np.pi / 2), ones * (np.pi / 2), z])])
    return jnp.pad(rows, ((0, 8 - rows.shape[0]), (0, 0)))


def _block_tables(counts, n_tok):
    bm = MOE_BM
    n_blocks = n_tok * TOP_K // bm + N_EXPERTS
    nblk = (counts + bm - 1) // bm
    end = jnp.cumsum(nblk)
    start = end - nblk
    n_used = end[-1]
    i = jnp.minimum(jnp.arange(n_blocks, dtype=jnp.int32), n_used - 1)
    block_e = (end[None, :] <= i[:, None]).sum(axis=1).astype(jnp.int32)
    first = jnp.sum(jnp.where(jnp.arange(N_EXPERTS)[None, :] == block_e[:, None], start[None, :], 0), axis=1)
    block_row = block_e * (n_tok // bm) + (i - first)
    fresh = jnp.concatenate([jnp.ones((1,), jnp.bool_), block_e[1:] != block_e[:-1]])
    later = jnp.where(block_e[None, :] > block_e[:, None], block_e[None, :], N_EXPERTS)
    nxt = jnp.min(later, axis=1)
    weight_e = jnp.where(jnp.logical_or(fresh, nxt == N_EXPERTS), block_e, nxt).astype(jnp.int32)
    cnt = jnp.sum(jnp.where(jnp.arange(N_EXPERTS)[None, :] == block_e[:, None], counts[None, :], 0), axis=1)
    block_valid = jnp.clip(cnt - (i - first) * bm, 0, bm).astype(jnp.int32)
    return (block_e, weight_e, block_row.astype(jnp.int32), block_valid,
            n_used.astype(jnp.int32).reshape(1))


def kernel(x, positions, w_in, conv_w, q_norm_g, w_uq, kv_norm_g, w_uk, w_uv, w_conv_branch,
           w_attn_branch, w_out, ln1_g, ln1_b, w_router, b_router, w_gate_up, b_gate_up, w_down,
           b_down, ln2_g, ln2_b):
    b, s, d = x.shape
    t = b * s
    x2d = x.reshape(t, d)
    pos = positions.reshape(t, 1).astype(jnp.int32)
    rope = _rope_rows()
    for l in range(DEPTH):
        wg3, wlat, wgates, wuqa = _prepare_weights(w_in[l], w_uq[l])
        q, kn, kpe, v, ycg, sga = _proj_call(
            x2d, pos, rope, wg3, wlat, wgates, conv_w[l], q_norm_g[l][None, :], kv_norm_g[l][None, :],
            wuqa, w_uk[l].astype(_BF), w_uv[l].astype(_BF), w_conv_branch[l].astype(_BF), s)
        attn = _attn_call(q, kn, kpe.reshape(b, s, -1), v)
        wr_t = w_router[l].T
        wr_hi = wr_t.astype(_BF)
        wr_lo = (wr_t - wr_hi.astype(_F32)).astype(_BF)
        h1, h1pa, h1pb, slots, tw_pad, counts = _post_call(
            attn, ycg, sga, x2d, w_attn_branch[l].astype(_BF), w_out[l].astype(_BF),
            ln1_g[l][None, :], ln1_b[l][None, :], jnp.concatenate([wr_hi, wr_lo], axis=0),
            jnp.broadcast_to(b_router[l][:, None], (N_EXPERTS, 128)))
        block_e, weight_e, block_row, block_valid, n_used = _block_tables(counts[:, 0].astype(jnp.int32), t)
        xa, xb = _sc_scatter_rows([h1pa, h1pb], slots, TOP_K, N_EXPERTS * t)
        ya, yb = _moe_call(block_e, weight_e, block_row, block_valid, n_used, xa, xb,
                           w_gate_up[l], b_gate_up[l][:, None, :], w_down[l], b_down[l][:, None, :])
        yga, ygb = _sc_gather_rows([ya, yb], slots, TOP_K)
        x2d = _combine_call(yga, ygb, tw_pad, h1, ln2_g[l][None, :], ln2_b[l][None, :])
    return x2d.reshape(b, s, d)
```

```python
import functools

import numpy as np
import jax
import jax.numpy as jnp
from jax import lax
from jax.experimental import pallas as pl
from jax.experimental.pallas import tpu as pltpu
from jax.experimental.pallas import tpu_sc as plsc

D_MODEL = 1024
CHUNK = 64
CONV_DIM = D_MODEL
CONV_WIDTH = 3
N_HEADS = 8
QK_NOPE_DIM = 128
QK_ROPE_DIM = 64
V_HEAD_DIM = 128
QK_HEAD_DIM = QK_NOPE_DIM + QK_ROPE_DIM
Q_LORA_RANK = 384
KV_LORA_RANK = 256
ROPE_BASE = 10000.0
N_EXPERTS = 32
TOP_K = 4
D_FF_EXPERT = D_MODEL
SWIGLU_LIMIT = 7.0
SWIGLU_ALPHA = 1.702
LN_EPS = 1e-5
RMS_EPS = 1e-6
DEPTH = 1
DEEPNORM_ALPHA = (2 * DEPTH) ** 0.25

QK_PAD = 256
ROPE_PAD = 128

PROJ_TM = 512
ATTN_TQ = 512
ATTN_TK = 512
ONES_ROWS = 16
POST_TM = 512
MOE_BM = 512
MOE_ROW_STEPS = 4
COMB_TM = 512
SC_WINDOW = 128
PACKED = D_MODEL // 2
HALF = PACKED // 2
HALO = 8
NEG_BIG = -1e30
LOG2E = 1.4426950408889634
VMEM_LIMIT = 56 * 1024 * 1024

_BF = jnp.bfloat16
_F32 = jnp.float32


def _dot(a, b):
    return jnp.dot(a, b, preferred_element_type=_F32)


def _const_spec(shape):
    nd = len(shape)
    return pl.BlockSpec(shape, lambda *_: (0,) * nd, pipeline_mode=pl.Buffered(1))


def _pack_rows(a):
    half = a.shape[1] // 2
    lo = lax.bitcast_convert_type(a[:, :half].astype(_BF).astype(_F32), jnp.int32)
    hi = lax.bitcast_convert_type(a[:, half:].astype(_BF).astype(_F32), jnp.int32)
    return lax.shift_right_logical(lo, 16) | (hi & jnp.int32(-65536))


def _unpack_rows(p):
    lo = lax.bitcast_convert_type(lax.shift_left(p, 16), _F32)
    hi = lax.bitcast_convert_type(p & jnp.int32(-65536), _F32)
    return lo, hi


def _sc_mesh():
    return plsc.VectorSubcoreMesh(core_axis_name="core", subcore_axis_name="subcore")


def _sc_scatter_rows(tables, idx, n_copies, n_out):
    n_rows, width = tables[0].shape
    n_tab = len(tables)
    window = SC_WINDOW

    @functools.partial(pl.kernel, mesh=_sc_mesh(),
                       out_type=[jax.ShapeDtypeStruct((n_out, width), tables[0].dtype)] * n_tab)
    def scatter_kernel(*refs):
        idx_hbm = refs[n_tab]
        for src_hbm, out_hbm in zip(refs[:n_tab], refs[n_tab + 1:]):
            def body(src_vmem, idx_vmem, out_hbm=out_hbm):
                for k in range(n_copies):
                    pltpu.sync_copy(src_vmem, out_hbm.at[idx_vmem.at[k]])

            pltpu.emit_pipeline(
                body,
                grid=(n_rows // window,),
                in_specs=[pl.BlockSpec((window, width), lambda i: (i, 0)),
                          pl.BlockSpec((idx.shape[0], window), lambda i: (0, i))],
                out_specs=[],
                core_axis_name=("core", "subcore"),
                dimension_semantics=(pltpu.PARALLEL,),
            )(src_hbm, idx_hbm)

    return scatter_kernel(*tables, idx)


def _sc_gather_rows(tables, idx, n_copies):
    n_rows = idx.shape[1]
    width = tables[0].shape[1]
    n_tab = len(tables)
    window = SC_WINDOW
    steps = n_rows // window

    @functools.partial(pl.kernel, mesh=_sc_mesh(),
                       out_type=[jax.ShapeDtypeStruct((n_copies * n_rows, width), tables[0].dtype)] * n_tab)
    def gather_kernel(*refs):
        idx_hbm = refs[n_tab]
        for table_hbm, out_hbm in zip(refs[:n_tab], refs[n_tab + 1:]):
            def body(idx_vmem, out_vmem, table_hbm=table_hbm):
                pltpu.sync_copy(table_hbm.at[idx_vmem.at[0]], out_vmem)

            pltpu.emit_pipeline(
                body,
                grid=(n_copies, steps),
                in_specs=[pl.BlockSpec((1, window), lambda k, i: (k, i))],
                out_specs=[pl.BlockSpec((window, width), lambda k, i: (k * steps + i, 0))],
                core_axis_name=("core", "subcore"),
                dimension_semantics=(pltpu.PARALLEL, pltpu.PARALLEL),
            )(idx_hbm, out_hbm)

    return gather_kernel(*tables, idx)


def _proj_kernel(tiles_per_seq,
                 x_ref, pos_ref, rope_ref, wg3_ref, wlat_ref, wgates_ref, convw_ref,
                 qg_ref, kvg_ref, wuqa_ref, wuk_ref, wuv_ref, wcb_ref,
                 qt_ref, kn_ref, kpe_ref, v_ref, ycg_ref, sga_ref, ubuf):
    i = pl.program_id(0)
    tm = x_ref.shape[0]
    xb = x_ref[...].astype(_BF)

    @pl.when(i % tiles_per_seq == 0)
    def _():
        ubuf[0:HALO, :] = jnp.zeros((HALO, CONV_DIM), _F32)

    g3 = _dot(xb, wg3_ref[...])
    u = g3[:, CONV_DIM:2 * CONV_DIM] * g3[:, 2 * CONV_DIM:]
    ubuf[HALO:HALO + tm, :] = u
    cw = convw_ref[...]
    conv = (cw[2:3, :] * u + cw[1:2, :] * ubuf[HALO - 1:HALO - 1 + tm, :]
            + cw[0:1, :] * ubuf[HALO - 2:HALO - 2 + tm, :])
    ubuf[0:HALO, :] = ubuf[tm:tm + HALO, :]
    yc = _dot((g3[:, :CONV_DIM] * conv).astype(_BF), wcb_ref[...])
    gates = _dot(xb, wgates_ref[...])
    ycg_ref[...] = (jax.nn.sigmoid(gates[:, :D_MODEL]) * yc).astype(_BF)
    sga_ref[...] = jax.nn.sigmoid(gates[:, D_MODEL:]).astype(_BF)

    lat = _dot(xb, wlat_ref[...])
    trig = jnp.sin(pos_ref[...].astype(_F32) * rope_ref[0:1, :] + rope_ref[3:4, :])
    cosb = trig * rope_ref[1:2, :]
    sinb = pltpu.roll(trig, QK_ROPE_DIM, axis=1) * rope_ref[2:3, :]
    q_lat = lat[:, :Q_LORA_RANK]
    rq = q_lat * lax.rsqrt(jnp.mean(q_lat * q_lat, axis=-1, keepdims=True) + RMS_EPS) * qg_ref[...]
    kv_lat = lat[:, Q_LORA_RANK:Q_LORA_RANK + KV_LORA_RANK]
    ckv = (kv_lat * lax.rsqrt(jnp.mean(kv_lat * kv_lat, axis=-1, keepdims=True) + RMS_EPS)
           * kvg_ref[...]).astype(_BF)
    o = Q_LORA_RANK + KV_LORA_RANK
    lane = lax.broadcasted_iota(jnp.int32, (tm, ROPE_PAD), 1)
    half = QK_ROPE_DIM // 2

    def swap_halves(u):
        return jnp.where(lane < half, pltpu.roll(u, ROPE_PAD - half, axis=1), pltpu.roll(u, half, axis=1))

    k_pe = lat[:, o:o + ROPE_PAD]
    kpe_ref[...] = (k_pe * cosb + swap_halves(k_pe) * sinb).astype(_BF)
    kn = _dot(ckv, wuk_ref[...]).astype(_BF)
    vv = _dot(ckv, wuv_ref[...]).astype(_BF)
    for h in range(N_HEADS):
        kn_ref[h] = kn[:, h * QK_NOPE_DIM:(h + 1) * QK_NOPE_DIM]
        v_ref[h] = vv[:, h * V_HEAD_DIM:(h + 1) * V_HEAD_DIM]

    rqb = rq.astype(_BF)
    qa = _dot(rqb, wuqa_ref[...])
    scale = QK_HEAD_DIM ** -0.5 * LOG2E
    for h in range(N_HEADS):
        lo = h * QK_PAD
        q_pe = qa[:, lo + QK_NOPE_DIM:lo + QK_PAD]
        hi = q_pe * cosb + swap_halves(q_pe) * sinb
        q_h = jnp.concatenate([qa[:, lo:lo + QK_NOPE_DIM] * scale, hi * scale], axis=1).astype(_BF)
        qt_ref[h] = q_h.T


def _proj_call(x2d, pos, rope, wg3, wlat, wgates, convw, qg, kvg, wuqa, wuk, wuv, wcb, seq):
    t = x2d.shape[0]
    tm = PROJ_TM
    tps = seq // tm
    row = lambda w: pl.BlockSpec((tm, w), lambda i: (i, 0))
    heads = lambda w: pl.BlockSpec((None, N_HEADS, tm, w), lambda i: (i // tps, 0, i % tps, 0))
    head_major = lambda w: jax.ShapeDtypeStruct((t // seq, N_HEADS, seq, w), _BF)
    outs = [
        jax.ShapeDtypeStruct((t // seq, N_HEADS, QK_PAD, seq), _BF),
        head_major(QK_NOPE_DIM),
        jax.ShapeDtypeStruct((t, ROPE_PAD), _BF),
        head_major(V_HEAD_DIM),
        jax.ShapeDtypeStruct((t, D_MODEL), _BF),
        jax.ShapeDtypeStruct((t, D_MODEL), _BF),
    ]
    return pl.pallas_call(
        functools.partial(_proj_kernel, tps),
        grid=(t // tm,),
        in_specs=[row(D_MODEL), row(1)]
        + [_const_spec(a.shape)
           for a in (rope, wg3, wlat, wgates, convw, qg, kvg, wuqa, wuk, wuv, wcb)],
        out_specs=[pl.BlockSpec((None, N_HEADS, QK_PAD, tm), lambda i: (i // tps, 0, 0, i % tps)),
                   heads(QK_NOPE_DIM), row(ROPE_PAD), heads(V_HEAD_DIM),
                   row(D_MODEL), row(D_MODEL)],
        out_shape=outs,
        scratch_shapes=[pltpu.VMEM((tm + HALO, CONV_DIM), _F32)],
        compiler_params=pltpu.CompilerParams(dimension_semantics=("arbitrary",),
                                             vmem_limit_bytes=VMEM_LIMIT),
        name="proj_conv_qkv",
    )(x2d, pos, rope, wg3, wlat, wgates, convw, qg, kvg, wuqa, wuk, wuv, wcb)


def _attn_kernel(qt_ref, kn_ref, kpe_ref, v_ref, o_ref, kfull, vt, s0, s1, p0, p1, a0, a1, m_ref, acc):
    tq, tk = ATTN_TQ, ATTN_TK
    n_tiles = qt_ref.shape[1] // tq
    s_bufs, p_bufs, a_bufs = (s0, s1), (p0, p1), (a0, a1)
    stages = [(tile, blk) for tile in range(n_tiles) for blk in range(tile + 1)]

    kfull[:, :QK_NOPE_DIM] = kn_ref[...]
    kfull[:, QK_NOPE_DIM:] = kpe_ref[...]
    vt[:V_HEAD_DIM, :] = v_ref[...].T
    vt[V_HEAD_DIM:, :] = jnp.ones((vt.shape[0] - V_HEAD_DIM, vt.shape[1]), _BF)

    hk, hq = tk // 2, tq // 2

    def scores_t(t):
        tile, blk = stages[t]
        s_ref = s_bufs[t % 2]
        k0, q0 = blk * tk, tile * tq
        if blk == tile:
            s_ref[0:hk, :] = _dot(kfull[k0:k0 + hk, :], qt_ref[:, q0:q0 + tq])
            s_ref[hk:tk, hq:tq] = _dot(kfull[k0 + hk:k0 + tk, :], qt_ref[:, q0 + hq:q0 + tq])
        else:
            s_ref[...] = _dot(kfull[k0:k0 + tk, :], qt_ref[:, q0:q0 + tq])

    key_chunk = lax.broadcasted_iota(jnp.int32, (tk, tq), 0) // CHUNK
    qry_chunk = lax.broadcasted_iota(jnp.int32, (tk, tq), 1) // CHUNK
    causal = key_chunk <= qry_chunk

    def softmax_t(t):
        tile, blk = stages[t]
        par = t % 2
        if blk == tile:
            top = jnp.where(causal[0:hk, :], s_bufs[par][0:hk, :], NEG_BIG)
            bot = jnp.where(causal[hk:tk, hq:tq], s_bufs[par][hk:tk, hq:tq], NEG_BIG)
            bot_max = jnp.concatenate([jnp.full((1, hq), NEG_BIG, _F32),
                                       jnp.max(bot, axis=0, keepdims=True)], axis=1)
            blk_max = jnp.maximum(jnp.max(top, axis=0, keepdims=True), bot_max)
        else:
            s_t = s_bufs[par][...]
            blk_max = jnp.max(s_t, axis=0, keepdims=True)
        if blk == 0:
            m_new = blk_max
        else:
            m_old = m_ref[...]
            m_new = jnp.maximum(m_old, blk_max)
            a_bufs[par][...] = jnp.exp2(m_old - m_new)
        m_ref[...] = m_new
        if blk == tile:
            p_bufs[par][0:hk, :] = jnp.exp2(top - m_new).astype(_BF)
            p_bufs[par][hk:tk, hq:tq] = jnp.exp2(bot - m_new[:, hq:tq]).astype(_BF)
        else:
            p_bufs[par][...] = jnp.exp2(s_t - m_new).astype(_BF)

    def accumulate(t):
        tile, blk = stages[t]
        par = t % 2
        k0 = blk * tk
        if blk == tile:
            pv = jnp.concatenate([_dot(vt[:, k0:k0 + hk], p_bufs[par][0:hk, 0:hq]),
                                  _dot(vt[:, k0:k0 + tk], p_bufs[par][:, hq:tq])], axis=1)
        else:
            pv = _dot(vt[:, k0:k0 + tk], p_bufs[par][...])
        acc[...] = pv if blk == 0 else a_bufs[par][...] * acc[...] + pv
        if blk == tile:
            out_t = acc[:V_HEAD_DIM, :] / acc[V_HEAD_DIM:V_HEAD_DIM + 1, :]
            o_ref[tile * tq:(tile + 1) * tq, :] = out_t.T.astype(_BF)

    scores_t(0)
    for t in range(len(stages)):
        if t + 1 < len(stages):
            scores_t(t + 1)
        if t > 0:
            accumulate(t - 1)
        softmax_t(t)
    accumulate(len(stages) - 1)


def _attn_call(q, kn, kpe, v):
    b, _, _, s = q.shape
    tq = ATTN_TQ
    assert ATTN_TQ == ATTN_TK and s % tq == 0
    per_head = lambda w: pl.BlockSpec((None, None, s, w), lambda bi, h: (bi, h, 0, 0))
    return pl.pallas_call(
        _attn_kernel,
        grid=(b, N_HEADS),
        in_specs=[pl.BlockSpec((None, None, QK_PAD, s), lambda bi, h: (bi, h, 0, 0)), per_head(QK_NOPE_DIM),
                  pl.BlockSpec((None, s, ROPE_PAD), lambda bi, h: (bi, 0, 0)), per_head(V_HEAD_DIM)],
        out_specs=per_head(V_HEAD_DIM),
        out_shape=jax.ShapeDtypeStruct((b, N_HEADS, s, V_HEAD_DIM), _BF),
        scratch_shapes=[pltpu.VMEM((s, QK_PAD), _BF), pltpu.VMEM((V_HEAD_DIM + ONES_ROWS, s), _BF),
                        pltpu.VMEM((ATTN_TK, tq), _F32), pltpu.VMEM((ATTN_TK, tq), _F32),
                        pltpu.VMEM((ATTN_TK, tq), _BF), pltpu.VMEM((ATTN_TK, tq), _BF),
                        pltpu.VMEM((1, tq), _F32), pltpu.VMEM((1, tq), _F32),
                        pltpu.VMEM((1, tq), _F32),
                        pltpu.VMEM((V_HEAD_DIM + ONES_ROWS, tq), _F32)],
        compiler_params=pltpu.CompilerParams(dimension_semantics=("arbitrary", "arbitrary"),
                                             vmem_limit_bytes=VMEM_LIMIT),
        name="chunk_causal_attention",
    )(q, kn, kpe, v)


def _layer_norm(z, g, b):
    mu = jnp.mean(z, axis=-1, keepdims=True)
    zc = z - mu
    var = jnp.mean(zc * zc, axis=-1, keepdims=True)
    return zc * lax.rsqrt(var + LN_EPS) * g + b


def _post_kernel(expert_cap, attn_ref, ycg_ref, sga_ref, x_ref, wab_ref, wout_ref, g1_ref, b1_ref,
                 wr_ref, br_ref, tri_ref, h1_ref, h1pa_ref, h1pb_ref, slot_ref, tw_ref, cnt_ref, run):
    tm = x_ref.shape[0]

    @pl.when(pl.program_id(0) == 0)
    def _():
        run[...] = jnp.zeros(run.shape, _F32)

    attn = jnp.concatenate([attn_ref[h] for h in range(N_HEADS)], axis=1)
    y_attn = _dot(attn, wab_ref[...])
    merged = ycg_ref[...].astype(_F32) + sga_ref[...].astype(_F32) * y_attn
    mix = _dot(merged.astype(_BF), wout_ref[...])
    h1 = _layer_norm(DEEPNORM_ALPHA * x_ref[...] + mix, g1_ref[...], b1_ref[...])
    h1_ref[...] = h1
    h1p = _pack_rows(h1)
    h1pa_ref[...] = h1p[:, :HALF]
    h1pb_ref[...] = h1p[:, HALF:]

    nt = (((1,), (1,)), ((), ()))
    h_hi = h1.astype(_BF)
    h_lo = (h1 - h_hi.astype(_F32)).astype(_BF)
    both = lax.dot_general(wr_ref[...], h_hi, nt, preferred_element_type=_F32)
    logits = (both[:N_EXPERTS] + both[N_EXPERTS:] + br_ref[:, 0:1]
              + lax.dot_general(wr_ref[:N_EXPERTS, :], h_lo, nt, preferred_element_type=_F32))
    expert = lax.broadcasted_iota(jnp.int32, (N_EXPERTS, tm), 0)
    work = logits
    chosen = jnp.zeros((N_EXPERTS, tm), jnp.bool_)
    sels, exps = [], []
    for k in range(TOP_K):
        mx = jnp.max(work, axis=0, keepdims=True)
        sel = jnp.min(jnp.where(work == mx, expert, N_EXPERTS), axis=0, keepdims=True)
        hit = expert == sel
        sels.append(sel)
        exps.append(jnp.exp(mx - (mx if k == 0 else top0)))
        if k == 0:
            top0 = mx
        chosen = jnp.logical_or(chosen, hit)
        work = jnp.where(hit, -jnp.inf, work)
    denom = exps[0] + exps[1] + exps[2] + exps[3]

    onehot = chosen.astype(_F32)
    before = _dot(onehot.astype(_BF), tri_ref[...]) + run[:, 0:1]
    rows = []
    for k in range(TOP_K):
        rank = jnp.sum(jnp.where(expert == sels[k], before, 0.0), axis=0, keepdims=True)
        rows.append(sels[k] * expert_cap + rank.astype(jnp.int32))
    slot_ref[...] = jnp.concatenate(rows + [jnp.zeros((8 - TOP_K, tm), jnp.int32)], axis=0)
    run[...] = run[...] + jnp.sum(onehot, axis=1, keepdims=True)
    cnt_ref[...] = run[...]
    tw_t = jnp.concatenate([e / denom for e in exps] + [jnp.zeros((128 - TOP_K, tm), _F32)], axis=0)
    tw_ref[...] = tw_t.T


def _post_call(attn, ycg, sga, x2d, wab, wout, g1, b1, wr, br):
    t = x2d.shape[0]
    tm = POST_TM
    row = lambda w: pl.BlockSpec((tm, w), lambda i: (i, 0))
    tps = attn.shape[2] // tm
    tri = (jnp.arange(tm)[:, None] < jnp.arange(tm)[None, :]).astype(_BF)
    outs = [
        jax.ShapeDtypeStruct((t, D_MODEL), _F32),
        jax.ShapeDtypeStruct((t, HALF), jnp.int32),
        jax.ShapeDtypeStruct((t, HALF), jnp.int32),
        jax.ShapeDtypeStruct((8, t), jnp.int32),
        jax.ShapeDtypeStruct((t, 128), _F32),
        jax.ShapeDtypeStruct((N_EXPERTS, 128), _F32),
    ]
    out_specs = [row(D_MODEL), row(HALF), row(HALF), pl.BlockSpec((8, tm), lambda i: (0, i)),
                 row(128), pl.BlockSpec((N_EXPERTS, 128), lambda i: (0, 0))]
    return pl.pallas_call(
        functools.partial(_post_kernel, t),
        grid=(t // tm,),
        in_specs=[pl.BlockSpec((None, N_HEADS, tm, V_HEAD_DIM), lambda i: (i // tps, 0, i % tps, 0))]
        + [row(D_MODEL)] * 3
        + [_const_spec(a.shape) for a in (wab, wout, g1, b1, wr, br, tri)],
        out_specs=out_specs,
        out_shape=outs,
        scratch_shapes=[pltpu.VMEM((N_EXPERTS, 128), _F32)],
        compiler_params=pltpu.CompilerParams(dimension_semantics=("arbitrary",),
                                             vmem_limit_bytes=VMEM_LIMIT),
        name="merge_out_ln1_router",
    )(attn, ycg, sga, x2d, wab, wout, g1, b1, wr, br, tri)


def _moe_kernel(be_ref, we_ref, rb_ref, bv_ref, nused_ref, xa_ref, xb_ref, wgu_ref, bgu_ref, wd_ref, bd_ref,
                ya_ref, yb_ref, wgu_bf, wd_bf):
    i = pl.program_id(0)
    prev = be_ref[jnp.maximum(i - 1, 0)]
    fresh = jnp.logical_or(i == 0, be_ref[i] != prev)
    used = i < nused_ref[0]

    @pl.when(jnp.logical_and(fresh, used))
    def _():
        wgu_bf[...] = wgu_ref[...].astype(_BF)
        wd_bf[...] = wd_ref[...].astype(_BF)

    def ffn(r0, rows):
        a_lo, a_hi = _unpack_rows(xa_ref[r0:r0 + rows, :])
        b_lo, b_hi = _unpack_rows(xb_ref[r0:r0 + rows, :])
        xs = jnp.concatenate([a_lo, b_lo, a_hi, b_hi], axis=1).astype(_BF)
        gu = _dot(xs, wgu_bf[...]) + bgu_ref[...]
        gate = jnp.minimum(gu[:, :D_FF_EXPERT], SWIGLU_LIMIT)
        up = jnp.clip(gu[:, D_FF_EXPERT:], -SWIGLU_LIMIT, SWIGLU_LIMIT)
        hid = (up + 1.0) * (gate * jax.nn.sigmoid(SWIGLU_ALPHA * gate))
        yp = _pack_rows(_dot(hid.astype(_BF), wd_bf[...]) + bd_ref[...])
        ya_ref[r0:r0 + rows, :] = yp[:, :HALF]
        yb_ref[r0:r0 + rows, :] = yp[:, HALF:]

    bm = xa_ref.shape[0]
    step = bm // MOE_ROW_STEPS
    need = (bv_ref[i] + step - 1) // step
    for n in range(1, MOE_ROW_STEPS + 1):
        @pl.when(jnp.logical_and(used, need == n))
        def _(n=n):
            ffn(0, n * step)


def _moe_call(block_e, weight_e, block_row, block_valid, n_used, xa, xb, wgu, bgu, wd, bd):
    n_slots = xa.shape[0]
    bm = MOE_BM
    n_blocks = block_e.shape[0]
    grid_spec = pltpu.PrefetchScalarGridSpec(
        num_scalar_prefetch=5,
        grid=(n_blocks,),
        in_specs=[
            pl.BlockSpec((bm, HALF), lambda i, be, we, rb, bv, nu: (rb[i], 0)),
            pl.BlockSpec((bm, HALF), lambda i, be, we, rb, bv, nu: (rb[i], 0)),
            pl.BlockSpec((None, D_MODEL, 2 * D_FF_EXPERT), lambda i, be, we, rb, bv, nu: (we[i], 0, 0)),
            pl.BlockSpec((None, 1, 2 * D_FF_EXPERT), lambda i, be, we, rb, bv, nu: (be[i], 0, 0)),
            pl.BlockSpec((None, D_FF_EXPERT, D_MODEL), lambda i, be, we, rb, bv, nu: (we[i], 0, 0)),
            pl.BlockSpec((None, 1, D_MODEL), lambda i, be, we, rb, bv, nu: (be[i], 0, 0)),
        ],
        out_specs=[pl.BlockSpec((bm, HALF), lambda i, be, we, rb, bv, nu: (rb[i], 0))] * 2,
        scratch_shapes=[pltpu.VMEM((D_MODEL, 2 * D_FF_EXPERT), _BF),
                        pltpu.VMEM((D_FF_EXPERT, D_MODEL), _BF)],
    )
    return pl.pallas_call(
        _moe_kernel,
        grid_spec=grid_spec,
        out_shape=[jax.ShapeDtypeStruct((n_slots, HALF), jnp.int32)] * 2,
        compiler_params=pltpu.CompilerParams(dimension_semantics=("arbitrary",),
                                             vmem_limit_bytes=VMEM_LIMIT),
        name="expert_ffn",
    )(block_e, weight_e, block_row, block_valid, n_used, xa, xb, wgu, bgu, wd, bd)


def _combine_kernel(*refs):
    ya_refs, yb_refs = refs[:TOP_K], refs[TOP_K:2 * TOP_K]
    tw_ref, h1_ref, g2_ref, b2_ref, o_ref = refs[2 * TOP_K:]
    tw = tw_ref[...]
    parts = [jnp.zeros((tw.shape[0], HALF), _F32) for _ in range(4)]
    for k in range(TOP_K):
        a_lo, a_hi = _unpack_rows(ya_refs[k][...])
        b_lo, b_hi = _unpack_rows(yb_refs[k][...])
        w = tw[:, k:k + 1]
        parts = [p + w * y for p, y in zip(parts, (a_lo, b_lo, a_hi, b_hi))]
    ffn = jnp.concatenate(parts, axis=1)
    o_ref[...] = _layer_norm(DEEPNORM_ALPHA * h1_ref[...] + ffn, g2_ref[...], b2_ref[...])


def _combine_call(yga, ygb, tw, h1, g2, b2):
    t = h1.shape[0]
    tm = COMB_TM
    row = lambda w: pl.BlockSpec((tm, w), lambda i: (i, 0))
    krow = lambda k: pl.BlockSpec((tm, HALF), lambda i: (k * (t // tm) + i, 0))
    return pl.pallas_call(
        _combine_kernel,
        grid=(t // tm,),
        in_specs=[krow(k) for k in range(TOP_K)] * 2 + [row(128), row(D_MODEL),
                  _const_spec(g2.shape), _const_spec(b2.shape)],
        out_specs=row(D_MODEL),
        out_shape=jax.ShapeDtypeStruct((t, D_MODEL), _F32),
        compiler_params=pltpu.CompilerParams(dimension_semantics=("arbitrary",),
                                             vmem_limit_bytes=VMEM_LIMIT),
        name="combine_ln2",
    )(*([yga] * TOP_K + [ygb] * TOP_K), tw, h1, g2, b2)


def _prepare_weights(w_in, w_uq):
    o_q = 3 * CONV_DIM
    o_kv = o_q + Q_LORA_RANK
    o_pe = o_kv + KV_LORA_RANK
    o_gc = o_pe + QK_ROPE_DIM
    wg3 = w_in[:, :o_q]
    kpe = w_in[:, o_pe:o_gc]
    zpad = jnp.zeros((D_MODEL, ROPE_PAD - QK_ROPE_DIM), w_in.dtype)
    wlat = jnp.concatenate([w_in[:, o_q:o_pe], kpe, zpad], axis=1)
    wgates = w_in[:, o_gc:]
    wq = w_uq.reshape(Q_LORA_RANK, N_HEADS, QK_HEAD_DIM)
    wuqa = jnp.pad(wq, ((0, 0), (0, 0), (0, QK_PAD - QK_HEAD_DIM)))
    return (wg3.astype(_BF), wlat.astype(_BF), wgates.astype(_BF),
            wuqa.reshape(Q_LORA_RANK, N_HEADS * QK_PAD).astype(_BF))


def _rope_rows():
    half = QK_ROPE_DIM // 2
    inv_freq = ROPE_BASE ** (-jnp.arange(0, QK_ROPE_DIM, 2, dtype=_F32) / QK_ROPE_DIM)
    z = jnp.zeros((ROPE_PAD - QK_ROPE_DIM,), _F32)
    ones = jnp.ones((half,), _F32)
    rows = jnp.stack([jnp.concatenate([inv_freq] * 4),
                      jnp.concatenate([ones, ones, z]),
                      jnp.concatenate([-ones, ones, z]),
                      jnp.concatenate([ones * (np.pi / 2), ones * (np.pi / 2), z])])
    return jnp.pad(rows, ((0, 8 - rows.shape[0]), (0, 0)))


def _block_tables(counts, n_tok):
    bm = MOE_BM
    n_blocks = n_tok * TOP_K // bm + N_EXPERTS
    nblk = (counts + bm - 1) // bm
    end = jnp.cumsum(nblk)
    start = end - nblk
    n_used = end[-1]
    i = jnp.minimum(jnp.arange(n_blocks, dtype=jnp.int32), n_used - 1)
    block_e = (end[None, :] <= i[:, None]).sum(axis=1).astype(jnp.int32)
    first = jnp.sum(jnp.where(jnp.arange(N_EXPERTS)[None, :] == block_e[:, None], start[None, :], 0), axis=1)
    block_row = block_e * (n_tok // bm) + (i - first)
    fresh = jnp.concatenate([jnp.ones((1,), jnp.bool_), block_e[1:] != block_e[:-1]])
    later = jnp.where(block_e[None, :] > block_e[:, None], block_e[None, :], N_EXPERTS)
    nxt = jnp.min(later, axis=1)
    weight_e = jnp.where(jnp.logical_or(fresh, nxt == N_EXPERTS), block_e, nxt).astype(jnp.int32)
    cnt = jnp.sum(jnp.where(jnp.arange(N_EXPERTS)[None, :] == block_e[:, None], counts[None, :], 0), axis=1)
    block_valid = jnp.clip(cnt - (i - first) * bm, 0, bm).astype(jnp.int32)
    return (block_e, weight_e, block_row.astype(jnp.int32), block_valid,
            n_used.astype(jnp.int32).reshape(1))


def kernel(x, positions, w_in, conv_w, q_norm_g, w_uq, kv_norm_g, w_uk, w_uv, w_conv_branch,
           w_attn_branch, w_out, ln1_g, ln1_b, w_router, b_router, w_gate_up, b_gate_up, w_down,
           b_down, ln2_g, ln2_b):
    b, s, d = x.shape
    t = b * s
    x2d = x.reshape(t, d)
    pos = positions.reshape(t, 1).astype(jnp.int32)
    rope = _rope_rows()
    for l in range(DEPTH):
        wg3, wlat, wgates, wuqa = _prepare_weights(w_in[l], w_uq[l])
        q, kn, kpe, v, ycg, sga = _proj_call(
            x2d, pos, rope, wg3, wlat, wgates, conv_w[l], q_norm_g[l][None, :], kv_norm_g[l][None, :],
            wuqa, w_uk[l].astype(_BF), w_uv[l].astype(_BF), w_conv_branch[l].astype(_BF), s)
        attn = _attn_call(q, kn, kpe.reshape(b, s, -1), v)
        wr_t = w_router[l].T
        wr_hi = wr_t.astype(_BF)
        wr_lo = (wr_t - wr_hi.astype(_F32)).astype(_BF)
        h1, h1pa, h1pb, slots, tw_pad, counts = _post_call(
            attn, ycg, sga, x2d, w_attn_branch[l].astype(_BF), w_out[l].astype(_BF),
            ln1_g[l][None, :], ln1_b[l][None, :], jnp.concatenate([wr_hi, wr_lo], axis=0),
            jnp.broadcast_to(b_router[l][:, None], (N_EXPERTS, 128)))
        block_e, weight_e, block_row, block_valid, n_used = _block_tables(counts[:, 0].astype(jnp.int32), t)
        xa, xb = _sc_scatter_rows([h1pa, h1pb], slots, TOP_K, N_EXPERTS * t)
        ya, yb = _moe_call(block_e, weight_e, block_row, block_valid, n_used, xa, xb,
                           w_gate_up[l], b_gate_up[l][:, None, :], w_down[l], b_down[l][:, None, :])
        yga, ygb = _sc_gather_rows([ya, yb], slots, TOP_K)
        x2d = _combine_call(yga, ygb, tw_pad, h1, ln2_g[l][None, :], ln2_b[l][None, :])
    return x2d.reshape(b, s, d)
```

```python
import functools

import numpy as np
import jax
import jax.numpy as jnp
from jax import lax
from jax.experimental import pallas as pl
from jax.experimental.pallas import tpu as pltpu
from jax.experimental.pallas import tpu_sc as plsc

D_MODEL = 1024
CHUNK = 64
CONV_DIM = D_MODEL
CONV_WIDTH = 3
N_HEADS = 8
QK_NOPE_DIM = 128
QK_ROPE_DIM = 64
V_HEAD_DIM = 128
QK_HEAD_DIM = QK_NOPE_DIM + QK_ROPE_DIM
Q_LORA_RANK = 384
KV_LORA_RANK = 256
ROPE_BASE = 10000.0
N_EXPERTS = 32
TOP_K = 4
D_FF_EXPERT = D_MODEL
SWIGLU_LIMIT = 7.0
SWIGLU_ALPHA = 1.702
LN_EPS = 1e-5
RMS_EPS = 1e-6
DEPTH = 1
DEEPNORM_ALPHA = (2 * DEPTH) ** 0.25

QK_PAD = 256
ROPE_PAD = 128

PROJ_TM = 512
ATTN_TQ = 512
ATTN_TK = 512
ONES_ROWS = 16
POST_TM = 512
MOE_BM = 512
MOE_ROW_STEPS = 4
COMB_TM = 512
SC_WINDOW = 128
PACKED = D_MODEL // 2
HALF = PACKED // 2
HALO = 8
NEG_BIG = -1e30
LOG2E = 1.4426950408889634
VMEM_LIMIT = 56 * 1024 * 1024

_BF = jnp.bfloat16
_F32 = jnp.float32


def _dot(a, b):
    return jnp.dot(a, b, preferred_element_type=_F32)


def _const_spec(shape):
    nd = len(shape)
    return pl.BlockSpec(shape, lambda *_: (0,) * nd, pipeline_mode=pl.Buffered(1))


def _pack_rows(a):
    half = a.shape[1] // 2
    lo = lax.bitcast_convert_type(a[:, :half].astype(_BF).astype(_F32), jnp.int32)
    hi = lax.bitcast_convert_type(a[:, half:].astype(_BF).astype(_F32), jnp.int32)
    return lax.shift_right_logical(lo, 16) | (hi & jnp.int32(-65536))


def _unpack_rows(p):
    lo = lax.bitcast_convert_type(lax.shift_left(p, 16), _F32)
    hi = lax.bitcast_convert_type(p & jnp.int32(-65536), _F32)
    return lo, hi


def _sc_mesh():
    return plsc.VectorSubcoreMesh(core_axis_name="core", subcore_axis_name="subcore")


def _sc_scatter_rows(tables, idx, n_copies, n_out):
    n_rows, width = tables[0].shape
    n_tab = len(tables)
    window = SC_WINDOW

    @functools.partial(pl.kernel, mesh=_sc_mesh(),
                       out_type=[jax.ShapeDtypeStruct((n_out, width), tables[0].dtype)] * n_tab)
    def scatter_kernel(*refs):
        idx_hbm = refs[n_tab]
        for src_hbm, out_hbm in zip(refs[:n_tab], refs[n_tab + 1:]):
            def body(src_vmem, idx_vmem, out_hbm=out_hbm):
                for k in range(n_copies):
                    pltpu.sync_copy(src_vmem, out_hbm.at[idx_vmem.at[k]])

            pltpu.emit_pipeline(
                body,
                grid=(n_rows // window,),
                in_specs=[pl.BlockSpec((window, width), lambda i: (i, 0)),
                          pl.BlockSpec((idx.shape[0], window), lambda i: (0, i))],
                out_specs=[],
                core_axis_name=("core", "subcore"),
                dimension_semantics=(pltpu.PARALLEL,),
            )(src_hbm, idx_hbm)

    return scatter_kernel(*tables, idx)


def _sc_gather_rows(tables, idx, n_copies):
    n_rows = idx.shape[1]
    width = tables[0].shape[1]
    n_tab = len(tables)
    window = SC_WINDOW
    steps = n_rows // window

    @functools.partial(pl.kernel, mesh=_sc_mesh(),
                       out_type=[jax.ShapeDtypeStruct((n_copies * n_rows, width), tables[0].dtype)] * n_tab)
    def gather_kernel(*refs):
        idx_hbm = refs[n_tab]
        for table_hbm, out_hbm in zip(refs[:n_tab], refs[n_tab + 1:]):
            def body(idx_vmem, out_vmem, table_hbm=table_hbm):
                pltpu.sync_copy(table_hbm.at[idx_vmem.at[0]], out_vmem)

            pltpu.emit_pipeline(
                body,
                grid=(n_copies, steps),
                in_specs=[pl.BlockSpec((1, window), lambda k, i: (k, i))],
                out_specs=[pl.BlockSpec((window, width), lambda k, i: (k * steps + i, 0))],
                core_axis_name=("core", "subcore"),
                dimension_semantics=(pltpu.PARALLEL, pltpu.PARALLEL),
            )(idx_hbm, out_hbm)

    return gather_kernel(*tables, idx)


def _proj_kernel(tiles_per_seq,
                 x_ref, pos_ref, rope_ref, wg3_ref, wlat_ref, wgates_ref, convw_ref,
                 qg_ref, kvg_ref, wuqa_ref, wuk_ref, wuv_ref, wcb_ref,
                 qt_ref, kn_ref, kpe_ref, v_ref, ycg_ref, sga_ref, ubuf):
    i = pl.program_id(0)
    tm = x_ref.shape[0]
    xb = x_ref[...].astype(_BF)

    @pl.when(i % tiles_per_seq == 0)
    def _():
        ubuf[0:HALO, :] = jnp.zeros((HALO, CONV_DIM), _F32)

    g3 = _dot(xb, wg3_ref[...])
    u = g3[:, CONV_DIM:2 * CONV_DIM] * g3[:, 2 * CONV_DIM:]
    ubuf[HALO:HALO + tm, :] = u
    cw = convw_ref[...]
    conv = (cw[2:3, :] * u + cw[1:2, :] * ubuf[HALO - 1:HALO - 1 + tm, :]
            + cw[0:1, :] * ubuf[HALO - 2:HALO - 2 + tm, :])
    ubuf[0:HALO, :] = ubuf[tm:tm + HALO, :]
    yc = _dot((g3[:, :CONV_DIM] * conv).astype(_BF), wcb_ref[...])
    gates = _dot(xb, wgates_ref[...])
    ycg_ref[...] = (jax.nn.sigmoid(gates[:, :D_MODEL]) * yc).astype(_BF)
    sga_ref[...] = jax.nn.sigmoid(gates[:, D_MODEL:]).astype(_BF)

    lat = _dot(xb, wlat_ref[...])
    trig = jnp.sin(pos_ref[...].astype(_F32) * rope_ref[0:1, :] + rope_ref[3:4, :])
    cosb = trig * rope_ref[1:2, :]
    sinb = pltpu.roll(trig, QK_ROPE_DIM, axis=1) * rope_ref[2:3, :]
    q_lat = lat[:, :Q_LORA_RANK]
    rq = q_lat * lax.rsqrt(jnp.mean(q_lat * q_lat, axis=-1, keepdims=True) + RMS_EPS) * qg_ref[...]
    kv_lat = lat[:, Q_LORA_RANK:Q_LORA_RANK + KV_LORA_RANK]
    ckv = (kv_lat * lax.rsqrt(jnp.mean(kv_lat * kv_lat, axis=-1, keepdims=True) + RMS_EPS)
           * kvg_ref[...]).astype(_BF)
    o = Q_LORA_RANK + KV_LORA_RANK
    lane = lax.broadcasted_iota(jnp.int32, (tm, ROPE_PAD), 1)
    half = QK_ROPE_DIM // 2

    def swap_halves(u):
        return jnp.where(lane < half, pltpu.roll(u, ROPE_PAD - half, axis=1), pltpu.roll(u, half, axis=1))

    k_pe = lat[:, o:o + ROPE_PAD]
    kpe_ref[...] = (k_pe * cosb + swap_halves(k_pe) * sinb).astype(_BF)
    kn = _dot(ckv, wuk_ref[...]).astype(_BF)
    vv = _dot(ckv, wuv_ref[...]).astype(_BF)
    for h in range(N_HEADS):
        kn_ref[h] = kn[:, h * QK_NOPE_DIM:(h + 1) * QK_NOPE_DIM]
        v_ref[h] = vv[:, h * V_HEAD_DIM:(h + 1) * V_HEAD_DIM]

    qa_t = _dot(wuqa_ref[...], rq.T.astype(_BF))
    cos_t, sin_t = cosb.T, sinb.T
    scale = QK_HEAD_DIM ** -0.5 * LOG2E
    for h in range(N_HEADS):
        lo = h * QK_PAD
        pe_t = qa_t[lo + QK_NOPE_DIM:lo + QK_PAD, :]
        sw_t = jnp.concatenate([pe_t[half:2 * half], pe_t[0:half], pe_t[2 * half:]], axis=0)
        hi_t = pe_t * cos_t + sw_t * sin_t
        qt_ref[h, :QK_NOPE_DIM, :] = (qa_t[lo:lo + QK_NOPE_DIM, :] * scale).astype(_BF)
        qt_ref[h, QK_NOPE_DIM:, :] = (hi_t * scale).astype(_BF)


def _proj_call(x2d, pos, rope, wg3, wlat, wgates, convw, qg, kvg, wuqa, wuk, wuv, wcb, seq):
    t = x2d.shape[0]
    tm = PROJ_TM
    tps = seq // tm
    row = lambda w: pl.BlockSpec((tm, w), lambda i: (i, 0))
    heads = lambda w: pl.BlockSpec((None, N_HEADS, tm, w), lambda i: (i // tps, 0, i % tps, 0))
    head_major = lambda w: jax.ShapeDtypeStruct((t // seq, N_HEADS, seq, w), _BF)
    outs = [
        jax.ShapeDtypeStruct((t // seq, N_HEADS, QK_PAD, seq), _BF),
        head_major(QK_NOPE_DIM),
        jax.ShapeDtypeStruct((t, ROPE_PAD), _BF),
        head_major(V_HEAD_DIM),
        jax.ShapeDtypeStruct((t, D_MODEL), _BF),
        jax.ShapeDtypeStruct((t, D_MODEL), _BF),
    ]
    return pl.pallas_call(
        functools.partial(_proj_kernel, tps),
        grid=(t // tm,),
        in_specs=[row(D_MODEL), row(1)]
        + [_const_spec(a.shape)
           for a in (rope, wg3, wlat, wgates, convw, qg, kvg, wuqa, wuk, wuv, wcb)],
        out_specs=[pl.BlockSpec((None, N_HEADS, QK_PAD, tm), lambda i: (i // tps, 0, 0, i % tps)),
                   heads(QK_NOPE_DIM), row(ROPE_PAD), heads(V_HEAD_DIM),
                   row(D_MODEL), row(D_MODEL)],
        out_shape=outs,
        scratch_shapes=[pltpu.VMEM((tm + HALO, CONV_DIM), _F32)],
        compiler_params=pltpu.CompilerParams(dimension_semantics=("arbitrary",),
                                             vmem_limit_bytes=VMEM_LIMIT),
        name="proj_conv_qkv",
    )(x2d, pos, rope, wg3, wlat, wgates, convw, qg, kvg, wuqa, wuk, wuv, wcb)


def _attn_kernel(qt_ref, kn_ref, kpe_ref, v_ref, o_ref, kfull, vt, s0, s1, p0, p1, a0, a1, m_ref, acc):
    tq, tk = ATTN_TQ, ATTN_TK
    n_tiles = qt_ref.shape[1] // tq
    s_bufs, p_bufs, a_bufs = (s0, s1), (p0, p1), (a0, a1)
    stages = [(tile, blk) for tile in range(n_tiles) for blk in range(tile + 1)]

    kfull[:, :QK_NOPE_DIM] = kn_ref[...]
    kfull[:, QK_NOPE_DIM:] = kpe_ref[...]
    vt[:V_HEAD_DIM, :] = v_ref[...].T
    vt[V_HEAD_DIM:, :] = jnp.ones((vt.shape[0] - V_HEAD_DIM, vt.shape[1]), _BF)

    hk, hq = tk // 2, tq // 2

    def scores_t(t):
        tile, blk = stages[t]
        s_ref = s_bufs[t % 2]
        k0, q0 = blk * tk, tile * tq
        if blk == tile:
            s_ref[0:hk, :] = _dot(kfull[k0:k0 + hk, :], qt_ref[:, q0:q0 + tq])
            s_ref[hk:tk, hq:tq] = _dot(kfull[k0 + hk:k0 + tk, :], qt_ref[:, q0 + hq:q0 + tq])
        else:
            s_ref[...] = _dot(kfull[k0:k0 + tk, :], qt_ref[:, q0:q0 + tq])

    key_chunk = lax.broadcasted_iota(jnp.int32, (tk, tq), 0) // CHUNK
    qry_chunk = lax.broadcasted_iota(jnp.int32, (tk, tq), 1) // CHUNK
    causal = key_chunk <= qry_chunk

    def softmax_t(t):
        tile, blk = stages[t]
        par = t % 2
        if blk == tile:
            top = jnp.where(causal[0:hk, :], s_bufs[par][0:hk, :], NEG_BIG)
            bot = jnp.where(causal[hk:tk, hq:tq], s_bufs[par][hk:tk, hq:tq], NEG_BIG)
            bot_max = jnp.concatenate([jnp.full((1, hq), NEG_BIG, _F32),
                                       jnp.max(bot, axis=0, keepdims=True)], axis=1)
            blk_max = jnp.maximum(jnp.max(top, axis=0, keepdims=True), bot_max)
        else:
            s_t = s_bufs[par][...]
            blk_max = jnp.max(s_t, axis=0, keepdims=True)
        if blk == 0:
            m_new = blk_max
        else:
            m_old = m_ref[...]
            m_new = jnp.maximum(m_old, blk_max)
            a_bufs[par][...] = jnp.exp2(m_old - m_new)
        m_ref[...] = m_new
        if blk == tile:
            p_bufs[par][0:hk, :] = jnp.exp2(top - m_new).astype(_BF)
            p_bufs[par][hk:tk, hq:tq] = jnp.exp2(bot - m_new[:, hq:tq]).astype(_BF)
        else:
            p_bufs[par][...] = jnp.exp2(s_t - m_new).astype(_BF)

    def accumulate(t):
        tile, blk = stages[t]
        par = t % 2
        k0 = blk * tk
        if blk == tile:
            pv = jnp.concatenate([_dot(vt[:, k0:k0 + hk], p_bufs[par][0:hk, 0:hq]),
                                  _dot(vt[:, k0:k0 + tk], p_bufs[par][:, hq:tq])], axis=1)
        else:
            pv = _dot(vt[:, k0:k0 + tk], p_bufs[par][...])
        acc[...] = pv if blk == 0 else a_bufs[par][...] * acc[...] + pv
        if blk == tile:
            out_t = acc[:V_HEAD_DIM, :] / acc[V_HEAD_DIM:V_HEAD_DIM + 1, :]
            o_ref[tile * tq:(tile + 1) * tq, :] = out_t.T.astype(_BF)

    scores_t(0)
    for t in range(len(stages)):
        if t + 1 < len(stages):
            scores_t(t + 1)
        if t > 0:
            accumulate(t - 1)
        softmax_t(t)
    accumulate(len(stages) - 1)


def _attn_call(q, kn, kpe, v):
    b, _, _, s = q.shape
    tq = ATTN_TQ
    assert ATTN_TQ == ATTN_TK and s % tq == 0
    per_head = lambda w: pl.BlockSpec((None, None, s, w), lambda bi, h: (bi, h, 0, 0))
    return pl.pallas_call(
        _attn_kernel,
        grid=(b, N_HEADS),
        in_specs=[pl.BlockSpec((None, None, QK_PAD, s), lambda bi, h: (bi, h, 0, 0)), per_head(QK_NOPE_DIM),
                  pl.BlockSpec((None, s, ROPE_PAD), lambda bi, h: (bi, 0, 0)), per_head(V_HEAD_DIM)],
        out_specs=per_head(V_HEAD_DIM),
        out_shape=jax.ShapeDtypeStruct((b, N_HEADS, s, V_HEAD_DIM), _BF),
        scratch_shapes=[pltpu.VMEM((s, QK_PAD), _BF), pltpu.VMEM((V_HEAD_DIM + ONES_ROWS, s), _BF),
                        pltpu.VMEM((ATTN_TK, tq), _F32), pltpu.VMEM((ATTN_TK, tq), _F32),
                        pltpu.VMEM((ATTN_TK, tq), _BF), pltpu.VMEM((ATTN_TK, tq), _BF),
                        pltpu.VMEM((1, tq), _F32), pltpu.VMEM((1, tq), _F32),
                        pltpu.VMEM((1, tq), _F32),
                        pltpu.VMEM((V_HEAD_DIM + ONES_ROWS, tq), _F32)],
        compiler_params=pltpu.CompilerParams(dimension_semantics=("arbitrary", "arbitrary"),
                                             vmem_limit_bytes=VMEM_LIMIT),
        name="chunk_causal_attention",
    )(q, kn, kpe, v)


def _layer_norm(z, g, b):
    mu = jnp.mean(z, axis=-1, keepdims=True)
    zc = z - mu
    var = jnp.mean(zc * zc, axis=-1, keepdims=True)
    return zc * lax.rsqrt(var + LN_EPS) * g + b


def _post_kernel(expert_cap, attn_ref, ycg_ref, sga_ref, x_ref, wab_ref, wout_ref, g1_ref, b1_ref,
                 wr_ref, br_ref, tri_ref, h1_ref, h1pa_ref, h1pb_ref, slot_ref, tw_ref, cnt_ref, run):
    tm = x_ref.shape[0]

    @pl.when(pl.program_id(0) == 0)
    def _():
        run[...] = jnp.zeros(run.shape, _F32)

    attn = jnp.concatenate([attn_ref[h] for h in range(N_HEADS)], axis=1)
    y_attn = _dot(attn, wab_ref[...])
    merged = ycg_ref[...].astype(_F32) + sga_ref[...].astype(_F32) * y_attn
    mix = _dot(merged.astype(_BF), wout_ref[...])
    h1 = _layer_norm(DEEPNORM_ALPHA * x_ref[...] + mix, g1_ref[...], b1_ref[...])
    h1_ref[...] = h1
    h1p = _pack_rows(h1)
    h1pa_ref[...] = h1p[:, :HALF]
    h1pb_ref[...] = h1p[:, HALF:]

    nt = (((1,), (1,)), ((), ()))
    h_hi = h1.astype(_BF)
    h_lo = (h1 - h_hi.astype(_F32)).astype(_BF)
    both = lax.dot_general(wr_ref[...], h_hi, nt, preferred_element_type=_F32)
    logits = (both[:N_EXPERTS] + both[N_EXPERTS:] + br_ref[:, 0:1]
              + lax.dot_general(wr_ref[:N_EXPERTS, :], h_lo, nt, preferred_element_type=_F32))
    expert = lax.broadcasted_iota(jnp.int32, (N_EXPERTS, tm), 0)
    work = logits
    chosen = jnp.zeros((N_EXPERTS, tm), jnp.bool_)
    sels, exps = [], []
    for k in range(TOP_K):
        mx = jnp.max(work, axis=0, keepdims=True)
        sel = jnp.min(jnp.where(work == mx, expert, N_EXPERTS), axis=0, keepdims=True)
        hit = expert == sel
        sels.append(sel)
        exps.append(jnp.exp(mx - (mx if k == 0 else top0)))
        if k == 0:
            top0 = mx
        chosen = jnp.logical_or(chosen, hit)
        work = jnp.where(hit, -jnp.inf, work)
    denom = exps[0] + exps[1] + exps[2] + exps[3]

    onehot = chosen.astype(_F32)
    before = _dot(onehot.astype(_BF), tri_ref[...]) + run[:, 0:1]
    rows = []
    for k in range(TOP_K):
        rank = jnp.sum(jnp.where(expert == sels[k], before, 0.0), axis=0, keepdims=True)
        rows.append(sels[k] * expert_cap + rank.astype(jnp.int32))
    slot_ref[...] = jnp.concatenate(rows + [jnp.zeros((8 - TOP_K, tm), jnp.int32)], axis=0)
    run[...] = run[...] + jnp.sum(onehot, axis=1, keepdims=True)
    cnt_ref[...] = run[...]
    tw_t = jnp.concatenate([e / denom for e in exps] + [jnp.zeros((128 - TOP_K, tm), _F32)], axis=0)
    tw_ref[...] = tw_t.T


def _post_call(attn, ycg, sga, x2d, wab, wout, g1, b1, wr, br):
    t = x2d.shape[0]
    tm = POST_TM
    row = lambda w: pl.BlockSpec((tm, w), lambda i: (i, 0))
    tps = attn.shape[2] // tm
    tri = (jnp.arange(tm)[:, None] < jnp.arange(tm)[None, :]).astype(_BF)
    outs = [
        jax.ShapeDtypeStruct((t, D_MODEL), _F32),
        jax.ShapeDtypeStruct((t, HALF), jnp.int32),
        jax.ShapeDtypeStruct((t, HALF), jnp.int32),
        jax.ShapeDtypeStruct((8, t), jnp.int32),
        jax.ShapeDtypeStruct((t, 128), _F32),
        jax.ShapeDtypeStruct((N_EXPERTS, 128), _F32),
    ]
    out_specs = [row(D_MODEL), row(HALF), row(HALF), pl.BlockSpec((8, tm), lambda i: (0, i)),
                 row(128), pl.BlockSpec((N_EXPERTS, 128), lambda i: (0, 0))]
    return pl.pallas_call(
        functools.partial(_post_kernel, t),
        grid=(t // tm,),
        in_specs=[pl.BlockSpec((None, N_HEADS, tm, V_HEAD_DIM), lambda i: (i // tps, 0, i % tps, 0))]
        + [row(D_MODEL)] * 3
        + [_const_spec(a.shape) for a in (wab, wout, g1, b1, wr, br, tri)],
        out_specs=out_specs,
        out_shape=outs,
        scratch_shapes=[pltpu.VMEM((N_EXPERTS, 128), _F32)],
        compiler_params=pltpu.CompilerParams(dimension_semantics=("arbitrary",),
                                             vmem_limit_bytes=VMEM_LIMIT),
        name="merge_out_ln1_router",
    )(attn, ycg, sga, x2d, wab, wout, g1, b1, wr, br, tri)


def _moe_kernel(be_ref, we_ref, rb_ref, bv_ref, nused_ref, xa_ref, xb_ref, wgu_ref, bgu_ref, wd_ref, bd_ref,
                ya_ref, yb_ref, wgu_bf, wd_bf):
    i = pl.program_id(0)
    prev = be_ref[jnp.maximum(i - 1, 0)]
    fresh = jnp.logical_or(i == 0, be_ref[i] != prev)
    used = i < nused_ref[0]

    @pl.when(jnp.logical_and(fresh, used))
    def _():
        wgu_bf[...] = wgu_ref[...].astype(_BF)
        wd_bf[...] = wd_ref[...].astype(_BF)

    def ffn(r0, rows):
        a_lo, a_hi = _unpack_rows(xa_ref[r0:r0 + rows, :])
        b_lo, b_hi = _unpack_rows(xb_ref[r0:r0 + rows, :])
        xs = jnp.concatenate([a_lo, b_lo, a_hi, b_hi], axis=1).astype(_BF)
        gu = _dot(xs, wgu_bf[...]) + bgu_ref[...]
        gate = jnp.minimum(gu[:, :D_FF_EXPERT], SWIGLU_LIMIT)
        up = jnp.clip(gu[:, D_FF_EXPERT:], -SWIGLU_LIMIT, SWIGLU_LIMIT)
        hid = (up + 1.0) * (gate * jax.nn.sigmoid(SWIGLU_ALPHA * gate))
        yp = _pack_rows(_dot(hid.astype(_BF), wd_bf[...]) + bd_ref[...])
        ya_ref[r0:r0 + rows, :] = yp[:, :HALF]
        yb_ref[r0:r0 + rows, :] = yp[:, HALF:]

    bm = xa_ref.shape[0]
    step = bm // MOE_ROW_STEPS
    need = (bv_ref[i] + step - 1) // step
    for n in range(1, MOE_ROW_STEPS + 1):
        @pl.when(jnp.logical_and(used, need == n))
        def _(n=n):
            ffn(0, n * step)


def _moe_call(block_e, weight_e, block_row, block_valid, n_used, xa, xb, wgu, bgu, wd, bd):
    n_slots = xa.shape[0]
    bm = MOE_BM
    n_blocks = block_e.shape[0]
    grid_spec = pltpu.PrefetchScalarGridSpec(
        num_scalar_prefetch=5,
        grid=(n_blocks,),
        in_specs=[
            pl.BlockSpec((bm, HALF), lambda i, be, we, rb, bv, nu: (rb[i], 0)),
            pl.BlockSpec((bm, HALF), lambda i, be, we, rb, bv, nu: (rb[i], 0)),
            pl.BlockSpec((None, D_MODEL, 2 * D_FF_EXPERT), lambda i, be, we, rb, bv, nu: (we[i], 0, 0)),
            pl.BlockSpec((None, 1, 2 * D_FF_EXPERT), lambda i, be, we, rb, bv, nu: (be[i], 0, 0)),
            pl.BlockSpec((None, D_FF_EXPERT, D_MODEL), lambda i, be, we, rb, bv, nu: (we[i], 0, 0)),
            pl.BlockSpec((None, 1, D_MODEL), lambda i, be, we, rb, bv, nu: (be[i], 0, 0)),
        ],
        out_specs=[pl.BlockSpec((bm, HALF), lambda i, be, we, rb, bv, nu: (rb[i], 0))] * 2,
        scratch_shapes=[pltpu.VMEM((D_MODEL, 2 * D_FF_EXPERT), _BF),
                        pltpu.VMEM((D_FF_EXPERT, D_MODEL), _BF)],
    )
    return pl.pallas_call(
        _moe_kernel,
        grid_spec=grid_spec,
        out_shape=[jax.ShapeDtypeStruct((n_slots, HALF), jnp.int32)] * 2,
        compiler_params=pltpu.CompilerParams(dimension_semantics=("arbitrary",),
                                             vmem_limit_bytes=VMEM_LIMIT),
        name="expert_ffn",
    )(block_e, weight_e, block_row, block_valid, n_used, xa, xb, wgu, bgu, wd, bd)


def _combine_kernel(*refs):
    ya_refs, yb_refs = refs[:TOP_K], refs[TOP_K:2 * TOP_K]
    tw_ref, h1_ref, g2_ref, b2_ref, o_ref = refs[2 * TOP_K:]
    tw = tw_ref[...]
    parts = [jnp.zeros((tw.shape[0], HALF), _F32) for _ in range(4)]
    for k in range(TOP_K):
        a_lo, a_hi = _unpack_rows(ya_refs[k][...])
        b_lo, b_hi = _unpack_rows(yb_refs[k][...])
        w = tw[:, k:k + 1]
        parts = [p + w * y for p, y in zip(parts, (a_lo, b_lo, a_hi, b_hi))]
    ffn = jnp.concatenate(parts, axis=1)
    o_ref[...] = _layer_norm(DEEPNORM_ALPHA * h1_ref[...] + ffn, g2_ref[...], b2_ref[...])


def _combine_call(yga, ygb, tw, h1, g2, b2):
    t = h1.shape[0]
    tm = COMB_TM
    row = lambda w: pl.BlockSpec((tm, w), lambda i: (i, 0))
    krow = lambda k: pl.BlockSpec((tm, HALF), lambda i: (k * (t // tm) + i, 0))
    return pl.pallas_call(
        _combine_kernel,
        grid=(t // tm,),
        in_specs=[krow(k) for k in range(TOP_K)] * 2 + [row(128), row(D_MODEL),
                  _const_spec(g2.shape), _const_spec(b2.shape)],
        out_specs=row(D_MODEL),
        out_shape=jax.ShapeDtypeStruct((t, D_MODEL), _F32),
        compiler_params=pltpu.CompilerParams(dimension_semantics=("arbitrary",),
                                             vmem_limit_bytes=VMEM_LIMIT),
        name="combine_ln2",
    )(*([yga] * TOP_K + [ygb] * TOP_K), tw, h1, g2, b2)


def _prepare_weights(w_in, w_uq):
    o_q = 3 * CONV_DIM
    o_kv = o_q + Q_LORA_RANK
    o_pe = o_kv + KV_LORA_RANK
    o_gc = o_pe + QK_ROPE_DIM
    wg3 = w_in[:, :o_q]
    kpe = w_in[:, o_pe:o_gc]
    zpad = jnp.zeros((D_MODEL, ROPE_PAD - QK_ROPE_DIM), w_in.dtype)
    wlat = jnp.concatenate([w_in[:, o_q:o_pe], kpe, zpad], axis=1)
    wgates = w_in[:, o_gc:]
    wq = w_uq.reshape(Q_LORA_RANK, N_HEADS, QK_HEAD_DIM)
    wuqa = jnp.pad(wq, ((0, 0), (0, 0), (0, QK_PAD - QK_HEAD_DIM)))
    return (wg3.astype(_BF), wlat.astype(_BF), wgates.astype(_BF),
            wuqa.reshape(Q_LORA_RANK, N_HEADS * QK_PAD).T.astype(_BF))


def _rope_rows():
    half = QK_ROPE_DIM // 2
    inv_freq = ROPE_BASE ** (-jnp.arange(0, QK_ROPE_DIM, 2, dtype=_F32) / QK_ROPE_DIM)
    z = jnp.zeros((ROPE_PAD - QK_ROPE_DIM,), _F32)
    ones = jnp.ones((half,), _F32)
    rows = jnp.stack([jnp.concatenate([inv_freq] * 4),
                      jnp.concatenate([ones, ones, z]),
                      jnp.concatenate([-ones, ones, z]),
                      jnp.concatenate([ones * (np.pi / 2), ones * (np.pi / 2), z])])
    return jnp.pad(rows, ((0, 8 - rows.shape[0]), (0, 0)))


def _block_tables(counts, n_tok):
    bm = MOE_BM
    n_blocks = n_tok * TOP_K // bm + N_EXPERTS
    nblk = (counts + bm - 1) // bm
    end = jnp.cumsum(nblk)
    start = end - nblk
    n_used = end[-1]
    i = jnp.minimum(jnp.arange(n_blocks, dtype=jnp.int32), n_used - 1)
    block_e = (end[None, :] <= i[:, None]).sum(axis=1).astype(jnp.int32)
    first = jnp.sum(jnp.where(jnp.arange(N_EXPERTS)[None, :] == block_e[:, None], start[None, :], 0), axis=1)
    block_row = block_e * (n_tok // bm) + (i - first)
    fresh = jnp.concatenate([jnp.ones((1,), jnp.bool_), block_e[1:] != block_e[:-1]])
    later = jnp.where(block_e[None, :] > block_e[:, None], block_e[None, :], N_EXPERTS)
    nxt = jnp.min(later, axis=1)
    weight_e = jnp.where(jnp.logical_or(fresh, nxt == N_EXPERTS), block_e, nxt).astype(jnp.int32)
    cnt = jnp.sum(jnp.where(jnp.arange(N_EXPERTS)[None, :] == block_e[:, None], counts[None, :], 0), axis=1)
    block_valid = jnp.clip(cnt - (i - first) * bm, 0, bm).astype(jnp.int32)
    return (block_e, weight_e, block_row.astype(jnp.int32), block_valid,
            n_used.astype(jnp.int32).reshape(1))


def kernel(x, positions, w_in, conv_w, q_norm_g, w_uq, kv_norm_g, w_uk, w_uv, w_conv_branch,
           w_attn_branch, w_out, ln1_g, ln1_b, w_router, b_router, w_gate_up, b_gate_up, w_down,
           b_down, ln2_g, ln2_b):
    b, s, d = x.shape
    t = b * s
    x2d = x.reshape(t, d)
    pos = positions.reshape(t, 1).astype(jnp.int32)
    rope = _rope_rows()
    for l in range(DEPTH):
        wg3, wlat, wgates, wuqa = _prepare_weights(w_in[l], w_uq[l])
        q, kn, kpe, v, ycg, sga = _proj_call(
            x2d, pos, rope, wg3, wlat, wgates, conv_w[l], q_norm_g[l][None, :], kv_norm_g[l][None, :],
            wuqa, w_uk[l].astype(_BF), w_uv[l].astype(_BF), w_conv_branch[l].astype(_BF), s)
        attn = _attn_call(q, kn, kpe.reshape(b, s, -1), v)
        wr_t = w_router[l].T
        wr_hi = wr_t.astype(_BF)
        wr_lo = (wr_t - wr_hi.astype(_F32)).astype(_BF)
        h1, h1pa, h1pb, slots, tw_pad, counts = _post_call(
            attn, ycg, sga, x2d, w_attn_branch[l].astype(_BF), w_out[l].astype(_BF),
            ln1_g[l][None, :], ln1_b[l][None, :], jnp.concatenate([wr_hi, wr_lo], axis=0),
            jnp.broadcast_to(b_router[l][:, None], (N_EXPERTS, 128)))
        block_e, weight_e, block_row, block_valid, n_used = _block_tables(counts[:, 0].astype(jnp.int32), t)
        xa, xb = _sc_scatter_rows([h1pa, h1pb], slots, TOP_K, N_EXPERTS * t)
        ya, yb = _moe_call(block_e, weight_e, block_row, block_valid, n_used, xa, xb,
                           w_gate_up[l], b_gate_up[l][:, None, :], w_down[l], b_down[l][:, None, :])
        yga, ygb = _sc_gather_rows([ya, yb], slots, TOP_K)
        x2d = _combine_call(yga, ygb, tw_pad, h1, ln2_g[l][None, :], ln2_b[l][None, :])
    return x2d.reshape(b, s, d)
```

```python
import functools

import numpy as np
import jax
import jax.numpy as jnp
from jax import lax
from jax.experimental import pallas as pl
from jax.experimental.pallas import tpu as pltpu
from jax.experimental.pallas import tpu_sc as plsc

D_MODEL = 1024
CHUNK = 64
CONV_DIM = D_MODEL
CONV_WIDTH = 3
N_HEADS = 8
QK_NOPE_DIM = 128
QK_ROPE_DIM = 64
V_HEAD_DIM = 128
QK_HEAD_DIM = QK_NOPE_DIM + QK_ROPE_DIM
Q_LORA_RANK = 384
KV_LORA_RANK = 256
ROPE_BASE = 10000.0
N_EXPERTS = 32
TOP_K = 4
D_FF_EXPERT = D_MODEL
SWIGLU_LIMIT = 7.0
SWIGLU_ALPHA = 1.702
LN_EPS = 1e-5
RMS_EPS = 1e-6
DEPTH = 1
DEEPNORM_ALPHA = (2 * DEPTH) ** 0.25

LANES = 128
QK_PAD = 2 * LANES
ROPE_PAD = LANES

PROJ_TM = 512
ATTN_TQ = 512
ATTN_TK = 512
ONES_ROWS = 16
POST_TM = 512
MOE_BM = 512
MOE_ROW_STEPS = 4
COMB_TM = 512
SC_WINDOW = 128
PACKED = D_MODEL // 2
HALF = PACKED // 2
HALO = 8
NEG_BIG = -1e30
LOG2E = 1.4426950408889634
VMEM_LIMIT = 56 * 1024 * 1024

_BF = jnp.bfloat16
_F32 = jnp.float32


def _dot(a, b):
    return jnp.dot(a, b, preferred_element_type=_F32)


def _const_spec(shape):
    nd = len(shape)
    return pl.BlockSpec(shape, lambda *_: (0,) * nd, pipeline_mode=pl.Buffered(1))


def _pack_rows(a):
    half = a.shape[1] // 2
    lo = lax.bitcast_convert_type(a[:, :half].astype(_BF).astype(_F32), jnp.int32)
    hi = lax.bitcast_convert_type(a[:, half:].astype(_BF).astype(_F32), jnp.int32)
    return lax.shift_right_logical(lo, 16) | (hi & jnp.int32(-65536))


def _unpack_rows(p):
    lo = lax.bitcast_convert_type(lax.shift_left(p, 16), _F32)
    hi = lax.bitcast_convert_type(p & jnp.int32(-65536), _F32)
    return lo, hi


def _sc_mesh():
    return plsc.VectorSubcoreMesh(core_axis_name="core", subcore_axis_name="subcore")


def _sc_scatter_rows(tables, idx, n_copies, n_out):
    n_rows, width = tables[0].shape
    n_tab = len(tables)
    window = SC_WINDOW

    @functools.partial(pl.kernel, mesh=_sc_mesh(),
                       out_type=[jax.ShapeDtypeStruct((n_out, width), tables[0].dtype)] * n_tab)
    def scatter_kernel(*refs):
        idx_hbm = refs[n_tab]
        for src_hbm, out_hbm in zip(refs[:n_tab], refs[n_tab + 1:]):
            def body(src_vmem, idx_vmem, out_hbm=out_hbm):
                for k in range(n_copies):
                    pltpu.sync_copy(src_vmem, out_hbm.at[idx_vmem.at[k]])

            pltpu.emit_pipeline(
                body,
                grid=(n_rows // window,),
                in_specs=[pl.BlockSpec((window, width), lambda i: (i, 0)),
                          pl.BlockSpec((idx.shape[0], window), lambda i: (0, i))],
                out_specs=[],
                core_axis_name=("core", "subcore"),
                dimension_semantics=(pltpu.PARALLEL,),
            )(src_hbm, idx_hbm)

    return scatter_kernel(*tables, idx)


def _sc_gather_rows(tables, idx, n_copies):
    n_rows = idx.shape[1]
    width = tables[0].shape[1]
    n_tab = len(tables)
    window = SC_WINDOW
    steps = n_rows // window

    @functools.partial(pl.kernel, mesh=_sc_mesh(),
                       out_type=[jax.ShapeDtypeStruct((n_copies * n_rows, width), tables[0].dtype)] * n_tab)
    def gather_kernel(*refs):
        idx_hbm = refs[n_tab]
        for table_hbm, out_hbm in zip(refs[:n_tab], refs[n_tab + 1:]):
            def body(idx_vmem, out_vmem, table_hbm=table_hbm):
                pltpu.sync_copy(table_hbm.at[idx_vmem.at[0]], out_vmem)

            pltpu.emit_pipeline(
                body,
                grid=(n_copies, steps),
                in_specs=[pl.BlockSpec((1, window), lambda k, i: (k, i))],
                out_specs=[pl.BlockSpec((window, width), lambda k, i: (k * steps + i, 0))],
                core_axis_name=("core", "subcore"),
                dimension_semantics=(pltpu.PARALLEL, pltpu.PARALLEL),
            )(idx_hbm, out_hbm)

    return gather_kernel(*tables, idx)


def _proj_kernel(tiles_per_seq,
                 x_ref, pos_ref, rope_ref, wg3_ref, wlat_ref, wgates_ref, convw_ref,
                 qg_ref, kvg_ref, wuqa_ref, wuk_ref, wuv_ref, wcb_ref,
                 qt_ref, kn_ref, kpe_ref, v_ref, ycg_ref, sga_ref, ubuf):
    i = pl.program_id(0)
    tm = x_ref.shape[0]
    xb = x_ref[...].astype(_BF)

    @pl.when(i % tiles_per_seq == 0)
    def _():
        ubuf[0:HALO, :] = jnp.zeros((HALO, CONV_DIM), _F32)

    g3 = _dot(xb, wg3_ref[...])
    u = g3[:, CONV_DIM:2 * CONV_DIM] * g3[:, 2 * CONV_DIM:]
    ubuf[HALO:HALO + tm, :] = u
    cw = convw_ref[...]
    conv = (cw[2:3, :] * u + cw[1:2, :] * ubuf[HALO - 1:HALO - 1 + tm, :]
            + cw[0:1, :] * ubuf[HALO - 2:HALO - 2 + tm, :])
    ubuf[0:HALO, :] = ubuf[tm:tm + HALO, :]
    yc = _dot((g3[:, :CONV_DIM] * conv).astype(_BF), wcb_ref[...])
    gates = _dot(xb, wgates_ref[...])
    ycg_ref[...] = (jax.nn.sigmoid(gates[:, :D_MODEL]) * yc).astype(_BF)
    sga_ref[...] = jax.nn.sigmoid(gates[:, D_MODEL:]).astype(_BF)

    lat = _dot(xb, wlat_ref[...])
    trig = jnp.sin(pos_ref[...].astype(_F32) * rope_ref[0:1, :] + rope_ref[3:4, :])
    cosb = trig * rope_ref[1:2, :]
    sinb = pltpu.roll(trig, QK_ROPE_DIM, axis=1) * rope_ref[2:3, :]
    q_lat = lat[:, :Q_LORA_RANK]
    rq = q_lat * lax.rsqrt(jnp.mean(q_lat * q_lat, axis=-1, keepdims=True) + RMS_EPS) * qg_ref[...]
    kv_lat = lat[:, Q_LORA_RANK:Q_LORA_RANK + KV_LORA_RANK]
    ckv = (kv_lat * lax.rsqrt(jnp.mean(kv_lat * kv_lat, axis=-1, keepdims=True) + RMS_EPS)
           * kvg_ref[...]).astype(_BF)
    o = Q_LORA_RANK + KV_LORA_RANK
    lane = lax.broadcasted_iota(jnp.int32, (tm, ROPE_PAD), 1)
    half = QK_ROPE_DIM // 2

    def swap_halves(u):
        return jnp.where(lane < half, pltpu.roll(u, ROPE_PAD - half, axis=1), pltpu.roll(u, half, axis=1))

    k_pe = lat[:, o:o + ROPE_PAD]
    kpe_ref[...] = (k_pe * cosb + swap_halves(k_pe) * sinb).astype(_BF)
    kn = _dot(ckv, wuk_ref[...]).astype(_BF)
    vv = _dot(ckv, wuv_ref[...]).astype(_BF)
    for h in range(N_HEADS):
        kn_ref[h] = kn[:, h * QK_NOPE_DIM:(h + 1) * QK_NOPE_DIM]
        v_ref[h] = vv[:, h * V_HEAD_DIM:(h + 1) * V_HEAD_DIM]

    qa_t = _dot(wuqa_ref[...], rq.T.astype(_BF))
    cos_t, sin_t = cosb.T, sinb.T
    scale = QK_HEAD_DIM ** -0.5 * LOG2E
    for h in range(N_HEADS):
        lo = h * QK_PAD
        pe_t = qa_t[lo + QK_NOPE_DIM:lo + QK_PAD, :]
        sw_t = jnp.concatenate([pe_t[half:2 * half], pe_t[0:half], pe_t[2 * half:]], axis=0)
        hi_t = pe_t * cos_t + sw_t * sin_t
        qt_ref[h, :QK_NOPE_DIM, :] = (qa_t[lo:lo + QK_NOPE_DIM, :] * scale).astype(_BF)
        qt_ref[h, QK_NOPE_DIM:, :] = (hi_t * scale).astype(_BF)


def _proj_call(x2d, pos, rope, wg3, wlat, wgates, convw, qg, kvg, wuqa, wuk, wuv, wcb, seq):
    t = x2d.shape[0]
    tm = PROJ_TM
    tps = seq // tm
    row = lambda w: pl.BlockSpec((tm, w), lambda i: (i, 0))
    heads = lambda w: pl.BlockSpec((None, N_HEADS, tm, w), lambda i: (i // tps, 0, i % tps, 0))
    head_major = lambda w: jax.ShapeDtypeStruct((t // seq, N_HEADS, seq, w), _BF)
    outs = [
        jax.ShapeDtypeStruct((t // seq, N_HEADS, QK_PAD, seq), _BF),
        head_major(QK_NOPE_DIM),
        jax.ShapeDtypeStruct((t, ROPE_PAD), _BF),
        head_major(V_HEAD_DIM),
        jax.ShapeDtypeStruct((t, D_MODEL), _BF),
        jax.ShapeDtypeStruct((t, D_MODEL), _BF),
    ]
    return pl.pallas_call(
        functools.partial(_proj_kernel, tps),
        grid=(t // tm,),
        in_specs=[row(D_MODEL), row(1)]
        + [_const_spec(a.shape)
           for a in (rope, wg3, wlat, wgates, convw, qg, kvg, wuqa, wuk, wuv, wcb)],
        out_specs=[pl.BlockSpec((None, N_HEADS, QK_PAD, tm), lambda i: (i // tps, 0, 0, i % tps)),
                   heads(QK_NOPE_DIM), row(ROPE_PAD), heads(V_HEAD_DIM),
                   row(D_MODEL), row(D_MODEL)],
        out_shape=outs,
        scratch_shapes=[pltpu.VMEM((tm + HALO, CONV_DIM), _F32)],
        compiler_params=pltpu.CompilerParams(dimension_semantics=("arbitrary",),
                                             vmem_limit_bytes=VMEM_LIMIT),
        name="proj_conv_qkv",
    )(x2d, pos, rope, wg3, wlat, wgates, convw, qg, kvg, wuqa, wuk, wuv, wcb)


def _attn_kernel(qt_ref, kn_ref, kpe_ref, v_ref, o_ref, kfull, vt, s0, s1, p0, p1, a0, a1, m_ref, acc):
    tq, tk = ATTN_TQ, ATTN_TK
    n_tiles = qt_ref.shape[1] // tq
    s_bufs, p_bufs, a_bufs = (s0, s1), (p0, p1), (a0, a1)
    stages = [(tile, blk) for tile in range(n_tiles) for blk in range(tile + 1)]

    kfull[:, :QK_NOPE_DIM] = kn_ref[...]
    kfull[:, QK_NOPE_DIM:] = kpe_ref[...]
    vt[:V_HEAD_DIM, :] = v_ref[...].T
    vt[V_HEAD_DIM:, :] = jnp.ones((vt.shape[0] - V_HEAD_DIM, vt.shape[1]), _BF)

    hk, hq = tk // 2, tq // 2

    def scores_t(t):
        tile, blk = stages[t]
        s_ref = s_bufs[t % 2]
        k0, q0 = blk * tk, tile * tq
        if blk == tile:
            s_ref[0:hk, :] = _dot(kfull[k0:k0 + hk, :], qt_ref[:, q0:q0 + tq])
            s_ref[hk:tk, hq:tq] = _dot(kfull[k0 + hk:k0 + tk, :], qt_ref[:, q0 + hq:q0 + tq])
        else:
            s_ref[...] = _dot(kfull[k0:k0 + tk, :], qt_ref[:, q0:q0 + tq])

    key_chunk = lax.broadcasted_iota(jnp.int32, (tk, tq), 0) // CHUNK
    qry_chunk = lax.broadcasted_iota(jnp.int32, (tk, tq), 1) // CHUNK
    causal = key_chunk <= qry_chunk

    def softmax_t(t):
        tile, blk = stages[t]
        par = t % 2
        if blk == tile:
            top = jnp.where(causal[0:hk, :], s_bufs[par][0:hk, :], NEG_BIG)
            bot = jnp.where(causal[hk:tk, hq:tq], s_bufs[par][hk:tk, hq:tq], NEG_BIG)
            bot_max = jnp.concatenate([jnp.full((1, hq), NEG_BIG, _F32),
                                       jnp.max(bot, axis=0, keepdims=True)], axis=1)
            blk_max = jnp.maximum(jnp.max(top, axis=0, keepdims=True), bot_max)
        else:
            s_t = s_bufs[par][...]
            blk_max = jnp.max(s_t, axis=0, keepdims=True)
        if blk == 0:
            m_new = blk_max
        else:
            m_old = m_ref[...]
            m_new = jnp.maximum(m_old, blk_max)
            a_bufs[par][...] = jnp.exp2(m_old - m_new)
        m_ref[...] = m_new
        if blk == tile:
            p_bufs[par][0:hk, :] = jnp.exp2(top - m_new).astype(_BF)
            p_bufs[par][hk:tk, hq:tq] = jnp.exp2(bot - m_new[:, hq:tq]).astype(_BF)
        else:
            p_bufs[par][...] = jnp.exp2(s_t - m_new).astype(_BF)

    def accumulate(t):
        tile, blk = stages[t]
        par = t % 2
        k0 = blk * tk
        if blk == tile:
            pv = jnp.concatenate([_dot(vt[:, k0:k0 + hk], p_bufs[par][0:hk, 0:hq]),
                                  _dot(vt[:, k0:k0 + tk], p_bufs[par][:, hq:tq])], axis=1)
        else:
            pv = _dot(vt[:, k0:k0 + tk], p_bufs[par][...])
        acc[...] = pv if blk == 0 else a_bufs[par][...] * acc[...] + pv
        if blk == tile:
            out_t = acc[:V_HEAD_DIM, :] / acc[V_HEAD_DIM:V_HEAD_DIM + 1, :]
            o_ref[tile * tq:(tile + 1) * tq, :] = out_t.T.astype(_BF)

    scores_t(0)
    for t in range(len(stages)):
        if t + 1 < len(stages):
            scores_t(t + 1)
        if t > 0:
            accumulate(t - 1)
        softmax_t(t)
    accumulate(len(stages) - 1)


def _attn_call(q, kn, kpe, v):
    b, _, _, s = q.shape
    tq = ATTN_TQ
    assert ATTN_TQ == ATTN_TK and s % tq == 0
    per_head = lambda w: pl.BlockSpec((None, None, s, w), lambda bi, h: (bi, h, 0, 0))
    return pl.pallas_call(
        _attn_kernel,
        grid=(b, N_HEADS),
        in_specs=[pl.BlockSpec((None, None, QK_PAD, s), lambda bi, h: (bi, h, 0, 0)), per_head(QK_NOPE_DIM),
                  pl.BlockSpec((None, s, ROPE_PAD), lambda bi, h: (bi, 0, 0)), per_head(V_HEAD_DIM)],
        out_specs=per_head(V_HEAD_DIM),
        out_shape=jax.ShapeDtypeStruct((b, N_HEADS, s, V_HEAD_DIM), _BF),
        scratch_shapes=[pltpu.VMEM((s, QK_PAD), _BF), pltpu.VMEM((V_HEAD_DIM + ONES_ROWS, s), _BF),
                        pltpu.VMEM((ATTN_TK, tq), _F32), pltpu.VMEM((ATTN_TK, tq), _F32),
                        pltpu.VMEM((ATTN_TK, tq), _BF), pltpu.VMEM((ATTN_TK, tq), _BF),
                        pltpu.VMEM((1, tq), _F32), pltpu.VMEM((1, tq), _F32),
                        pltpu.VMEM((1, tq), _F32),
                        pltpu.VMEM((V_HEAD_DIM + ONES_ROWS, tq), _F32)],
        compiler_params=pltpu.CompilerParams(dimension_semantics=("arbitrary", "arbitrary"),
                                             vmem_limit_bytes=VMEM_LIMIT),
        name="chunk_causal_attention",
    )(q, kn, kpe, v)


def _layer_norm(z, g, b):
    mu = jnp.mean(z, axis=-1, keepdims=True)
    zc = z - mu
    var = jnp.mean(zc * zc, axis=-1, keepdims=True)
    return zc * lax.rsqrt(var + LN_EPS) * g + b


def _post_kernel(expert_cap, attn_ref, ycg_ref, sga_ref, x_ref, wab_ref, wout_ref, g1_ref, b1_ref,
                 wr_ref, br_ref, tri_ref, h1_ref, h1pa_ref, h1pb_ref, slot_ref, tw_ref, cnt_ref, run):
    tm = x_ref.shape[0]

    @pl.when(pl.program_id(0) == 0)
    def _():
        run[...] = jnp.zeros(run.shape, _F32)

    attn = jnp.concatenate([attn_ref[h] for h in range(N_HEADS)], axis=1)
    y_attn = _dot(attn, wab_ref[...])
    merged = ycg_ref[...].astype(_F32) + sga_ref[...].astype(_F32) * y_attn
    mix = _dot(merged.astype(_BF), wout_ref[...])
    h1 = _layer_norm(DEEPNORM_ALPHA * x_ref[...] + mix, g1_ref[...], b1_ref[...])
    h1_ref[...] = h1
    h1p = _pack_rows(h1)
    h1pa_ref[...] = h1p[:, :HALF]
    h1pb_ref[...] = h1p[:, HALF:]

    nt = (((1,), (1,)), ((), ()))
    h_hi = h1.astype(_BF)
    h_lo = (h1 - h_hi.astype(_F32)).astype(_BF)
    both = lax.dot_general(wr_ref[...], h_hi, nt, preferred_element_type=_F32)
    logits = (both[:N_EXPERTS] + both[N_EXPERTS:] + br_ref[:, 0:1]
              + lax.dot_general(wr_ref[:N_EXPERTS, :], h_lo, nt, preferred_element_type=_F32))
    expert = lax.broadcasted_iota(jnp.int32, (N_EXPERTS, tm), 0)
    work = logits
    chosen = jnp.zeros((N_EXPERTS, tm), jnp.bool_)
    sels, exps = [], []
    for k in range(TOP_K):
        mx = jnp.max(work, axis=0, keepdims=True)
        sel = jnp.min(jnp.where(work == mx, expert, N_EXPERTS), axis=0, keepdims=True)
        hit = expert == sel
        sels.append(sel)
        exps.append(jnp.exp(mx - (mx if k == 0 else top0)))
        if k == 0:
            top0 = mx
        chosen = jnp.logical_or(chosen, hit)
        work = jnp.where(hit, -jnp.inf, work)
    denom = exps[0] + exps[1] + exps[2] + exps[3]

    onehot = chosen.astype(_F32)
    before = _dot(onehot.astype(_BF), tri_ref[...]) + run[:, 0:1]
    rows = []
    for k in range(TOP_K):
        rank = jnp.sum(jnp.where(expert == sels[k], before, 0.0), axis=0, keepdims=True)
        rows.append(sels[k] * expert_cap + rank.astype(jnp.int32))
    slot_ref[...] = jnp.concatenate(rows + [jnp.zeros((8 - TOP_K, tm), jnp.int32)], axis=0)
    run[...] = run[...] + jnp.sum(onehot, axis=1, keepdims=True)
    cnt_ref[...] = run[...]
    tw_t = jnp.concatenate([e / denom for e in exps] + [jnp.zeros((LANES - TOP_K, tm), _F32)], axis=0)
    tw_ref[...] = tw_t.T


def _post_call(attn, ycg, sga, x2d, wab, wout, g1, b1, wr, br):
    t = x2d.shape[0]
    tm = POST_TM
    row = lambda w: pl.BlockSpec((tm, w), lambda i: (i, 0))
    tps = attn.shape[2] // tm
    tri = (jnp.arange(tm)[:, None] < jnp.arange(tm)[None, :]).astype(_BF)
    outs = [
        jax.ShapeDtypeStruct((t, D_MODEL), _F32),
        jax.ShapeDtypeStruct((t, HALF), jnp.int32),
        jax.ShapeDtypeStruct((t, HALF), jnp.int32),
        jax.ShapeDtypeStruct((8, t), jnp.int32),
        jax.ShapeDtypeStruct((t, LANES), _F32),
        jax.ShapeDtypeStruct((N_EXPERTS, LANES), _F32),
    ]
    out_specs = [row(D_MODEL), row(HALF), row(HALF), pl.BlockSpec((8, tm), lambda i: (0, i)),
                 row(LANES), pl.BlockSpec((N_EXPERTS, LANES), lambda i: (0, 0))]
    return pl.pallas_call(
        functools.partial(_post_kernel, t),
        grid=(t // tm,),
        in_specs=[pl.BlockSpec((None, N_HEADS, tm, V_HEAD_DIM), lambda i: (i // tps, 0, i % tps, 0))]
        + [row(D_MODEL)] * 3
        + [_const_spec(a.shape) for a in (wab, wout, g1, b1, wr, br, tri)],
        out_specs=out_specs,
        out_shape=outs,
        scratch_shapes=[pltpu.VMEM((N_EXPERTS, LANES), _F32)],
        compiler_params=pltpu.CompilerParams(dimension_semantics=("arbitrary",),
                                             vmem_limit_bytes=VMEM_LIMIT),
        name="merge_out_ln1_router",
    )(attn, ycg, sga, x2d, wab, wout, g1, b1, wr, br, tri)


def _moe_kernel(be_ref, we_ref, rb_ref, bv_ref, nused_ref, xa_ref, xb_ref, wgu_ref, bgu_ref, wd_ref, bd_ref,
                ya_ref, yb_ref, wgu_bf, wd_bf):
    i = pl.program_id(0)
    prev = be_ref[jnp.maximum(i - 1, 0)]
    fresh = jnp.logical_or(i == 0, be_ref[i] != prev)
    used = i < nused_ref[0]

    @pl.when(jnp.logical_and(fresh, used))
    def _():
        wgu_bf[...] = wgu_ref[...].astype(_BF)
        wd_bf[...] = wd_ref[...].astype(_BF)

    def ffn(r0, rows):
        a_lo, a_hi = _unpack_rows(xa_ref[r0:r0 + rows, :])
        b_lo, b_hi = _unpack_rows(xb_ref[r0:r0 + rows, :])
        xs = jnp.concatenate([a_lo, b_lo, a_hi, b_hi], axis=1).astype(_BF)
        gu = _dot(xs, wgu_bf[...]) + bgu_ref[...]
        gate = jnp.minimum(gu[:, :D_FF_EXPERT], SWIGLU_LIMIT)
        up = jnp.clip(gu[:, D_FF_EXPERT:], -SWIGLU_LIMIT, SWIGLU_LIMIT)
        hid = (up + 1.0) * (gate * jax.nn.sigmoid(SWIGLU_ALPHA * gate))
        yp = _pack_rows(_dot(hid.astype(_BF), wd_bf[...]) + bd_ref[...])
        ya_ref[r0:r0 + rows, :] = yp[:, :HALF]
        yb_ref[r0:r0 + rows, :] = yp[:, HALF:]

    bm = xa_ref.shape[0]
    step = bm // MOE_ROW_STEPS
    need = (bv_ref[i] + step - 1) // step
    for n in range(1, MOE_ROW_STEPS + 1):
        @pl.when(jnp.logical_and(used, need == n))
        def _(n=n):
            ffn(0, n * step)


def _moe_call(block_e, weight_e, block_row, block_valid, n_used, xa, xb, wgu, bgu, wd, bd):
    n_slots = xa.shape[0]
    bm = MOE_BM
    n_blocks = block_e.shape[0]
    grid_spec = pltpu.PrefetchScalarGridSpec(
        num_scalar_prefetch=5,
        grid=(n_blocks,),
        in_specs=[
            pl.BlockSpec((bm, HALF), lambda i, be, we, rb, bv, nu: (rb[i], 0)),
            pl.BlockSpec((bm, HALF), lambda i, be, we, rb, bv, nu: (rb[i], 0)),
            pl.BlockSpec((None, D_MODEL, 2 * D_FF_EXPERT), lambda i, be, we, rb, bv, nu: (we[i], 0, 0)),
            pl.BlockSpec((None, 1, 2 * D_FF_EXPERT), lambda i, be, we, rb, bv, nu: (be[i], 0, 0)),
            pl.BlockSpec((None, D_FF_EXPERT, D_MODEL), lambda i, be, we, rb, bv, nu: (we[i], 0, 0)),
            pl.BlockSpec((None, 1, D_MODEL), lambda i, be, we, rb, bv, nu: (be[i], 0, 0)),
        ],
        out_specs=[pl.BlockSpec((bm, HALF), lambda i, be, we, rb, bv, nu: (rb[i], 0))] * 2,
        scratch_shapes=[pltpu.VMEM((D_MODEL, 2 * D_FF_EXPERT), _BF),
                        pltpu.VMEM((D_FF_EXPERT, D_MODEL), _BF)],
    )
    return pl.pallas_call(
        _moe_kernel,
        grid_spec=grid_spec,
        out_shape=[jax.ShapeDtypeStruct((n_slots, HALF), jnp.int32)] * 2,
        compiler_params=pltpu.CompilerParams(dimension_semantics=("arbitrary",),
                                             vmem_limit_bytes=VMEM_LIMIT),
        name="expert_ffn",
    )(block_e, weight_e, block_row, block_valid, n_used, xa, xb, wgu, bgu, wd, bd)


def _combine_kernel(*refs):
    ya_refs, yb_refs = refs[:TOP_K], refs[TOP_K:2 * TOP_K]
    tw_ref, h1_ref, g2_ref, b2_ref, o_ref = refs[2 * TOP_K:]
    tw = tw_ref[...]
    parts = [jnp.zeros((tw.shape[0], HALF), _F32) for _ in range(4)]
    for k in range(TOP_K):
        a_lo, a_hi = _unpack_rows(ya_refs[k][...])
        b_lo, b_hi = _unpack_rows(yb_refs[k][...])
        w = tw[:, k:k + 1]
        parts = [p + w * y for p, y in zip(parts, (a_lo, b_lo, a_hi, b_hi))]
    ffn = jnp.concatenate(parts, axis=1)
    o_ref[...] = _layer_norm(DEEPNORM_ALPHA * h1_ref[...] + ffn, g2_ref[...], b2_ref[...])


def _combine_call(yga, ygb, tw, h1, g2, b2):
    t = h1.shape[0]
    tm = COMB_TM
    row = lambda w: pl.BlockSpec((tm, w), lambda i: (i, 0))
    krow = lambda k: pl.BlockSpec((tm, HALF), lambda i: (k * (t // tm) + i, 0))
    return pl.pallas_call(
        _combine_kernel,
        grid=(t // tm,),
        in_specs=[krow(k) for k in range(TOP_K)] * 2 + [row(LANES), row(D_MODEL),
                  _const_spec(g2.shape), _const_spec(b2.shape)],
        out_specs=row(D_MODEL),
        out_shape=jax.ShapeDtypeStruct((t, D_MODEL), _F32),
        compiler_params=pltpu.CompilerParams(dimension_semantics=("arbitrary",),
                                             vmem_limit_bytes=VMEM_LIMIT),
        name="combine_ln2",
    )(*([yga] * TOP_K + [ygb] * TOP_K), tw, h1, g2, b2)


def _prepare_weights(w_in, w_uq):
    o_q = 3 * CONV_DIM
    o_kv = o_q + Q_LORA_RANK
    o_pe = o_kv + KV_LORA_RANK
    o_gc = o_pe + QK_ROPE_DIM
    wg3 = w_in[:, :o_q]
    kpe = w_in[:, o_pe:o_gc]
    zpad = jnp.zeros((D_MODEL, ROPE_PAD - QK_ROPE_DIM), w_in.dtype)
    wlat = jnp.concatenate([w_in[:, o_q:o_pe], kpe, zpad], axis=1)
    wgates = w_in[:, o_gc:]
    wq = w_uq.reshape(Q_LORA_RANK, N_HEADS, QK_HEAD_DIM)
    wuqa = jnp.pad(wq, ((0, 0), (0, 0), (0, QK_PAD - QK_HEAD_DIM)))
    return (wg3.astype(_BF), wlat.astype(_BF), wgates.astype(_BF),
            wuqa.reshape(Q_LORA_RANK, N_HEADS * QK_PAD).T.astype(_BF))


def _rope_rows():
    half = QK_ROPE_DIM // 2
    inv_freq = ROPE_BASE ** (-jnp.arange(0, QK_ROPE_DIM, 2, dtype=_F32) / QK_ROPE_DIM)
    z = jnp.zeros((ROPE_PAD - QK_ROPE_DIM,), _F32)
    ones = jnp.ones((half,), _F32)
    rows = jnp.stack([jnp.concatenate([inv_freq] * 4),
                      jnp.concatenate([ones, ones, z]),
                      jnp.concatenate([-ones, ones, z]),
                      jnp.concatenate([ones * (np.pi / 2), ones * (np.pi / 2), z])])
    return jnp.pad(rows, ((0, 8 - rows.shape[0]), (0, 0)))


def _block_tables(counts, n_tok):
    bm = MOE_BM
    n_blocks = n_tok * TOP_K // bm + N_EXPERTS
    nblk = (counts + bm - 1) // bm
    end = jnp.cumsum(nblk)
    start = end - nblk
    n_used = end[-1]
    i = jnp.minimum(jnp.arange(n_blocks, dtype=jnp.int32), n_used - 1)
    block_e = (end[None, :] <= i[:, None]).sum(axis=1).astype(jnp.int32)
    first = jnp.sum(jnp.where(jnp.arange(N_EXPERTS)[None, :] == block_e[:, None], start[None, :], 0), axis=1)
    block_row = block_e * (n_tok // bm) + (i - first)
    fresh = jnp.concatenate([jnp.ones((1,), jnp.bool_), block_e[1:] != block_e[:-1]])
    later = jnp.where(block_e[None, :] > block_e[:, None], block_e[None, :], N_EXPERTS)
    nxt = jnp.min(later, axis=1)
    weight_e = jnp.where(jnp.logical_or(fresh, nxt == N_EXPERTS), block_e, nxt).astype(jnp.int32)
    cnt = jnp.sum(jnp.where(jnp.arange(N_EXPERTS)[None, :] == block_e[:, None], counts[None, :], 0), axis=1)
    block_valid = jnp.clip(cnt - (i - first) * bm, 0, bm).astype(jnp.int32)
    return (block_e, weight_e, block_row.astype(jnp.int32), block_valid,
            n_used.astype(jnp.int32).reshape(1))


def kernel(x, positions, w_in, conv_w, q_norm_g, w_uq, kv_norm_g, w_uk, w_uv, w_conv_branch,
           w_attn_branch, w_out, ln1_g, ln1_b, w_router, b_router, w_gate_up, b_gate_up, w_down,
           b_down, ln2_g, ln2_b):
    b, s, d = x.shape
    t = b * s
    x2d = x.reshape(t, d)
    pos = positions.reshape(t, 1).astype(jnp.int32)
    rope = _rope_rows()
    for l in range(DEPTH):
        wg3, wlat, wgates, wuqa = _prepare_weights(w_in[l], w_uq[l])
        q, kn, kpe, v, ycg, sga = _proj_call(
            x2d, pos, rope, wg3, wlat, wgates, conv_w[l], q_norm_g[l][None, :], kv_norm_g[l][None, :],
            wuqa, w_uk[l].astype(_BF), w_uv[l].astype(_BF), w_conv_branch[l].astype(_BF), s)
        attn = _attn_call(q, kn, kpe.reshape(b, s, -1), v)
        wr_t = w_router[l].T
        wr_hi = wr_t.astype(_BF)
        wr_lo = (wr_t - wr_hi.astype(_F32)).astype(_BF)
        h1, h1pa, h1pb, slots, tw_pad, counts = _post_call(
            attn, ycg, sga, x2d, w_attn_branch[l].astype(_BF), w_out[l].astype(_BF),
            ln1_g[l][None, :], ln1_b[l][None, :], jnp.concatenate([wr_hi, wr_lo], axis=0),
            jnp.broadcast_to(b_router[l][:, None], (N_EXPERTS, LANES)))
        block_e, weight_e, block_row, block_valid, n_used = _block_tables(counts[:, 0].astype(jnp.int32), t)
        xa, xb = _sc_scatter_rows([h1pa, h1pb], slots, TOP_K, N_EXPERTS * t)
        ya, yb = _moe_call(block_e, weight_e, block_row, block_valid, n_used, xa, xb,
                           w_gate_up[l], b_gate_up[l][:, None, :], w_down[l], b_down[l][:, None, :])
        yga, ygb = _sc_gather_rows([ya, yb], slots, TOP_K)
        x2d = _combine_call(yga, ygb, tw_pad, h1, ln2_g[l][None, :], ln2_b[l][None, :])
    return x2d.reshape(b, s, d)
```

```python
import functools

import numpy as np
import jax
import jax.numpy as jnp
from jax import lax
from jax.experimental import pallas as pl
from jax.experimental.pallas import tpu as pltpu
from jax.experimental.pallas import tpu_sc as plsc

D_MODEL = 1024
CHUNK = 64
CONV_DIM = D_MODEL
CONV_WIDTH = 3
N_HEADS = 8
QK_NOPE_DIM = 128
QK_ROPE_DIM = 64
V_HEAD_DIM = 128
QK_HEAD_DIM = QK_NOPE_DIM + QK_ROPE_DIM
Q_LORA_RANK = 384
KV_LORA_RANK = 256
ROPE_BASE = 10000.0
N_EXPERTS = 32
TOP_K = 4
D_FF_EXPERT = D_MODEL
SWIGLU_LIMIT = 7.0
SWIGLU_ALPHA = 1.702
LN_EPS = 1e-5
RMS_EPS = 1e-6
DEPTH = 1
DEEPNORM_ALPHA = (2 * DEPTH) ** 0.25

LANES = 128
QK_PAD = 2 * LANES
ROPE_PAD = LANES

PROJ_TM = 512
ATTN_TQ = 512
ATTN_TK = 512
ONES_ROWS = 16
POST_TM = 512
MOE_BM = 1024
MOE_ROW_STEPS = 8
COMB_TM = 512
SC_WINDOW = 128
PACKED = D_MODEL // 2
HALF = PACKED // 2
HALO = 8
NEG_BIG = -1e30
LOG2E = 1.4426950408889634
VMEM_LIMIT = 56 * 1024 * 1024

_BF = jnp.bfloat16
_F32 = jnp.float32


def _dot(a, b):
    return jnp.dot(a, b, preferred_element_type=_F32)


def _const_spec(shape):
    nd = len(shape)
    return pl.BlockSpec(shape, lambda *_: (0,) * nd, pipeline_mode=pl.Buffered(1))


def _pack_rows(a):
    half = a.shape[1] // 2
    lo = lax.bitcast_convert_type(a[:, :half].astype(_BF).astype(_F32), jnp.int32)
    hi = lax.bitcast_convert_type(a[:, half:].astype(_BF).astype(_F32), jnp.int32)
    return lax.shift_right_logical(lo, 16) | (hi & jnp.int32(-65536))


def _unpack_rows(p):
    lo = lax.bitcast_convert_type(lax.shift_left(p, 16), _F32)
    hi = lax.bitcast_convert_type(p & jnp.int32(-65536), _F32)
    return lo, hi


def _sc_mesh():
    return plsc.VectorSubcoreMesh(core_axis_name="core", subcore_axis_name="subcore")


def _sc_scatter_rows(tables, idx, n_copies, n_out):
    n_rows, width = tables[0].shape
    n_tab = len(tables)
    window = SC_WINDOW

    @functools.partial(pl.kernel, mesh=_sc_mesh(),
                       out_type=[jax.ShapeDtypeStruct((n_out, width), tables[0].dtype)] * n_tab)
    def scatter_kernel(*refs):
        idx_hbm = refs[n_tab]
        for src_hbm, out_hbm in zip(refs[:n_tab], refs[n_tab + 1:]):
            def body(src_vmem, idx_vmem, out_hbm=out_hbm):
                for k in range(n_copies):
                    pltpu.sync_copy(src_vmem, out_hbm.at[idx_vmem.at[k]])

            pltpu.emit_pipeline(
                body,
                grid=(n_rows // window,),
                in_specs=[pl.BlockSpec((window, width), lambda i: (i, 0)),
                          pl.BlockSpec((idx.shape[0], window), lambda i: (0, i))],
                out_specs=[],
                core_axis_name=("core", "subcore"),
                dimension_semantics=(pltpu.PARALLEL,),
            )(src_hbm, idx_hbm)

    return scatter_kernel(*tables, idx)


def _sc_gather_rows(tables, idx, n_copies):
    n_rows = idx.shape[1]
    width = tables[0].shape[1]
    n_tab = len(tables)
    window = SC_WINDOW
    steps = n_rows // window

    @functools.partial(pl.kernel, mesh=_sc_mesh(),
                       out_type=[jax.ShapeDtypeStruct((n_copies * n_rows, width), tables[0].dtype)] * n_tab)
    def gather_kernel(*refs):
        idx_hbm = refs[n_tab]
        for table_hbm, out_hbm in zip(refs[:n_tab], refs[n_tab + 1:]):
            def body(idx_vmem, out_vmem, table_hbm=table_hbm):
                pltpu.sync_copy(table_hbm.at[idx_vmem.at[0]], out_vmem)

            pltpu.emit_pipeline(
                body,
                grid=(n_copies, steps),
                in_specs=[pl.BlockSpec((1, window), lambda k, i: (k, i))],
                out_specs=[pl.BlockSpec((window, width), lambda k, i: (k * steps + i, 0))],
                core_axis_name=("core", "subcore"),
                dimension_semantics=(pltpu.PARALLEL, pltpu.PARALLEL),
            )(idx_hbm, out_hbm)

    return gather_kernel(*tables, idx)


def _proj_kernel(tiles_per_seq,
                 x_ref, pos_ref, rope_ref, wg3_ref, wlat_ref, wgates_ref, convw_ref,
                 qg_ref, kvg_ref, wuqa_ref, wuk_ref, wuv_ref, wcb_ref,
                 qt_ref, kn_ref, kpe_ref, v_ref, ycg_ref, sga_ref, ubuf):
    i = pl.program_id(0)
    tm = x_ref.shape[0]
    xb = x_ref[...].astype(_BF)

    @pl.when(i % tiles_per_seq == 0)
    def _():
        ubuf[0:HALO, :] = jnp.zeros((HALO, CONV_DIM), _F32)

    g3 = _dot(xb, wg3_ref[...])
    u = g3[:, CONV_DIM:2 * CONV_DIM] * g3[:, 2 * CONV_DIM:]
    ubuf[HALO:HALO + tm, :] = u
    cw = convw_ref[...]
    conv = (cw[2:3, :] * u + cw[1:2, :] * ubuf[HALO - 1:HALO - 1 + tm, :]
            + cw[0:1, :] * ubuf[HALO - 2:HALO - 2 + tm, :])
    ubuf[0:HALO, :] = ubuf[tm:tm + HALO, :]
    yc = _dot((g3[:, :CONV_DIM] * conv).astype(_BF), wcb_ref[...])
    gates = _dot(xb, wgates_ref[...])
    ycg_ref[...] = (jax.nn.sigmoid(gates[:, :D_MODEL]) * yc).astype(_BF)
    sga_ref[...] = jax.nn.sigmoid(gates[:, D_MODEL:]).astype(_BF)

    lat = _dot(xb, wlat_ref[...])
    trig = jnp.sin(pos_ref[...].astype(_F32) * rope_ref[0:1, :] + rope_ref[3:4, :])
    cosb = trig * rope_ref[1:2, :]
    sinb = pltpu.roll(trig, QK_ROPE_DIM, axis=1) * rope_ref[2:3, :]
    q_lat = lat[:, :Q_LORA_RANK]
    rq = q_lat * lax.rsqrt(jnp.mean(q_lat * q_lat, axis=-1, keepdims=True) + RMS_EPS) * qg_ref[...]
    kv_lat = lat[:, Q_LORA_RANK:Q_LORA_RANK + KV_LORA_RANK]
    ckv = (kv_lat * lax.rsqrt(jnp.mean(kv_lat * kv_lat, axis=-1, keepdims=True) + RMS_EPS)
           * kvg_ref[...]).astype(_BF)
    o = Q_LORA_RANK + KV_LORA_RANK
    lane = lax.broadcasted_iota(jnp.int32, (tm, ROPE_PAD), 1)
    half = QK_ROPE_DIM // 2

    def swap_halves(u):
        return jnp.where(lane < half, pltpu.roll(u, ROPE_PAD - half, axis=1), pltpu.roll(u, half, axis=1))

    k_pe = lat[:, o:o + ROPE_PAD]
    kpe_ref[...] = (k_pe * cosb + swap_halves(k_pe) * sinb).astype(_BF)
    kn = _dot(ckv, wuk_ref[...]).astype(_BF)
    vv = _dot(ckv, wuv_ref[...]).astype(_BF)
    for h in range(N_HEADS):
        kn_ref[h] = kn[:, h * QK_NOPE_DIM:(h + 1) * QK_NOPE_DIM]
        v_ref[h] = vv[:, h * V_HEAD_DIM:(h + 1) * V_HEAD_DIM]

    qa_t = _dot(wuqa_ref[...], rq.T.astype(_BF))
    cos_t, sin_t = cosb.T, sinb.T
    scale = QK_HEAD_DIM ** -0.5 * LOG2E
    for h in range(N_HEADS):
        lo = h * QK_PAD
        pe_t = qa_t[lo + QK_NOPE_DIM:lo + QK_PAD, :]
        sw_t = jnp.concatenate([pe_t[half:2 * half], pe_t[0:half], pe_t[2 * half:]], axis=0)
        hi_t = pe_t * cos_t + sw_t * sin_t
        qt_ref[h, :QK_NOPE_DIM, :] = (qa_t[lo:lo + QK_NOPE_DIM, :] * scale).astype(_BF)
        qt_ref[h, QK_NOPE_DIM:, :] = (hi_t * scale).astype(_BF)


def _proj_call(x2d, pos, rope, wg3, wlat, wgates, convw, qg, kvg, wuqa, wuk, wuv, wcb, seq):
    t = x2d.shape[0]
    tm = PROJ_TM
    tps = seq // tm
    row = lambda w: pl.BlockSpec((tm, w), lambda i: (i, 0))
    heads = lambda w: pl.BlockSpec((None, N_HEADS, tm, w), lambda i: (i // tps, 0, i % tps, 0))
    head_major = lambda w: jax.ShapeDtypeStruct((t // seq, N_HEADS, seq, w), _BF)
    outs = [
        jax.ShapeDtypeStruct((t // seq, N_HEADS, QK_PAD, seq), _BF),
        head_major(QK_NOPE_DIM),
        jax.ShapeDtypeStruct((t, ROPE_PAD), _BF),
        head_major(V_HEAD_DIM),
        jax.ShapeDtypeStruct((t, D_MODEL), _BF),
        jax.ShapeDtypeStruct((t, D_MODEL), _BF),
    ]
    return pl.pallas_call(
        functools.partial(_proj_kernel, tps),
        grid=(t // tm,),
        in_specs=[row(D_MODEL), row(1)]
        + [_const_spec(a.shape)
           for a in (rope, wg3, wlat, wgates, convw, qg, kvg, wuqa, wuk, wuv, wcb)],
        out_specs=[pl.BlockSpec((None, N_HEADS, QK_PAD, tm), lambda i: (i // tps, 0, 0, i % tps)),
                   heads(QK_NOPE_DIM), row(ROPE_PAD), heads(V_HEAD_DIM),
                   row(D_MODEL), row(D_MODEL)],
        out_shape=outs,
        scratch_shapes=[pltpu.VMEM((tm + HALO, CONV_DIM), _F32)],
        compiler_params=pltpu.CompilerParams(dimension_semantics=("arbitrary",),
                                             vmem_limit_bytes=VMEM_LIMIT),
        name="proj_conv_qkv",
    )(x2d, pos, rope, wg3, wlat, wgates, convw, qg, kvg, wuqa, wuk, wuv, wcb)


def _attn_kernel(qt_ref, kn_ref, kpe_ref, v_ref, o_ref, kfull, vt, s0, s1, p0, p1, a0, a1, m_ref, acc):
    tq, tk = ATTN_TQ, ATTN_TK
    n_tiles = qt_ref.shape[1] // tq
    s_bufs, p_bufs, a_bufs = (s0, s1), (p0, p1), (a0, a1)
    stages = [(tile, blk) for tile in range(n_tiles) for blk in range(tile + 1)]

    kfull[:, :QK_NOPE_DIM] = kn_ref[...]
    kfull[:, QK_NOPE_DIM:] = kpe_ref[...]
    vt[:V_HEAD_DIM, :] = v_ref[...].T
    vt[V_HEAD_DIM:, :] = jnp.ones((vt.shape[0] - V_HEAD_DIM, vt.shape[1]), _BF)

    hk, hq = tk // 2, tq // 2

    def scores_t(t):
        tile, blk = stages[t]
        s_ref = s_bufs[t % 2]
        k0, q0 = blk * tk, tile * tq
        if blk == tile:
            s_ref[0:hk, :] = _dot(kfull[k0:k0 + hk, :], qt_ref[:, q0:q0 + tq])
            s_ref[hk:tk, hq:tq] = _dot(kfull[k0 + hk:k0 + tk, :], qt_ref[:, q0 + hq:q0 + tq])
        else:
            s_ref[...] = _dot(kfull[k0:k0 + tk, :], qt_ref[:, q0:q0 + tq])

    key_chunk = lax.broadcasted_iota(jnp.int32, (tk, tq), 0) // CHUNK
    qry_chunk = lax.broadcasted_iota(jnp.int32, (tk, tq), 1) // CHUNK
    causal = key_chunk <= qry_chunk

    def softmax_t(t):
        tile, blk = stages[t]
        par = t % 2
        if blk == tile:
            top = jnp.where(causal[0:hk, :], s_bufs[par][0:hk, :], NEG_BIG)
            bot = jnp.where(causal[hk:tk, hq:tq], s_bufs[par][hk:tk, hq:tq], NEG_BIG)
            bot_max = jnp.concatenate([jnp.full((1, hq), NEG_BIG, _F32),
                                       jnp.max(bot, axis=0, keepdims=True)], axis=1)
            blk_max = jnp.maximum(jnp.max(top, axis=0, keepdims=True), bot_max)
        else:
            s_t = s_bufs[par][...]
            blk_max = jnp.max(s_t, axis=0, keepdims=True)
        if blk == 0:
            m_new = blk_max
        else:
            m_old = m_ref[...]
            m_new = jnp.maximum(m_old, blk_max)
            a_bufs[par][...] = jnp.exp2(m_old - m_new)
        m_ref[...] = m_new
        if blk == tile:
            p_bufs[par][0:hk, :] = jnp.exp2(top - m_new).astype(_BF)
            p_bufs[par][hk:tk, hq:tq] = jnp.exp2(bot - m_new[:, hq:tq]).astype(_BF)
        else:
            p_bufs[par][...] = jnp.exp2(s_t - m_new).astype(_BF)

    def accumulate(t):
        tile, blk = stages[t]
        par = t % 2
        k0 = blk * tk
        if blk == tile:
            pv = jnp.concatenate([_dot(vt[:, k0:k0 + hk], p_bufs[par][0:hk, 0:hq]),
                                  _dot(vt[:, k0:k0 + tk], p_bufs[par][:, hq:tq])], axis=1)
        else:
            pv = _dot(vt[:, k0:k0 + tk], p_bufs[par][...])
        acc[...] = pv if blk == 0 else a_bufs[par][...] * acc[...] + pv
        if blk == tile:
            out_t = acc[:V_HEAD_DIM, :] / acc[V_HEAD_DIM:V_HEAD_DIM + 1, :]
            o_ref[tile * tq:(tile + 1) * tq, :] = out_t.T.astype(_BF)

    scores_t(0)
    for t in range(len(stages)):
        if t + 1 < len(stages):
            scores_t(t + 1)
        if t > 0:
            accumulate(t - 1)
        softmax_t(t)
    accumulate(len(stages) - 1)


def _attn_call(q, kn, kpe, v):
    b, _, _, s = q.shape
    tq = ATTN_TQ
    assert ATTN_TQ == ATTN_TK and s % tq == 0
    per_head = lambda w: pl.BlockSpec((None, None, s, w), lambda bi, h: (bi, h, 0, 0))
    return pl.pallas_call(
        _attn_kernel,
        grid=(b, N_HEADS),
        in_specs=[pl.BlockSpec((None, None, QK_PAD, s), lambda bi, h: (bi, h, 0, 0)), per_head(QK_NOPE_DIM),
                  pl.BlockSpec((None, s, ROPE_PAD), lambda bi, h: (bi, 0, 0)), per_head(V_HEAD_DIM)],
        out_specs=per_head(V_HEAD_DIM),
        out_shape=jax.ShapeDtypeStruct((b, N_HEADS, s, V_HEAD_DIM), _BF),
        scratch_shapes=[pltpu.VMEM((s, QK_PAD), _BF), pltpu.VMEM((V_HEAD_DIM + ONES_ROWS, s), _BF),
                        pltpu.VMEM((ATTN_TK, tq), _F32), pltpu.VMEM((ATTN_TK, tq), _F32),
                        pltpu.VMEM((ATTN_TK, tq), _BF), pltpu.VMEM((ATTN_TK, tq), _BF),
                        pltpu.VMEM((1, tq), _F32), pltpu.VMEM((1, tq), _F32),
                        pltpu.VMEM((1, tq), _F32),
                        pltpu.VMEM((V_HEAD_DIM + ONES_ROWS, tq), _F32)],
        compiler_params=pltpu.CompilerParams(dimension_semantics=("arbitrary", "arbitrary"),
                                             vmem_limit_bytes=VMEM_LIMIT),
        name="chunk_causal_attention",
    )(q, kn, kpe, v)


def _layer_norm(z, g, b):
    mu = jnp.mean(z, axis=-1, keepdims=True)
    zc = z - mu
    var = jnp.mean(zc * zc, axis=-1, keepdims=True)
    return zc * lax.rsqrt(var + LN_EPS) * g + b


def _post_kernel(expert_cap, attn_ref, ycg_ref, sga_ref, x_ref, wab_ref, wout_ref, g1_ref, b1_ref,
                 wr_ref, br_ref, tri_ref, h1_ref, h1pa_ref, h1pb_ref, slot_ref, tw_ref, cnt_ref, run):
    tm = x_ref.shape[0]

    @pl.when(pl.program_id(0) == 0)
    def _():
        run[...] = jnp.zeros(run.shape, _F32)

    attn = jnp.concatenate([attn_ref[h] for h in range(N_HEADS)], axis=1)
    y_attn = _dot(attn, wab_ref[...])
    merged = ycg_ref[...].astype(_F32) + sga_ref[...].astype(_F32) * y_attn
    mix = _dot(merged.astype(_BF), wout_ref[...])
    h1 = _layer_norm(DEEPNORM_ALPHA * x_ref[...] + mix, g1_ref[...], b1_ref[...])
    h1_ref[...] = h1
    h1p = _pack_rows(h1)
    h1pa_ref[...] = h1p[:, :HALF]
    h1pb_ref[...] = h1p[:, HALF:]

    nt = (((1,), (1,)), ((), ()))
    h_hi = h1.astype(_BF)
    h_lo = (h1 - h_hi.astype(_F32)).astype(_BF)
    both = lax.dot_general(wr_ref[...], h_hi, nt, preferred_element_type=_F32)
    logits = (both[:N_EXPERTS] + both[N_EXPERTS:] + br_ref[:, 0:1]
              + lax.dot_general(wr_ref[:N_EXPERTS, :], h_lo, nt, preferred_element_type=_F32))
    expert = lax.broadcasted_iota(jnp.int32, (N_EXPERTS, tm), 0)
    work = logits
    chosen = jnp.zeros((N_EXPERTS, tm), jnp.bool_)
    sels, exps = [], []
    for k in range(TOP_K):
        mx = jnp.max(work, axis=0, keepdims=True)
        sel = jnp.min(jnp.where(work == mx, expert, N_EXPERTS), axis=0, keepdims=True)
        hit = expert == sel
        sels.append(sel)
        exps.append(jnp.exp(mx - (mx if k == 0 else top0)))
        if k == 0:
            top0 = mx
        chosen = jnp.logical_or(chosen, hit)
        work = jnp.where(hit, -jnp.inf, work)
    denom = exps[0] + exps[1] + exps[2] + exps[3]

    onehot = chosen.astype(_F32)
    before = _dot(onehot.astype(_BF), tri_ref[...]) + run[:, 0:1]
    rows = []
    for k in range(TOP_K):
        rank = jnp.sum(jnp.where(expert == sels[k], before, 0.0), axis=0, keepdims=True)
        rows.append(sels[k] * expert_cap + rank.astype(jnp.int32))
    slot_ref[...] = jnp.concatenate(rows + [jnp.zeros((8 - TOP_K, tm), jnp.int32)], axis=0)
    run[...] = run[...] + jnp.sum(onehot, axis=1, keepdims=True)
    cnt_ref[...] = run[...]
    tw_t = jnp.concatenate([e / denom for e in exps] + [jnp.zeros((LANES - TOP_K, tm), _F32)], axis=0)
    tw_ref[...] = tw_t.T


def _post_call(attn, ycg, sga, x2d, wab, wout, g1, b1, wr, br):
    t = x2d.shape[0]
    tm = POST_TM
    row = lambda w: pl.BlockSpec((tm, w), lambda i: (i, 0))
    tps = attn.shape[2] // tm
    tri = (jnp.arange(tm)[:, None] < jnp.arange(tm)[None, :]).astype(_BF)
    outs = [
        jax.ShapeDtypeStruct((t, D_MODEL), _F32),
        jax.ShapeDtypeStruct((t, HALF), jnp.int32),
        jax.ShapeDtypeStruct((t, HALF), jnp.int32),
        jax.ShapeDtypeStruct((8, t), jnp.int32),
        jax.ShapeDtypeStruct((t, LANES), _F32),
        jax.ShapeDtypeStruct((N_EXPERTS, LANES), _F32),
    ]
    out_specs = [row(D_MODEL), row(HALF), row(HALF), pl.BlockSpec((8, tm), lambda i: (0, i)),
                 row(LANES), pl.BlockSpec((N_EXPERTS, LANES), lambda i: (0, 0))]
    return pl.pallas_call(
        functools.partial(_post_kernel, t),
        grid=(t // tm,),
        in_specs=[pl.BlockSpec((None, N_HEADS, tm, V_HEAD_DIM), lambda i: (i // tps, 0, i % tps, 0))]
        + [row(D_MODEL)] * 3
        + [_const_spec(a.shape) for a in (wab, wout, g1, b1, wr, br, tri)],
        out_specs=out_specs,
        out_shape=outs,
        scratch_shapes=[pltpu.VMEM((N_EXPERTS, LANES), _F32)],
        compiler_params=pltpu.CompilerParams(dimension_semantics=("arbitrary",),
                                             vmem_limit_bytes=VMEM_LIMIT),
        name="merge_out_ln1_router",
    )(attn, ycg, sga, x2d, wab, wout, g1, b1, wr, br, tri)


def _moe_kernel(be_ref, we_ref, rb_ref, bv_ref, nused_ref, xa_ref, xb_ref, wgu_ref, bgu_ref, wd_ref, bd_ref,
                ya_ref, yb_ref, wgu_bf, wd_bf):
    i = pl.program_id(0)
    prev = be_ref[jnp.maximum(i - 1, 0)]
    fresh = jnp.logical_or(i == 0, be_ref[i] != prev)
    used = i < nused_ref[0]

    @pl.when(jnp.logical_and(fresh, used))
    def _():
        wgu_bf[...] = wgu_ref[...].astype(_BF)
        wd_bf[...] = wd_ref[...].astype(_BF)

    def ffn(r0, rows):
        a_lo, a_hi = _unpack_rows(xa_ref[r0:r0 + rows, :])
        b_lo, b_hi = _unpack_rows(xb_ref[r0:r0 + rows, :])
        xs = jnp.concatenate([a_lo, b_lo, a_hi, b_hi], axis=1).astype(_BF)
        gu = _dot(xs, wgu_bf[...]) + bgu_ref[...]
        gate = jnp.minimum(gu[:, :D_FF_EXPERT], SWIGLU_LIMIT)
        up = jnp.clip(gu[:, D_FF_EXPERT:], -SWIGLU_LIMIT, SWIGLU_LIMIT)
        hid = (up + 1.0) * (gate * jax.nn.sigmoid(SWIGLU_ALPHA * gate))
        yp = _pack_rows(_dot(hid.astype(_BF), wd_bf[...]) + bd_ref[...])
        ya_ref[r0:r0 + rows, :] = yp[:, :HALF]
        yb_ref[r0:r0 + rows, :] = yp[:, HALF:]

    bm = xa_ref.shape[0]
    step = bm // MOE_ROW_STEPS
    need = (bv_ref[i] + step - 1) // step
    for n in range(1, MOE_ROW_STEPS + 1):
        @pl.when(jnp.logical_and(used, need == n))
        def _(n=n):
            ffn(0, n * step)


def _moe_call(block_e, weight_e, block_row, block_valid, n_used, xa, xb, wgu, bgu, wd, bd):
    n_slots = xa.shape[0]
    bm = MOE_BM
    n_blocks = block_e.shape[0]
    grid_spec = pltpu.PrefetchScalarGridSpec(
        num_scalar_prefetch=5,
        grid=(n_blocks,),
        in_specs=[
            pl.BlockSpec((bm, HALF), lambda i, be, we, rb, bv, nu: (rb[i], 0)),
            pl.BlockSpec((bm, HALF), lambda i, be, we, rb, bv, nu: (rb[i], 0)),
            pl.BlockSpec((None, D_MODEL, 2 * D_FF_EXPERT), lambda i, be, we, rb, bv, nu: (we[i], 0, 0)),
            pl.BlockSpec((None, 1, 2 * D_FF_EXPERT), lambda i, be, we, rb, bv, nu: (be[i], 0, 0)),
            pl.BlockSpec((None, D_FF_EXPERT, D_MODEL), lambda i, be, we, rb, bv, nu: (we[i], 0, 0)),
            pl.BlockSpec((None, 1, D_MODEL), lambda i, be, we, rb, bv, nu: (be[i], 0, 0)),
        ],
        out_specs=[pl.BlockSpec((bm, HALF), lambda i, be, we, rb, bv, nu: (rb[i], 0))] * 2,
        scratch_shapes=[pltpu.VMEM((D_MODEL, 2 * D_FF_EXPERT), _BF),
                        pltpu.VMEM((D_FF_EXPERT, D_MODEL), _BF)],
    )
    return pl.pallas_call(
        _moe_kernel,
        grid_spec=grid_spec,
        out_shape=[jax.ShapeDtypeStruct((n_slots, HALF), jnp.int32)] * 2,
        compiler_params=pltpu.CompilerParams(dimension_semantics=("arbitrary",),
                                             vmem_limit_bytes=VMEM_LIMIT),
        name="expert_ffn",
    )(block_e, weight_e, block_row, block_valid, n_used, xa, xb, wgu, bgu, wd, bd)


def _combine_kernel(*refs):
    ya_refs, yb_refs = refs[:TOP_K], refs[TOP_K:2 * TOP_K]
    tw_ref, h1_ref, g2_ref, b2_ref, o_ref = refs[2 * TOP_K:]
    tw = tw_ref[...]
    parts = [jnp.zeros((tw.shape[0], HALF), _F32) for _ in range(4)]
    for k in range(TOP_K):
        a_lo, a_hi = _unpack_rows(ya_refs[k][...])
        b_lo, b_hi = _unpack_rows(yb_refs[k][...])
        w = tw[:, k:k + 1]
        parts = [p + w * y for p, y in zip(parts, (a_lo, b_lo, a_hi, b_hi))]
    ffn = jnp.concatenate(parts, axis=1)
    o_ref[...] = _layer_norm(DEEPNORM_ALPHA * h1_ref[...] + ffn, g2_ref[...], b2_ref[...])


def _combine_call(yga, ygb, tw, h1, g2, b2):
    t = h1.shape[0]
    tm = COMB_TM
    row = lambda w: pl.BlockSpec((tm, w), lambda i: (i, 0))
    krow = lambda k: pl.BlockSpec((tm, HALF), lambda i: (k * (t // tm) + i, 0))
    return pl.pallas_call(
        _combine_kernel,
        grid=(t // tm,),
        in_specs=[krow(k) for k in range(TOP_K)] * 2 + [row(LANES), row(D_MODEL),
                  _const_spec(g2.shape), _const_spec(b2.shape)],
        out_specs=row(D_MODEL),
        out_shape=jax.ShapeDtypeStruct((t, D_MODEL), _F32),
        compiler_params=pltpu.CompilerParams(dimension_semantics=("arbitrary",),
                                             vmem_limit_bytes=VMEM_LIMIT),
        name="combine_ln2",
    )(*([yga] * TOP_K + [ygb] * TOP_K), tw, h1, g2, b2)


def _prepare_weights(w_in, w_uq):
    o_q = 3 * CONV_DIM
    o_kv = o_q + Q_LORA_RANK
    o_pe = o_kv + KV_LORA_RANK
    o_gc = o_pe + QK_ROPE_DIM
    wg3 = w_in[:, :o_q]
    kpe = w_in[:, o_pe:o_gc]
    zpad = jnp.zeros((D_MODEL, ROPE_PAD - QK_ROPE_DIM), w_in.dtype)
    wlat = jnp.concatenate([w_in[:, o_q:o_pe], kpe, zpad], axis=1)
    wgates = w_in[:, o_gc:]
    wq = w_uq.reshape(Q_LORA_RANK, N_HEADS, QK_HEAD_DIM)
    wuqa = jnp.pad(wq, ((0, 0), (0, 0), (0, QK_PAD - QK_HEAD_DIM)))
    return (wg3.astype(_BF), wlat.astype(_BF), wgates.astype(_BF),
            wuqa.reshape(Q_LORA_RANK, N_HEADS * QK_PAD).T.astype(_BF))


def _rope_rows():
    half = QK_ROPE_DIM // 2
    inv_freq = ROPE_BASE ** (-jnp.arange(0, QK_ROPE_DIM, 2, dtype=_F32) / QK_ROPE_DIM)
    z = jnp.zeros((ROPE_PAD - QK_ROPE_DIM,), _F32)
    ones = jnp.ones((half,), _F32)
    rows = jnp.stack([jnp.concatenate([inv_freq] * 4),
                      jnp.concatenate([ones, ones, z]),
                      jnp.concatenate([-ones, ones, z]),
                      jnp.concatenate([ones * (np.pi / 2), ones * (np.pi / 2), z])])
    return jnp.pad(rows, ((0, 8 - rows.shape[0]), (0, 0)))


def _block_tables(counts, n_tok):
    bm = MOE_BM
    n_blocks = n_tok * TOP_K // bm + N_EXPERTS
    nblk = (counts + bm - 1) // bm
    end = jnp.cumsum(nblk)
    start = end - nblk
    n_used = end[-1]
    i = jnp.minimum(jnp.arange(n_blocks, dtype=jnp.int32), n_used - 1)
    block_e = (end[None, :] <= i[:, None]).sum(axis=1).astype(jnp.int32)
    first = jnp.sum(jnp.where(jnp.arange(N_EXPERTS)[None, :] == block_e[:, None], start[None, :], 0), axis=1)
    block_row = block_e * (n_tok // bm) + (i - first)
    fresh = jnp.concatenate([jnp.ones((1,), jnp.bool_), block_e[1:] != block_e[:-1]])
    later = jnp.where(block_e[None, :] > block_e[:, None], block_e[None, :], N_EXPERTS)
    nxt = jnp.min(later, axis=1)
    weight_e = jnp.where(jnp.logical_or(fresh, nxt == N_EXPERTS), block_e, nxt).astype(jnp.int32)
    cnt = jnp.sum(jnp.where(jnp.arange(N_EXPERTS)[None, :] == block_e[:, None], counts[None, :], 0), axis=1)
    block_valid = jnp.clip(cnt - (i - first) * bm, 0, bm).astype(jnp.int32)
    return (block_e, weight_e, block_row.astype(jnp.int32), block_valid,
            n_used.astype(jnp.int32).reshape(1))


def kernel(x, positions, w_in, conv_w, q_norm_g, w_uq, kv_norm_g, w_uk, w_uv, w_conv_branch,
           w_attn_branch, w_out, ln1_g, ln1_b, w_router, b_router, w_gate_up, b_gate_up, w_down,
           b_down, ln2_g, ln2_b):
    b, s, d = x.shape
    t = b * s
    x2d = x.reshape(t, d)
    pos = positions.reshape(t, 1).astype(jnp.int32)
    rope = _rope_rows()
    for l in range(DEPTH):
        wg3, wlat, wgates, wuqa = _prepare_weights(w_in[l], w_uq[l])
        q, kn, kpe, v, ycg, sga = _proj_call(
            x2d, pos, rope, wg3, wlat, wgates, conv_w[l], q_norm_g[l][None, :], kv_norm_g[l][None, :],
            wuqa, w_uk[l].astype(_BF), w_uv[l].astype(_BF), w_conv_branch[l].astype(_BF), s)
        attn = _attn_call(q, kn, kpe.reshape(b, s, -1), v)
        wr_t = w_router[l].T
        wr_hi = wr_t.astype(_BF)
        wr_lo = (wr_t - wr_hi.astype(_F32)).astype(_BF)
        h1, h1pa, h1pb, slots, tw_pad, counts = _post_call(
            attn, ycg, sga, x2d, w_attn_branch[l].astype(_BF), w_out[l].astype(_BF),
            ln1_g[l][None, :], ln1_b[l][None, :], jnp.concatenate([wr_hi, wr_lo], axis=0),
            jnp.broadcast_to(b_router[l][:, None], (N_EXPERTS, LANES)))
        block_e, weight_e, block_row, block_valid, n_used = _block_tables(counts[:, 0].astype(jnp.int32), t)
        xa, xb = _sc_scatter_rows([h1pa, h1pb], slots, TOP_K, N_EXPERTS * t)
        ya, yb = _moe_call(block_e, weight_e, block_row, block_valid, n_used, xa, xb,
                           w_gate_up[l], b_gate_up[l][:, None, :], w_down[l], b_down[l][:, None, :])
        yga, ygb = _sc_gather_rows([ya, yb], slots, TOP_K)
        x2d = _combine_call(yga, ygb, tw_pad, h1, ln2_g[l][None, :], ln2_b[l][None, :])
    return x2d.reshape(b, s, d)
```

```python
import functools

import numpy as np
import jax
import jax.numpy as jnp
from jax import lax
from jax.experimental import pallas as pl
from jax.experimental.pallas import tpu as pltpu
from jax.experimental.pallas import tpu_sc as plsc

D_MODEL = 1024
CHUNK = 64
CONV_DIM = D_MODEL
CONV_WIDTH = 3
N_HEADS = 8
QK_NOPE_DIM = 128
QK_ROPE_DIM = 64
V_HEAD_DIM = 128
QK_HEAD_DIM = QK_NOPE_DIM + QK_ROPE_DIM
Q_LORA_RANK = 384
KV_LORA_RANK = 256
ROPE_BASE = 10000.0
N_EXPERTS = 32
TOP_K = 4
D_FF_EXPERT = D_MODEL
SWIGLU_LIMIT = 7.0
SWIGLU_ALPHA = 1.702
LN_EPS = 1e-5
RMS_EPS = 1e-6
DEPTH = 1
DEEPNORM_ALPHA = (2 * DEPTH) ** 0.25

LANES = 128
QK_PAD = 2 * LANES
ROPE_PAD = LANES

PROJ_TM = 512
ATTN_TQ = 512
ATTN_TK = 512
ONES_ROWS = 16
POST_TM = 512
MOE_BM = 1024
MOE_ROW_STEPS = 8
COMB_TM = 1024
SC_WINDOW = 128
PACKED = D_MODEL // 2
HALF = PACKED // 2
HALO = 8
NEG_BIG = -1e30
LOG2E = 1.4426950408889634
VMEM_LIMIT = 56 * 1024 * 1024

_BF = jnp.bfloat16
_F32 = jnp.float32


def _dot(a, b):
    return jnp.dot(a, b, preferred_element_type=_F32)


def _const_spec(shape):
    nd = len(shape)
    return pl.BlockSpec(shape, lambda *_: (0,) * nd, pipeline_mode=pl.Buffered(1))


def _pack_rows(a):
    half = a.shape[1] // 2
    lo = lax.bitcast_convert_type(a[:, :half].astype(_BF).astype(_F32), jnp.int32)
    hi = lax.bitcast_convert_type(a[:, half:].astype(_BF).astype(_F32), jnp.int32)
    return lax.shift_right_logical(lo, 16) | (hi & jnp.int32(-65536))


def _unpack_rows(p):
    lo = lax.bitcast_convert_type(lax.shift_left(p, 16), _F32)
    hi = lax.bitcast_convert_type(p & jnp.int32(-65536), _F32)
    return lo, hi


def _sc_mesh():
    return plsc.VectorSubcoreMesh(core_axis_name="core", subcore_axis_name="subcore")


def _sc_scatter_rows(tables, idx, n_copies, n_out):
    n_rows, width = tables[0].shape
    n_tab = len(tables)
    window = SC_WINDOW

    @functools.partial(pl.kernel, mesh=_sc_mesh(),
                       out_type=[jax.ShapeDtypeStruct((n_out, width), tables[0].dtype)] * n_tab)
    def scatter_kernel(*refs):
        idx_hbm = refs[n_tab]
        for src_hbm, out_hbm in zip(refs[:n_tab], refs[n_tab + 1:]):
            def body(src_vmem, idx_vmem, out_hbm=out_hbm):
                for k in range(n_copies):
                    pltpu.sync_copy(src_vmem, out_hbm.at[idx_vmem.at[k]])

            pltpu.emit_pipeline(
                body,
                grid=(n_rows // window,),
                in_specs=[pl.BlockSpec((window, width), lambda i: (i, 0)),
                          pl.BlockSpec((idx.shape[0], window), lambda i: (0, i))],
                out_specs=[],
                core_axis_name=("core", "subcore"),
                dimension_semantics=(pltpu.PARALLEL,),
            )(src_hbm, idx_hbm)

    return scatter_kernel(*tables, idx)


def _sc_gather_rows(tables, idx, n_copies):
    n_rows = idx.shape[1]
    width = tables[0].shape[1]
    n_tab = len(tables)
    window = SC_WINDOW
    steps = n_rows // window

    @functools.partial(pl.kernel, mesh=_sc_mesh(),
                       out_type=[jax.ShapeDtypeStruct((n_copies * n_rows, width), tables[0].dtype)] * n_tab)
    def gather_kernel(*refs):
        idx_hbm = refs[n_tab]
        for table_hbm, out_hbm in zip(refs[:n_tab], refs[n_tab + 1:]):
            def body(idx_vmem, out_vmem, table_hbm=table_hbm):
                pltpu.sync_copy(table_hbm.at[idx_vmem.at[0]], out_vmem)

            pltpu.emit_pipeline(
                body,
                grid=(n_copies, steps),
                in_specs=[pl.BlockSpec((1, window), lambda k, i: (k, i))],
                out_specs=[pl.BlockSpec((window, width), lambda k, i: (k * steps + i, 0))],
                core_axis_name=("core", "subcore"),
                dimension_semantics=(pltpu.PARALLEL, pltpu.PARALLEL),
            )(idx_hbm, out_hbm)

    return gather_kernel(*tables, idx)


def _proj_kernel(tiles_per_seq,
                 x_ref, pos_ref, rope_ref, wg3_ref, wlat_ref, wgates_ref, convw_ref,
                 qg_ref, kvg_ref, wuqa_ref, wuk_ref, wuv_ref, wcb_ref,
                 qt_ref, kn_ref, kpe_ref, v_ref, ycg_ref, sga_ref, ubuf):
    i = pl.program_id(0)
    tm = x_ref.shape[0]
    xb = x_ref[...].astype(_BF)

    @pl.when(i % tiles_per_seq == 0)
    def _():
        ubuf[0:HALO, :] = jnp.zeros((HALO, CONV_DIM), _F32)

    g3 = _dot(xb, wg3_ref[...])
    u = g3[:, CONV_DIM:2 * CONV_DIM] * g3[:, 2 * CONV_DIM:]
    ubuf[HALO:HALO + tm, :] = u
    cw = convw_ref[...]
    conv = (cw[2:3, :] * u + cw[1:2, :] * ubuf[HALO - 1:HALO - 1 + tm, :]
            + cw[0:1, :] * ubuf[HALO - 2:HALO - 2 + tm, :])
    ubuf[0:HALO, :] = ubuf[tm:tm + HALO, :]
    yc = _dot((g3[:, :CONV_DIM] * conv).astype(_BF), wcb_ref[...])
    gates = _dot(xb, wgates_ref[...])
    ycg_ref[...] = (jax.nn.sigmoid(gates[:, :D_MODEL]) * yc).astype(_BF)
    sga_ref[...] = jax.nn.sigmoid(gates[:, D_MODEL:]).astype(_BF)

    lat = _dot(xb, wlat_ref[...])
    trig = jnp.sin(pos_ref[...].astype(_F32) * rope_ref[0:1, :] + rope_ref[3:4, :])
    cosb = trig * rope_ref[1:2, :]
    sinb = pltpu.roll(trig, QK_ROPE_DIM, axis=1) * rope_ref[2:3, :]
    q_lat = lat[:, :Q_LORA_RANK]
    rq = q_lat * lax.rsqrt(jnp.mean(q_lat * q_lat, axis=-1, keepdims=True) + RMS_EPS) * qg_ref[...]
    kv_lat = lat[:, Q_LORA_RANK:Q_LORA_RANK + KV_LORA_RANK]
    ckv = (kv_lat * lax.rsqrt(jnp.mean(kv_lat * kv_lat, axis=-1, keepdims=True) + RMS_EPS)
           * kvg_ref[...]).astype(_BF)
    o = Q_LORA_RANK + KV_LORA_RANK
    lane = lax.broadcasted_iota(jnp.int32, (tm, ROPE_PAD), 1)
    half = QK_ROPE_DIM // 2

    def swap_halves(u):
        return jnp.where(lane < half, pltpu.roll(u, ROPE_PAD - half, axis=1), pltpu.roll(u, half, axis=1))

    k_pe = lat[:, o:o + ROPE_PAD]
    kpe_ref[...] = (k_pe * cosb + swap_halves(k_pe) * sinb).astype(_BF)
    kn = _dot(ckv, wuk_ref[...]).astype(_BF)
    vv = _dot(ckv, wuv_ref[...]).astype(_BF)
    for h in range(N_HEADS):
        kn_ref[h] = kn[:, h * QK_NOPE_DIM:(h + 1) * QK_NOPE_DIM]
        v_ref[h] = vv[:, h * V_HEAD_DIM:(h + 1) * V_HEAD_DIM]

    qa_t = _dot(wuqa_ref[...], rq.T.astype(_BF))
    cos_t, sin_t = cosb.T, sinb.T
    scale = QK_HEAD_DIM ** -0.5 * LOG2E
    for h in range(N_HEADS):
        lo = h * QK_PAD
        pe_t = qa_t[lo + QK_NOPE_DIM:lo + QK_PAD, :]
        sw_t = jnp.concatenate([pe_t[half:2 * half], pe_t[0:half], pe_t[2 * half:]], axis=0)
        hi_t = pe_t * cos_t + sw_t * sin_t
        qt_ref[h, :QK_NOPE_DIM, :] = (qa_t[lo:lo + QK_NOPE_DIM, :] * scale).astype(_BF)
        qt_ref[h, QK_NOPE_DIM:, :] = (hi_t * scale).astype(_BF)


def _proj_call(x2d, pos, rope, wg3, wlat, wgates, convw, qg, kvg, wuqa, wuk, wuv, wcb, seq):
    t = x2d.shape[0]
    tm = PROJ_TM
    tps = seq // tm
    row = lambda w: pl.BlockSpec((tm, w), lambda i: (i, 0))
    heads = lambda w: pl.BlockSpec((None, N_HEADS, tm, w), lambda i: (i // tps, 0, i % tps, 0))
    head_major = lambda w: jax.ShapeDtypeStruct((t // seq, N_HEADS, seq, w), _BF)
    outs = [
        jax.ShapeDtypeStruct((t // seq, N_HEADS, QK_PAD, seq), _BF),
        head_major(QK_NOPE_DIM),
        jax.ShapeDtypeStruct((t, ROPE_PAD), _BF),
        head_major(V_HEAD_DIM),
        jax.ShapeDtypeStruct((t, D_MODEL), _BF),
        jax.ShapeDtypeStruct((t, D_MODEL), _BF),
    ]
    return pl.pallas_call(
        functools.partial(_proj_kernel, tps),
        grid=(t // tm,),
        in_specs=[row(D_MODEL), row(1)]
        + [_const_spec(a.shape)
           for a in (rope, wg3, wlat, wgates, convw, qg, kvg, wuqa, wuk, wuv, wcb)],
        out_specs=[pl.BlockSpec((None, N_HEADS, QK_PAD, tm), lambda i: (i // tps, 0, 0, i % tps)),
                   heads(QK_NOPE_DIM), row(ROPE_PAD), heads(V_HEAD_DIM),
                   row(D_MODEL), row(D_MODEL)],
        out_shape=outs,
        scratch_shapes=[pltpu.VMEM((tm + HALO, CONV_DIM), _F32)],
        compiler_params=pltpu.CompilerParams(dimension_semantics=("arbitrary",),
                                             vmem_limit_bytes=VMEM_LIMIT),
        name="proj_conv_qkv",
    )(x2d, pos, rope, wg3, wlat, wgates, convw, qg, kvg, wuqa, wuk, wuv, wcb)


def _attn_kernel(qt_ref, kn_ref, kpe_ref, v_ref, o_ref, kfull, vt, s0, s1, p0, p1, a0, a1, m_ref, acc):
    tq, tk = ATTN_TQ, ATTN_TK
    n_tiles = qt_ref.shape[1] // tq
    s_bufs, p_bufs, a_bufs = (s0, s1), (p0, p1), (a0, a1)
    stages = [(tile, blk) for tile in range(n_tiles) for blk in range(tile + 1)]

    kfull[:, :QK_NOPE_DIM] = kn_ref[...]
    kfull[:, QK_NOPE_DIM:] = kpe_ref[...]
    vt[:V_HEAD_DIM, :] = v_ref[...].T
    vt[V_HEAD_DIM:, :] = jnp.ones((vt.shape[0] - V_HEAD_DIM, vt.shape[1]), _BF)

    hk, hq = tk // 2, tq // 2

    def scores_t(t):
        tile, blk = stages[t]
        s_ref = s_bufs[t % 2]
        k0, q0 = blk * tk, tile * tq
        if blk == tile:
            s_ref[0:hk, :] = _dot(kfull[k0:k0 + hk, :], qt_ref[:, q0:q0 + tq])
            s_ref[hk:tk, hq:tq] = _dot(kfull[k0 + hk:k0 + tk, :], qt_ref[:, q0 + hq:q0 + tq])
        else:
            s_ref[...] = _dot(kfull[k0:k0 + tk, :], qt_ref[:, q0:q0 + tq])

    key_chunk = lax.broadcasted_iota(jnp.int32, (tk, tq), 0) // CHUNK
    qry_chunk = lax.broadcasted_iota(jnp.int32, (tk, tq), 1) // CHUNK
    causal = key_chunk <= qry_chunk

    def softmax_t(t):
        tile, blk = stages[t]
        par = t % 2
        if blk == tile:
            top = jnp.where(causal[0:hk, :], s_bufs[par][0:hk, :], NEG_BIG)
            bot = jnp.where(causal[hk:tk, hq:tq], s_bufs[par][hk:tk, hq:tq], NEG_BIG)
            bot_max = jnp.concatenate([jnp.full((1, hq), NEG_BIG, _F32),
                                       jnp.max(bot, axis=0, keepdims=True)], axis=1)
            blk_max = jnp.maximum(jnp.max(top, axis=0, keepdims=True), bot_max)
        else:
            s_t = s_bufs[par][...]
            blk_max = jnp.max(s_t, axis=0, keepdims=True)
        if blk == 0:
            m_new = blk_max
        else:
            m_old = m_ref[...]
            m_new = jnp.maximum(m_old, blk_max)
            a_bufs[par][...] = jnp.exp2(m_old - m_new)
        m_ref[...] = m_new
        if blk == tile:
            p_bufs[par][0:hk, :] = jnp.exp2(top - m_new).astype(_BF)
            p_bufs[par][hk:tk, hq:tq] = jnp.exp2(bot - m_new[:, hq:tq]).astype(_BF)
        else:
            p_bufs[par][...] = jnp.exp2(s_t - m_new).astype(_BF)

    def accumulate(t):
        tile, blk = stages[t]
        par = t % 2
        k0 = blk * tk
        if blk == tile:
            pv = jnp.concatenate([_dot(vt[:, k0:k0 + hk], p_bufs[par][0:hk, 0:hq]),
                                  _dot(vt[:, k0:k0 + tk], p_bufs[par][:, hq:tq])], axis=1)
        else:
            pv = _dot(vt[:, k0:k0 + tk], p_bufs[par][...])
        acc[...] = pv if blk == 0 else a_bufs[par][...] * acc[...] + pv
        if blk == tile:
            out_t = acc[:V_HEAD_DIM, :] / acc[V_HEAD_DIM:V_HEAD_DIM + 1, :]
            o_ref[tile * tq:(tile + 1) * tq, :] = out_t.T.astype(_BF)

    scores_t(0)
    for t in range(len(stages)):
        if t + 1 < len(stages):
            scores_t(t + 1)
        if t > 0:
            accumulate(t - 1)
        softmax_t(t)
    accumulate(len(stages) - 1)


def _attn_call(q, kn, kpe, v):
    b, _, _, s = q.shape
    tq = ATTN_TQ
    assert ATTN_TQ == ATTN_TK and s % tq == 0
    per_head = lambda w: pl.BlockSpec((None, None, s, w), lambda bi, h: (bi, h, 0, 0))
    return pl.pallas_call(
        _attn_kernel,
        grid=(b, N_HEADS),
        in_specs=[pl.BlockSpec((None, None, QK_PAD, s), lambda bi, h: (bi, h, 0, 0)), per_head(QK_NOPE_DIM),
                  pl.BlockSpec((None, s, ROPE_PAD), lambda bi, h: (bi, 0, 0)), per_head(V_HEAD_DIM)],
        out_specs=per_head(V_HEAD_DIM),
        out_shape=jax.ShapeDtypeStruct((b, N_HEADS, s, V_HEAD_DIM), _BF),
        scratch_shapes=[pltpu.VMEM((s, QK_PAD), _BF), pltpu.VMEM((V_HEAD_DIM + ONES_ROWS, s), _BF),
                        pltpu.VMEM((ATTN_TK, tq), _F32), pltpu.VMEM((ATTN_TK, tq), _F32),
                        pltpu.VMEM((ATTN_TK, tq), _BF), pltpu.VMEM((ATTN_TK, tq), _BF),
                        pltpu.VMEM((1, tq), _F32), pltpu.VMEM((1, tq), _F32),
                        pltpu.VMEM((1, tq), _F32),
                        pltpu.VMEM((V_HEAD_DIM + ONES_ROWS, tq), _F32)],
        compiler_params=pltpu.CompilerParams(dimension_semantics=("arbitrary", "arbitrary"),
                                             vmem_limit_bytes=VMEM_LIMIT),
        name="chunk_causal_attention",
    )(q, kn, kpe, v)


def _layer_norm(z, g, b):
    mu = jnp.mean(z, axis=-1, keepdims=True)
    zc = z - mu
    var = jnp.mean(zc * zc, axis=-1, keepdims=True)
    return zc * lax.rsqrt(var + LN_EPS) * g + b


def _post_kernel(expert_cap, attn_ref, ycg_ref, sga_ref, x_ref, wab_ref, wout_ref, g1_ref, b1_ref,
                 wr_ref, br_ref, tri_ref, h1_ref, h1pa_ref, h1pb_ref, slot_ref, tw_ref, cnt_ref, run):
    tm = x_ref.shape[0]

    @pl.when(pl.program_id(0) == 0)
    def _():
        run[...] = jnp.zeros(run.shape, _F32)

    attn = jnp.concatenate([attn_ref[h] for h in range(N_HEADS)], axis=1)
    y_attn = _dot(attn, wab_ref[...])
    merged = ycg_ref[...].astype(_F32) + sga_ref[...].astype(_F32) * y_attn
    mix = _dot(merged.astype(_BF), wout_ref[...])
    h1 = _layer_norm(DEEPNORM_ALPHA * x_ref[...] + mix, g1_ref[...], b1_ref[...])
    h1_ref[...] = h1
    h1p = _pack_rows(h1)
    h1pa_ref[...] = h1p[:, :HALF]
    h1pb_ref[...] = h1p[:, HALF:]

    nt = (((1,), (1,)), ((), ()))
    h_hi = h1.astype(_BF)
    h_lo = (h1 - h_hi.astype(_F32)).astype(_BF)
    both = lax.dot_general(wr_ref[...], h_hi, nt, preferred_element_type=_F32)
    logits = (both[:N_EXPERTS] + both[N_EXPERTS:] + br_ref[:, 0:1]
              + lax.dot_general(wr_ref[:N_EXPERTS, :], h_lo, nt, preferred_element_type=_F32))
    expert = lax.broadcasted_iota(jnp.int32, (N_EXPERTS, tm), 0)
    work = logits
    chosen = jnp.zeros((N_EXPERTS, tm), jnp.bool_)
    sels, exps = [], []
    for k in range(TOP_K):
        mx = jnp.max(work, axis=0, keepdims=True)
        sel = jnp.min(jnp.where(work == mx, expert, N_EXPERTS), axis=0, keepdims=True)
        hit = expert == sel
        sels.append(sel)
        exps.append(jnp.exp(mx - (mx if k == 0 else top0)))
        if k == 0:
            top0 = mx
        chosen = jnp.logical_or(chosen, hit)
        work = jnp.where(hit, -jnp.inf, work)
    denom = exps[0] + exps[1] + exps[2] + exps[3]

    onehot = chosen.astype(_F32)
    before = _dot(onehot.astype(_BF), tri_ref[...]) + run[:, 0:1]
    rows = []
    for k in range(TOP_K):
        rank = jnp.sum(jnp.where(expert == sels[k], before, 0.0), axis=0, keepdims=True)
        rows.append(sels[k] * expert_cap + rank.astype(jnp.int32))
    slot_ref[...] = jnp.concatenate(rows + [jnp.zeros((8 - TOP_K, tm), jnp.int32)], axis=0)
    run[...] = run[...] + jnp.sum(onehot, axis=1, keepdims=True)
    cnt_ref[...] = run[...]
    tw_t = jnp.concatenate([e / denom for e in exps] + [jnp.zeros((LANES - TOP_K, tm), _F32)], axis=0)
    tw_ref[...] = tw_t.T


def _post_call(attn, ycg, sga, x2d, wab, wout, g1, b1, wr, br):
    t = x2d.shape[0]
    tm = POST_TM
    row = lambda w: pl.BlockSpec((tm, w), lambda i: (i, 0))
    tps = attn.shape[2] // tm
    tri = (jnp.arange(tm)[:, None] < jnp.arange(tm)[None, :]).astype(_BF)
    outs = [
        jax.ShapeDtypeStruct((t, D_MODEL), _F32),
        jax.ShapeDtypeStruct((t, HALF), jnp.int32),
        jax.ShapeDtypeStruct((t, HALF), jnp.int32),
        jax.ShapeDtypeStruct((8, t), jnp.int32),
        jax.ShapeDtypeStruct((t, LANES), _F32),
        jax.ShapeDtypeStruct((N_EXPERTS, LANES), _F32),
    ]
    out_specs = [row(D_MODEL), row(HALF), row(HALF), pl.BlockSpec((8, tm), lambda i: (0, i)),
                 row(LANES), pl.BlockSpec((N_EXPERTS, LANES), lambda i: (0, 0))]
    return pl.pallas_call(
        functools.partial(_post_kernel, t),
        grid=(t // tm,),
        in_specs=[pl.BlockSpec((None, N_HEADS, tm, V_HEAD_DIM), lambda i: (i // tps, 0, i % tps, 0))]
        + [row(D_MODEL)] * 3
        + [_const_spec(a.shape) for a in (wab, wout, g1, b1, wr, br, tri)],
        out_specs=out_specs,
        out_shape=outs,
        scratch_shapes=[pltpu.VMEM((N_EXPERTS, LANES), _F32)],
        compiler_params=pltpu.CompilerParams(dimension_semantics=("arbitrary",),
                                             vmem_limit_bytes=VMEM_LIMIT),
        name="merge_out_ln1_router",
    )(attn, ycg, sga, x2d, wab, wout, g1, b1, wr, br, tri)


def _moe_kernel(be_ref, we_ref, rb_ref, bv_ref, nused_ref, xa_ref, xb_ref, wgu_ref, bgu_ref, wd_ref, bd_ref,
                ya_ref, yb_ref, wgu_bf, wd_bf):
    i = pl.program_id(0)
    prev = be_ref[jnp.maximum(i - 1, 0)]
    fresh = jnp.logical_or(i == 0, be_ref[i] != prev)
    used = i < nused_ref[0]

    @pl.when(jnp.logical_and(fresh, used))
    def _():
        wgu_bf[...] = wgu_ref[...].astype(_BF)
        wd_bf[...] = wd_ref[...].astype(_BF)

    def ffn(r0, rows):
        a_lo, a_hi = _unpack_rows(xa_ref[r0:r0 + rows, :])
        b_lo, b_hi = _unpack_rows(xb_ref[r0:r0 + rows, :])
        xs = jnp.concatenate([a_lo, b_lo, a_hi, b_hi], axis=1).astype(_BF)
        gu = _dot(xs, wgu_bf[...]) + bgu_ref[...]
        gate = jnp.minimum(gu[:, :D_FF_EXPERT], SWIGLU_LIMIT)
        up = jnp.clip(gu[:, D_FF_EXPERT:], -SWIGLU_LIMIT, SWIGLU_LIMIT)
        hid = (up + 1.0) * (gate * jax.nn.sigmoid(SWIGLU_ALPHA * gate))
        yp = _pack_rows(_dot(hid.astype(_BF), wd_bf[...]) + bd_ref[...])
        ya_ref[r0:r0 + rows, :] = yp[:, :HALF]
        yb_ref[r0:r0 + rows, :] = yp[:, HALF:]

    bm = xa_ref.shape[0]
    step = bm // MOE_ROW_STEPS
    need = (bv_ref[i] + step - 1) // step
    for n in range(1, MOE_ROW_STEPS + 1):
        @pl.when(jnp.logical_and(used, need == n))
        def _(n=n):
            ffn(0, n * step)


def _moe_call(block_e, weight_e, block_row, block_valid, n_used, xa, xb, wgu, bgu, wd, bd):
    n_slots = xa.shape[0]
    bm = MOE_BM
    n_blocks = block_e.shape[0]
    grid_spec = pltpu.PrefetchScalarGridSpec(
        num_scalar_prefetch=5,
        grid=(n_blocks,),
        in_specs=[
            pl.BlockSpec((bm, HALF), lambda i, be, we, rb, bv, nu: (rb[i], 0)),
            pl.BlockSpec((bm, HALF), lambda i, be, we, rb, bv, nu: (rb[i], 0)),
            pl.BlockSpec((None, D_MODEL, 2 * D_FF_EXPERT), lambda i, be, we, rb, bv, nu: (we[i], 0, 0)),
            pl.BlockSpec((None, 1, 2 * D_FF_EXPERT), lambda i, be, we, rb, bv, nu: (be[i], 0, 0)),
            pl.BlockSpec((None, D_FF_EXPERT, D_MODEL), lambda i, be, we, rb, bv, nu: (we[i], 0, 0)),
            pl.BlockSpec((None, 1, D_MODEL), lambda i, be, we, rb, bv, nu: (be[i], 0, 0)),
        ],
        out_specs=[pl.BlockSpec((bm, HALF), lambda i, be, we, rb, bv, nu: (rb[i], 0))] * 2,
        scratch_shapes=[pltpu.VMEM((D_MODEL, 2 * D_FF_EXPERT), _BF),
                        pltpu.VMEM((D_FF_EXPERT, D_MODEL), _BF)],
    )
    return pl.pallas_call(
        _moe_kernel,
        grid_spec=grid_spec,
        out_shape=[jax.ShapeDtypeStruct((n_slots, HALF), jnp.int32)] * 2,
        compiler_params=pltpu.CompilerParams(dimension_semantics=("arbitrary",),
                                             vmem_limit_bytes=VMEM_LIMIT),
        name="expert_ffn",
    )(block_e, weight_e, block_row, block_valid, n_used, xa, xb, wgu, bgu, wd, bd)


def _combine_kernel(*refs):
    ya_refs, yb_refs = refs[:TOP_K], refs[TOP_K:2 * TOP_K]
    tw_ref, h1_ref, g2_ref, b2_ref, o_ref = refs[2 * TOP_K:]
    tw = tw_ref[...]
    parts = [jnp.zeros((tw.shape[0], HALF), _F32) for _ in range(4)]
    for k in range(TOP_K):
        a_lo, a_hi = _unpack_rows(ya_refs[k][...])
        b_lo, b_hi = _unpack_rows(yb_refs[k][...])
        w = tw[:, k:k + 1]
        parts = [p + w * y for p, y in zip(parts, (a_lo, b_lo, a_hi, b_hi))]
    ffn = jnp.concatenate(parts, axis=1)
    o_ref[...] = _layer_norm(DEEPNORM_ALPHA * h1_ref[...] + ffn, g2_ref[...], b2_ref[...])


def _combine_call(yga, ygb, tw, h1, g2, b2):
    t = h1.shape[0]
    tm = COMB_TM
    row = lambda w: pl.BlockSpec((tm, w), lambda i: (i, 0))
    krow = lambda k: pl.BlockSpec((tm, HALF), lambda i: (k * (t // tm) + i, 0))
    return pl.pallas_call(
        _combine_kernel,
        grid=(t // tm,),
        in_specs=[krow(k) for k in range(TOP_K)] * 2 + [row(LANES), row(D_MODEL),
                  _const_spec(g2.shape), _const_spec(b2.shape)],
        out_specs=row(D_MODEL),
        out_shape=jax.ShapeDtypeStruct((t, D_MODEL), _F32),
        compiler_params=pltpu.CompilerParams(dimension_semantics=("arbitrary",),
                                             vmem_limit_bytes=VMEM_LIMIT),
        name="combine_ln2",
    )(*([yga] * TOP_K + [ygb] * TOP_K), tw, h1, g2, b2)


def _prepare_weights(w_in, w_uq):
    o_q = 3 * CONV_DIM
    o_kv = o_q + Q_LORA_RANK
    o_pe = o_kv + KV_LORA_RANK
    o_gc = o_pe + QK_ROPE_DIM
    wg3 = w_in[:, :o_q]
    kpe = w_in[:, o_pe:o_gc]
    zpad = jnp.zeros((D_MODEL, ROPE_PAD - QK_ROPE_DIM), w_in.dtype)
    wlat = jnp.concatenate([w_in[:, o_q:o_pe], kpe, zpad], axis=1)
    wgates = w_in[:, o_gc:]
    wq = w_uq.reshape(Q_LORA_RANK, N_HEADS, QK_HEAD_DIM)
    wuqa = jnp.pad(wq, ((0, 0), (0, 0), (0, QK_PAD - QK_HEAD_DIM)))
    return (wg3.astype(_BF), wlat.astype(_BF), wgates.astype(_BF),
            wuqa.reshape(Q_LORA_RANK, N_HEADS * QK_PAD).T.astype(_BF))


def _rope_rows():
    half = QK_ROPE_DIM // 2
    inv_freq = ROPE_BASE ** (-jnp.arange(0, QK_ROPE_DIM, 2, dtype=_F32) / QK_ROPE_DIM)
    z = jnp.zeros((ROPE_PAD - QK_ROPE_DIM,), _F32)
    ones = jnp.ones((half,), _F32)
    rows = jnp.stack([jnp.concatenate([inv_freq] * 4),
                      jnp.concatenate([ones, ones, z]),
                      jnp.concatenate([-ones, ones, z]),
                      jnp.concatenate([ones * (np.pi / 2), ones * (np.pi / 2), z])])
    return jnp.pad(rows, ((0, 8 - rows.shape[0]), (0, 0)))


def _block_tables(counts, n_tok):
    bm = MOE_BM
    n_blocks = n_tok * TOP_K // bm + N_EXPERTS
    nblk = (counts + bm - 1) // bm
    end = jnp.cumsum(nblk)
    start = end - nblk
    n_used = end[-1]
    i = jnp.minimum(jnp.arange(n_blocks, dtype=jnp.int32), n_used - 1)
    block_e = (end[None, :] <= i[:, None]).sum(axis=1).astype(jnp.int32)
    first = jnp.sum(jnp.where(jnp.arange(N_EXPERTS)[None, :] == block_e[:, None], start[None, :], 0), axis=1)
    block_row = block_e * (n_tok // bm) + (i - first)
    fresh = jnp.concatenate([jnp.ones((1,), jnp.bool_), block_e[1:] != block_e[:-1]])
    later = jnp.where(block_e[None, :] > block_e[:, None], block_e[None, :], N_EXPERTS)
    nxt = jnp.min(later, axis=1)
    weight_e = jnp.where(jnp.logical_or(fresh, nxt == N_EXPERTS), block_e, nxt).astype(jnp.int32)
    cnt = jnp.sum(jnp.where(jnp.arange(N_EXPERTS)[None, :] == block_e[:, None], counts[None, :], 0), axis=1)
    block_valid = jnp.clip(cnt - (i - first) * bm, 0, bm).astype(jnp.int32)
    return (block_e, weight_e, block_row.astype(jnp.int32), block_valid,
            n_used.astype(jnp.int32).reshape(1))


def kernel(x, positions, w_in, conv_w, q_norm_g, w_uq, kv_norm_g, w_uk, w_uv, w_conv_branch,
           w_attn_branch, w_out, ln1_g, ln1_b, w_router, b_router, w_gate_up, b_gate_up, w_down,
           b_down, ln2_g, ln2_b):
    b, s, d = x.shape
    t = b * s
    x2d = x.reshape(t, d)
    pos = positions.reshape(t, 1).astype(jnp.int32)
    rope = _rope_rows()
    for l in range(DEPTH):
        wg3, wlat, wgates, wuqa = _prepare_weights(w_in[l], w_uq[l])
        q, kn, kpe, v, ycg, sga = _proj_call(
            x2d, pos, rope, wg3, wlat, wgates, conv_w[l], q_norm_g[l][None, :], kv_norm_g[l][None, :],
            wuqa, w_uk[l].astype(_BF), w_uv[l].astype(_BF), w_conv_branch[l].astype(_BF), s)
        attn = _attn_call(q, kn, kpe.reshape(b, s, -1), v)
        wr_t = w_router[l].T
        wr_hi = wr_t.astype(_BF)
        wr_lo = (wr_t - wr_hi.astype(_F32)).astype(_BF)
        h1, h1pa, h1pb, slots, tw_pad, counts = _post_call(
            attn, ycg, sga, x2d, w_attn_branch[l].astype(_BF), w_out[l].astype(_BF),
            ln1_g[l][None, :], ln1_b[l][None, :], jnp.concatenate([wr_hi, wr_lo], axis=0),
            jnp.broadcast_to(b_router[l][:, None], (N_EXPERTS, LANES)))
        block_e, weight_e, block_row, block_valid, n_used = _block_tables(counts[:, 0].astype(jnp.int32), t)
        xa, xb = _sc_scatter_rows([h1pa, h1pb], slots, TOP_K, N_EXPERTS * t)
        ya, yb = _moe_call(block_e, weight_e, block_row, block_valid, n_used, xa, xb,
                           w_gate_up[l], b_gate_up[l][:, None, :], w_down[l], b_down[l][:, None, :])
        yga, ygb = _sc_gather_rows([ya, yb], slots, TOP_K)
        x2d = _combine_call(yga, ygb, tw_pad, h1, ln2_g[l][None, :], ln2_b[l][None, :])
    return x2d.reshape(b, s, d)
```

```python
import functools

import numpy as np
import jax
import jax.numpy as jnp
from jax import lax
from jax.experimental import pallas as pl
from jax.experimental.pallas import tpu as pltpu
from jax.experimental.pallas import tpu_sc as plsc

D_MODEL = 1024
CHUNK = 64
CONV_DIM = D_MODEL
CONV_WIDTH = 3
N_HEADS = 8
QK_NOPE_DIM = 128
QK_ROPE_DIM = 64
V_HEAD_DIM = 128
QK_HEAD_DIM = QK_NOPE_DIM + QK_ROPE_DIM
Q_LORA_RANK = 384
KV_LORA_RANK = 256
ROPE_BASE = 10000.0
N_EXPERTS = 32
TOP_K = 4
D_FF_EXPERT = D_MODEL
SWIGLU_LIMIT = 7.0
SWIGLU_ALPHA = 1.702
LN_EPS = 1e-5
RMS_EPS = 1e-6
DEPTH = 1
DEEPNORM_ALPHA = (2 * DEPTH) ** 0.25

LANES = 128
QK_PAD = 2 * LANES
ROPE_PAD = LANES

PROJ_TM = 512
ATTN_TQ = 512
ATTN_TK = 512
ONES_ROWS = 16
POST_TM = 512
MOE_BM = 1024
MOE_ROW_STEPS = 8
COMB_TM = 512
SC_WINDOW = 128
PACKED = D_MODEL // 2
HALF = PACKED // 2
HALO = 8
NEG_BIG = -1e30
LOG2E = 1.4426950408889634
VMEM_LIMIT = 56 * 1024 * 1024

_BF = jnp.bfloat16
_F32 = jnp.float32


def _dot(a, b):
    return jnp.dot(a, b, preferred_element_type=_F32)


def _const_spec(shape):
    nd = len(shape)
    return pl.BlockSpec(shape, lambda *_: (0,) * nd, pipeline_mode=pl.Buffered(1))


def _pack_rows(a):
    half = a.shape[1] // 2
    lo = lax.bitcast_convert_type(a[:, :half].astype(_BF).astype(_F32), jnp.int32)
    hi = lax.bitcast_convert_type(a[:, half:].astype(_BF).astype(_F32), jnp.int32)
    return lax.shift_right_logical(lo, 16) | (hi & jnp.int32(-65536))


def _unpack_rows(p):
    lo = lax.bitcast_convert_type(lax.shift_left(p, 16), _F32)
    hi = lax.bitcast_convert_type(p & jnp.int32(-65536), _F32)
    return lo, hi


def _sc_mesh():
    return plsc.VectorSubcoreMesh(core_axis_name="core", subcore_axis_name="subcore")


def _sc_scatter_rows(tables, idx, n_copies, n_out):
    n_rows, width = tables[0].shape
    n_tab = len(tables)
    window = SC_WINDOW

    @functools.partial(pl.kernel, mesh=_sc_mesh(),
                       out_type=[jax.ShapeDtypeStruct((n_out, width), tables[0].dtype)] * n_tab)
    def scatter_kernel(*refs):
        idx_hbm = refs[n_tab]
        for src_hbm, out_hbm in zip(refs[:n_tab], refs[n_tab + 1:]):
            def body(src_vmem, idx_vmem, out_hbm=out_hbm):
                for k in range(n_copies):
                    pltpu.sync_copy(src_vmem, out_hbm.at[idx_vmem.at[k]])

            pltpu.emit_pipeline(
                body,
                grid=(n_rows // window,),
                in_specs=[pl.BlockSpec((window, width), lambda i: (i, 0)),
                          pl.BlockSpec((idx.shape[0], window), lambda i: (0, i))],
                out_specs=[],
                core_axis_name=("core", "subcore"),
                dimension_semantics=(pltpu.PARALLEL,),
            )(src_hbm, idx_hbm)

    return scatter_kernel(*tables, idx)


def _sc_gather_rows(tables, idx, n_copies):
    n_rows = idx.shape[1]
    width = tables[0].shape[1]
    n_tab = len(tables)
    window = SC_WINDOW
    steps = n_rows // window

    @functools.partial(pl.kernel, mesh=_sc_mesh(),
                       out_type=[jax.ShapeDtypeStruct((n_copies * n_rows, width), tables[0].dtype)] * n_tab)
    def gather_kernel(*refs):
        idx_hbm = refs[n_tab]
        for table_hbm, out_hbm in zip(refs[:n_tab], refs[n_tab + 1:]):
            def body(idx_vmem, out_vmem, table_hbm=table_hbm):
                pltpu.sync_copy(table_hbm.at[idx_vmem.at[0]], out_vmem)

            pltpu.emit_pipeline(
                body,
                grid=(n_copies, steps),
                in_specs=[pl.BlockSpec((1, window), lambda k, i: (k, i))],
                out_specs=[pl.BlockSpec((window, width), lambda k, i: (k * steps + i, 0))],
                core_axis_name=("core", "subcore"),
                dimension_semantics=(pltpu.PARALLEL, pltpu.PARALLEL),
            )(idx_hbm, out_hbm)

    return gather_kernel(*tables, idx)


def _proj_kernel(tiles_per_seq,
                 x_ref, pos_ref, rope_ref, wg3_ref, wlat_ref, wgates_ref, convw_ref,
                 qg_ref, kvg_ref, wuqa_ref, wuk_ref, wuv_ref, wcb_ref,
                 qt_ref, kn_ref, kpe_ref, v_ref, ycg_ref, sga_ref, ubuf):
    i = pl.program_id(0)
    tm = x_ref.shape[0]
    xb = x_ref[...].astype(_BF)

    @pl.when(i % tiles_per_seq == 0)
    def _():
        ubuf[0:HALO, :] = jnp.zeros((HALO, CONV_DIM), _F32)

    cw = convw_ref[...]
    hc = CONV_DIM // 2
    yc = None
    for j in range(2):
        c0 = j * hc
        gate_b = _dot(xb, wg3_ref[:, c0:c0 + hc])
        u = (_dot(xb, wg3_ref[:, CONV_DIM + c0:CONV_DIM + c0 + hc])
             * _dot(xb, wg3_ref[:, 2 * CONV_DIM + c0:2 * CONV_DIM + c0 + hc]))
        ubuf[HALO:HALO + tm, c0:c0 + hc] = u
        conv = (cw[2:3, c0:c0 + hc] * u + cw[1:2, c0:c0 + hc] * ubuf[HALO - 1:HALO - 1 + tm, c0:c0 + hc]
                + cw[0:1, c0:c0 + hc] * ubuf[HALO - 2:HALO - 2 + tm, c0:c0 + hc])
        part = _dot((gate_b * conv).astype(_BF), wcb_ref[c0:c0 + hc, :])
        yc = part if yc is None else yc + part
    ubuf[0:HALO, :] = ubuf[tm:tm + HALO, :]
    gates = _dot(xb, wgates_ref[...])
    ycg_ref[...] = (jax.nn.sigmoid(gates[:, :D_MODEL]) * yc).astype(_BF)
    sga_ref[...] = jax.nn.sigmoid(gates[:, D_MODEL:]).astype(_BF)

    lat = _dot(xb, wlat_ref[...])
    trig = jnp.sin(pos_ref[...].astype(_F32) * rope_ref[0:1, :] + rope_ref[3:4, :])
    cosb = trig * rope_ref[1:2, :]
    sinb = pltpu.roll(trig, QK_ROPE_DIM, axis=1) * rope_ref[2:3, :]
    q_lat = lat[:, :Q_LORA_RANK]
    rq = q_lat * lax.rsqrt(jnp.mean(q_lat * q_lat, axis=-1, keepdims=True) + RMS_EPS) * qg_ref[...]
    kv_lat = lat[:, Q_LORA_RANK:Q_LORA_RANK + KV_LORA_RANK]
    ckv = (kv_lat * lax.rsqrt(jnp.mean(kv_lat * kv_lat, axis=-1, keepdims=True) + RMS_EPS)
           * kvg_ref[...]).astype(_BF)
    o = Q_LORA_RANK + KV_LORA_RANK
    lane = lax.broadcasted_iota(jnp.int32, (tm, ROPE_PAD), 1)
    half = QK_ROPE_DIM // 2

    def swap_halves(u):
        return jnp.where(lane < half, pltpu.roll(u, ROPE_PAD - half, axis=1), pltpu.roll(u, half, axis=1))

    k_pe = lat[:, o:o + ROPE_PAD]
    kpe_ref[...] = (k_pe * cosb + swap_halves(k_pe) * sinb).astype(_BF)
    kn = _dot(ckv, wuk_ref[...]).astype(_BF)
    vv = _dot(ckv, wuv_ref[...]).astype(_BF)
    for h in range(N_HEADS):
        kn_ref[h] = kn[:, h * QK_NOPE_DIM:(h + 1) * QK_NOPE_DIM]
        v_ref[h] = vv[:, h * V_HEAD_DIM:(h + 1) * V_HEAD_DIM]

    qa_t = _dot(wuqa_ref[...], rq.T.astype(_BF))
    cos_t, sin_t = cosb.T, sinb.T
    scale = QK_HEAD_DIM ** -0.5 * LOG2E
    for h in range(N_HEADS):
        lo = h * QK_PAD
        pe_t = qa_t[lo + QK_NOPE_DIM:lo + QK_PAD, :]
        sw_t = jnp.concatenate([pe_t[half:2 * half], pe_t[0:half], pe_t[2 * half:]], axis=0)
        hi_t = pe_t * cos_t + sw_t * sin_t
        qt_ref[h, :QK_NOPE_DIM, :] = (qa_t[lo:lo + QK_NOPE_DIM, :] * scale).astype(_BF)
        qt_ref[h, QK_NOPE_DIM:, :] = (hi_t * scale).astype(_BF)


def _proj_call(x2d, pos, rope, wg3, wlat, wgates, convw, qg, kvg, wuqa, wuk, wuv, wcb, seq):
    t = x2d.shape[0]
    tm = PROJ_TM
    tps = seq // tm
    row = lambda w: pl.BlockSpec((tm, w), lambda i: (i, 0))
    heads = lambda w: pl.BlockSpec((None, N_HEADS, tm, w), lambda i: (i // tps, 0, i % tps, 0))
    head_major = lambda w: jax.ShapeDtypeStruct((t // seq, N_HEADS, seq, w), _BF)
    outs = [
        jax.ShapeDtypeStruct((t // seq, N_HEADS, QK_PAD, seq), _BF),
        head_major(QK_NOPE_DIM),
        jax.ShapeDtypeStruct((t, ROPE_PAD), _BF),
        head_major(V_HEAD_DIM),
        jax.ShapeDtypeStruct((t, D_MODEL), _BF),
        jax.ShapeDtypeStruct((t, D_MODEL), _BF),
    ]
    return pl.pallas_call(
        functools.partial(_proj_kernel, tps),
        grid=(t // tm,),
        in_specs=[row(D_MODEL), row(1)]
        + [_const_spec(a.shape)
           for a in (rope, wg3, wlat, wgates, convw, qg, kvg, wuqa, wuk, wuv, wcb)],
        out_specs=[pl.BlockSpec((None, N_HEADS, QK_PAD, tm), lambda i: (i // tps, 0, 0, i % tps)),
                   heads(QK_NOPE_DIM), row(ROPE_PAD), heads(V_HEAD_DIM),
                   row(D_MODEL), row(D_MODEL)],
        out_shape=outs,
        scratch_shapes=[pltpu.VMEM((tm + HALO, CONV_DIM), _F32)],
        compiler_params=pltpu.CompilerParams(dimension_semantics=("arbitrary",),
                                             vmem_limit_bytes=VMEM_LIMIT),
        name="proj_conv_qkv",
    )(x2d, pos, rope, wg3, wlat, wgates, convw, qg, kvg, wuqa, wuk, wuv, wcb)


def _attn_kernel(qt_ref, kn_ref, kpe_ref, v_ref, o_ref, kfull, vt, s0, s1, p0, p1, a0, a1, m_ref, acc):
    tq, tk = ATTN_TQ, ATTN_TK
    n_tiles = qt_ref.shape[1] // tq
    s_bufs, p_bufs, a_bufs = (s0, s1), (p0, p1), (a0, a1)
    stages = [(tile, blk) for tile in range(n_tiles) for blk in range(tile + 1)]

    kfull[:, :QK_NOPE_DIM] = kn_ref[...]
    kfull[:, QK_NOPE_DIM:] = kpe_ref[...]
    vt[:V_HEAD_DIM, :] = v_ref[...].T
    vt[V_HEAD_DIM:, :] = jnp.ones((vt.shape[0] - V_HEAD_DIM, vt.shape[1]), _BF)

    hk, hq = tk // 2, tq // 2

    def scores_t(t):
        tile, blk = stages[t]
        s_ref = s_bufs[t % 2]
        k0, q0 = blk * tk, tile * tq
        if blk == tile:
            s_ref[0:hk, :] = _dot(kfull[k0:k0 + hk, :], qt_ref[:, q0:q0 + tq])
            s_ref[hk:tk, hq:tq] = _dot(kfull[k0 + hk:k0 + tk, :], qt_ref[:, q0 + hq:q0 + tq])
        else:
            s_ref[...] = _dot(kfull[k0:k0 + tk, :], qt_ref[:, q0:q0 + tq])

    key_chunk = lax.broadcasted_iota(jnp.int32, (tk, tq), 0) // CHUNK
    qry_chunk = lax.broadcasted_iota(jnp.int32, (tk, tq), 1) // CHUNK
    causal = key_chunk <= qry_chunk

    def softmax_t(t):
        tile, blk = stages[t]
        par = t % 2
        if blk == tile:
            top = jnp.where(causal[0:hk, :], s_bufs[par][0:hk, :], NEG_BIG)
            bot = jnp.where(causal[hk:tk, hq:tq], s_bufs[par][hk:tk, hq:tq], NEG_BIG)
            bot_max = jnp.concatenate([jnp.full((1, hq), NEG_BIG, _F32),
                                       jnp.max(bot, axis=0, keepdims=True)], axis=1)
            blk_max = jnp.maximum(jnp.max(top, axis=0, keepdims=True), bot_max)
        else:
            s_t = s_bufs[par][...]
            blk_max = jnp.max(s_t, axis=0, keepdims=True)
        if blk == 0:
            m_new = blk_max
        else:
            m_old = m_ref[...]
            m_new = jnp.maximum(m_old, blk_max)
            a_bufs[par][...] = jnp.exp2(m_old - m_new)
        m_ref[...] = m_new
        if blk == tile:
            p_bufs[par][0:hk, :] = jnp.exp2(top - m_new).astype(_BF)
            p_bufs[par][hk:tk, hq:tq] = jnp.exp2(bot - m_new[:, hq:tq]).astype(_BF)
        else:
            p_bufs[par][...] = jnp.exp2(s_t - m_new).astype(_BF)

    def accumulate(t):
        tile, blk = stages[t]
        par = t % 2
        k0 = blk * tk
        if blk == tile:
            pv = jnp.concatenate([_dot(vt[:, k0:k0 + hk], p_bufs[par][0:hk, 0:hq]),
                                  _dot(vt[:, k0:k0 + tk], p_bufs[par][:, hq:tq])], axis=1)
        else:
            pv = _dot(vt[:, k0:k0 + tk], p_bufs[par][...])
        acc[...] = pv if blk == 0 else a_bufs[par][...] * acc[...] + pv
        if blk == tile:
            out_t = acc[:V_HEAD_DIM, :] / acc[V_HEAD_DIM:V_HEAD_DIM + 1, :]
            o_ref[tile * tq:(tile + 1) * tq, :] = out_t.T.astype(_BF)

    scores_t(0)
    for t in range(len(stages)):
        if t + 1 < len(stages):
            scores_t(t + 1)
        if t > 0:
            accumulate(t - 1)
        softmax_t(t)
    accumulate(len(stages) - 1)


def _attn_call(q, kn, kpe, v):
    b, _, _, s = q.shape
    tq = ATTN_TQ
    assert ATTN_TQ == ATTN_TK and s % tq == 0
    per_head = lambda w: pl.BlockSpec((None, None, s, w), lambda bi, h: (bi, h, 0, 0))
    return pl.pallas_call(
        _attn_kernel,
        grid=(b, N_HEADS),
        in_specs=[pl.BlockSpec((None, None, QK_PAD, s), lambda bi, h: (bi, h, 0, 0)), per_head(QK_NOPE_DIM),
                  pl.BlockSpec((None, s, ROPE_PAD), lambda bi, h: (bi, 0, 0)), per_head(V_HEAD_DIM)],
        out_specs=per_head(V_HEAD_DIM),
        out_shape=jax.ShapeDtypeStruct((b, N_HEADS, s, V_HEAD_DIM), _BF),
        scratch_shapes=[pltpu.VMEM((s, QK_PAD), _BF), pltpu.VMEM((V_HEAD_DIM + ONES_ROWS, s), _BF),
                        pltpu.VMEM((ATTN_TK, tq), _F32), pltpu.VMEM((ATTN_TK, tq), _F32),
                        pltpu.VMEM((ATTN_TK, tq), _BF), pltpu.VMEM((ATTN_TK, tq), _BF),
                        pltpu.VMEM((1, tq), _F32), pltpu.VMEM((1, tq), _F32),
                        pltpu.VMEM((1, tq), _F32),
                        pltpu.VMEM((V_HEAD_DIM + ONES_ROWS, tq), _F32)],
        compiler_params=pltpu.CompilerParams(dimension_semantics=("arbitrary", "arbitrary"),
                                             vmem_limit_bytes=VMEM_LIMIT),
        name="chunk_causal_attention",
    )(q, kn, kpe, v)


def _layer_norm(z, g, b):
    mu = jnp.mean(z, axis=-1, keepdims=True)
    zc = z - mu
    var = jnp.mean(zc * zc, axis=-1, keepdims=True)
    return zc * lax.rsqrt(var + LN_EPS) * g + b


def _post_kernel(expert_cap, attn_ref, ycg_ref, sga_ref, x_ref, wab_ref, wout_ref, g1_ref, b1_ref,
                 wr_ref, br_ref, tri_ref, h1_ref, h1pa_ref, h1pb_ref, slot_ref, tw_ref, cnt_ref, run):
    tm = x_ref.shape[0]

    @pl.when(pl.program_id(0) == 0)
    def _():
        run[...] = jnp.zeros(run.shape, _F32)

    attn = jnp.concatenate([attn_ref[h] for h in range(N_HEADS)], axis=1)
    y_attn = _dot(attn, wab_ref[...])
    merged = ycg_ref[...].astype(_F32) + sga_ref[...].astype(_F32) * y_attn
    mix = _dot(merged.astype(_BF), wout_ref[...])
    h1 = _layer_norm(DEEPNORM_ALPHA * x_ref[...] + mix, g1_ref[...], b1_ref[...])
    h1_ref[...] = h1
    h1p = _pack_rows(h1)
    h1pa_ref[...] = h1p[:, :HALF]
    h1pb_ref[...] = h1p[:, HALF:]

    nt = (((1,), (1,)), ((), ()))
    h_hi = h1.astype(_BF)
    h_lo = (h1 - h_hi.astype(_F32)).astype(_BF)
    both = lax.dot_general(wr_ref[...], h_hi, nt, preferred_element_type=_F32)
    logits = (both[:N_EXPERTS] + both[N_EXPERTS:] + br_ref[:, 0:1]
              + lax.dot_general(wr_ref[:N_EXPERTS, :], h_lo, nt, preferred_element_type=_F32))
    expert = lax.broadcasted_iota(jnp.int32, (N_EXPERTS, tm), 0)
    work = logits
    chosen = jnp.zeros((N_EXPERTS, tm), jnp.bool_)
    sels, exps = [], []
    for k in range(TOP_K):
        mx = jnp.max(work, axis=0, keepdims=True)
        sel = jnp.min(jnp.where(work == mx, expert, N_EXPERTS), axis=0, keepdims=True)
        hit = expert == sel
        sels.append(sel)
        exps.append(jnp.exp(mx - (mx if k == 0 else top0)))
        if k == 0:
            top0 = mx
        chosen = jnp.logical_or(chosen, hit)
        work = jnp.where(hit, -jnp.inf, work)
    denom = exps[0] + exps[1] + exps[2] + exps[3]

    onehot = chosen.astype(_F32)
    before = _dot(onehot.astype(_BF), tri_ref[...]) + run[:, 0:1]
    rows = []
    for k in range(TOP_K):
        rank = jnp.sum(jnp.where(expert == sels[k], before, 0.0), axis=0, keepdims=True)
        rows.append(sels[k] * expert_cap + rank.astype(jnp.int32))
    slot_ref[...] = jnp.concatenate(rows + [jnp.zeros((8 - TOP_K, tm), jnp.int32)], axis=0)
    run[...] = run[...] + jnp.sum(onehot, axis=1, keepdims=True)
    cnt_ref[...] = run[...]
    tw_t = jnp.concatenate([e / denom for e in exps] + [jnp.zeros((LANES - TOP_K, tm), _F32)], axis=0)
    tw_ref[...] = tw_t.T


def _post_call(attn, ycg, sga, x2d, wab, wout, g1, b1, wr, br):
    t = x2d.shape[0]
    tm = POST_TM
    row = lambda w: pl.BlockSpec((tm, w), lambda i: (i, 0))
    tps = attn.shape[2] // tm
    tri = (jnp.arange(tm)[:, None] < jnp.arange(tm)[None, :]).astype(_BF)
    outs = [
        jax.ShapeDtypeStruct((t, D_MODEL), _F32),
        jax.ShapeDtypeStruct((t, HALF), jnp.int32),
        jax.ShapeDtypeStruct((t, HALF), jnp.int32),
        jax.ShapeDtypeStruct((8, t), jnp.int32),
        jax.ShapeDtypeStruct((t, LANES), _F32),
        jax.ShapeDtypeStruct((N_EXPERTS, LANES), _F32),
    ]
    out_specs = [row(D_MODEL), row(HALF), row(HALF), pl.BlockSpec((8, tm), lambda i: (0, i)),
                 row(LANES), pl.BlockSpec((N_EXPERTS, LANES), lambda i: (0, 0))]
    return pl.pallas_call(
        functools.partial(_post_kernel, t),
        grid=(t // tm,),
        in_specs=[pl.BlockSpec((None, N_HEADS, tm, V_HEAD_DIM), lambda i: (i // tps, 0, i % tps, 0))]
        + [row(D_MODEL)] * 3
        + [_const_spec(a.shape) for a in (wab, wout, g1, b1, wr, br, tri)],
        out_specs=out_specs,
        out_shape=outs,
        scratch_shapes=[pltpu.VMEM((N_EXPERTS, LANES), _F32)],
        compiler_params=pltpu.CompilerParams(dimension_semantics=("arbitrary",),
                                             vmem_limit_bytes=VMEM_LIMIT),
        name="merge_out_ln1_router",
    )(attn, ycg, sga, x2d, wab, wout, g1, b1, wr, br, tri)


def _moe_kernel(be_ref, we_ref, rb_ref, bv_ref, nused_ref, xa_ref, xb_ref, wgu_ref, bgu_ref, wd_ref, bd_ref,
                ya_ref, yb_ref, wgu_bf, wd_bf):
    i = pl.program_id(0)
    prev = be_ref[jnp.maximum(i - 1, 0)]
    fresh = jnp.logical_or(i == 0, be_ref[i] != prev)
    used = i < nused_ref[0]

    @pl.when(jnp.logical_and(fresh, used))
    def _():
        wgu_bf[...] = wgu_ref[...].astype(_BF)
        wd_bf[...] = wd_ref[...].astype(_BF)

    def ffn(r0, rows):
        a_lo, a_hi = _unpack_rows(xa_ref[r0:r0 + rows, :])
        b_lo, b_hi = _unpack_rows(xb_ref[r0:r0 + rows, :])
        xs = jnp.concatenate([a_lo, b_lo, a_hi, b_hi], axis=1).astype(_BF)
        gu = _dot(xs, wgu_bf[...]) + bgu_ref[...]
        gate = jnp.minimum(gu[:, :D_FF_EXPERT], SWIGLU_LIMIT)
        up = jnp.clip(gu[:, D_FF_EXPERT:], -SWIGLU_LIMIT, SWIGLU_LIMIT)
        hid = (up + 1.0) * (gate * jax.nn.sigmoid(SWIGLU_ALPHA * gate))
        yp = _pack_rows(_dot(hid.astype(_BF), wd_bf[...]) + bd_ref[...])
        ya_ref[r0:r0 + rows, :] = yp[:, :HALF]
        yb_ref[r0:r0 + rows, :] = yp[:, HALF:]

    bm = xa_ref.shape[0]
    step = bm // MOE_ROW_STEPS
    need = (bv_ref[i] + step - 1) // step
    for n in range(1, MOE_ROW_STEPS + 1):
        @pl.when(jnp.logical_and(used, need == n))
        def _(n=n):
            ffn(0, n * step)


def _moe_call(block_e, weight_e, block_row, block_valid, n_used, xa, xb, wgu, bgu, wd, bd):
    n_slots = xa.shape[0]
    bm = MOE_BM
    n_blocks = block_e.shape[0]
    grid_spec = pltpu.PrefetchScalarGridSpec(
        num_scalar_prefetch=5,
        grid=(n_blocks,),
        in_specs=[
            pl.BlockSpec((bm, HALF), lambda i, be, we, rb, bv, nu: (rb[i], 0)),
            pl.BlockSpec((bm, HALF), lambda i, be, we, rb, bv, nu: (rb[i], 0)),
            pl.BlockSpec((None, D_MODEL, 2 * D_FF_EXPERT), lambda i, be, we, rb, bv, nu: (we[i], 0, 0)),
            pl.BlockSpec((None, 1, 2 * D_FF_EXPERT), lambda i, be, we, rb, bv, nu: (be[i], 0, 0)),
            pl.BlockSpec((None, D_FF_EXPERT, D_MODEL), lambda i, be, we, rb, bv, nu: (we[i], 0, 0)),
            pl.BlockSpec((None, 1, D_MODEL), lambda i, be, we, rb, bv, nu: (be[i], 0, 0)),
        ],
        out_specs=[pl.BlockSpec((bm, HALF), lambda i, be, we, rb, bv, nu: (rb[i], 0))] * 2,
        scratch_shapes=[pltpu.VMEM((D_MODEL, 2 * D_FF_EXPERT), _BF),
                        pltpu.VMEM((D_FF_EXPERT, D_MODEL), _BF)],
    )
    return pl.pallas_call(
        _moe_kernel,
        grid_spec=grid_spec,
        out_shape=[jax.ShapeDtypeStruct((n_slots, HALF), jnp.int32)] * 2,
        compiler_params=pltpu.CompilerParams(dimension_semantics=("arbitrary",),
                                             vmem_limit_bytes=VMEM_LIMIT),
        name="expert_ffn",
    )(block_e, weight_e, block_row, block_valid, n_used, xa, xb, wgu, bgu, wd, bd)


def _combine_kernel(*refs):
    ya_refs, yb_refs = refs[:TOP_K], refs[TOP_K:2 * TOP_K]
    tw_ref, h1_ref, g2_ref, b2_ref, o_ref = refs[2 * TOP_K:]
    tw = tw_ref[...]
    parts = [jnp.zeros((tw.shape[0], HALF), _F32) for _ in range(4)]
    for k in range(TOP_K):
        a_lo, a_hi = _unpack_rows(ya_refs[k][...])
        b_lo, b_hi = _unpack_rows(yb_refs[k][...])
        w = tw[:, k:k + 1]
        parts = [p + w * y for p, y in zip(parts, (a_lo, b_lo, a_hi, b_hi))]
    ffn = jnp.concatenate(parts, axis=1)
    o_ref[...] = _layer_norm(DEEPNORM_ALPHA * h1_ref[...] + ffn, g2_ref[...], b2_ref[...])


def _combine_call(yga, ygb, tw, h1, g2, b2):
    t = h1.shape[0]
    tm = COMB_TM
    row = lambda w: pl.BlockSpec((tm, w), lambda i: (i, 0))
    krow = lambda k: pl.BlockSpec((tm, HALF), lambda i: (k * (t // tm) + i, 0))
    return pl.pallas_call(
        _combine_kernel,
        grid=(t // tm,),
        in_specs=[krow(k) for k in range(TOP_K)] * 2 + [row(LANES), row(D_MODEL),
                  _const_spec(g2.shape), _const_spec(b2.shape)],
        out_specs=row(D_MODEL),
        out_shape=jax.ShapeDtypeStruct((t, D_MODEL), _F32),
        compiler_params=pltpu.CompilerParams(dimension_semantics=("arbitrary",),
                                             vmem_limit_bytes=VMEM_LIMIT),
        name="combine_ln2",
    )(*([yga] * TOP_K + [ygb] * TOP_K), tw, h1, g2, b2)


def _prepare_weights(w_in, w_uq):
    o_q = 3 * CONV_DIM
    o_kv = o_q + Q_LORA_RANK
    o_pe = o_kv + KV_LORA_RANK
    o_gc = o_pe + QK_ROPE_DIM
    wg3 = w_in[:, :o_q]
    kpe = w_in[:, o_pe:o_gc]
    zpad = jnp.zeros((D_MODEL, ROPE_PAD - QK_ROPE_DIM), w_in.dtype)
    wlat = jnp.concatenate([w_in[:, o_q:o_pe], kpe, zpad], axis=1)
    wgates = w_in[:, o_gc:]
    wq = w_uq.reshape(Q_LORA_RANK, N_HEADS, QK_HEAD_DIM)
    wuqa = jnp.pad(wq, ((0, 0), (0, 0), (0, QK_PAD - QK_HEAD_DIM)))
    return (wg3.astype(_BF), wlat.astype(_BF), wgates.astype(_BF),
            wuqa.reshape(Q_LORA_RANK, N_HEADS * QK_PAD).T.astype(_BF))


def _rope_rows():
    half = QK_ROPE_DIM // 2
    inv_freq = ROPE_BASE ** (-jnp.arange(0, QK_ROPE_DIM, 2, dtype=_F32) / QK_ROPE_DIM)
    z = jnp.zeros((ROPE_PAD - QK_ROPE_DIM,), _F32)
    ones = jnp.ones((half,), _F32)
    rows = jnp.stack([jnp.concatenate([inv_freq] * 4),
                      jnp.concatenate([ones, ones, z]),
                      jnp.concatenate([-ones, ones, z]),
                      jnp.concatenate([ones * (np.pi / 2), ones * (np.pi / 2), z])])
    return jnp.pad(rows, ((0, 8 - rows.shape[0]), (0, 0)))


def _block_tables(counts, n_tok):
    bm = MOE_BM
    n_blocks = n_tok * TOP_K // bm + N_EXPERTS
    nblk = (counts + bm - 1) // bm
    end = jnp.cumsum(nblk)
    start = end - nblk
    n_used = end[-1]
    i = jnp.minimum(jnp.arange(n_blocks, dtype=jnp.int32), n_used - 1)
    block_e = (end[None, :] <= i[:, None]).sum(axis=1).astype(jnp.int32)
    first = jnp.sum(jnp.where(jnp.arange(N_EXPERTS)[None, :] == block_e[:, None], start[None, :], 0), axis=1)
    block_row = block_e * (n_tok // bm) + (i - first)
    fresh = jnp.concatenate([jnp.ones((1,), jnp.bool_), block_e[1:] != block_e[:-1]])
    later = jnp.where(block_e[None, :] > block_e[:, None], block_e[None, :], N_EXPERTS)
    nxt = jnp.min(later, axis=1)
    weight_e = jnp.where(jnp.logical_or(fresh, nxt == N_EXPERTS), block_e, nxt).astype(jnp.int32)
    cnt = jnp.sum(jnp.where(jnp.arange(N_EXPERTS)[None, :] == block_e[:, None], counts[None, :], 0), axis=1)
    block_valid = jnp.clip(cnt - (i - first) * bm, 0, bm).astype(jnp.int32)
    return (block_e, weight_e, block_row.astype(jnp.int32), block_valid,
            n_used.astype(jnp.int32).reshape(1))


def kernel(x, positions, w_in, conv_w, q_norm_g, w_uq, kv_norm_g, w_uk, w_uv, w_conv_branch,
           w_attn_branch, w_out, ln1_g, ln1_b, w_router, b_router, w_gate_up, b_gate_up, w_down,
           b_down, ln2_g, ln2_b):
    b, s, d = x.shape
    t = b * s
    x2d = x.reshape(t, d)
    pos = positions.reshape(t, 1).astype(jnp.int32)
    rope = _rope_rows()
    for l in range(DEPTH):
        wg3, wlat, wgates, wuqa = _prepare_weights(w_in[l], w_uq[l])
        q, kn, kpe, v, ycg, sga = _proj_call(
            x2d, pos, rope, wg3, wlat, wgates, conv_w[l], q_norm_g[l][None, :], kv_norm_g[l][None, :],
            wuqa, w_uk[l].astype(_BF), w_uv[l].astype(_BF), w_conv_branch[l].astype(_BF), s)
        attn = _attn_call(q, kn, kpe.reshape(b, s, -1), v)
        wr_t = w_router[l].T
        wr_hi = wr_t.astype(_BF)
        wr_lo = (wr_t - wr_hi.astype(_F32)).astype(_BF)
        h1, h1pa, h1pb, slots, tw_pad, counts = _post_call(
            attn, ycg, sga, x2d, w_attn_branch[l].astype(_BF), w_out[l].astype(_BF),
            ln1_g[l][None, :], ln1_b[l][None, :], jnp.concatenate([wr_hi, wr_lo], axis=0),
            jnp.broadcast_to(b_router[l][:, None], (N_EXPERTS, LANES)))
        block_e, weight_e, block_row, block_valid, n_used = _block_tables(counts[:, 0].astype(jnp.int32), t)
        xa, xb = _sc_scatter_rows([h1pa, h1pb], slots, TOP_K, N_EXPERTS * t)
        ya, yb = _moe_call(block_e, weight_e, block_row, block_valid, n_used, xa, xb,
                           w_gate_up[l], b_gate_up[l][:, None, :], w_down[l], b_down[l][:, None, :])
        yga, ygb = _sc_gather_rows([ya, yb], slots, TOP_K)
        x2d = _combine_call(yga, ygb, tw_pad, h1, ln2_g[l][None, :], ln2_b[l][None, :])
    return x2d.reshape(b, s, d)
```
